```python
import jax, jax.numpy as jnp
from jax import lax
import numpy as np

D_MODEL = 1024
BATCH = 2
SEQ = 8192
DEPTH = 1
DEC_BATCH = 128
DEC_SEQ = 8
PAST_LEN = 2048
PAGE_SIZE = 128

H_FOX = 8
DH_FOX = 64
FOX_BLOCK = 128
FOX_BIAS_CENTER = 2.0
H_GLA = 4
DK_GLA = 64
DV_GLA = 128
GLA_GATE_RANK = 16
GLA_TAU = 16.0
GLA_CHUNK = 64
N_MEM = 256
H_MEM = 4
DH_MEM = D_MODEL // H_MEM
N_KEYS = 128
N_EXPERTS = N_KEYS * N_KEYS
H_PEER = 8
D_KEY = 256
PEER_TOPK = 16
PEER_BLOCK = 256

EPS = 1e-6
NEG_INF = -1e30
FOX_W = H_FOX * DH_FOX
GLA_KW = H_GLA * DK_GLA
GLA_VW = H_GLA * DV_GLA
MIX_W = FOX_W + GLA_VW
IN_SIZES = (FOX_W, FOX_W, FOX_W, H_FOX, GLA_KW, GLA_KW, GLA_VW, GLA_GATE_RANK, GLA_VW)
D_IN = sum(IN_SIZES)

kernel_name = 'hybrid_fox_gla_peer_decoder_step'


def rmsnorm(x, g):
    xf = x.astype(jnp.float32)
    y = xf * lax.rsqrt(jnp.mean(xf * xf, axis=-1, keepdims=True) + EPS)
    return (y * g.astype(jnp.float32)).astype(x.dtype)


def mixer_inputs(h, w_in, b_fox_f, w_gla_gate2, b_gla_gate):
    B, L, _ = h.shape
    splits = [int(s) for s in np.cumsum(IN_SIZES)[:-1]]
    fq, fk, fv, ff, gq, gk, gv, gg, gr = jnp.split(h @ w_in, splits, axis=-1)
    fq = fq.reshape(B, L, H_FOX, DH_FOX)
    fk = fk.reshape(B, L, H_FOX, DH_FOX)
    fv = fv.reshape(B, L, H_FOX, DH_FOX)
    logf = jax.nn.log_sigmoid((ff + b_fox_f).astype(jnp.float32))
    gq = gq.reshape(B, L, H_GLA, DK_GLA) * (DK_GLA ** -0.5)
    gk = gk.reshape(B, L, H_GLA, DK_GLA)
    gv = gv.reshape(B, L, H_GLA, DV_GLA)
    z = (gg @ w_gla_gate2 + b_gla_gate).astype(jnp.float32)
    log_a = (jax.nn.log_sigmoid(z) / GLA_TAU).reshape(B, L, H_GLA, DK_GLA)
    return fq, fk, fv, logf, gq, gk, gv, log_a, gr


def fox_prompt(q, k, v, logf):
    B, L, H, Dh = q.shape
    scale = Dh ** -0.5
    c_t = jnp.cumsum(logf, axis=1).transpose(0, 2, 1)
    nb = L // FOX_BLOCK
    qb = q.reshape(B, nb, FOX_BLOCK, H, Dh).transpose(1, 0, 2, 3, 4)
    cb = c_t.reshape(B, H, nb, FOX_BLOCK).transpose(2, 0, 1, 3)
    starts = jnp.arange(nb, dtype=jnp.int32) * FOX_BLOCK
    k_pos = jnp.arange(L, dtype=jnp.int32)

    def block(args):
        qi, ci, s0 = args
        s = jnp.einsum('bqhd,bkhd->bhqk', qi, k).astype(jnp.float32) * scale
        s = s + ci[..., :, None] - c_t[:, :, None, :]
        q_pos = s0 + jnp.arange(FOX_BLOCK, dtype=jnp.int32)
        s = jnp.where(k_pos[None, :] <= q_pos[:, None], s, NEG_INF)
        p = jax.nn.softmax(s, axis=-1).astype(v.dtype)
        return jnp.einsum('bhqk,bkhd->bqhd', p, v)

    o = lax.map(block, (qb, cb, starts))
    return o.transpose(1, 0, 2, 3, 4).reshape(B, L, H * Dh)


def fox_sample(q, k, v, logf, k_past, v_past, logf_past):
    B, L, H, Dh = q.shape
    P = k_past.shape[1]
    scale = Dh ** -0.5
    c_past = jnp.cumsum(logf_past.astype(jnp.float32), axis=1).transpose(0, 2, 1)
    c_new = c_past[:, :, -1:] + jnp.cumsum(logf, axis=1).transpose(0, 2, 1)
    s_past = jnp.einsum('bqhd,bkhd->bhqk', q, k_past).astype(jnp.float32) * scale
    s_past = s_past + c_new[..., :, None] - c_past[:, :, None, :]
    s_new = jnp.einsum('bqhd,bkhd->bhqk', q, k).astype(jnp.float32) * scale
    s_new = s_new + c_new[..., :, None] - c_new[:, :, None, :]
    s_new = jnp.where(jnp.tril(jnp.ones((L, L), dtype=bool)), s_new, NEG_INF)
    p = jax.nn.softmax(jnp.concatenate([s_past, s_new], axis=-1), axis=-1).astype(v.dtype)
    o = jnp.einsum('bhqk,bkhd->bqhd', p[..., :P], v_past) + jnp.einsum('bhqk,bkhd->bqhd', p[..., P:], v)
    return o.reshape(B, L, H * Dh)


def gla_chunk_size(L):
    return GLA_CHUNK if L % GLA_CHUNK == 0 else L


def gla_chunked(q, k, v, log_a, s0, chunk):
    B, L, H, dk = q.shape
    dv = v.shape[-1]
    n = L // chunk

    def blk(t):
        return t.astype(jnp.float32).reshape(B, n, chunk, H, t.shape[-1]).transpose(0, 3, 1, 2, 4)

    q, k, v, la = blk(q), blk(k), blk(v), blk(log_a)
    b = jnp.cumsum(la, axis=3)
    b_last = b[:, :, :, -1:, :]
    q_dec = q * jnp.exp(b)
    att = jnp.einsum('bhncd,bhnsd->bhncs', q_dec, k * jnp.exp(-b))
    att = jnp.where(jnp.tril(jnp.ones((chunk, chunk), dtype=bool)), att, 0.0)
    o_intra = jnp.einsum('bhncs,bhnse->bhnce', att, v)
    chunk_kv = jnp.einsum('bhnsd,bhnse->nbhde', k * jnp.exp(b_last - b), v)
    decay = jnp.exp(b_last[:, :, :, 0, :]).transpose(2, 0, 1, 3)

    def step(S, inp):
        dec, kv = inp
        return dec[..., None] * S + kv, S

    s_final, s_prev = lax.scan(step, s0.astype(jnp.float32), (decay, chunk_kv))
    o_inter = jnp.einsum('bhncd,nbhde->bhnce', q_dec, s_prev)
    o = (o_intra + o_inter).transpose(0, 2, 3, 1, 4).reshape(B, L, H, dv)
    return o, s_final


def mix_out(fox_o, gla_o, gr, g_gla_out, w_out):
    B, L = fox_o.shape[:2]
    gla_n = rmsnorm(gla_o, g_gla_out.reshape(H_GLA, DV_GLA)).reshape(B, L, GLA_VW)
    gla_n = gla_n * jax.nn.silu(gr.astype(jnp.float32))
    merged = jnp.concatenate([fox_o, gla_n.astype(fox_o.dtype)], axis=-1)
    return merged @ w_out


def memory_kv(mem, g_mem, w_mem_k, w_mem_v):
    B, M, _ = mem.shape
    m = rmsnorm(mem, g_mem)
    return (m @ w_mem_k).reshape(B, M, H_MEM, DH_MEM), (m @ w_mem_v).reshape(B, M, H_MEM, DH_MEM)


def cross_attn(h, mk, mv, w_q, w_o):
    B, L, _ = h.shape
    q = (h @ w_q).reshape(B, L, H_MEM, DH_MEM)
    s = jnp.einsum('blhd,bmhd->bhlm', q, mk).astype(jnp.float32) * (DH_MEM ** -0.5)
    p = jax.nn.softmax(s, axis=-1).astype(mv.dtype)
    o = jnp.einsum('bhlm,bmhd->blhd', p, mv).reshape(B, L, H_MEM * DH_MEM)
    return o @ w_o


def peer_ffn(h, w_q, subkeys, u, v):
    shp = h.shape
    hf = h.reshape(-1, shp[-1])
    T = hf.shape[0]
    pad = (-T) % PEER_BLOCK
    hp = jnp.pad(hf, ((0, pad), (0, 0))).reshape(-1, PEER_BLOCK, shp[-1])

    def block(hb):
        q = (hb @ w_q).reshape(PEER_BLOCK, H_PEER, 2, D_KEY // 2)
        s1 = jnp.einsum('thd,kd->thk', q[:, :, 0], subkeys[0]).astype(jnp.float32)
        s2 = jnp.einsum('thd,kd->thk', q[:, :, 1], subkeys[1]).astype(jnp.float32)
        v1, i1 = lax.top_k(s1, PEER_TOPK)
        v2, i2 = lax.top_k(s2, PEER_TOPK)
        cand = (v1[..., :, None] + v2[..., None, :]).reshape(PEER_BLOCK, H_PEER, PEER_TOPK * PEER_TOPK)
        sc, ci = lax.top_k(cand, PEER_TOPK)
        e = (jnp.take_along_axis(i1, ci // PEER_TOPK, axis=-1) * N_KEYS
             + jnp.take_along_axis(i2, ci % PEER_TOPK, axis=-1))
        g = jax.nn.softmax(sc, axis=-1)
        a = jax.nn.gelu(jnp.einsum('thkd,td->thk', u[e], hb).astype(jnp.float32))
        w = (g * a).astype(hb.dtype)
        return jnp.einsum('thk,thkd->td', w, v[e])

    out = lax.map(block, hp).reshape(-1, shp[-1])[:T]
    return out.reshape(shp)


def setup_inputs(seed: int = 0) -> dict:
    key = jax.random.key(seed)
    ks = iter(jax.random.split(key, 32))

    def nrm(shape, scale):
        return jax.random.normal(next(ks), shape, jnp.float32) * scale

    def gain(shape):
        return 1.0 + nrm(shape, 0.02)

    n_pages = PAST_LEN // PAGE_SIZE
    n_pool = (DEC_BATCH * n_pages * 5) // 4
    x_prompt = nrm((BATCH, SEQ, D_MODEL), 1.0)
    x_sample = nrm((DEC_BATCH, DEC_SEQ, D_MODEL), 1.0)
    cache_fox_k = nrm((DEPTH, n_pool, PAGE_SIZE, H_FOX, DH_FOX), 1.0)
    cache_fox_v = nrm((DEPTH, n_pool, PAGE_SIZE, H_FOX, DH_FOX), 1.0)
    cache_fox_logf = jax.nn.log_sigmoid(FOX_BIAS_CENTER + nrm((DEPTH, n_pool, PAGE_SIZE, H_FOX), 1.0))
    state_gla = nrm((DEPTH, DEC_BATCH, H_GLA, DK_GLA, DV_GLA), 1.0)
    cache_mem_k = nrm((DEPTH, DEC_BATCH, N_MEM, H_MEM, DH_MEM), 1.0)
    cache_mem_v = nrm((DEPTH, DEC_BATCH, N_MEM, H_MEM, DH_MEM), 1.0)
    page_table = jax.random.permutation(next(ks), n_pool)[:DEC_BATCH * n_pages]
    page_table = page_table.reshape(DEC_BATCH, n_pages).astype(jnp.int32)
    mem_prompt = nrm((BATCH, N_MEM, D_MODEL), 1.0)
    return {
        'x_prompt': x_prompt,
        'x_sample': x_sample,
        'cache_fox_k': cache_fox_k,
        'cache_fox_v': cache_fox_v,
        'cache_fox_logf': cache_fox_logf,
        'state_gla': state_gla,
        'cache_mem_k': cache_mem_k,
        'cache_mem_v': cache_mem_v,
        'page_table': page_table,
        'mem_prompt': mem_prompt,
        'g_mix': gain((DEPTH, D_MODEL)),
        'w_in': nrm((DEPTH, D_MODEL, D_IN), D_MODEL ** -0.5),
        'b_fox_f': FOX_BIAS_CENTER + nrm((DEPTH, H_FOX), 0.1),
        'w_gla_gate2': nrm((DEPTH, GLA_GATE_RANK, GLA_KW), GLA_GATE_RANK ** -0.5),
        'b_gla_gate': nrm((DEPTH, GLA_KW), 0.1),
        'g_gla_out': gain((DEPTH, GLA_VW)),
        'w_out': nrm((DEPTH, MIX_W, D_MODEL), MIX_W ** -0.5),
        'g_cross': gain((DEPTH, D_MODEL)),
        'g_mem': gain((DEPTH, D_MODEL)),
        'w_mem_k': nrm((DEPTH, D_MODEL, H_MEM * DH_MEM), D_MODEL ** -0.5),
        'w_mem_v': nrm((DEPTH, D_MODEL, H_MEM * DH_MEM), D_MODEL ** -0.5),
        'w_cross_q': nrm((DEPTH, D_MODEL, H_MEM * DH_MEM), D_MODEL ** -0.5),
        'w_cross_o': nrm((DEPTH, H_MEM * DH_MEM, D_MODEL), D_MODEL ** -0.5),
        'g_ffn': gain((DEPTH, D_MODEL)),
        'peer_w_q': nrm((DEPTH, D_MODEL, H_PEER * D_KEY), D_MODEL ** -0.5),
        'peer_subkeys': nrm((DEPTH, 2, N_KEYS, D_KEY // 2), (D_KEY // 2) ** -0.5),
        'peer_u': nrm((DEPTH, N_EXPERTS, D_MODEL), D_MODEL ** -0.5),
        'peer_v': nrm((DEPTH, N_EXPERTS, D_MODEL), D_MODEL ** -0.5),
        'g_final': gain((D_MODEL,)),
    }


def reference(x_prompt, x_sample, cache_fox_k, cache_fox_v, cache_fox_logf, state_gla,
              cache_mem_k, cache_mem_v, page_table, mem_prompt, g_mix, w_in, b_fox_f,
              w_gla_gate2, b_gla_gate, g_gla_out, w_out, g_cross, g_mem, w_mem_k, w_mem_v,
              w_cross_q, w_cross_o, g_ffn, peer_w_q, peer_subkeys, peer_u, peer_v, g_final):
    xp, xs = x_prompt, x_sample
    Bp, Lp = xp.shape[:2]
    Bs, Ls = xs.shape[:2]
    fk_p, fv_p, fl_p, gs_p, mk_p, mv_p = [], [], [], [], [], []
    fk_s, fv_s, fl_s, gs_s = [], [], [], []
    for l in range(DEPTH):
        h = rmsnorm(xp, g_mix[l])
        fq, fk, fv, logf, gq, gk, gv, log_a, gr = mixer_inputs(h, w_in[l], b_fox_f[l], w_gla_gate2[l], b_gla_gate[l])
        fox_o = fox_prompt(fq, fk, fv, logf)
        s0 = jnp.zeros((Bp, H_GLA, DK_GLA, DV_GLA), jnp.float32)
        gla_o, s_fin = gla_chunked(gq, gk, gv, log_a, s0, gla_chunk_size(Lp))
        xp = xp + mix_out(fox_o, gla_o, gr, g_gla_out[l], w_out[l])
        mk, mv = memory_kv(mem_prompt, g_mem[l], w_mem_k[l], w_mem_v[l])
        xp = xp + cross_attn(rmsnorm(xp, g_cross[l]), mk, mv, w_cross_q[l], w_cross_o[l])
        xp = xp + peer_ffn(rmsnorm(xp, g_ffn[l]), peer_w_q[l], peer_subkeys[l], peer_u[l], peer_v[l])
        fk_p.append(fk)
        fv_p.append(fv)
        fl_p.append(logf)
        gs_p.append(s_fin)
        mk_p.append(mk)
        mv_p.append(mv)
        h = rmsnorm(xs, g_mix[l])
        fq, fk, fv, logf, gq, gk, gv, log_a, gr = mixer_inputs(h, w_in[l], b_fox_f[l], w_gla_gate2[l], b_gla_gate[l])
        k_past = cache_fox_k[l][page_table].reshape(Bs, -1, H_FOX, DH_FOX)
        v_past = cache_fox_v[l][page_table].reshape(Bs, -1, H_FOX, DH_FOX)
        lf_past = cache_fox_logf[l][page_table].reshape(Bs, -1, H_FOX)
        fox_o = fox_sample(fq, fk, fv, logf, k_past, v_past, lf_past)
        gla_o, s_new = gla_chunked(gq, gk, gv, log_a, state_gla[l], gla_chunk_size(Ls))
        xs = xs + mix_out(fox_o, gla_o, gr, g_gla_out[l], w_out[l])
        xs = xs + cross_attn(rmsnorm(xs, g_cross[l]), cache_mem_k[l], cache_mem_v[l], w_cross_q[l], w_cross_o[l])
        xs = xs + peer_ffn(rmsnorm(xs, g_ffn[l]), peer_w_q[l], peer_subkeys[l], peer_u[l], peer_v[l])
        fk_s.append(fk)
        fv_s.append(fv)
        fl_s.append(logf)
        gs_s.append(s_new)
    y_prompt = rmsnorm(xp, g_final)
    y_sample = rmsnorm(xs, g_final)
    return (y_prompt, y_sample,
            jnp.stack(fk_p, axis=0), jnp.stack(fv_p, axis=0), jnp.stack(fl_p, axis=0),
            jnp.stack(gs_p, axis=0), jnp.stack(mk_p, axis=0), jnp.stack(mv_p, axis=0),
            jnp.stack(fk_s, axis=0), jnp.stack(fv_s, axis=0), jnp.stack(fl_s, axis=0),
            jnp.stack(gs_s, axis=0))
```

```python
import functools

import numpy as np
import jax
import jax.numpy as jnp
from jax import lax
from jax.experimental import pallas as pl
from jax.experimental.pallas import tpu as pltpu

F32 = jnp.float32
BF16 = jnp.bfloat16

D_MODEL = 1024
H_FOX, DH_FOX = 8, 64
H_GLA, DK_GLA, DV_GLA = 4, 64, 128
GLA_GATE_RANK = 16
GLA_TAU = 16.0
GLA_CHUNK = 64
H_MEM = 4
DH_MEM = D_MODEL // H_MEM
N_KEYS = 128
H_PEER = 8
PEER_TOPK = 16
EPS = 1e-6
NEG_INF = -1e30
FOX_W = H_FOX * DH_FOX
GLA_KW = H_GLA * DK_GLA
GLA_VW = H_GLA * DV_GLA
IN_SIZES = (FOX_W, FOX_W, FOX_W, H_FOX, GLA_KW, GLA_KW, GLA_VW, GLA_GATE_RANK, GLA_VW)

LANES = 128
VMEM_LIMIT_BYTES = 56 * 2**20

TOK_TILE = 512
INPROJ_TILE = 256
FOX_TILE = 512
GLA_GROUP = 512
PEER_TOK = 512
PEER_EBLK = 1024
PREP_EBLK = 512
N_CAND = 50
N_CAND_PAD = 64


def _cparams(*sem):
    return pltpu.CompilerParams(dimension_semantics=sem, vmem_limit_bytes=VMEM_LIMIT_BYTES)


def _dot(a, b):
    return jnp.dot(a, b, preferred_element_type=F32)


def _dot_nt(a, b):
    return lax.dot_general(a, b, (((1,), (1,)), ((), ())), preferred_element_type=F32)


def _split3(x):
    hi = x.astype(BF16)
    r = x - hi.astype(F32)
    mid = r.astype(BF16)
    lo = (r - mid.astype(F32)).astype(BF16)
    return hi, mid, lo


def _sel_dot_l(m01, x):
    hi, mid, lo = _split3(x)
    return (_dot(m01, hi) + _dot(m01, mid)) + _dot(m01, lo)


def _sel_dot_r(x, m01):
    hi, mid, lo = _split3(x)
    return (_dot(hi, m01) + _dot(mid, m01)) + _dot(lo, m01)


def _rms(x, g):
    return x * lax.rsqrt(jnp.mean(x * x, axis=-1, keepdims=True) + EPS) * g


def _logsig(x):
    return jnp.minimum(x, 0.0) - jnp.log1p(jnp.exp(-jnp.abs(x)))


def _full_spec(shape):
    nd = len(shape)
    return pl.BlockSpec(shape, lambda *_: (0,) * nd)


_C_FQ, _C_FK, _C_FV = 0, FOX_W, 2 * FOX_W
_C_GQ = 3 * FOX_W
_C_GK = _C_GQ + GLA_KW
_C_GV = _C_GK + GLA_KW
_C_GR = _C_GV + GLA_VW
_C_SM = _C_GR + GLA_VW
_C_END = _C_SM + LANES


def _inproj_kernel(x_ref, g_ref, w_ref, bsm_ref, wg_ref, bg_ref,
                   fq_ref, fk_ref, fv_ref, fkb_ref, fvb_ref, lf_ref,
                   gq_ref, gk_ref, gv_ref, la_ref, gr_ref):
    h = _rms(x_ref[...], g_ref[...]).astype(BF16)

    def seg(lo, hi):
        return _dot(h, w_ref[:, lo:hi])

    fq_ref[...] = (seg(_C_FQ, _C_FK) * (DH_FOX ** -0.5)).astype(BF16)
    fk = seg(_C_FK, _C_FV)
    fk_ref[...] = fk
    fkb_ref[...] = fk.astype(BF16)
    fv = seg(_C_FV, _C_GQ)
    fv_ref[...] = fv
    fvb_ref[...] = fv.astype(BF16)
    gq_ref[...] = seg(_C_GQ, _C_GK) * (DK_GLA ** -0.5)
    gk_ref[...] = seg(_C_GK, _C_GV)
    gv_ref[...] = seg(_C_GV, _C_GR)
    gr_ref[...] = seg(_C_GR, _C_SM)
    ysm = seg(_C_SM, _C_END)
    lf_ref[...] = _logsig(ysm + bsm_ref[...])[:, 0:H_FOX]
    z = _dot(ysm.astype(BF16), wg_ref[...]) + bg_ref[...]
    la_ref[...] = _logsig(z) / GLA_TAU


def _inproj(x2d, g_mix, w_perm, b_small, wg_pad, b_gate):
    t = x2d.shape[0]
    tt = min(INPROJ_TILE, t)
    row = lambda w: pl.BlockSpec((tt, w), lambda i: (i, 0))
    outs = [(FOX_W, BF16), (FOX_W, F32), (FOX_W, F32), (FOX_W, BF16), (FOX_W, BF16), (H_FOX, F32),
            (GLA_KW, F32), (GLA_KW, F32), (GLA_VW, F32), (GLA_KW, F32), (GLA_VW, F32)]
    return pl.pallas_call(
        _inproj_kernel,
        grid=(t // tt,),
        in_specs=[row(D_MODEL), _full_spec(g_mix.shape), _full_spec(w_perm.shape), _full_spec(b_small.shape),
                  _full_spec(wg_pad.shape), _full_spec(b_gate.shape)],
        out_specs=[row(w) for w, _ in outs],
        out_shape=[jax.ShapeDtypeStruct((t, w), dt) for w, dt in outs],
        compiler_params=_cparams("parallel"),
        name="inproj",
    )(x2d, g_mix, w_perm, b_small, wg_pad, b_gate)


def _cumsum_kernel(lf_ref, tri_ref, c_ref, carry_sc):
    @pl.when(pl.program_id(1) == 0)
    def _():
        carry_sc[...] = jnp.zeros_like(carry_sc)

    c = _sel_dot_r(lf_ref[0], tri_ref[...]) + carry_sc[...][:, 0:1]
    c_ref[0] = c
    carry_sc[...] = jnp.broadcast_to(c[:, -1:], carry_sc.shape)


def _cumsum_lanes(lf_t):
    b, h, l = lf_t.shape
    blk = min(512, l)
    tri = jnp.asarray(np.triu(np.ones((blk, blk), np.float32)), BF16)
    return pl.pallas_call(
        _cumsum_kernel,
        grid=(b, l // blk),
        in_specs=[pl.BlockSpec((1, h, blk), lambda i, j: (i, 0, j)), _full_spec(tri.shape)],
        out_specs=pl.BlockSpec((1, h, blk), lambda i, j: (i, 0, j)),
        out_shape=jax.ShapeDtypeStruct((b, h, l), F32),
        scratch_shapes=[pltpu.VMEM((h, LANES), F32)],
        compiler_params=_cparams("parallel", "arbitrary"),
        name="fox_cumsum",
    )(lf_t, tri)


def _fox_prompt_kernel(qi_tab, ki_tab, q_ref, k_ref, v_ref, cq_ref, ck_ref, o_ref,
                       qs_sc, m_sc, l_sc, acc_sc):
    t = pl.program_id(2)
    qi = qi_tab[t]
    ki = ki_tab[t]
    tq = q_ref.shape[0]
    tk = k_ref.shape[0]
    lane = lax.broadcasted_iota(jnp.int32, (tq, LANES), 1)

    @pl.when(ki == 0)
    def _():
        q = q_ref[...]
        zero = jnp.zeros_like(q)
        qs_sc[0:tq] = jnp.where(lane < DH_FOX, q, zero)
        qs_sc[tq:2 * tq] = jnp.where(lane >= DH_FOX, q, zero)
        m_sc[...] = jnp.full_like(m_sc, NEG_INF)
        l_sc[...] = jnp.zeros_like(l_sc)
        acc_sc[...] = jnp.zeros_like(acc_sc)

    def step(masked):
        k = k_ref[...]
        v = v_ref[...]
        for hh in range(2):
            rows = slice(hh * tq, (hh + 1) * tq)
            s = _dot_nt(qs_sc[rows], k)
            s = s + cq_ref[0, hh] - ck_ref[0, hh]
            if masked:
                r = lax.broadcasted_iota(jnp.int32, (tq, tk), 0)
                c = lax.broadcasted_iota(jnp.int32, (tq, tk), 1)
                s = jnp.where(c <= r, s, NEG_INF)
            m_prev = m_sc[rows]
            m_new = jnp.maximum(m_prev, jnp.max(s, axis=-1, keepdims=True))
            alpha = jnp.exp(m_prev - m_new)
            p = jnp.exp(s - m_new)
            l_sc[rows] = alpha * l_sc[rows] + jnp.sum(p, axis=-1, keepdims=True)
            acc_sc[rows] = alpha * acc_sc[rows] + _dot(p.astype(BF16), v)
            m_sc[rows] = m_new

    @pl.when(ki < qi)
    def _():
        step(False)

    @pl.when(ki == qi)
    def _():
        step(True)
        o0 = acc_sc[0:tq] / l_sc[0:tq]
        o1 = acc_sc[tq:2 * tq] / l_sc[tq:2 * tq]
        o_ref[...] = jnp.where(lane < DH_FOX, o0, o1).astype(o_ref.dtype)


def _fox_prompt(fq_bf, fk_bf, fv_bf, cq, ck, batch, seq):
    tq = min(FOX_TILE, seq)
    nq = seq // tq
    pairs = [(i, j) for i in range(nq) for j in range(i + 1)]
    qi_tab = jnp.asarray([p[0] for p in pairs], jnp.int32)
    ki_tab = jnp.asarray([p[1] for p in pairs], jnp.int32)
    hp = FOX_W // LANES
    grid_spec = pltpu.PrefetchScalarGridSpec(
        num_scalar_prefetch=2,
        grid=(batch, hp, len(pairs)),
        in_specs=[
            pl.BlockSpec((tq, LANES), lambda b, h, t, qt, kt: (b * nq + qt[t], h)),
            pl.BlockSpec((tq, LANES), lambda b, h, t, qt, kt: (b * nq + kt[t], h)),
            pl.BlockSpec((tq, LANES), lambda b, h, t, qt, kt: (b * nq + kt[t], h)),
            pl.BlockSpec((1, 2, tq, 1), lambda b, h, t, qt, kt: (b, h, qt[t], 0)),
            pl.BlockSpec((1, 2, 1, tq), lambda b, h, t, qt, kt: (b, h, 0, kt[t])),
        ],
        out_specs=pl.BlockSpec((tq, LANES), lambda b, h, t, qt, kt: (b * nq + qt[t], h)),
        scratch_shapes=[pltpu.VMEM((2 * tq, LANES), BF16), pltpu.VMEM((2 * tq, 1), F32),
                        pltpu.VMEM((2 * tq, 1), F32), pltpu.VMEM((2 * tq, LANES), F32)],
    )
    return pl.pallas_call(
        _fox_prompt_kernel,
        grid_spec=grid_spec,
        out_shape=jax.ShapeDtypeStruct((batch * seq, FOX_W), BF16),
        compiler_params=_cparams("parallel", "parallel", "arbitrary"),
        name="fox_prompt",
    )(qi_tab, ki_tab, fq_bf, fk_bf, fv_bf, cq, ck)


def _fox_sample_kernel(pt_ref, q_ref, kc_ref, vc_ref, lfc_ref, kn_ref, vn_ref, lfn_ref, tri_ref, tri8_ref,
                       o_ref, qbd_sc, m_sc, l_sc, acc_sc, carry_sc):
    p = pl.program_id(1)
    n_pages = pl.num_programs(1)
    ls = q_ref.shape[1]
    nr = H_FOX * ls
    row_head = lax.broadcasted_iota(jnp.int32, (nr, FOX_W), 0) // ls
    lane_head = lax.broadcasted_iota(jnp.int32, (nr, FOX_W), 1) // DH_FOX

    def rows_of_heads(x):
        return jnp.broadcast_to(x[:, None, :], (H_FOX, ls, x.shape[-1])).reshape(nr, x.shape[-1])

    cs_t = _sel_dot_r(lfn_ref[0], tri8_ref[...])
    cn_row = rows_of_heads(cs_t)
    tnew = lax.broadcasted_iota(jnp.int32, (nr, ls), 1)
    tq = lax.broadcasted_iota(jnp.int32, (nr, ls), 0) % ls
    cn_col = jnp.sum(jnp.where(tnew == tq, cn_row, 0.0), axis=-1, keepdims=True)

    @pl.when(p == 0)
    def _():
        q = q_ref[0]
        qt = jnp.concatenate([q] * H_FOX, axis=0)
        qbd_sc[...] = jnp.where(row_head == lane_head, qt, jnp.zeros_like(qt))
        m_sc[...] = jnp.full_like(m_sc, NEG_INF)
        l_sc[...] = jnp.zeros_like(l_sc)
        acc_sc[...] = jnp.zeros_like(acc_sc)
        carry_sc[...] = jnp.zeros_like(carry_sc)

    def online(s, v_bf):
        m_prev = m_sc[...]
        m_new = jnp.maximum(m_prev, jnp.max(s, axis=-1, keepdims=True))
        alpha = jnp.exp(m_prev - m_new)
        pr = jnp.exp(s - m_new)
        l_sc[...] = alpha * l_sc[...] + jnp.sum(pr, axis=-1, keepdims=True)
        acc_sc[...] = alpha * acc_sc[...] + _dot(pr.astype(BF16), v_bf)
        m_sc[...] = m_new

    c_page = _sel_dot_r(lfc_ref[0], tri_ref[...]) + carry_sc[...][:, 0:1]
    carry_sc[...] = jnp.broadcast_to(c_page[:, -1:], carry_sc.shape)
    s = _dot_nt(qbd_sc[...], kc_ref[0].astype(BF16))
    online(s + cn_col - rows_of_heads(c_page), vc_ref[0].astype(BF16))

    @pl.when(p == n_pages - 1)
    def _():
        c_tot = rows_of_heads(carry_sc[...])[:, 0:1]
        s_new = _dot_nt(qbd_sc[...], kn_ref[0].astype(BF16))
        s_new = s_new + cn_col - cn_row - c_tot
        s_new = jnp.where(tnew <= tq, s_new, NEG_INF)
        online(s_new, vn_ref[0].astype(BF16))
        o = acc_sc[...] / l_sc[...]
        o = jnp.where(row_head == lane_head, o, 0.0)
        out = o[0:ls]
        for h in range(1, H_FOX):
            out = out + o[h * ls:(h + 1) * ls]
        o_ref[0] = out.astype(o_ref.dtype)


def _fox_sample(fq_bf, fk, fv, lfn_t, kcache, vcache, lfcache_t, page_table):
    bs, n_pages = page_table.shape
    page = kcache.shape[1]
    ls = fq_bf.shape[1]
    nr = H_FOX * ls
    tri = jnp.asarray(np.triu(np.ones((page, page), np.float32)), BF16)
    tri8 = jnp.asarray(np.triu(np.ones((ls, ls), np.float32)), BF16)
    grid_spec = pltpu.PrefetchScalarGridSpec(
        num_scalar_prefetch=1,
        grid=(bs, n_pages),
        in_specs=[
            pl.BlockSpec((1, ls, FOX_W), lambda b, p, pt: (b, 0, 0)),
            pl.BlockSpec((1, page, FOX_W), lambda b, p, pt: (pt[b * n_pages + p], 0, 0)),
            pl.BlockSpec((1, page, FOX_W), lambda b, p, pt: (pt[b * n_pages + p], 0, 0)),
            pl.BlockSpec((1, H_FOX, page), lambda b, p, pt: (pt[b * n_pages + p], 0, 0)),
            pl.BlockSpec((1, ls, FOX_W), lambda b, p, pt: (b, 0, 0)),
            pl.BlockSpec((1, ls, FOX_W), lambda b, p, pt: (b, 0, 0)),
            pl.BlockSpec((1, H_FOX, ls), lambda b, p, pt: (b, 0, 0)),
            pl.BlockSpec(tri.shape, lambda b, p, pt: (0, 0)),
            pl.BlockSpec(tri8.shape, lambda b, p, pt: (0, 0)),
        ],
        out_specs=pl.BlockSpec((1, ls, FOX_W), lambda b, p, pt: (b, 0, 0)),
        scratch_shapes=[pltpu.VMEM((nr, FOX_W), BF16), pltpu.VMEM((nr, 1), F32), pltpu.VMEM((nr, 1), F32),
                        pltpu.VMEM((nr, FOX_W), F32), pltpu.VMEM((H_FOX, LANES), F32)],
    )
    return pl.pallas_call(
        _fox_sample_kernel,
        grid_spec=grid_spec,
        out_shape=jax.ShapeDtypeStruct((bs, ls, FOX_W), BF16),
        compiler_params=_cparams("parallel", "arbitrary"),
        name="fox_sample",
    )(page_table.reshape(-1), fq_bf, kcache, vcache, lfcache_t, fk, fv, lfn_t, tri, tri8)


def _gla_prompt_kernel(q_ref, k_ref, v_ref, la_ref, tril_ref, ones_ref, onesT_ref,
                       o_ref, sfin_ref, s_sc):
    g = pl.program_id(1)
    n_tok = q_ref.shape[0]
    n_chunks = n_tok // GLA_CHUNK

    @pl.when(g == 0)
    def _():
        s_sc[...] = jnp.zeros_like(s_sc)

    la = la_ref[...]
    b = _sel_dot_l(tril_ref[...], la)
    bl = _sel_dot_l(ones_ref[...], la)
    q_dec = q_ref[...] * jnp.exp(b)
    q_dec_bf = q_dec.astype(BF16)
    k = k_ref[...]
    kd_bf = (k * jnp.exp(-b)).astype(BF16)
    kl_t = (k * jnp.exp(bl - b)).T
    v_bf = v_ref[...].astype(BF16)
    bl_col = _sel_dot_r(la.T, onesT_ref[...])

    r = lax.broadcasted_iota(jnp.int32, (n_tok, n_tok), 0)
    c = lax.broadcasted_iota(jnp.int32, (n_tok, n_tok), 1)
    causal = (r // GLA_CHUNK == c // GLA_CHUNK) & (c <= r)
    lane_q = lax.broadcasted_iota(jnp.int32, (n_tok, GLA_KW), 1) // DK_GLA
    zero_q = jnp.zeros_like(q_dec_bf)
    for h in range(H_GLA):
        att = _dot_nt(jnp.where(lane_q == h, q_dec_bf, zero_q), kd_bf)
        att = jnp.where(causal, att, 0.0)
        o_ref[:, h * DV_GLA:(h + 1) * DV_GLA] = _dot(att.astype(BF16), v_bf[:, h * DV_GLA:(h + 1) * DV_GLA])

    row_head = lax.broadcasted_iota(jnp.int32, (GLA_KW, GLA_VW), 0) // DK_GLA
    lane_head = lax.broadcasted_iota(jnp.int32, (GLA_KW, GLA_VW), 1) // DV_GLA
    same_head = row_head == lane_head
    tok_chunk = lax.broadcasted_iota(jnp.int32, (GLA_KW, n_tok), 1) // GLA_CHUNK
    for n in range(n_chunks):
        rows = slice(n * GLA_CHUNK, (n + 1) * GLA_CHUNK)
        s_prev = s_sc[...]
        o_ref[rows, :] += _dot(q_dec_bf[rows], s_prev.astype(BF16))
        kv = _dot(jnp.where(tok_chunk == n, kl_t, 0.0).astype(BF16), v_bf)
        dec = jnp.exp(bl_col[:, n * LANES:(n + 1) * LANES])
        dec = jnp.concatenate([dec] * (GLA_VW // LANES), axis=1)
        s_sc[...] = jnp.where(same_head, dec * s_prev + kv, 0.0)

    @pl.when(g == pl.num_programs(1) - 1)
    def _():
        s = s_sc[...]
        for h in range(H_GLA):
            sfin_ref[0, h] = s[h * DK_GLA:(h + 1) * DK_GLA, h * DV_GLA:(h + 1) * DV_GLA]


def _gla_prompt(gq, gk, gv, la, batch, seq):
    grp = min(GLA_GROUP, seq)
    ng = seq // grp
    nch = grp // GLA_CHUNK
    idx = np.arange(grp)
    same = (idx[:, None] // GLA_CHUNK) == (idx[None, :] // GLA_CHUNK)
    tril = jnp.asarray((same & (idx[None, :] <= idx[:, None])).astype(np.float32), BF16)
    ones = jnp.asarray(same.astype(np.float32), BF16)
    ones_t = jnp.asarray((idx[:, None] // GLA_CHUNK == np.arange(nch * LANES)[None, :] // LANES).astype(np.float32),
                         BF16)
    row = lambda w: pl.BlockSpec((grp, w), lambda b, g: (b * ng + g, 0))
    return pl.pallas_call(
        _gla_prompt_kernel,
        grid=(batch, ng),
        in_specs=[row(GLA_KW), row(GLA_KW), row(GLA_VW), row(GLA_KW),
                  _full_spec(tril.shape), _full_spec(ones.shape), _full_spec(ones_t.shape)],
        out_specs=[row(GLA_VW), pl.BlockSpec((1, H_GLA, DK_GLA, DV_GLA), lambda b, g: (b, 0, 0, 0))],
        out_shape=[jax.ShapeDtypeStruct((batch * seq, GLA_VW), F32),
                   jax.ShapeDtypeStruct((batch, H_GLA, DK_GLA, DV_GLA), F32)],
        scratch_shapes=[pltpu.VMEM((GLA_KW, GLA_VW), F32)],
        compiler_params=_cparams("parallel", "arbitrary"),
        name="gla_prompt",
    )(gq, gk, gv, la, tril, ones, ones_t)


def _gla_sample_kernel(q_ref, k_ref, v_ref, la_ref, s0_ref, tril_ref, o_ref, s1_ref):
    n, ls, _ = q_ref.shape
    la = la_ref[...]
    tril = jnp.broadcast_to(tril_ref[...][None], (n, ls, ls))
    hi, mid, lo = _split3(la)
    bdot = lambda x: jnp.einsum('nts,nsd->ntd', tril, x, preferred_element_type=F32)
    b = (bdot(hi) + bdot(mid)) + bdot(lo)
    b_last = b[:, ls - 1:ls, :]
    q_dec = (q_ref[...] * jnp.exp(b)).astype(BF16)
    k = k_ref[...]
    kd = (k * jnp.exp(-b)).astype(BF16)
    kl = (k * jnp.exp(b_last - b)).astype(BF16)
    v = v_ref[...].astype(BF16)
    s0 = s0_ref[...]
    att = jnp.einsum('ncd,nsd->ncs', q_dec, kd, preferred_element_type=F32)
    r = lax.broadcasted_iota(jnp.int32, (n, ls, ls), 1)
    c = lax.broadcasted_iota(jnp.int32, (n, ls, ls), 2)
    att = jnp.where(c <= r, att, 0.0)
    o = jnp.einsum('ncs,nse->nce', att.astype(BF16), v, preferred_element_type=F32)
    o = o + jnp.einsum('ncd,nde->nce', q_dec, s0.astype(BF16), preferred_element_type=F32)
    o_ref[...] = o
    kv = jnp.einsum('nds,nse->nde', jnp.swapaxes(kl, 1, 2), v, preferred_element_type=F32)
    dec = jnp.swapaxes(jnp.exp(b_last), 1, 2)
    s1_ref[...] = dec * s0 + kv


def _gla_sample(gq, gk, gv, la, state):
    n, ls, _ = gq.shape
    nb = min(32, n)
    tril = jnp.asarray(np.tril(np.ones((ls, ls), np.float32)), BF16)
    blk = lambda a, c: pl.BlockSpec((nb, a, c), lambda i: (i, 0, 0))
    return pl.pallas_call(
        _gla_sample_kernel,
        grid=(n // nb,),
        in_specs=[blk(ls, DK_GLA), blk(ls, DK_GLA), blk(ls, DV_GLA), blk(ls, DK_GLA), blk(DK_GLA, DV_GLA),
                  _full_spec(tril.shape)],
        out_specs=[blk(ls, DV_GLA), blk(DK_GLA, DV_GLA)],
        out_shape=[jax.ShapeDtypeStruct((n, ls, DV_GLA), F32), jax.ShapeDtypeStruct((n, DK_GLA, DV_GLA), F32)],
        compiler_params=_cparams("parallel"),
        name="gla_sample",
    )(gq, gk, gv, la, state, tril)


def _mixproj_kernel(x_ref, fox_ref, gla_ref, gr_ref, gg_ref, wo_ref, gc_ref, wq_ref, x1_ref, q_ref):
    gla = gla_ref[...]
    gr = gr_ref[...]
    acc = _dot(fox_ref[...], wo_ref[0:FOX_W, :])
    for h in range(H_GLA):
        cols = slice(h * DV_GLA, (h + 1) * DV_GLA)
        gh = _rms(gla[:, cols], gg_ref[:, cols])
        grh = gr[:, cols]
        gh = gh * (grh * jax.nn.sigmoid(grh))
        acc = acc + _dot(gh.astype(BF16), wo_ref[FOX_W + h * DV_GLA:FOX_W + (h + 1) * DV_GLA, :])
    x1 = x_ref[...] + acc
    x1_ref[...] = x1
    h2 = _rms(x1, gc_ref[...]).astype(BF16)
    q_ref[...] = (_dot(h2, wq_ref[...]) * (DH_MEM ** -0.5)).astype(BF16)


def _mixproj(x2d, fox_o, gla_o, gr, g_gla, w_out_bf, g_cross, w_q_bf):
    t = x2d.shape[0]
    tt = min(TOK_TILE, t)
    row = lambda w: pl.BlockSpec((tt, w), lambda i: (i, 0))
    return pl.pallas_call(
        _mixproj_kernel,
        grid=(t // tt,),
        in_specs=[row(D_MODEL), row(FOX_W), row(GLA_VW), row(GLA_VW), _full_spec(g_gla.shape),
                  _full_spec(w_out_bf.shape), _full_spec(g_cross.shape), _full_spec(w_q_bf.shape)],
        out_specs=[row(D_MODEL), row(D_MODEL)],
        out_shape=[jax.ShapeDtypeStruct((t, D_MODEL), F32), jax.ShapeDtypeStruct((t, D_MODEL), BF16)],
        compiler_params=_cparams("parallel"),
        name="mixproj",
    )(x2d, fox_o, gla_o, gr, g_gla, w_out_bf, g_cross, w_q_bf)


def _memkv_kernel(m_ref, g_ref, wk_ref, wv_ref, k_ref, v_ref):
    m = _rms(m_ref[...], g_ref[...]).astype(BF16)
    k_ref[...] = _dot(m, wk_ref[...])
    v_ref[...] = _dot(m, wv_ref[...])


def _memkv(mem2d, g_mem, wk_bf, wv_bf):
    t = mem2d.shape[0]
    tt = min(TOK_TILE, t)
    row = pl.BlockSpec((tt, D_MODEL), lambda i: (i, 0))
    return pl.pallas_call(
        _memkv_kernel,
        grid=(t // tt,),
        in_specs=[row, _full_spec(g_mem.shape), _full_spec(wk_bf.shape), _full_spec(wv_bf.shape)],
        out_specs=[row, row],
        out_shape=[jax.ShapeDtypeStruct((t, D_MODEL), F32)] * 2,
        compiler_params=_cparams("parallel"),
        name="memkv",
    )(mem2d, g_mem, wk_bf, wv_bf)


def _cross_kernel(q_ref, mk_ref, mv_ref, o_ref):
    q = q_ref[0]
    mk = mk_ref[0].astype(BF16)
    mv = mv_ref[0].astype(BF16)
    for h in range(H_MEM):
        cols = slice(h * DH_MEM, (h + 1) * DH_MEM)
        s = _dot_nt(q[:, cols], mk[:, cols])
        e = jnp.exp(s - jnp.max(s, axis=-1, keepdims=True))
        p = e / jnp.sum(e, axis=-1, keepdims=True)
        o_ref[0, :, cols] = _dot(p.astype(BF16), mv[:, cols]).astype(o_ref.dtype)


def _cross(q3, mk3, mv3):
    b, lq, _ = q3.shape
    m = mk3.shape[1]
    tl = min(TOK_TILE, lq)
    return pl.pallas_call(
        _cross_kernel,
        grid=(b, lq // tl),
        in_specs=[pl.BlockSpec((1, tl, D_MODEL), lambda i, j: (i, j, 0)),
                  pl.BlockSpec((1, m, D_MODEL), lambda i, j: (i, 0, 0)),
                  pl.BlockSpec((1, m, D_MODEL), lambda i, j: (i, 0, 0))],
        out_specs=pl.BlockSpec((1, tl, D_MODEL), lambda i, j: (i, j, 0)),
        out_shape=jax.ShapeDtypeStruct((b, lq, D_MODEL), BF16),
        compiler_params=_cparams("parallel", "parallel"),
        name="cross_attn",
    )(q3, mk3, mv3)


def _peer_prep_kernel(u_ref, v_ref, ub_ref, vt_ref):
    ub_ref[...] = u_ref[...].astype(BF16)
    vt_ref[...] = v_ref[...].T.astype(BF16)


def _peer_prep(u, v):
    ne = u.shape[0]
    eb = min(PREP_EBLK, ne)
    return pl.pallas_call(
        _peer_prep_kernel,
        grid=(ne // eb,),
        in_specs=[pl.BlockSpec((eb, D_MODEL), lambda i: (i, 0))] * 2,
        out_specs=[pl.BlockSpec((eb, D_MODEL), lambda i: (i, 0)), pl.BlockSpec((D_MODEL, eb), lambda i: (0, i))],
        out_shape=[jax.ShapeDtypeStruct((ne, D_MODEL), BF16), jax.ShapeDtypeStruct((D_MODEL, ne), BF16)],
        compiler_params=_cparams("parallel"),
        name="peer_prep",
    )(u, v)


def _top16_rows(s):
    nrow = s.shape[0]
    rowid = lax.broadcasted_iota(jnp.int32, s.shape, 0)
    rank = jnp.full(s.shape, float(PEER_TOPK), F32)
    work = s
    vals = []
    for r in range(PEER_TOPK):
        mx = jnp.max(work, axis=0, keepdims=True)
        idx = jnp.min(jnp.where(work == mx, rowid, nrow), axis=0, keepdims=True)
        sel = rowid == idx
        rank = jnp.where(sel, float(r), rank)
        work = jnp.where(sel, -jnp.inf, work)
        vals.append(mx)
    return rank, jnp.concatenate(vals, axis=0)


def _peer_route_kernel(x_ref, x1_ref, wo_ref, g_ref, wq_ref, sk_ref, ca_ref, cb_ref, cg_ref,
                       x2_ref, hn_ref, c1_ref, nrow_ref, p2_ref, rk2_ref, q_sc, s_sc):
    tt = x_ref.shape[0]
    nch = tt // LANES
    x2 = x1_ref[...] + _dot(x_ref[...], wo_ref[...])
    x2_ref[...] = x2
    hn = _rms(x2, g_ref[...]).astype(BF16)
    hn_ref[...] = hn
    q_sc[...] = _dot(hn, wq_ref[...]).astype(BF16)
    for hj in range(2 * H_PEER):
        st = _dot_nt(sk_ref[hj % 2], q_sc[:, hj * N_KEYS:(hj + 1) * N_KEYS])
        for ch in range(nch):
            s_sc[hj, ch] = st[:, ch * LANES:(ch + 1) * LANES]

    cand_row = lax.broadcasted_iota(jnp.int32, (N_CAND_PAD, LANES), 0)

    def unit(i, carry):
        h = i // nch
        ch = i % nch
        s1 = s_sc[2 * h, ch]
        s2 = s_sc[2 * h + 1, ch]
        rank1, v1 = _top16_rows(s1)
        rank2, v2 = _top16_rows(s2)
        cand = _sel_dot_l(ca_ref[...], v1) + _sel_dot_l(cb_ref[...], v2)
        cand = jnp.where(cand_row < N_CAND, cand, -jnp.inf)
        crank, _ = _top16_rows(cand)
        chosen = crank < float(PEER_TOPK)
        n_of_a = _dot(cg_ref[...], jnp.where(chosen, 1.0, 0.0).astype(BF16))
        z = jnp.sum(jnp.where(chosen, jnp.exp(cand - cand[0:1]), 0.0), axis=0, keepdims=True)
        nrow = jnp.zeros_like(rank1)
        for a in range(PEER_TOPK):
            nrow = jnp.where(rank1 == float(a), n_of_a[a:a + 1], nrow)
        c1_ref[h, ch] = jnp.exp(s1 - v1[0:1]) / z
        nrow_ref[h, ch] = nrow
        p2_ref[h, ch] = jnp.exp(s2 - v2[0:1])
        rk2_ref[h, ch] = rank2
        return carry

    lax.fori_loop(0, H_PEER * nch, unit, 0)


def _peer_route(oc2d, x1, w_o_bf, g_ffn, w_q_bf, sk_bf):
    t = x1.shape[0]
    tt = min(PEER_TOK, t)
    nch = tt // LANES
    pairs = [(a, b) for a in range(PEER_TOPK) for b in range(PEER_TOPK) if (a + 1) * (b + 1) <= PEER_TOPK]
    assert len(pairs) == N_CAND
    ca = np.zeros((N_CAND_PAD, PEER_TOPK), np.float32)
    cb = np.zeros((N_CAND_PAD, PEER_TOPK), np.float32)
    cg = np.zeros((PEER_TOPK, N_CAND_PAD), np.float32)
    for i, (a, b) in enumerate(pairs):
        ca[i, a] = 1.0
        cb[i, b] = 1.0
        cg[a, i] = 1.0
    ca, cb, cg = (jnp.asarray(m, BF16) for m in (ca, cb, cg))
    row = lambda w: pl.BlockSpec((tt, w), lambda i: (i, 0))
    sel = pl.BlockSpec((H_PEER, nch, N_KEYS, LANES), lambda i: (0, i, 0, 0))
    sel_shape = jax.ShapeDtypeStruct((H_PEER, t // LANES, N_KEYS, LANES), F32)
    return pl.pallas_call(
        _peer_route_kernel,
        grid=(t // tt,),
        in_specs=[row(D_MODEL), row(D_MODEL), _full_spec(w_o_bf.shape), _full_spec(g_ffn.shape),
                  _full_spec(w_q_bf.shape), _full_spec(sk_bf.shape),
                  _full_spec(ca.shape), _full_spec(cb.shape), _full_spec(cg.shape)],
        out_specs=[row(D_MODEL), row(D_MODEL), sel, sel, sel, sel],
        out_shape=[jax.ShapeDtypeStruct((t, D_MODEL), F32), jax.ShapeDtypeStruct((t, D_MODEL), BF16),
                   sel_shape, sel_shape, sel_shape, sel_shape],
        scratch_shapes=[pltpu.VMEM((tt, 2 * H_PEER * N_KEYS), BF16),
                        pltpu.VMEM((2 * H_PEER, nch, N_KEYS, LANES), F32)],
        compiler_params=_cparams("parallel"),
        name="peer_route",
    )(oc2d, x1, w_o_bf, g_ffn, w_q_bf, sk_bf, ca, cb, cg)


def _peer_mix_kernel(x2_ref, hn_ref, u_ref, vt_ref, c1_ref, nrow_ref, p2_ref, rk2_ref, gf_ref,
                     y_ref, a_sc, w_sc, acc_sc, *, final_norm):
    e = pl.program_id(1)
    tt = hn_ref.shape[0]
    nch = tt // LANES
    n_i1 = u_ref.shape[0] // N_KEYS

    @pl.when(e == 0)
    def _():
        acc_sc[...] = jnp.zeros_like(acc_sc)

    a_sc[...] = _dot_nt(u_ref[...], hn_ref[...])
    for ch in range(nch):
        cols = slice(ch * LANES, (ch + 1) * LANES)
        for r in range(n_i1):
            rows = slice(r * N_KEYS, (r + 1) * N_KEYS)
            w = jnp.zeros((N_KEYS, LANES), F32)
            for h in range(H_PEER):
                hit = rk2_ref[h, ch] < nrow_ref[h, ch, r:r + 1, :]
                w = w + jnp.where(hit, p2_ref[h, ch], 0.0) * c1_ref[h, ch, r:r + 1, :]
            w_sc[rows, cols] = (w * jax.nn.gelu(a_sc[rows, cols])).astype(BF16)
    acc_sc[...] += _dot(vt_ref[...], w_sc[...])

    @pl.when(e == pl.num_programs(1) - 1)
    def _():
        y = x2_ref[...] + acc_sc[...].T
        y_ref[...] = _rms(y, gf_ref[...]) if final_norm else y


def _peer_mix(x2, hn, u_bf, vt_bf, c1, nrow, p2, rk2, g_final, final_norm):
    t = x2.shape[0]
    tt = min(PEER_TOK, t)
    nch = tt // LANES
    ne = u_bf.shape[0]
    eb = min(PEER_EBLK, ne)
    n_i1 = eb // N_KEYS
    row = lambda w: pl.BlockSpec((tt, w), lambda i, e: (i, 0))
    part = pl.BlockSpec((H_PEER, nch, n_i1, LANES), lambda i, e: (0, i, e, 0))
    full = pl.BlockSpec((H_PEER, nch, N_KEYS, LANES), lambda i, e: (0, i, 0, 0))
    return pl.pallas_call(
        functools.partial(_peer_mix_kernel, final_norm=final_norm),
        grid=(t // tt, ne // eb),
        in_specs=[row(D_MODEL), row(D_MODEL),
                  pl.BlockSpec((eb, D_MODEL), lambda i, e: (e, 0)),
                  pl.BlockSpec((D_MODEL, eb), lambda i, e: (0, e)),
                  part, part, full, full, pl.BlockSpec(g_final.shape, lambda i, e: (0, 0))],
        out_specs=row(D_MODEL),
        out_shape=jax.ShapeDtypeStruct((t, D_MODEL), F32),
        scratch_shapes=[pltpu.VMEM((eb, tt), F32), pltpu.VMEM((eb, tt), BF16), pltpu.VMEM((D_MODEL, tt), F32)],
        compiler_params=_cparams("parallel", "arbitrary"),
        name="peer_mix",
    )(x2, hn, u_bf, vt_bf, c1, nrow, p2, rk2, g_final)


def _prep_in_weights(w_in, b_fox_f, w_gate2, b_gate):
    o = np.cumsum((0,) + IN_SIZES)
    seg = lambda i: w_in[:, int(o[i]):int(o[i + 1])]
    pad = jnp.zeros((D_MODEL, LANES - H_FOX - GLA_GATE_RANK), w_in.dtype)
    w_perm = jnp.concatenate([seg(0), seg(1), seg(2), seg(4), seg(5), seg(6), seg(8), seg(3), seg(7), pad],
                             axis=1).astype(BF16)
    b_small = jnp.zeros((1, LANES), F32).at[0, 0:H_FOX].set(b_fox_f)
    wg_pad = jnp.zeros((LANES, GLA_KW), F32).at[H_FOX:H_FOX + GLA_GATE_RANK].set(w_gate2).astype(BF16)
    return w_perm, b_small, wg_pad, b_gate.reshape(1, GLA_KW)


def kernel(x_prompt, x_sample, cache_fox_k, cache_fox_v, cache_fox_logf, state_gla, cache_mem_k, cache_mem_v, page_table, mem_prompt, g_mix, w_in, b_fox_f, w_gla_gate2, b_gla_gate, g_gla_out, w_out, g_cross, g_mem, w_mem_k, w_mem_v, w_cross_q, w_cross_o, g_ffn, peer_w_q, peer_subkeys, peer_u, peer_v, g_final):
    depth = w_in.shape[0]
    bp, lp, _ = x_prompt.shape
    bs, ls, _ = x_sample.shape
    n_mem = mem_prompt.shape[1]
    n_pool, page = cache_fox_k.shape[1:3]
    xp = x_prompt.reshape(bp * lp, D_MODEL)
    xs = x_sample.reshape(bs * ls, D_MODEL)
    vec = lambda a: a.reshape(1, -1)
    outs = [[] for _ in range(10)]
    for l in range(depth):
        w_perm, b_small, wg_pad, b_gate = _prep_in_weights(w_in[l], b_fox_f[l], w_gla_gate2[l], b_gla_gate[l])
        w_out_bf = w_out[l].astype(BF16)
        w_cq_bf = w_cross_q[l].astype(BF16)
        w_co_bf = w_cross_o[l].astype(BF16)
        w_pq_bf = peer_w_q[l].astype(BF16)
        sk_bf = peer_subkeys[l].astype(BF16)
        u_bf, vt_bf = _peer_prep(peer_u[l], peer_v[l])
        last = l == depth - 1

        def tail(x, fox_o, gla_o, gr, mk3, mv3, nb):
            x1, qc = _mixproj(x, fox_o, gla_o, gr, vec(g_gla_out[l]), w_out_bf, vec(g_cross[l]), w_cq_bf)
            oc = _cross(qc.reshape(nb, -1, D_MODEL), mk3, mv3).reshape(-1, D_MODEL)
            x2, hn, c1, nrow, p2, rk2 = _peer_route(oc, x1, w_co_bf, vec(g_ffn[l]), w_pq_bf, sk_bf)
            return _peer_mix(x2, hn, u_bf, vt_bf, c1, nrow, p2, rk2, vec(g_final), last)

        fq_bf, fk, fv, fk_bf, fv_bf, lf, gq, gk, gv, la, gr = _inproj(xp, vec(g_mix[l]), w_perm, b_small, wg_pad, b_gate)
        c = _cumsum_lanes(lf.reshape(bp, lp, H_FOX).transpose(0, 2, 1))
        fox_o = _fox_prompt(fq_bf, fk_bf, fv_bf, c.reshape(bp, H_FOX, lp, 1), c.reshape(bp, H_FOX, 1, lp), bp, lp)
        gla_o, s_fin = _gla_prompt(gq, gk, gv, la, bp, lp)
        mk, mv = _memkv(mem_prompt.reshape(bp * n_mem, D_MODEL), vec(g_mem[l]), w_mem_k[l].astype(BF16),
                        w_mem_v[l].astype(BF16))
        mk3 = mk.reshape(bp, n_mem, D_MODEL)
        mv3 = mv.reshape(bp, n_mem, D_MODEL)
        xp = tail(xp, fox_o, gla_o, gr, mk3, mv3, bp)
        outs[0].append(fk.reshape(bp, lp, H_FOX, DH_FOX))
        outs[1].append(fv.reshape(bp, lp, H_FOX, DH_FOX))
        outs[2].append(lf.reshape(bp, lp, H_FOX))
        outs[3].append(s_fin)
        outs[4].append(mk.reshape(bp, n_mem, H_MEM, DH_MEM))
        outs[5].append(mv.reshape(bp, n_mem, H_MEM, DH_MEM))

        fq_bf, fk, fv, _, _, lf, gq, gk, gv, la, gr = _inproj(xs, vec(g_mix[l]), w_perm, b_small, wg_pad, b_gate)
        seqs = lambda a: a.reshape(bs, ls, FOX_W)
        fox_o = _fox_sample(seqs(fq_bf), seqs(fk), seqs(fv), lf.reshape(bs, ls, H_FOX).transpose(0, 2, 1),
                            cache_fox_k[l].reshape(n_pool, page, FOX_W), cache_fox_v[l].reshape(n_pool, page, FOX_W),
                            cache_fox_logf[l].transpose(0, 2, 1), page_table).reshape(bs * ls, FOX_W)
        heads = lambda a, d: a.reshape(bs, ls, H_GLA, d).transpose(0, 2, 1, 3).reshape(bs * H_GLA, ls, d)
        gla_o, s_new = _gla_sample(heads(gq, DK_GLA), heads(gk, DK_GLA), heads(gv, DV_GLA), heads(la, DK_GLA),
                                   state_gla[l].reshape(bs * H_GLA, DK_GLA, DV_GLA))
        gla_o = gla_o.reshape(bs, H_GLA, ls, DV_GLA).transpose(0, 2, 1, 3).reshape(bs * ls, GLA_VW)
        xs = tail(xs, fox_o, gla_o, gr, cache_mem_k[l].reshape(bs, n_mem, D_MODEL),
                  cache_mem_v[l].reshape(bs, n_mem, D_MODEL), bs)
        outs[6].append(fk.reshape(bs, ls, H_FOX, DH_FOX))
        outs[7].append(fv.reshape(bs, ls, H_FOX, DH_FOX))
        outs[8].append(lf.reshape(bs, ls, H_FOX))
        outs[9].append(s_new.reshape(bs, H_GLA, DK_GLA, DV_GLA))
    stacked = [jnp.stack(o, axis=0) for o in outs]
    return (xp.reshape(bp, lp, D_MODEL), xs.reshape(bs, ls, D_MODEL), *stacked)
```

```python
import functools

import numpy as np
import jax
import jax.numpy as jnp
from jax import lax
from jax.experimental import pallas as pl
from jax.experimental.pallas import tpu as pltpu

F32 = jnp.float32
BF16 = jnp.bfloat16

D_MODEL = 1024
H_FOX, DH_FOX = 8, 64
H_GLA, DK_GLA, DV_GLA = 4, 64, 128
GLA_GATE_RANK = 16
GLA_TAU = 16.0
GLA_CHUNK = 64
H_MEM = 4
DH_MEM = D_MODEL // H_MEM
N_KEYS = 128
H_PEER = 8
PEER_TOPK = 16
EPS = 1e-6
NEG_INF = -1e30
FOX_W = H_FOX * DH_FOX
GLA_KW = H_GLA * DK_GLA
GLA_VW = H_GLA * DV_GLA
IN_SIZES = (FOX_W, FOX_W, FOX_W, H_FOX, GLA_KW, GLA_KW, GLA_VW, GLA_GATE_RANK, GLA_VW)

LANES = 128
VMEM_LIMIT_BYTES = 56 * 2**20

TOK_TILE = 512
INPROJ_TILE = 256
FOX_TILE = 512
GLA_GROUP = 512
PEER_TOK = 512
PEER_EBLK = 2048
FOX_PAGES_PER_STEP = 8
PREP_EBLK = 512
N_CAND = 50
N_CAND_PAD = 64


def _cparams(*sem):
    return pltpu.CompilerParams(dimension_semantics=sem, vmem_limit_bytes=VMEM_LIMIT_BYTES)


def _dot(a, b):
    return jnp.dot(a, b, preferred_element_type=F32)


def _dot_nt(a, b):
    return lax.dot_general(a, b, (((1,), (1,)), ((), ())), preferred_element_type=F32)


def _split3(x):
    hi = x.astype(BF16)
    r = x - hi.astype(F32)
    mid = r.astype(BF16)
    lo = (r - mid.astype(F32)).astype(BF16)
    return hi, mid, lo


def _sel_dot_l(m01, x):
    hi, mid, lo = _split3(x)
    return (_dot(m01, hi) + _dot(m01, mid)) + _dot(m01, lo)


def _sel_dot_r(x, m01):
    hi, mid, lo = _split3(x)
    return (_dot(hi, m01) + _dot(mid, m01)) + _dot(lo, m01)


def _rms(x, g):
    return x * lax.rsqrt(jnp.mean(x * x, axis=-1, keepdims=True) + EPS) * g


def _logsig(x):
    return jnp.minimum(x, 0.0) - jnp.log1p(jnp.exp(-jnp.abs(x)))


def _full_spec(shape):
    nd = len(shape)
    return pl.BlockSpec(shape, lambda *_: (0,) * nd)


_C_FQ, _C_FK, _C_FV = 0, FOX_W, 2 * FOX_W
_C_GQ = 3 * FOX_W
_C_GK = _C_GQ + GLA_KW
_C_GV = _C_GK + GLA_KW
_C_GR = _C_GV + GLA_VW
_C_SM = _C_GR + GLA_VW
_C_END = _C_SM + LANES


def _inproj_kernel(x_ref, g_ref, w_ref, bsm_ref, wg_ref, bg_ref,
                   fq_ref, fk_ref, fv_ref, lf_ref, gq_ref, gk_ref, gv_ref, la_ref, gr_ref):
    h = _rms(x_ref[...], g_ref[...]).astype(BF16)

    def seg(lo, hi):
        return _dot(h, w_ref[:, lo:hi])

    fq_ref[...] = (seg(_C_FQ, _C_FK) * (DH_FOX ** -0.5)).astype(BF16)
    fk_ref[...] = seg(_C_FK, _C_FV)
    fv_ref[...] = seg(_C_FV, _C_GQ)
    gq_ref[...] = seg(_C_GQ, _C_GK) * (DK_GLA ** -0.5)
    gk_ref[...] = seg(_C_GK, _C_GV)
    gv_ref[...] = seg(_C_GV, _C_GR)
    gr_ref[...] = seg(_C_GR, _C_SM)
    ysm = seg(_C_SM, _C_END)
    lf_ref[...] = _logsig(ysm + bsm_ref[...])[:, 0:H_FOX]
    z = _dot(ysm.astype(BF16), wg_ref[...]) + bg_ref[...]
    la_ref[...] = _logsig(z) / GLA_TAU


def _inproj(x2d, g_mix, w_perm, b_small, wg_pad, b_gate):
    t = x2d.shape[0]
    tt = min(INPROJ_TILE, t)
    row = lambda w: pl.BlockSpec((tt, w), lambda i: (i, 0))
    outs = [(FOX_W, BF16), (FOX_W, F32), (FOX_W, F32), (H_FOX, F32),
            (GLA_KW, F32), (GLA_KW, F32), (GLA_VW, F32), (GLA_KW, F32), (GLA_VW, F32)]
    return pl.pallas_call(
        _inproj_kernel,
        grid=(t // tt,),
        in_specs=[row(D_MODEL), _full_spec(g_mix.shape), _full_spec(w_perm.shape), _full_spec(b_small.shape),
                  _full_spec(wg_pad.shape), _full_spec(b_gate.shape)],
        out_specs=[row(w) for w, _ in outs],
        out_shape=[jax.ShapeDtypeStruct((t, w), dt) for w, dt in outs],
        compiler_params=_cparams("parallel"),
        name="inproj",
    )(x2d, g_mix, w_perm, b_small, wg_pad, b_gate)


_FOX_QW = 4 * LANES
_FOX_KW = 2 * LANES
_N_HP = FOX_W // LANES


def _fox_aug_tables():
    pq = np.zeros((3, H_FOX, _N_HP * 2 * LANES), np.float32)
    oq = np.zeros((1, _N_HP * 2 * LANES), np.float32)
    pk = np.zeros((3, H_FOX, _N_HP * LANES), np.float32)
    ok = np.zeros((1, _N_HP * LANES), np.float32)
    for hp in range(_N_HP):
        for half in range(2):
            head = 2 * hp + half
            qb = (hp * 2 + half) * LANES
            kb = hp * LANES
            for j in range(3):
                pq[j, head, qb + 6 * half + j] = 1.0
                oq[0, qb + 6 * half + 3 + j] = 1.0
                pk[j, head, kb + 6 * half + 3 + j] = 1.0
                ok[0, kb + 6 * half + j] = 1.0
    return jnp.asarray(pq, BF16), jnp.asarray(oq, F32), jnp.asarray(pk, BF16), jnp.asarray(ok, F32)


def _fox_prep_kernel(q_ref, k_ref, v_ref, lf_ref, tril_ref, pq_ref, oq_ref, pk_ref, ok_ref,
                     qc_ref, kc_ref, vt_ref, carry_sc):
    @pl.when(pl.program_id(1) == 0)
    def _():
        carry_sc[...] = jnp.zeros_like(carry_sc)

    c = _sel_dot_l(tril_ref[...], lf_ref[...]) + carry_sc[0:1, 0:H_FOX]
    carry_sc[0:1, 0:H_FOX] = c[-1:, :]
    parts = _split3(c)
    aug_q = oq_ref[...]
    aug_k = ok_ref[...]
    for j in range(3):
        aug_q = aug_q + _dot(parts[j], pq_ref[j])
        aug_k = aug_k - _dot(parts[j], pk_ref[j])
    aug_q = aug_q.astype(BF16)
    aug_k = aug_k.astype(BF16)
    lane = lax.broadcasted_iota(jnp.int32, (q_ref.shape[0], LANES), 1)
    for hp in range(_N_HP):
        q = q_ref[:, hp * LANES:(hp + 1) * LANES]
        zero = jnp.zeros_like(q)
        base = hp * _FOX_QW
        qc_ref[:, base:base + LANES] = jnp.where(lane < DH_FOX, q, zero)
        qc_ref[:, base + LANES:base + 2 * LANES] = aug_q[:, (2 * hp) * LANES:(2 * hp + 1) * LANES]
        qc_ref[:, base + 2 * LANES:base + 3 * LANES] = jnp.where(lane >= DH_FOX, q, zero)
        qc_ref[:, base + 3 * LANES:base + 4 * LANES] = aug_q[:, (2 * hp + 1) * LANES:(2 * hp + 2) * LANES]
        kc_ref[:, hp * _FOX_KW:hp * _FOX_KW + LANES] = k_ref[:, hp * LANES:(hp + 1) * LANES].astype(BF16)
        kc_ref[:, hp * _FOX_KW + LANES:(hp + 1) * _FOX_KW] = aug_k[:, hp * LANES:(hp + 1) * LANES]
    vt_ref[...] = v_ref[...].T.astype(BF16)


def _fox_prep(fq_bf, fk_bf, fv, lf, batch, seq):
    tt = min(FOX_TILE, seq)
    nt = seq // tt
    tril = jnp.asarray(np.tril(np.ones((tt, tt), np.float32)), BF16)
    pq, oq, pk, ok = _fox_aug_tables()
    row = lambda w: pl.BlockSpec((tt, w), lambda b, i: (b * nt + i, 0))
    consts = [tril, pq, oq, pk, ok]
    return pl.pallas_call(
        _fox_prep_kernel,
        grid=(batch, nt),
        in_specs=[row(FOX_W), row(FOX_W), row(FOX_W), row(H_FOX)] + [_full_spec(a.shape) for a in consts],
        out_specs=[row(_N_HP * _FOX_QW), row(_N_HP * _FOX_KW),
                   pl.BlockSpec((FOX_W, tt), lambda b, i: (0, b * nt + i))],
        out_shape=[jax.ShapeDtypeStruct((batch * seq, _N_HP * _FOX_QW), BF16),
                   jax.ShapeDtypeStruct((batch * seq, _N_HP * _FOX_KW), BF16),
                   jax.ShapeDtypeStruct((FOX_W, batch * seq), BF16)],
        scratch_shapes=[pltpu.VMEM((8, LANES), F32)],
        compiler_params=_cparams("parallel", "arbitrary"),
        name="fox_prep",
    )(fq_bf, fk_bf, fv, lf, *consts)


def _fox_prompt_kernel(qi_tab, ki_tab, q_ref, k_ref, vt_ref, o_ref,
                       m_sc, l_sc, acc_sc):
    t = pl.program_id(2)
    qi = qi_tab[t]
    ki = ki_tab[t]
    tq = q_ref.shape[0]
    tk = k_ref.shape[0]

    @pl.when(ki == 0)
    def _():
        m_sc[...] = jnp.full_like(m_sc, NEG_INF)
        l_sc[...] = jnp.zeros_like(l_sc)
        acc_sc[...] = jnp.zeros_like(acc_sc)

    def step(masked):
        kc = k_ref[...]
        vt = vt_ref[...]
        for hh in range(2):
            s = _dot_nt(kc, q_ref[:, hh * _FOX_KW:(hh + 1) * _FOX_KW])
            if masked:
                kpos = lax.broadcasted_iota(jnp.int32, (tk, tq), 0)
                qpos = lax.broadcasted_iota(jnp.int32, (tk, tq), 1)
                s = jnp.where(kpos <= qpos, s, NEG_INF)
            m_prev = m_sc[hh:hh + 1, :]
            m_new = jnp.maximum(m_prev, jnp.max(s, axis=0, keepdims=True))
            alpha = jnp.exp(m_prev - m_new)
            p = jnp.exp(s - m_new)
            l_sc[hh:hh + 1, :] = alpha * l_sc[hh:hh + 1, :] + jnp.sum(p, axis=0, keepdims=True)
            acc_sc[hh] = alpha * acc_sc[hh] + _dot(vt, p.astype(BF16))
            m_sc[hh:hh + 1, :] = m_new

    @pl.when(ki < qi)
    def _():
        step(False)

    @pl.when(ki == qi)
    def _():
        step(True)
        o0 = acc_sc[0] / l_sc[0:1, :]
        o1 = acc_sc[1] / l_sc[1:2, :]
        dim = lax.broadcasted_iota(jnp.int32, (LANES, tq), 0)
        o_ref[...] = jnp.where(dim < DH_FOX, o0, o1).T.astype(o_ref.dtype)


def _fox_prompt(qc, kc, vt, batch, seq):
    tq = min(FOX_TILE, seq)
    nq = seq // tq
    pairs = [(i, j) for i in range(nq) for j in range(i + 1)]
    qi_tab = jnp.asarray([p[0] for p in pairs], jnp.int32)
    ki_tab = jnp.asarray([p[1] for p in pairs], jnp.int32)
    grid_spec = pltpu.PrefetchScalarGridSpec(
        num_scalar_prefetch=2,
        grid=(batch, _N_HP, len(pairs)),
        in_specs=[
            pl.BlockSpec((tq, _FOX_QW), lambda b, h, t, qt, kt: (b * nq + qt[t], h)),
            pl.BlockSpec((tq, _FOX_KW), lambda b, h, t, qt, kt: (b * nq + kt[t], h)),
            pl.BlockSpec((LANES, tq), lambda b, h, t, qt, kt: (h, b * nq + kt[t])),
        ],
        out_specs=pl.BlockSpec((tq, LANES), lambda b, h, t, qt, kt: (b * nq + qt[t], h)),
        scratch_shapes=[pltpu.VMEM((2, tq), F32), pltpu.VMEM((2, tq), F32), pltpu.VMEM((2, LANES, tq), F32)],
    )
    return pl.pallas_call(
        _fox_prompt_kernel,
        grid_spec=grid_spec,
        out_shape=jax.ShapeDtypeStruct((batch * seq, FOX_W), BF16),
        compiler_params=_cparams("parallel", "parallel", "arbitrary"),
        name="fox_prompt",
    )(qi_tab, ki_tab, qc, kc, vt)


def _fox_sample_kernel(pt_ref, q_ref, *refs, n_pp):
    kc, vc, lfc = refs[0:n_pp], refs[n_pp:2 * n_pp], refs[2 * n_pp:3 * n_pp]
    kn_ref, vn_ref, lfn_ref, rep_ref, lstrict_ref, mnew_ref, o_ref, m_sc, l_sc, acc_sc, carry_sc = refs[3 * n_pp:]
    step = pl.program_id(1)
    nr = q_ref.shape[1]
    ls = nr // H_FOX
    npk = lfc[0].shape[2]

    @pl.when(step == 0)
    def _():
        m_sc[...] = jnp.full_like(m_sc, NEG_INF)
        l_sc[...] = jnp.zeros_like(l_sc)
        acc_sc[...] = jnp.zeros_like(acc_sc)
        carry_sc[...] = jnp.zeros_like(carry_sc)

    def online(s, v_bf):
        m_prev = m_sc[...]
        m_new = jnp.maximum(m_prev, jnp.max(s, axis=-1, keepdims=True))
        alpha = jnp.exp(m_prev - m_new)
        pr = jnp.exp(s - m_new)
        l_sc[...] = alpha * l_sc[...] + jnp.sum(pr, axis=-1, keepdims=True)
        acc_sc[...] = alpha * acc_sc[...] + _dot(pr.astype(BF16), v_bf)
        m_sc[...] = m_new

    cs_flat = _sel_dot_r(lfn_ref[0], mnew_ref[...])
    row = lax.broadcasted_iota(jnp.int32, (nr, nr), 0)
    col = lax.broadcasted_iota(jnp.int32, (nr, nr), 1)
    own = col == (row % ls) * H_FOX + row // ls
    cn_col = jnp.sum(jnp.where(own, jnp.broadcast_to(cs_flat, (nr, nr)), 0.0), axis=1, keepdims=True)

    c = jnp.concatenate([r[0] for r in lfc], axis=0)
    lane = lax.broadcasted_iota(jnp.int32, c.shape, 1)
    sh = H_FOX
    while sh < npk:
        c = c + jnp.where(lane >= sh, pltpu.roll(c, sh, 1), 0.0)
        sh *= 2
    tot = c[:, npk - H_FOX:]
    off = _sel_dot_l(lstrict_ref[...], tot) + carry_sc[0:1, 0:H_FOX]
    carry_new = off[n_pp - 1:n_pp] + tot[n_pp - 1:n_pp]
    carry_sc[0:1, 0:H_FOX] = carry_new
    c_past = c + _sel_dot_r(off, rep_ref[...])

    q = q_ref[0]
    key_head = lax.broadcasted_iota(jnp.int32, (nr, npk), 1) % H_FOX
    row_head = lax.broadcasted_iota(jnp.int32, (nr, npk), 0) // ls
    for j in range(n_pp):
        k2d = kc[j][0, 0].reshape(npk, DH_FOX).astype(BF16)
        v2d = vc[j][0, 0].reshape(npk, DH_FOX).astype(BF16)
        s = _dot_nt(q, k2d) + cn_col - c_past[j:j + 1, :]
        online(jnp.where(key_head == row_head, s, NEG_INF), v2d)

    @pl.when(step == pl.num_programs(1) - 1)
    def _():
        c_tot = _sel_dot_r(carry_new, rep_ref[:, 0:nr])
        s_new = _dot_nt(q, kn_ref[0].reshape(nr, DH_FOX).astype(BF16))
        s_new = s_new + cn_col - cs_flat - c_tot
        keep = jnp.where(col % H_FOX == row // ls, col // H_FOX, ls) <= row % ls
        online(jnp.where(keep, s_new, NEG_INF), vn_ref[0].reshape(nr, DH_FOX).astype(BF16))
        o_ref[0] = (acc_sc[...] / l_sc[...]).astype(o_ref.dtype)


def _fox_sample(q_hm, kn, vn, lfn_flat, kcache, vcache, lf_flat, page_table, layer):
    bs, n_pages = page_table.shape
    n_pp = min(FOX_PAGES_PER_STEP, n_pages)
    assert n_pages % n_pp == 0
    page = kcache.shape[2]
    npk = page * H_FOX
    nr = q_hm.shape[1]
    ls = nr // H_FOX
    tok = np.arange(npk) // H_FOX
    head = np.arange(npk) % H_FOX
    rep = jnp.asarray((np.arange(H_FOX)[:, None] == head[None, :]).astype(np.float32), BF16)
    lstrict = jnp.asarray(np.tril(np.ones((n_pp, n_pp), np.float32), -1), BF16)
    mnew = jnp.asarray(((head[:nr, None] == head[None, :nr]) & (tok[:nr, None] <= tok[None, :nr])).astype(np.float32),
                       BF16)
    page_spec = lambda j: pl.BlockSpec(
        (1, 1, page, H_FOX, DH_FOX), lambda b, s, pt: (layer, pt[b * n_pages + s * n_pp + j], 0, 0, 0))
    lf_spec = lambda j: pl.BlockSpec((1, 1, npk), lambda b, s, pt: (pt[b * n_pages + s * n_pp + j], 0, 0))
    per_seq = lambda *shape: pl.BlockSpec((1,) + shape, lambda b, s, pt: (b,) + (0,) * len(shape))
    const = lambda a: pl.BlockSpec(a.shape, lambda b, s, pt: (0,) * a.ndim)
    grid_spec = pltpu.PrefetchScalarGridSpec(
        num_scalar_prefetch=1,
        grid=(bs, n_pages // n_pp),
        in_specs=([per_seq(nr, DH_FOX)] + [page_spec(j) for j in range(n_pp)] * 2 + [lf_spec(j) for j in range(n_pp)]
                  + [per_seq(ls, H_FOX, DH_FOX), per_seq(ls, H_FOX, DH_FOX), per_seq(1, nr),
                     const(rep), const(lstrict), const(mnew)]),
        out_specs=per_seq(nr, DH_FOX),
        scratch_shapes=[pltpu.VMEM((nr, 1), F32), pltpu.VMEM((nr, 1), F32), pltpu.VMEM((nr, DH_FOX), F32),
                        pltpu.VMEM((8, LANES), F32)],
    )
    return pl.pallas_call(
        functools.partial(_fox_sample_kernel, n_pp=n_pp),
        grid_spec=grid_spec,
        out_shape=jax.ShapeDtypeStruct((bs, nr, DH_FOX), BF16),
        compiler_params=_cparams("parallel", "arbitrary"),
        name="fox_sample",
    )(page_table.reshape(-1), q_hm, *([kcache] * n_pp), *([vcache] * n_pp), *([lf_flat] * n_pp),
      kn, vn, lfn_flat, rep, lstrict, mnew)


def _gla_prompt_kernel(q_ref, k_ref, v_ref, la_ref, tril_ref, ones_ref, onesT_ref,
                       o_ref, sfin_ref, s_sc):
    g = pl.program_id(1)
    n_tok = q_ref.shape[0]
    n_chunks = n_tok // GLA_CHUNK

    @pl.when(g == 0)
    def _():
        s_sc[...] = jnp.zeros_like(s_sc)

    la = la_ref[...]
    b = _sel_dot_l(tril_ref[...], la)
    bl = _sel_dot_l(ones_ref[...], la)
    q_dec = q_ref[...] * jnp.exp(b)
    q_dec_bf = q_dec.astype(BF16)
    k = k_ref[...]
    kd_bf = (k * jnp.exp(-b)).astype(BF16)
    kl_t = (k * jnp.exp(bl - b)).T
    v_bf = v_ref[...].astype(BF16)
    bl_col = _sel_dot_r(la.T, onesT_ref[...])

    r = lax.broadcasted_iota(jnp.int32, (n_tok, n_tok), 0)
    c = lax.broadcasted_iota(jnp.int32, (n_tok, n_tok), 1)
    causal = (r // GLA_CHUNK == c // GLA_CHUNK) & (c <= r)
    lane_q = lax.broadcasted_iota(jnp.int32, (n_tok, GLA_KW), 1) // DK_GLA
    zero_q = jnp.zeros_like(q_dec_bf)
    for h in range(H_GLA):
        att = _dot_nt(jnp.where(lane_q == h, q_dec_bf, zero_q), kd_bf)
        att = jnp.where(causal, att, 0.0)
        o_ref[:, h * DV_GLA:(h + 1) * DV_GLA] = _dot(att.astype(BF16), v_bf[:, h * DV_GLA:(h + 1) * DV_GLA])

    row_head = lax.broadcasted_iota(jnp.int32, (GLA_KW, GLA_VW), 0) // DK_GLA
    lane_head = lax.broadcasted_iota(jnp.int32, (GLA_KW, GLA_VW), 1) // DV_GLA
    same_head = row_head == lane_head
    tok_chunk = lax.broadcasted_iota(jnp.int32, (GLA_KW, n_tok), 1) // GLA_CHUNK
    for n in range(n_chunks):
        rows = slice(n * GLA_CHUNK, (n + 1) * GLA_CHUNK)
        s_prev = s_sc[...]
        o_ref[rows, :] += _dot(q_dec_bf[rows], s_prev.astype(BF16))
        kv = _dot(jnp.where(tok_chunk == n, kl_t, 0.0).astype(BF16), v_bf)
        dec = jnp.exp(bl_col[:, n * LANES:(n + 1) * LANES])
        dec = jnp.concatenate([dec] * (GLA_VW // LANES), axis=1)
        s_sc[...] = jnp.where(same_head, dec * s_prev + kv, 0.0)

    @pl.when(g == pl.num_programs(1) - 1)
    def _():
        s = s_sc[...]
        for h in range(H_GLA):
            sfin_ref[0, h] = s[h * DK_GLA:(h + 1) * DK_GLA, h * DV_GLA:(h + 1) * DV_GLA]


def _gla_prompt(gq, gk, gv, la, batch, seq):
    grp = min(GLA_GROUP, seq)
    ng = seq // grp
    nch = grp // GLA_CHUNK
    idx = np.arange(grp)
    same = (idx[:, None] // GLA_CHUNK) == (idx[None, :] // GLA_CHUNK)
    tril = jnp.asarray((same & (idx[None, :] <= idx[:, None])).astype(np.float32), BF16)
    ones = jnp.asarray(same.astype(np.float32), BF16)
    ones_t = jnp.asarray((idx[:, None] // GLA_CHUNK == np.arange(nch * LANES)[None, :] // LANES).astype(np.float32),
                         BF16)
    row = lambda w: pl.BlockSpec((grp, w), lambda b, g: (b * ng + g, 0))
    return pl.pallas_call(
        _gla_prompt_kernel,
        grid=(batch, ng),
        in_specs=[row(GLA_KW), row(GLA_KW), row(GLA_VW), row(GLA_KW),
                  _full_spec(tril.shape), _full_spec(ones.shape), _full_spec(ones_t.shape)],
        out_specs=[row(GLA_VW), pl.BlockSpec((1, H_GLA, DK_GLA, DV_GLA), lambda b, g: (b, 0, 0, 0))],
        out_shape=[jax.ShapeDtypeStruct((batch * seq, GLA_VW), F32),
                   jax.ShapeDtypeStruct((batch, H_GLA, DK_GLA, DV_GLA), F32)],
        scratch_shapes=[pltpu.VMEM((GLA_KW, GLA_VW), F32)],
        compiler_params=_cparams("parallel", "arbitrary"),
        name="gla_prompt",
    )(gq, gk, gv, la, tril, ones, ones_t)


def _gla_sample_kernel(q_ref, k_ref, v_ref, la_ref, s0_ref, tril_ref, o_ref, s1_ref):
    n, ls, _ = q_ref.shape
    la = la_ref[...]
    tril = jnp.broadcast_to(tril_ref[...][None], (n, ls, ls))
    hi, mid, lo = _split3(la)
    bdot = lambda x: jnp.einsum('nts,nsd->ntd', tril, x, preferred_element_type=F32)
    b = (bdot(hi) + bdot(mid)) + bdot(lo)
    b_last = b[:, ls - 1:ls, :]
    q_dec = (q_ref[...] * jnp.exp(b)).astype(BF16)
    k = k_ref[...]
    kd = (k * jnp.exp(-b)).astype(BF16)
    kl = (k * jnp.exp(b_last - b)).astype(BF16)
    v = v_ref[...].astype(BF16)
    s0 = s0_ref[...]
    att = jnp.einsum('ncd,nsd->ncs', q_dec, kd, preferred_element_type=F32)
    r = lax.broadcasted_iota(jnp.int32, (n, ls, ls), 1)
    c = lax.broadcasted_iota(jnp.int32, (n, ls, ls), 2)
    att = jnp.where(c <= r, att, 0.0)
    o = jnp.einsum('ncs,nse->nce', att.astype(BF16), v, preferred_element_type=F32)
    o = o + jnp.einsum('ncd,nde->nce', q_dec, s0.astype(BF16), preferred_element_type=F32)
    o_ref[...] = o
    kv = jnp.einsum('nds,nse->nde', jnp.swapaxes(kl, 1, 2), v, preferred_element_type=F32)
    dec = jnp.swapaxes(jnp.exp(b_last), 1, 2)
    s1_ref[...] = dec * s0 + kv


def _gla_sample(gq, gk, gv, la, state):
    n, ls, _ = gq.shape
    nb = min(32, n)
    tril = jnp.asarray(np.tril(np.ones((ls, ls), np.float32)), BF16)
    blk = lambda a, c: pl.BlockSpec((nb, a, c), lambda i: (i, 0, 0))
    return pl.pallas_call(
        _gla_sample_kernel,
        grid=(n // nb,),
        in_specs=[blk(ls, DK_GLA), blk(ls, DK_GLA), blk(ls, DV_GLA), blk(ls, DK_GLA), blk(DK_GLA, DV_GLA),
                  _full_spec(tril.shape)],
        out_specs=[blk(ls, DV_GLA), blk(DK_GLA, DV_GLA)],
        out_shape=[jax.ShapeDtypeStruct((n, ls, DV_GLA), F32), jax.ShapeDtypeStruct((n, DK_GLA, DV_GLA), F32)],
        compiler_params=_cparams("parallel"),
        name="gla_sample",
    )(gq, gk, gv, la, state, tril)


def _mixproj_kernel(x_ref, fox_ref, gla_ref, gr_ref, gg_ref, wo_ref, gc_ref, wq_ref, x1_ref, q_ref):
    gla = gla_ref[...]
    gr = gr_ref[...]
    acc = _dot(fox_ref[...], wo_ref[0:FOX_W, :])
    for h in range(H_GLA):
        cols = slice(h * DV_GLA, (h + 1) * DV_GLA)
        gh = _rms(gla[:, cols], gg_ref[:, cols])
        grh = gr[:, cols]
        gh = gh * (grh * jax.nn.sigmoid(grh))
        acc = acc + _dot(gh.astype(BF16), wo_ref[FOX_W + h * DV_GLA:FOX_W + (h + 1) * DV_GLA, :])
    x1 = x_ref[...] + acc
    x1_ref[...] = x1
    h2 = _rms(x1, gc_ref[...]).astype(BF16)
    q_ref[...] = (_dot(h2, wq_ref[...]) * (DH_MEM ** -0.5)).astype(BF16)


def _mixproj(x2d, fox_o, gla_o, gr, g_gla, w_out_bf, g_cross, w_q_bf):
    t = x2d.shape[0]
    tt = min(TOK_TILE, t)
    row = lambda w: pl.BlockSpec((tt, w), lambda i: (i, 0))
    return pl.pallas_call(
        _mixproj_kernel,
        grid=(t // tt,),
        in_specs=[row(D_MODEL), row(FOX_W), row(GLA_VW), row(GLA_VW), _full_spec(g_gla.shape),
                  _full_spec(w_out_bf.shape), _full_spec(g_cross.shape), _full_spec(w_q_bf.shape)],
        out_specs=[row(D_MODEL), row(D_MODEL)],
        out_shape=[jax.ShapeDtypeStruct((t, D_MODEL), F32), jax.ShapeDtypeStruct((t, D_MODEL), BF16)],
        compiler_params=_cparams("parallel"),
        name="mixproj",
    )(x2d, fox_o, gla_o, gr, g_gla, w_out_bf, g_cross, w_q_bf)


def _memkv_kernel(m_ref, g_ref, wk_ref, wv_ref, k_ref, v_ref):
    m = _rms(m_ref[...], g_ref[...]).astype(BF16)
    k_ref[...] = _dot(m, wk_ref[...])
    v_ref[...] = _dot(m, wv_ref[...])


def _memkv(mem2d, g_mem, wk_bf, wv_bf):
    t = mem2d.shape[0]
    tt = min(TOK_TILE, t)
    row = pl.BlockSpec((tt, D_MODEL), lambda i: (i, 0))
    return pl.pallas_call(
        _memkv_kernel,
        grid=(t // tt,),
        in_specs=[row, _full_spec(g_mem.shape), _full_spec(wk_bf.shape), _full_spec(wv_bf.shape)],
        out_specs=[row, row],
        out_shape=[jax.ShapeDtypeStruct((t, D_MODEL), F32)] * 2,
        compiler_params=_cparams("parallel"),
        name="memkv",
    )(mem2d, g_mem, wk_bf, wv_bf)


def _cross_kernel(q_ref, mk_ref, mv_ref, o_ref, *, per_head_memory):
    q = q_ref[0]
    for h in range(H_MEM):
        cols = slice(h * DH_MEM, (h + 1) * DH_MEM)
        if per_head_memory:
            mk = mk_ref[0, 0, :, h, :].astype(BF16)
            mv = mv_ref[0, 0, :, h, :].astype(BF16)
        else:
            mk = mk_ref[0, :, cols].astype(BF16)
            mv = mv_ref[0, :, cols].astype(BF16)
        s = _dot_nt(q[:, cols], mk)
        e = jnp.exp(s - jnp.max(s, axis=-1, keepdims=True))
        p = e / jnp.sum(e, axis=-1, keepdims=True)
        o_ref[0, :, cols] = _dot(p.astype(BF16), mv).astype(o_ref.dtype)


def _cross(q3, mk, mv, layer=None):
    b, lq, _ = q3.shape
    tl = min(TOK_TILE, lq)
    if layer is None:
        mem_spec = pl.BlockSpec((1,) + mk.shape[1:], lambda i, j: (i, 0, 0))
    else:
        mem_spec = pl.BlockSpec((1, 1) + mk.shape[2:], lambda i, j: (layer, i, 0, 0, 0))
    return pl.pallas_call(
        functools.partial(_cross_kernel, per_head_memory=layer is not None),
        grid=(b, lq // tl),
        in_specs=[pl.BlockSpec((1, tl, D_MODEL), lambda i, j: (i, j, 0)), mem_spec, mem_spec],
        out_specs=pl.BlockSpec((1, tl, D_MODEL), lambda i, j: (i, j, 0)),
        out_shape=jax.ShapeDtypeStruct((b, lq, D_MODEL), BF16),
        compiler_params=_cparams("parallel", "parallel"),
        name="cross_attn",
    )(q3, mk, mv)


def _peer_prep_kernel(u_ref, v_ref, ub_ref, vt_ref):
    ub_ref[...] = u_ref[...].astype(BF16)
    vt_ref[...] = v_ref[...].T.astype(BF16)


def _peer_prep(u, v):
    ne = u.shape[0]
    eb = min(PREP_EBLK, ne)
    return pl.pallas_call(
        _peer_prep_kernel,
        grid=(ne // eb,),
        in_specs=[pl.BlockSpec((eb, D_MODEL), lambda i: (i, 0))] * 2,
        out_specs=[pl.BlockSpec((eb, D_MODEL), lambda i: (i, 0)), pl.BlockSpec((D_MODEL, eb), lambda i: (0, i))],
        out_shape=[jax.ShapeDtypeStruct((ne, D_MODEL), BF16), jax.ShapeDtypeStruct((D_MODEL, ne), BF16)],
        compiler_params=_cparams("parallel"),
        name="peer_prep",
    )(u, v)


def _top16_rows(s):
    nrow = s.shape[0]
    rowid = lax.broadcasted_iota(jnp.int32, s.shape, 0)
    rank = jnp.full(s.shape, float(PEER_TOPK), F32)
    work = s
    vals = []
    for r in range(PEER_TOPK):
        mx = jnp.max(work, axis=0, keepdims=True)
        idx = jnp.min(jnp.where(work == mx, rowid, nrow), axis=0, keepdims=True)
        sel = rowid == idx
        rank = jnp.where(sel, float(r), rank)
        work = jnp.where(sel, -jnp.inf, work)
        vals.append(mx)
    return rank, jnp.concatenate(vals, axis=0)


def _peer_route_kernel(x_ref, x1_ref, wo_ref, g_ref, wq_ref, sk_ref, ca_ref, cb_ref, cg_ref,
                       x2_ref, hn_ref, c1_ref, nrow_ref, p2_ref, rk2_ref, q_sc, s_sc):
    tt = x_ref.shape[0]
    nch = tt // LANES
    x2 = x1_ref[...] + _dot(x_ref[...], wo_ref[...])
    x2_ref[...] = x2
    hn = _rms(x2, g_ref[...]).astype(BF16)
    hn_ref[...] = hn
    q_sc[...] = _dot(hn, wq_ref[...]).astype(BF16)
    for hj in range(2 * H_PEER):
        st = _dot_nt(sk_ref[hj % 2], q_sc[:, hj * N_KEYS:(hj + 1) * N_KEYS])
        for ch in range(nch):
            s_sc[hj, ch] = st[:, ch * LANES:(ch + 1) * LANES]

    cand_row = lax.broadcasted_iota(jnp.int32, (N_CAND_PAD, LANES), 0)

    def unit(i, carry):
        h = i // nch
        ch = i % nch
        s1 = s_sc[2 * h, ch]
        s2 = s_sc[2 * h + 1, ch]
        rank1, v1 = _top16_rows(s1)
        rank2, v2 = _top16_rows(s2)
        cand = _sel_dot_l(ca_ref[...], v1) + _sel_dot_l(cb_ref[...], v2)
        cand = jnp.where(cand_row < N_CAND, cand, -jnp.inf)
        crank, _ = _top16_rows(cand)
        chosen = crank < float(PEER_TOPK)
        n_of_a = _dot(cg_ref[...], jnp.where(chosen, 1.0, 0.0).astype(BF16))
        z = jnp.sum(jnp.where(chosen, jnp.exp(cand - cand[0:1]), 0.0), axis=0, keepdims=True)
        nrow = jnp.zeros_like(rank1)
        for a in range(PEER_TOPK):
            nrow = jnp.where(rank1 == float(a), n_of_a[a:a + 1], nrow)
        c1_ref[h, ch] = jnp.exp(s1 - v1[0:1]) / z
        nrow_ref[h, ch] = nrow
        p2_ref[h, ch] = jnp.exp(s2 - v2[0:1]).astype(p2_ref.dtype)
        rk2_ref[h, ch] = rank2.astype(rk2_ref.dtype)
        return carry

    lax.fori_loop(0, H_PEER * nch, unit, 0)


def _peer_route(oc2d, x1, w_o_bf, g_ffn, w_q_bf, sk_bf):
    t = x1.shape[0]
    tt = min(PEER_TOK, t)
    nch = tt // LANES
    pairs = [(a, b) for a in range(PEER_TOPK) for b in range(PEER_TOPK) if (a + 1) * (b + 1) <= PEER_TOPK]
    assert len(pairs) == N_CAND
    ca = np.zeros((N_CAND_PAD, PEER_TOPK), np.float32)
    cb = np.zeros((N_CAND_PAD, PEER_TOPK), np.float32)
    cg = np.zeros((PEER_TOPK, N_CAND_PAD), np.float32)
    for i, (a, b) in enumerate(pairs):
        ca[i, a] = 1.0
        cb[i, b] = 1.0
        cg[a, i] = 1.0
    ca, cb, cg = (jnp.asarray(m, BF16) for m in (ca, cb, cg))
    row = lambda w: pl.BlockSpec((tt, w), lambda i: (i, 0))
    sel = pl.BlockSpec((H_PEER, nch, N_KEYS, LANES), lambda i: (0, i, 0, 0))
    sel_shape = lambda dt: jax.ShapeDtypeStruct((H_PEER, t // LANES, N_KEYS, LANES), dt)
    return pl.pallas_call(
        _peer_route_kernel,
        grid=(t // tt,),
        in_specs=[row(D_MODEL), row(D_MODEL), _full_spec(w_o_bf.shape), _full_spec(g_ffn.shape),
                  _full_spec(w_q_bf.shape), _full_spec(sk_bf.shape),
                  _full_spec(ca.shape), _full_spec(cb.shape), _full_spec(cg.shape)],
        out_specs=[row(D_MODEL), row(D_MODEL), sel, sel, sel, sel],
        out_shape=[jax.ShapeDtypeStruct((t, D_MODEL), F32), jax.ShapeDtypeStruct((t, D_MODEL), BF16),
                   sel_shape(F32), sel_shape(F32), sel_shape(BF16), sel_shape(BF16)],
        scratch_shapes=[pltpu.VMEM((tt, 2 * H_PEER * N_KEYS), BF16),
                        pltpu.VMEM((2 * H_PEER, nch, N_KEYS, LANES), F32)],
        compiler_params=_cparams("parallel"),
        name="peer_route",
    )(oc2d, x1, w_o_bf, g_ffn, w_q_bf, sk_bf, ca, cb, cg)


def _peer_mix_kernel(x2_ref, hn_ref, u_ref, vt_ref, c1_ref, nrow_ref, p2_ref, rk2_ref, gf_ref,
                     y_ref, a_sc, w_sc, acc_sc, *, final_norm):
    e = pl.program_id(1)
    tt = hn_ref.shape[0]
    nch = tt // LANES
    n_i1 = u_ref.shape[0] // N_KEYS

    @pl.when(e == 0)
    def _():
        acc_sc[...] = jnp.zeros_like(acc_sc)

    a_sc[...] = jax.nn.gelu(_dot_nt(u_ref[...], hn_ref[...])).astype(BF16)
    for ch in range(nch):
        cols = slice(ch * LANES, (ch + 1) * LANES)
        for r in range(n_i1):
            rows = slice(r * N_KEYS, (r + 1) * N_KEYS)
            w = jnp.zeros((N_KEYS, LANES), BF16)
            for h in range(H_PEER):
                wide = lambda ref: jnp.broadcast_to(ref[h, ch, r:r + 1, :], (N_KEYS, LANES)).astype(BF16)
                p2 = p2_ref[h, ch]
                w = w + jnp.where(rk2_ref[h, ch] < wide(nrow_ref), p2, jnp.zeros_like(p2)) * wide(c1_ref)
            w_sc[rows, cols] = w * a_sc[rows, cols]
    acc_sc[...] += _dot(vt_ref[...], w_sc[...])

    @pl.when(e == pl.num_programs(1) - 1)
    def _():
        y = x2_ref[...] + acc_sc[...].T
        y_ref[...] = _rms(y, gf_ref[...]) if final_norm else y


def _peer_mix(x2, hn, u_bf, vt_bf, c1, nrow, p2, rk2, g_final, final_norm):
    t = x2.shape[0]
    tt = min(PEER_TOK, t)
    nch = tt // LANES
    ne = u_bf.shape[0]
    eb = min(PEER_EBLK, ne)
    n_i1 = eb // N_KEYS
    row = lambda w: pl.BlockSpec((tt, w), lambda i, e: (i, 0))
    part = pl.BlockSpec((H_PEER, nch, n_i1, LANES), lambda i, e: (0, i, e, 0))
    full = pl.BlockSpec((H_PEER, nch, N_KEYS, LANES), lambda i, e: (0, i, 0, 0))
    return pl.pallas_call(
        functools.partial(_peer_mix_kernel, final_norm=final_norm),
        grid=(t // tt, ne // eb),
        in_specs=[row(D_MODEL), row(D_MODEL),
                  pl.BlockSpec((eb, D_MODEL), lambda i, e: (e, 0)),
                  pl.BlockSpec((D_MODEL, eb), lambda i, e: (0, e)),
                  part, part, full, full, pl.BlockSpec(g_final.shape, lambda i, e: (0, 0))],
        out_specs=row(D_MODEL),
        out_shape=jax.ShapeDtypeStruct((t, D_MODEL), F32),
        scratch_shapes=[pltpu.VMEM((eb, tt), BF16), pltpu.VMEM((eb, tt), BF16), pltpu.VMEM((D_MODEL, tt), F32)],
        compiler_params=_cparams("parallel", "arbitrary"),
        name="peer_mix",
    )(x2, hn, u_bf, vt_bf, c1, nrow, p2, rk2, g_final)


def _prep_in_weights(w_in, b_fox_f, w_gate2, b_gate):
    o = np.cumsum((0,) + IN_SIZES)
    seg = lambda i: w_in[:, int(o[i]):int(o[i + 1])]
    pad = jnp.zeros((D_MODEL, LANES - H_FOX - GLA_GATE_RANK), w_in.dtype)
    w_perm = jnp.concatenate([seg(0), seg(1), seg(2), seg(4), seg(5), seg(6), seg(8), seg(3), seg(7), pad],
                             axis=1).astype(BF16)
    b_small = jnp.zeros((1, LANES), F32).at[0, 0:H_FOX].set(b_fox_f)
    wg_pad = jnp.zeros((LANES, GLA_KW), F32).at[H_FOX:H_FOX + GLA_GATE_RANK].set(w_gate2).astype(BF16)
    return w_perm, b_small, wg_pad, b_gate.reshape(1, GLA_KW)


def kernel(x_prompt, x_sample, cache_fox_k, cache_fox_v, cache_fox_logf, state_gla, cache_mem_k, cache_mem_v, page_table, mem_prompt, g_mix, w_in, b_fox_f, w_gla_gate2, b_gla_gate, g_gla_out, w_out, g_cross, g_mem, w_mem_k, w_mem_v, w_cross_q, w_cross_o, g_ffn, peer_w_q, peer_subkeys, peer_u, peer_v, g_final):
    depth = w_in.shape[0]
    bp, lp, _ = x_prompt.shape
    bs, ls, _ = x_sample.shape
    n_mem = mem_prompt.shape[1]
    n_pool, page = cache_fox_k.shape[1:3]
    xp = x_prompt.reshape(bp * lp, D_MODEL)
    xs = x_sample.reshape(bs * ls, D_MODEL)
    vec = lambda a: a.reshape(1, -1)
    outs = [[] for _ in range(10)]
    for l in range(depth):
        w_perm, b_small, wg_pad, b_gate = _prep_in_weights(w_in[l], b_fox_f[l], w_gla_gate2[l], b_gla_gate[l])
        w_out_bf = w_out[l].astype(BF16)
        w_cq_bf = w_cross_q[l].astype(BF16)
        w_co_bf = w_cross_o[l].astype(BF16)
        w_pq_bf = peer_w_q[l].astype(BF16)
        sk_bf = peer_subkeys[l].astype(BF16)
        u_bf, vt_bf = _peer_prep(peer_u[l], peer_v[l])
        last = l == depth - 1

        def tail(x, fox_o, gla_o, gr, mk, mv, nb, mem_layer):
            x1, qc = _mixproj(x, fox_o, gla_o, gr, vec(g_gla_out[l]), w_out_bf, vec(g_cross[l]), w_cq_bf)
            oc = _cross(qc.reshape(nb, -1, D_MODEL), mk, mv, mem_layer).reshape(-1, D_MODEL)
            x2, hn, c1, nrow, p2, rk2 = _peer_route(oc, x1, w_co_bf, vec(g_ffn[l]), w_pq_bf, sk_bf)
            return _peer_mix(x2, hn, u_bf, vt_bf, c1, nrow, p2, rk2, vec(g_final), last)

        fq_bf, fk, fv, lf, gq, gk, gv, la, gr = _inproj(xp, vec(g_mix[l]), w_perm, b_small, wg_pad, b_gate)
        fox_o = _fox_prompt(*_fox_prep(fq_bf, fk, fv, lf, bp, lp), bp, lp)
        gla_o, s_fin = _gla_prompt(gq, gk, gv, la, bp, lp)
        mk, mv = _memkv(mem_prompt.reshape(bp * n_mem, D_MODEL), vec(g_mem[l]), w_mem_k[l].astype(BF16),
                        w_mem_v[l].astype(BF16))
        mk3 = mk.reshape(bp, n_mem, D_MODEL)
        mv3 = mv.reshape(bp, n_mem, D_MODEL)
        xp = tail(xp, fox_o, gla_o, gr, mk3, mv3, bp, None)
        outs[0].append(fk.reshape(bp, lp, H_FOX, DH_FOX))
        outs[1].append(fv.reshape(bp, lp, H_FOX, DH_FOX))
        outs[2].append(lf.reshape(bp, lp, H_FOX))
        outs[3].append(s_fin)
        outs[4].append(mk.reshape(bp, n_mem, H_MEM, DH_MEM))
        outs[5].append(mv.reshape(bp, n_mem, H_MEM, DH_MEM))

        fq_bf, fk, fv, lf, gq, gk, gv, la, gr = _inproj(xs, vec(g_mix[l]), w_perm, b_small, wg_pad, b_gate)
        per_head = lambda a: a.reshape(bs, ls, H_FOX, DH_FOX)
        q_hm = per_head(fq_bf).transpose(0, 2, 1, 3).reshape(bs, H_FOX * ls, DH_FOX)
        fox_o = _fox_sample(q_hm, per_head(fk), per_head(fv), lf.reshape(bs, 1, ls * H_FOX),
                            cache_fox_k, cache_fox_v, cache_fox_logf[l].reshape(n_pool, 1, page * H_FOX),
                            page_table, l)
        fox_o = fox_o.reshape(bs, H_FOX, ls, DH_FOX).transpose(0, 2, 1, 3).reshape(bs * ls, FOX_W)
        heads = lambda a, d: a.reshape(bs, ls, H_GLA, d).transpose(0, 2, 1, 3).reshape(bs * H_GLA, ls, d)
        gla_o, s_new = _gla_sample(heads(gq, DK_GLA), heads(gk, DK_GLA), heads(gv, DV_GLA), heads(la, DK_GLA),
                                   state_gla[l].reshape(bs * H_GLA, DK_GLA, DV_GLA))
        gla_o = gla_o.reshape(bs, H_GLA, ls, DV_GLA).transpose(0, 2, 1, 3).reshape(bs * ls, GLA_VW)
        xs = tail(xs, fox_o, gla_o, gr, cache_mem_k, cache_mem_v, bs, l)
        outs[6].append(fk.reshape(bs, ls, H_FOX, DH_FOX))
        outs[7].append(fv.reshape(bs, ls, H_FOX, DH_FOX))
        outs[8].append(lf.reshape(bs, ls, H_FOX))
        outs[9].append(s_new.reshape(bs, H_GLA, DK_GLA, DV_GLA))
    stacked = [jnp.stack(o, axis=0) for o in outs]
    return (xp.reshape(bp, lp, D_MODEL), xs.reshape(bs, ls, D_MODEL), *stacked)
```

```python
import functools

import numpy as np
import jax
import jax.numpy as jnp
from jax import lax
from jax.experimental import pallas as pl
from jax.experimental.pallas import tpu as pltpu

F32 = jnp.float32
BF16 = jnp.bfloat16

D_MODEL = 1024
H_FOX, DH_FOX = 8, 64
H_GLA, DK_GLA, DV_GLA = 4, 64, 128
GLA_GATE_RANK = 16
GLA_TAU = 16.0
GLA_CHUNK = 64
H_MEM = 4
DH_MEM = D_MODEL // H_MEM
N_KEYS = 128
H_PEER = 8
PEER_TOPK = 16
EPS = 1e-6
NEG_INF = -1e30
FOX_W = H_FOX * DH_FOX
GLA_KW = H_GLA * DK_GLA
GLA_VW = H_GLA * DV_GLA
IN_SIZES = (FOX_W, FOX_W, FOX_W, H_FOX, GLA_KW, GLA_KW, GLA_VW, GLA_GATE_RANK, GLA_VW)

LANES = 128
VMEM_LIMIT_BYTES = 56 * 2**20

TOK_TILE = 512
INPROJ_TILE = 256
FOX_TILE = 512
GLA_GROUP = 512
PEER_TOK = 512
PEER_EBLK = 2048
PEER_SUB_I1 = 8
FOX_PAGES_PER_STEP = 8
PREP_EBLK = 512
N_CAND = 50
N_CAND_PAD = 64


def _cparams(*sem):
    return pltpu.CompilerParams(dimension_semantics=sem, vmem_limit_bytes=VMEM_LIMIT_BYTES)


def _dot(a, b):
    return jnp.dot(a, b, preferred_element_type=F32)


def _dot_nt(a, b):
    return lax.dot_general(a, b, (((1,), (1,)), ((), ())), preferred_element_type=F32)


def _split3(x):
    hi = x.astype(BF16)
    r = x - hi.astype(F32)
    mid = r.astype(BF16)
    lo = (r - mid.astype(F32)).astype(BF16)
    return hi, mid, lo


def _sel_dot_l(m01, x):
    hi, mid, lo = _split3(x)
    return (_dot(m01, hi) + _dot(m01, mid)) + _dot(m01, lo)


def _sel_dot_r(x, m01):
    hi, mid, lo = _split3(x)
    return (_dot(hi, m01) + _dot(mid, m01)) + _dot(lo, m01)


def _rms(x, g):
    return x * lax.rsqrt(jnp.mean(x * x, axis=-1, keepdims=True) + EPS) * g


def _logsig(x):
    return jnp.minimum(x, 0.0) - jnp.log1p(jnp.exp(-jnp.abs(x)))


def _full_spec(shape):
    nd = len(shape)
    return pl.BlockSpec(shape, lambda *_: (0,) * nd)


_C_FQ, _C_FK, _C_FV = 0, FOX_W, 2 * FOX_W
_C_GQ = 3 * FOX_W
_C_GK = _C_GQ + GLA_KW
_C_GV = _C_GK + GLA_KW
_C_GR = _C_GV + GLA_VW
_C_SM = _C_GR + GLA_VW
_C_END = _C_SM + LANES


def _inproj_kernel(x_ref, g_ref, w_ref, bsm_ref, wg_ref, bg_ref,
                   fq_ref, fk_ref, fv_ref, lf_ref, gq_ref, gk_ref, gv_ref, la_ref, gr_ref):
    h = _rms(x_ref[...], g_ref[...]).astype(BF16)

    def seg(lo, hi):
        return _dot(h, w_ref[:, lo:hi])

    fq_ref[...] = (seg(_C_FQ, _C_FK) * (DH_FOX ** -0.5)).astype(BF16)
    fk_ref[...] = seg(_C_FK, _C_FV)
    fv_ref[...] = seg(_C_FV, _C_GQ)
    gq_ref[...] = seg(_C_GQ, _C_GK) * (DK_GLA ** -0.5)
    gk_ref[...] = seg(_C_GK, _C_GV)
    gv_ref[...] = seg(_C_GV, _C_GR)
    gr_ref[...] = seg(_C_GR, _C_SM)
    ysm = seg(_C_SM, _C_END)
    lf_ref[...] = _logsig(ysm + bsm_ref[...])[:, 0:H_FOX]
    z = _dot(ysm.astype(BF16), wg_ref[...]) + bg_ref[...]
    la_ref[...] = _logsig(z) / GLA_TAU


def _inproj(x2d, g_mix, w_perm, b_small, wg_pad, b_gate):
    t = x2d.shape[0]
    tt = min(INPROJ_TILE, t)
    row = lambda w: pl.BlockSpec((tt, w), lambda i: (i, 0))
    outs = [(FOX_W, BF16), (FOX_W, F32), (FOX_W, F32), (H_FOX, F32),
            (GLA_KW, F32), (GLA_KW, F32), (GLA_VW, F32), (GLA_KW, F32), (GLA_VW, F32)]
    return pl.pallas_call(
        _inproj_kernel,
        grid=(t // tt,),
        in_specs=[row(D_MODEL), _full_spec(g_mix.shape), _full_spec(w_perm.shape), _full_spec(b_small.shape),
                  _full_spec(wg_pad.shape), _full_spec(b_gate.shape)],
        out_specs=[row(w) for w, _ in outs],
        out_shape=[jax.ShapeDtypeStruct((t, w), dt) for w, dt in outs],
        compiler_params=_cparams("parallel"),
        name="inproj",
    )(x2d, g_mix, w_perm, b_small, wg_pad, b_gate)


_FOX_QW = 4 * LANES
_FOX_KW = 2 * LANES
_N_HP = FOX_W // LANES


def _fox_aug_tables():
    pq = np.zeros((3, H_FOX, _N_HP * 2 * LANES), np.float32)
    oq = np.zeros((1, _N_HP * 2 * LANES), np.float32)
    pk = np.zeros((3, H_FOX, _N_HP * LANES), np.float32)
    ok = np.zeros((1, _N_HP * LANES), np.float32)
    for hp in range(_N_HP):
        for half in range(2):
            head = 2 * hp + half
            qb = (hp * 2 + half) * LANES
            kb = hp * LANES
            for j in range(3):
                pq[j, head, qb + 6 * half + j] = 1.0
                oq[0, qb + 6 * half + 3 + j] = 1.0
                pk[j, head, kb + 6 * half + 3 + j] = 1.0
                ok[0, kb + 6 * half + j] = 1.0
    return jnp.asarray(pq, BF16), jnp.asarray(oq, F32), jnp.asarray(pk, BF16), jnp.asarray(ok, F32)


def _fox_prep_kernel(q_ref, k_ref, v_ref, lf_ref, tril_ref, pq_ref, oq_ref, pk_ref, ok_ref, eye_ref,
                     qc_ref, kc_ref, vt_ref, fkt_ref, fvt_ref, lft_ref, carry_sc):
    @pl.when(pl.program_id(1) == 0)
    def _():
        carry_sc[...] = jnp.zeros_like(carry_sc)

    lf_parts = _split3(lf_ref[...])
    eye = eye_ref[...]
    lft_ref[0] = (_dot_nt(eye, lf_parts[0]) + _dot_nt(eye, lf_parts[1])) + _dot_nt(eye, lf_parts[2])
    c = _sel_dot_l(tril_ref[...], lf_ref[...]) + carry_sc[0:1, 0:H_FOX]
    carry_sc[0:1, 0:H_FOX] = c[-1:, :]
    parts = _split3(c)
    aug_q = oq_ref[...]
    aug_k = ok_ref[...]
    for j in range(3):
        aug_q = aug_q + _dot(parts[j], pq_ref[j])
        aug_k = aug_k - _dot(parts[j], pk_ref[j])
    aug_q = aug_q.astype(BF16)
    aug_k = aug_k.astype(BF16)
    lane = lax.broadcasted_iota(jnp.int32, (q_ref.shape[0], LANES), 1)
    for hp in range(_N_HP):
        q = q_ref[:, hp * LANES:(hp + 1) * LANES]
        zero = jnp.zeros_like(q)
        base = hp * _FOX_QW
        qc_ref[:, base:base + LANES] = jnp.where(lane < DH_FOX, q, zero)
        qc_ref[:, base + LANES:base + 2 * LANES] = aug_q[:, (2 * hp) * LANES:(2 * hp + 1) * LANES]
        qc_ref[:, base + 2 * LANES:base + 3 * LANES] = jnp.where(lane >= DH_FOX, q, zero)
        qc_ref[:, base + 3 * LANES:base + 4 * LANES] = aug_q[:, (2 * hp + 1) * LANES:(2 * hp + 2) * LANES]
        kc_ref[:, hp * _FOX_KW:hp * _FOX_KW + LANES] = k_ref[:, hp * LANES:(hp + 1) * LANES].astype(BF16)
        kc_ref[:, hp * _FOX_KW + LANES:(hp + 1) * _FOX_KW] = aug_k[:, hp * LANES:(hp + 1) * LANES]
    v_t = v_ref[...].T
    vt_ref[0] = v_t.astype(BF16)
    fvt_ref[0] = v_t
    fkt_ref[0] = k_ref[...].T


def _fox_prep(fq_bf, fk, fv, lf, batch, seq):
    tt = min(FOX_TILE, seq)
    nt = seq // tt
    tril = jnp.asarray(np.tril(np.ones((tt, tt), np.float32)), BF16)
    consts = [tril, *_fox_aug_tables(), jnp.eye(H_FOX, dtype=BF16)]
    row = lambda w: pl.BlockSpec((tt, w), lambda b, i: (b * nt + i, 0))
    col = lambda h: pl.BlockSpec((1, h, tt), lambda b, i: (b, 0, i))
    return pl.pallas_call(
        _fox_prep_kernel,
        grid=(batch, nt),
        in_specs=[row(FOX_W), row(FOX_W), row(FOX_W), row(H_FOX)] + [_full_spec(a.shape) for a in consts],
        out_specs=[row(_N_HP * _FOX_QW), row(_N_HP * _FOX_KW), col(FOX_W), col(FOX_W), col(FOX_W), col(H_FOX)],
        out_shape=[jax.ShapeDtypeStruct((batch * seq, _N_HP * _FOX_QW), BF16),
                   jax.ShapeDtypeStruct((batch * seq, _N_HP * _FOX_KW), BF16),
                   jax.ShapeDtypeStruct((batch, FOX_W, seq), BF16),
                   jax.ShapeDtypeStruct((batch, FOX_W, seq), F32),
                   jax.ShapeDtypeStruct((batch, FOX_W, seq), F32),
                   jax.ShapeDtypeStruct((batch, H_FOX, seq), F32)],
        scratch_shapes=[pltpu.VMEM((8, LANES), F32)],
        compiler_params=_cparams("parallel", "arbitrary"),
        name="fox_prep",
    )(fq_bf, fk, fv, lf, *consts)


def _fox_prompt_kernel(qi_tab, ki_tab, q_ref, k_ref, vt_ref, o_ref,
                       m_sc, l_sc, acc_sc):
    t = pl.program_id(2)
    qi = qi_tab[t]
    ki = ki_tab[t]
    tq = q_ref.shape[0]
    tk = k_ref.shape[0]

    @pl.when(ki == 0)
    def _():
        m_sc[...] = jnp.full_like(m_sc, NEG_INF)
        l_sc[...] = jnp.zeros_like(l_sc)
        acc_sc[...] = jnp.zeros_like(acc_sc)

    def step(masked):
        kc = k_ref[...]
        vt = vt_ref[0]
        for hh in range(2):
            s = _dot_nt(kc, q_ref[:, hh * _FOX_KW:(hh + 1) * _FOX_KW])
            if masked:
                kpos = lax.broadcasted_iota(jnp.int32, (tk, tq), 0)
                qpos = lax.broadcasted_iota(jnp.int32, (tk, tq), 1)
                s = jnp.where(kpos <= qpos, s, NEG_INF)
            m_prev = m_sc[hh:hh + 1, :]
            m_new = jnp.maximum(m_prev, jnp.max(s, axis=0, keepdims=True))
            alpha = jnp.exp(m_prev - m_new)
            p = jnp.exp(s - m_new)
            l_sc[hh:hh + 1, :] = alpha * l_sc[hh:hh + 1, :] + jnp.sum(p, axis=0, keepdims=True)
            acc_sc[hh] = alpha * acc_sc[hh] + _dot(vt, p.astype(BF16))
            m_sc[hh:hh + 1, :] = m_new

    @pl.when(ki < qi)
    def _():
        step(False)

    @pl.when(ki == qi)
    def _():
        step(True)
        o0 = acc_sc[0] / l_sc[0:1, :]
        o1 = acc_sc[1] / l_sc[1:2, :]
        dim = lax.broadcasted_iota(jnp.int32, (LANES, tq), 0)
        o_ref[...] = jnp.where(dim < DH_FOX, o0, o1).T.astype(o_ref.dtype)


def _fox_prompt(qc, kc, vt, batch, seq):
    tq = min(FOX_TILE, seq)
    nq = seq // tq
    pairs = [(i, j) for i in range(nq) for j in range(i + 1)]
    qi_tab = jnp.asarray([p[0] for p in pairs], jnp.int32)
    ki_tab = jnp.asarray([p[1] for p in pairs], jnp.int32)
    grid_spec = pltpu.PrefetchScalarGridSpec(
        num_scalar_prefetch=2,
        grid=(batch, _N_HP, len(pairs)),
        in_specs=[
            pl.BlockSpec((tq, _FOX_QW), lambda b, h, t, qt, kt: (b * nq + qt[t], h)),
            pl.BlockSpec((tq, _FOX_KW), lambda b, h, t, qt, kt: (b * nq + kt[t], h)),
            pl.BlockSpec((1, LANES, tq), lambda b, h, t, qt, kt: (b, h, kt[t])),
        ],
        out_specs=pl.BlockSpec((tq, LANES), lambda b, h, t, qt, kt: (b * nq + qt[t], h)),
        scratch_shapes=[pltpu.VMEM((2, tq), F32), pltpu.VMEM((2, tq), F32), pltpu.VMEM((2, LANES, tq), F32)],
    )
    return pl.pallas_call(
        _fox_prompt_kernel,
        grid_spec=grid_spec,
        out_shape=jax.ShapeDtypeStruct((batch * seq, FOX_W), BF16),
        compiler_params=_cparams("parallel", "parallel", "arbitrary"),
        name="fox_prompt",
    )(qi_tab, ki_tab, qc, kc, vt)


def _fox_sample_kernel(pt_ref, q_ref, *refs, n_pp):
    kc, vc, lfc = refs[0:n_pp], refs[n_pp:2 * n_pp], refs[2 * n_pp:3 * n_pp]
    kn_ref, vn_ref, lfn_ref, tri_ref, trin_ref, o_ref, qbd_sc, m_sc, l_sc, acc_sc, carry_sc = refs[3 * n_pp:]
    step = pl.program_id(1)
    ls = q_ref.shape[1]
    nr = H_FOX * ls
    page = tri_ref.shape[0]
    row_head = lax.broadcasted_iota(jnp.int32, (nr, FOX_W), 0) // ls
    lane_head = lax.broadcasted_iota(jnp.int32, (nr, FOX_W), 1) // DH_FOX

    def rows_of_heads(x):
        return jnp.broadcast_to(x[:, None, :], (H_FOX, ls, x.shape[-1])).reshape(nr, x.shape[-1])

    cn_row = rows_of_heads(_sel_dot_r(lfn_ref[0], trin_ref[...]))
    tnew = lax.broadcasted_iota(jnp.int32, (nr, ls), 1)
    tq = lax.broadcasted_iota(jnp.int32, (nr, ls), 0) % ls
    cn_col = jnp.sum(jnp.where(tnew == tq, cn_row, 0.0), axis=-1, keepdims=True)

    @pl.when(step == 0)
    def _():
        qt = jnp.concatenate([q_ref[0]] * H_FOX, axis=0)
        qbd_sc[...] = jnp.where(row_head == lane_head, qt, jnp.zeros_like(qt))
        m_sc[...] = jnp.full_like(m_sc, NEG_INF)
        l_sc[...] = jnp.zeros_like(l_sc)
        acc_sc[...] = jnp.zeros_like(acc_sc)
        carry_sc[...] = jnp.zeros_like(carry_sc)

    def online(scores, weighted_values):
        m_prev = m_sc[...]
        m_new = m_prev
        for s in scores:
            m_new = jnp.maximum(m_new, jnp.max(s, axis=-1, keepdims=True))
        alpha = jnp.exp(m_prev - m_new)
        l_new = alpha * l_sc[...]
        acc = alpha * acc_sc[...]
        for s, pv in zip(scores, weighted_values):
            pr = jnp.exp(s - m_new)
            l_new = l_new + jnp.sum(pr, axis=-1, keepdims=True)
            acc = acc + pv(pr.astype(BF16))
        m_sc[...] = m_new
        l_sc[...] = l_new
        acc_sc[...] = acc

    c_loc = _sel_dot_r(jnp.concatenate([r[0, 0] for r in lfc], axis=0), tri_ref[...])
    carry = carry_sc[...][:, 0:1]
    qbd = qbd_sc[...]
    scores = []
    for j in range(n_pp):
        c_page = c_loc[j * H_FOX:(j + 1) * H_FOX] + carry
        carry = c_page[:, page - 1:page]
        s = _dot(qbd, kc[j][0, 0].astype(BF16))
        scores.append(s + cn_col - rows_of_heads(c_page))
    carry_sc[...] = jnp.broadcast_to(carry, carry_sc.shape)
    online(scores, [lambda p, j=j: _dot_nt(p, vc[j][0, 0].astype(BF16)) for j in range(n_pp)])

    @pl.when(step == pl.num_programs(1) - 1)
    def _():
        s_new = _dot_nt(qbd, kn_ref[0].astype(BF16))
        s_new = s_new + cn_col - cn_row - rows_of_heads(jnp.broadcast_to(carry, (H_FOX, LANES)))[:, 0:1]
        s_new = jnp.where(tnew <= tq, s_new, NEG_INF)
        online([s_new], [lambda p: _dot(p, vn_ref[0].astype(BF16))])
        o = acc_sc[...] / l_sc[...]
        o = jnp.where(row_head == lane_head, o, 0.0)
        out = o[0:ls]
        for h in range(1, H_FOX):
            out = out + o[h * ls:(h + 1) * ls]
        o_ref[0] = out.astype(o_ref.dtype)


def _fox_sample(fq_bf, fk, fv, lfn_t, kcache_t, vcache_t, lfcache_t, page_table, layer):
    bs, n_pages = page_table.shape
    n_pp = min(FOX_PAGES_PER_STEP, n_pages)
    assert n_pages % n_pp == 0
    page = kcache_t.shape[3]
    ls = fq_bf.shape[1]
    nr = H_FOX * ls
    tri = jnp.asarray(np.triu(np.ones((page, page), np.float32)), BF16)
    trin = jnp.asarray(np.triu(np.ones((ls, ls), np.float32)), BF16)
    page_spec = lambda j, rows: pl.BlockSpec(
        (1, 1, rows, page), lambda b, s, pt: (layer, pt[b * n_pages + s * n_pp + j], 0, 0))
    per_seq = lambda *shape: pl.BlockSpec((1,) + shape, lambda b, s, pt: (b,) + (0,) * len(shape))
    const = lambda a: pl.BlockSpec(a.shape, lambda b, s, pt: (0,) * a.ndim)
    grid_spec = pltpu.PrefetchScalarGridSpec(
        num_scalar_prefetch=1,
        grid=(bs, n_pages // n_pp),
        in_specs=([per_seq(ls, FOX_W)] + [page_spec(j, FOX_W) for j in range(n_pp)] * 2
                  + [page_spec(j, H_FOX) for j in range(n_pp)]
                  + [per_seq(ls, FOX_W), per_seq(ls, FOX_W), per_seq(H_FOX, ls), const(tri), const(trin)]),
        out_specs=per_seq(ls, FOX_W),
        scratch_shapes=[pltpu.VMEM((nr, FOX_W), BF16), pltpu.VMEM((nr, 1), F32), pltpu.VMEM((nr, 1), F32),
                        pltpu.VMEM((nr, FOX_W), F32), pltpu.VMEM((H_FOX, LANES), F32)],
    )
    return pl.pallas_call(
        functools.partial(_fox_sample_kernel, n_pp=n_pp),
        grid_spec=grid_spec,
        out_shape=jax.ShapeDtypeStruct((bs, ls, FOX_W), BF16),
        compiler_params=_cparams("parallel", "arbitrary"),
        name="fox_sample",
    )(page_table.reshape(-1), fq_bf, *([kcache_t] * n_pp), *([vcache_t] * n_pp), *([lfcache_t] * n_pp),
      fk, fv, lfn_t, tri, trin)


def _gla_prompt_kernel(q_ref, k_ref, v_ref, la_ref, tril_ref, ones_ref, onesT_ref,
                       o_ref, sfin_ref, s_sc):
    g = pl.program_id(1)
    n_tok = q_ref.shape[0]
    n_chunks = n_tok // GLA_CHUNK

    @pl.when(g == 0)
    def _():
        s_sc[...] = jnp.zeros_like(s_sc)

    la = la_ref[...]
    b = _sel_dot_l(tril_ref[...], la)
    bl = _sel_dot_l(ones_ref[...], la)
    q_dec = q_ref[...] * jnp.exp(b)
    q_dec_bf = q_dec.astype(BF16)
    k = k_ref[...]
    kd_bf = (k * jnp.exp(-b)).astype(BF16)
    kl_t = (k * jnp.exp(bl - b)).T
    v_bf = v_ref[...].astype(BF16)
    bl_col = _sel_dot_r(la.T, onesT_ref[...])

    r = lax.broadcasted_iota(jnp.int32, (n_tok, n_tok), 0)
    c = lax.broadcasted_iota(jnp.int32, (n_tok, n_tok), 1)
    causal = (r // GLA_CHUNK == c // GLA_CHUNK) & (c <= r)
    lane_q = lax.broadcasted_iota(jnp.int32, (n_tok, GLA_KW), 1) // DK_GLA
    zero_q = jnp.zeros_like(q_dec_bf)
    for h in range(H_GLA):
        att = _dot_nt(jnp.where(lane_q == h, q_dec_bf, zero_q), kd_bf)
        att = jnp.where(causal, att, 0.0)
        o_ref[:, h * DV_GLA:(h + 1) * DV_GLA] = _dot(att.astype(BF16), v_bf[:, h * DV_GLA:(h + 1) * DV_GLA])

    row_head = lax.broadcasted_iota(jnp.int32, (GLA_KW, GLA_VW), 0) // DK_GLA
    lane_head = lax.broadcasted_iota(jnp.int32, (GLA_KW, GLA_VW), 1) // DV_GLA
    same_head = row_head == lane_head
    tok_chunk = lax.broadcasted_iota(jnp.int32, (GLA_KW, n_tok), 1) // GLA_CHUNK
    for n in range(n_chunks):
        rows = slice(n * GLA_CHUNK, (n + 1) * GLA_CHUNK)
        s_prev = s_sc[...]
        o_ref[rows, :] += _dot(q_dec_bf[rows], s_prev.astype(BF16))
        kv = _dot(jnp.where(tok_chunk == n, kl_t, 0.0).astype(BF16), v_bf)
        dec = jnp.exp(bl_col[:, n * LANES:(n + 1) * LANES])
        dec = jnp.concatenate([dec] * (GLA_VW // LANES), axis=1)
        s_sc[...] = jnp.where(same_head, dec * s_prev + kv, 0.0)

    @pl.when(g == pl.num_programs(1) - 1)
    def _():
        s = s_sc[...]
        for h in range(H_GLA):
            sfin_ref[0, h] = s[h * DK_GLA:(h + 1) * DK_GLA, h * DV_GLA:(h + 1) * DV_GLA]


def _gla_prompt(gq, gk, gv, la, batch, seq):
    grp = min(GLA_GROUP, seq)
    ng = seq // grp
    nch = grp // GLA_CHUNK
    idx = np.arange(grp)
    same = (idx[:, None] // GLA_CHUNK) == (idx[None, :] // GLA_CHUNK)
    tril = jnp.asarray((same & (idx[None, :] <= idx[:, None])).astype(np.float32), BF16)
    ones = jnp.asarray(same.astype(np.float32), BF16)
    ones_t = jnp.asarray((idx[:, None] // GLA_CHUNK == np.arange(nch * LANES)[None, :] // LANES).astype(np.float32),
                         BF16)
    row = lambda w: pl.BlockSpec((grp, w), lambda b, g: (b * ng + g, 0))
    return pl.pallas_call(
        _gla_prompt_kernel,
        grid=(batch, ng),
        in_specs=[row(GLA_KW), row(GLA_KW), row(GLA_VW), row(GLA_KW),
                  _full_spec(tril.shape), _full_spec(ones.shape), _full_spec(ones_t.shape)],
        out_specs=[row(GLA_VW), pl.BlockSpec((1, H_GLA, DK_GLA, DV_GLA), lambda b, g: (b, 0, 0, 0))],
        out_shape=[jax.ShapeDtypeStruct((batch * seq, GLA_VW), F32),
                   jax.ShapeDtypeStruct((batch, H_GLA, DK_GLA, DV_GLA), F32)],
        scratch_shapes=[pltpu.VMEM((GLA_KW, GLA_VW), F32)],
        compiler_params=_cparams("parallel", "arbitrary"),
        name="gla_prompt",
    )(gq, gk, gv, la, tril, ones, ones_t)


def _gla_sample_kernel(q_ref, k_ref, v_ref, la_ref, s0_ref, tril_ref, o_ref, s1_ref):
    n, ls, _ = q_ref.shape
    la = la_ref[...]
    tril = jnp.broadcast_to(tril_ref[...][None], (n, ls, ls))
    hi, mid, lo = _split3(la)
    bdot = lambda x: jnp.einsum('nts,nsd->ntd', tril, x, preferred_element_type=F32)
    b = (bdot(hi) + bdot(mid)) + bdot(lo)
    b_last = b[:, ls - 1:ls, :]
    q_dec = (q_ref[...] * jnp.exp(b)).astype(BF16)
    k = k_ref[...]
    kd = (k * jnp.exp(-b)).astype(BF16)
    kl = (k * jnp.exp(b_last - b)).astype(BF16)
    v = v_ref[...].astype(BF16)
    s0 = s0_ref[...]
    att = jnp.einsum('ncd,nsd->ncs', q_dec, kd, preferred_element_type=F32)
    r = lax.broadcasted_iota(jnp.int32, (n, ls, ls), 1)
    c = lax.broadcasted_iota(jnp.int32, (n, ls, ls), 2)
    att = jnp.where(c <= r, att, 0.0)
    o = jnp.einsum('ncs,nse->nce', att.astype(BF16), v, preferred_element_type=F32)
    o = o + jnp.einsum('ncd,nde->nce', q_dec, s0.astype(BF16), preferred_element_type=F32)
    o_ref[...] = o
    kv = jnp.einsum('nds,nse->nde', jnp.swapaxes(kl, 1, 2), v, preferred_element_type=F32)
    dec = jnp.swapaxes(jnp.exp(b_last), 1, 2)
    s1_ref[...] = dec * s0 + kv


def _gla_sample(gq, gk, gv, la, state):
    n, ls, _ = gq.shape
    nb = min(32, n)
    tril = jnp.asarray(np.tril(np.ones((ls, ls), np.float32)), BF16)
    blk = lambda a, c: pl.BlockSpec((nb, a, c), lambda i: (i, 0, 0))
    return pl.pallas_call(
        _gla_sample_kernel,
        grid=(n // nb,),
        in_specs=[blk(ls, DK_GLA), blk(ls, DK_GLA), blk(ls, DV_GLA), blk(ls, DK_GLA), blk(DK_GLA, DV_GLA),
                  _full_spec(tril.shape)],
        out_specs=[blk(ls, DV_GLA), blk(DK_GLA, DV_GLA)],
        out_shape=[jax.ShapeDtypeStruct((n, ls, DV_GLA), F32), jax.ShapeDtypeStruct((n, DK_GLA, DV_GLA), F32)],
        compiler_params=_cparams("parallel"),
        name="gla_sample",
    )(gq, gk, gv, la, state, tril)


def _mixproj_kernel(x_ref, fox_ref, gla_ref, gr_ref, gg_ref, wo_ref, gc_ref, wq_ref, x1_ref, q_ref):
    gla = gla_ref[...]
    gr = gr_ref[...]
    acc = _dot(fox_ref[...], wo_ref[0:FOX_W, :])
    for h in range(H_GLA):
        cols = slice(h * DV_GLA, (h + 1) * DV_GLA)
        gh = _rms(gla[:, cols], gg_ref[:, cols])
        grh = gr[:, cols]
        gh = gh * (grh * jax.nn.sigmoid(grh))
        acc = acc + _dot(gh.astype(BF16), wo_ref[FOX_W + h * DV_GLA:FOX_W + (h + 1) * DV_GLA, :])
    x1 = x_ref[...] + acc
    x1_ref[...] = x1
    h2 = _rms(x1, gc_ref[...]).astype(BF16)
    q_ref[...] = (_dot(h2, wq_ref[...]) * (DH_MEM ** -0.5)).astype(BF16)


def _mixproj(x2d, fox_o, gla_o, gr, g_gla, w_out_bf, g_cross, w_q_bf):
    t = x2d.shape[0]
    tt = min(TOK_TILE, t)
    row = lambda w: pl.BlockSpec((tt, w), lambda i: (i, 0))
    return pl.pallas_call(
        _mixproj_kernel,
        grid=(t // tt,),
        in_specs=[row(D_MODEL), row(FOX_W), row(GLA_VW), row(GLA_VW), _full_spec(g_gla.shape),
                  _full_spec(w_out_bf.shape), _full_spec(g_cross.shape), _full_spec(w_q_bf.shape)],
        out_specs=[row(D_MODEL), row(D_MODEL)],
        out_shape=[jax.ShapeDtypeStruct((t, D_MODEL), F32), jax.ShapeDtypeStruct((t, D_MODEL), BF16)],
        compiler_params=_cparams("parallel"),
        name="mixproj",
    )(x2d, fox_o, gla_o, gr, g_gla, w_out_bf, g_cross, w_q_bf)


def _memkv_kernel(m_ref, g_ref, wk_ref, wv_ref, k_ref, v_ref):
    m = _rms(m_ref[...], g_ref[...]).astype(BF16)
    k_ref[...] = _dot(m, wk_ref[...])
    v_ref[...] = _dot(m, wv_ref[...])


def _memkv(mem2d, g_mem, wk_bf, wv_bf):
    t = mem2d.shape[0]
    tt = min(TOK_TILE, t)
    row = pl.BlockSpec((tt, D_MODEL), lambda i: (i, 0))
    return pl.pallas_call(
        _memkv_kernel,
        grid=(t // tt,),
        in_specs=[row, _full_spec(g_mem.shape), _full_spec(wk_bf.shape), _full_spec(wv_bf.shape)],
        out_specs=[row, row],
        out_shape=[jax.ShapeDtypeStruct((t, D_MODEL), F32)] * 2,
        compiler_params=_cparams("parallel"),
        name="memkv",
    )(mem2d, g_mem, wk_bf, wv_bf)


def _cross_kernel(q_ref, mk_ref, mv_ref, o_ref, *, per_head_memory):
    q = q_ref[0]
    for h in range(H_MEM):
        cols = slice(h * DH_MEM, (h + 1) * DH_MEM)
        if per_head_memory:
            mk = mk_ref[0, 0, :, h, :].astype(BF16)
            mv = mv_ref[0, 0, :, h, :].astype(BF16)
        else:
            mk = mk_ref[0, :, cols].astype(BF16)
            mv = mv_ref[0, :, cols].astype(BF16)
        s = _dot_nt(q[:, cols], mk)
        e = jnp.exp(s - jnp.max(s, axis=-1, keepdims=True))
        p = e / jnp.sum(e, axis=-1, keepdims=True)
        o_ref[0, :, cols] = _dot(p.astype(BF16), mv).astype(o_ref.dtype)


def _cross(q3, mk, mv, layer=None):
    b, lq, _ = q3.shape
    tl = min(TOK_TILE, lq)
    if layer is None:
        mem_spec = pl.BlockSpec((1,) + mk.shape[1:], lambda i, j: (i, 0, 0))
    else:
        mem_spec = pl.BlockSpec((1, 1) + mk.shape[2:], lambda i, j: (layer, i, 0, 0, 0))
    return pl.pallas_call(
        functools.partial(_cross_kernel, per_head_memory=layer is not None),
        grid=(b, lq // tl),
        in_specs=[pl.BlockSpec((1, tl, D_MODEL), lambda i, j: (i, j, 0)), mem_spec, mem_spec],
        out_specs=pl.BlockSpec((1, tl, D_MODEL), lambda i, j: (i, j, 0)),
        out_shape=jax.ShapeDtypeStruct((b, lq, D_MODEL), BF16),
        compiler_params=_cparams("parallel", "parallel"),
        name="cross_attn",
    )(q3, mk, mv)


def _peer_prep_kernel(u_ref, v_ref, ub_ref, vt_ref):
    ub_ref[...] = u_ref[...].astype(BF16)
    vt_ref[...] = v_ref[...].T.astype(BF16)


def _peer_prep(u, v):
    ne = u.shape[0]
    eb = min(PREP_EBLK, ne)
    return pl.pallas_call(
        _peer_prep_kernel,
        grid=(ne // eb,),
        in_specs=[pl.BlockSpec((eb, D_MODEL), lambda i: (i, 0))] * 2,
        out_specs=[pl.BlockSpec((eb, D_MODEL), lambda i: (i, 0)), pl.BlockSpec((D_MODEL, eb), lambda i: (0, i))],
        out_shape=[jax.ShapeDtypeStruct((ne, D_MODEL), BF16), jax.ShapeDtypeStruct((D_MODEL, ne), BF16)],
        compiler_params=_cparams("parallel"),
        name="peer_prep",
    )(u, v)


def _rows_per_word():
    return 4 // jnp.dtype(BF16).itemsize


def _pack_rows(x):
    return pltpu.bitcast(x, jnp.uint32)


def _unpack_rows(x):
    return pltpu.bitcast(x, BF16)


def _top16_rows(s):
    nrow = s.shape[0]
    rowid = lax.broadcasted_iota(jnp.int32, s.shape, 0)
    rank = jnp.full(s.shape, float(PEER_TOPK), F32)
    work = s
    vals = []
    for r in range(PEER_TOPK):
        mx = jnp.max(work, axis=0, keepdims=True)
        idx = jnp.min(jnp.where(work == mx, rowid, nrow), axis=0, keepdims=True)
        sel = rowid == idx
        rank = jnp.where(sel, float(r), rank)
        work = jnp.where(sel, -jnp.inf, work)
        vals.append(mx)
    return rank, jnp.concatenate(vals, axis=0)


def _peer_route_kernel(x_ref, x1_ref, wo_ref, g_ref, wq_ref, sk_ref, ca_ref, cb_ref, cg_ref,
                       x2_ref, hn_ref, c1_ref, nrow_ref, p2_ref, rk2_ref, q_sc, s_sc):
    tt = x_ref.shape[0]
    nch = tt // LANES
    x2 = x1_ref[...] + _dot(x_ref[...], wo_ref[...])
    x2_ref[...] = x2
    hn = _rms(x2, g_ref[...]).astype(BF16)
    hn_ref[...] = hn
    q_sc[...] = _dot(hn, wq_ref[...]).astype(BF16)
    for hj in range(2 * H_PEER):
        st = _dot_nt(sk_ref[hj % 2], q_sc[:, hj * N_KEYS:(hj + 1) * N_KEYS])
        for ch in range(nch):
            s_sc[hj, ch] = st[:, ch * LANES:(ch + 1) * LANES]

    cand_row = lax.broadcasted_iota(jnp.int32, (N_CAND_PAD, LANES), 0)

    def unit(i, carry):
        h = i // nch
        ch = i % nch
        s1 = s_sc[2 * h, ch]
        s2 = s_sc[2 * h + 1, ch]
        rank1, v1 = _top16_rows(s1)
        rank2, v2 = _top16_rows(s2)
        cand = _sel_dot_l(ca_ref[...], v1) + _sel_dot_l(cb_ref[...], v2)
        cand = jnp.where(cand_row < N_CAND, cand, -jnp.inf)
        crank, _ = _top16_rows(cand)
        chosen = crank < float(PEER_TOPK)
        n_of_a = _dot(cg_ref[...], jnp.where(chosen, 1.0, 0.0).astype(BF16))
        z = jnp.sum(jnp.where(chosen, jnp.exp(cand - cand[0:1]), 0.0), axis=0, keepdims=True)
        nrow = jnp.zeros_like(rank1)
        for a in range(PEER_TOPK):
            nrow = jnp.where(rank1 == float(a), n_of_a[a:a + 1], nrow)
        c1_ref[h, ch] = jnp.exp(s1 - v1[0:1]) / z
        nrow_ref[h, ch] = nrow
        p2_ref[h, ch] = _pack_rows(jnp.exp(s2 - v2[0:1]).astype(BF16))
        rk2_ref[h, ch] = _pack_rows(rank2.astype(BF16))
        return carry

    lax.fori_loop(0, H_PEER * nch, unit, 0)


def _peer_route(oc2d, x1, w_o_bf, g_ffn, w_q_bf, sk_bf):
    t = x1.shape[0]
    tt = min(PEER_TOK, t)
    nch = tt // LANES
    pairs = [(a, b) for a in range(PEER_TOPK) for b in range(PEER_TOPK) if (a + 1) * (b + 1) <= PEER_TOPK]
    assert len(pairs) == N_CAND
    ca = np.zeros((N_CAND_PAD, PEER_TOPK), np.float32)
    cb = np.zeros((N_CAND_PAD, PEER_TOPK), np.float32)
    cg = np.zeros((PEER_TOPK, N_CAND_PAD), np.float32)
    for i, (a, b) in enumerate(pairs):
        ca[i, a] = 1.0
        cb[i, b] = 1.0
        cg[a, i] = 1.0
    ca, cb, cg = (jnp.asarray(m, BF16) for m in (ca, cb, cg))
    row = lambda w: pl.BlockSpec((tt, w), lambda i: (i, 0))
    krows = N_KEYS // _rows_per_word()
    sel = pl.BlockSpec((H_PEER, nch, N_KEYS, LANES), lambda i: (0, i, 0, 0))
    sel_shape = jax.ShapeDtypeStruct((H_PEER, t // LANES, N_KEYS, LANES), F32)
    packed = pl.BlockSpec((H_PEER, nch, krows, LANES), lambda i: (0, i, 0, 0))
    packed_shape = jax.ShapeDtypeStruct((H_PEER, t // LANES, krows, LANES), jnp.uint32)
    return pl.pallas_call(
        _peer_route_kernel,
        grid=(t // tt,),
        in_specs=[row(D_MODEL), row(D_MODEL), _full_spec(w_o_bf.shape), _full_spec(g_ffn.shape),
                  _full_spec(w_q_bf.shape), _full_spec(sk_bf.shape),
                  _full_spec(ca.shape), _full_spec(cb.shape), _full_spec(cg.shape)],
        out_specs=[row(D_MODEL), row(D_MODEL), sel, sel, packed, packed],
        out_shape=[jax.ShapeDtypeStruct((t, D_MODEL), F32), jax.ShapeDtypeStruct((t, D_MODEL), BF16),
                   sel_shape, sel_shape, packed_shape, packed_shape],
        scratch_shapes=[pltpu.VMEM((tt, 2 * H_PEER * N_KEYS), BF16),
                        pltpu.VMEM((2 * H_PEER, nch, N_KEYS, LANES), F32)],
        compiler_params=_cparams("parallel"),
        name="peer_route",
    )(oc2d, x1, w_o_bf, g_ffn, w_q_bf, sk_bf, ca, cb, cg)


def _peer_mix_kernel(x2_ref, hn_ref, u_ref, vt_ref, c1_ref, nrow_ref, p2_ref, rk2_ref, gf_ref,
                     y_ref, a_sc, w_sc, acc_sc, *, final_norm):
    e = pl.program_id(1)
    tt = hn_ref.shape[0]
    nch = tt // LANES
    n_i1 = u_ref.shape[0] // N_KEYS

    @pl.when(e == 0)
    def _():
        acc_sc[...] = jnp.zeros_like(acc_sc)

    krows = N_KEYS // _rows_per_word()
    acc = acc_sc[...]
    for sb in range(n_i1 // PEER_SUB_I1):
        i1_lo = sb * PEER_SUB_I1
        ex = slice(i1_lo * N_KEYS, (i1_lo + PEER_SUB_I1) * N_KEYS)
        exw = slice(i1_lo * krows, (i1_lo + PEER_SUB_I1) * krows)
        a_sc[exw, :] = _pack_rows(jax.nn.gelu(_dot_nt(u_ref[ex, :], hn_ref[...])).astype(BF16))
        for ch in range(nch):
            cols = slice(ch * LANES, (ch + 1) * LANES)
            for r in range(i1_lo, i1_lo + PEER_SUB_I1):
                w = jnp.zeros((N_KEYS, LANES), BF16)
                for h in range(H_PEER):
                    wide = lambda ref: jnp.broadcast_to(ref[h, ch, r:r + 1, :], (N_KEYS, LANES)).astype(BF16)
                    p2 = _unpack_rows(p2_ref[h, ch])
                    hit = _unpack_rows(rk2_ref[h, ch]) < wide(nrow_ref)
                    w = w + jnp.where(hit, p2, jnp.zeros_like(p2)) * wide(c1_ref)
                gelu_a = _unpack_rows(a_sc[r * krows:(r + 1) * krows, cols])
                w_sc[r * N_KEYS:(r + 1) * N_KEYS, cols] = w * gelu_a
        acc = acc + _dot(vt_ref[:, ex], w_sc[ex, :])
    acc_sc[...] = acc

    @pl.when(e == pl.num_programs(1) - 1)
    def _():
        y = x2_ref[...] + acc_sc[...].T
        y_ref[...] = _rms(y, gf_ref[...]) if final_norm else y


def _peer_mix(x2, hn, u_bf, vt_bf, c1, nrow, p2, rk2, g_final, final_norm):
    t = x2.shape[0]
    tt = min(PEER_TOK, t)
    nch = tt // LANES
    ne = u_bf.shape[0]
    eb = min(PEER_EBLK, ne)
    n_i1 = eb // N_KEYS
    row = lambda w: pl.BlockSpec((tt, w), lambda i, e: (i, 0))
    part = pl.BlockSpec((H_PEER, nch, n_i1, LANES), lambda i, e: (0, i, e, 0))
    full = pl.BlockSpec((H_PEER, nch, N_KEYS // _rows_per_word(), LANES), lambda i, e: (0, i, 0, 0))
    return pl.pallas_call(
        functools.partial(_peer_mix_kernel, final_norm=final_norm),
        grid=(t // tt, ne // eb),
        in_specs=[row(D_MODEL), row(D_MODEL),
                  pl.BlockSpec((eb, D_MODEL), lambda i, e: (e, 0)),
                  pl.BlockSpec((D_MODEL, eb), lambda i, e: (0, e)),
                  part, part, full, full, pl.BlockSpec(g_final.shape, lambda i, e: (0, 0))],
        out_specs=row(D_MODEL),
        out_shape=jax.ShapeDtypeStruct((t, D_MODEL), F32),
        scratch_shapes=[pltpu.VMEM((eb // _rows_per_word(), tt), jnp.uint32), pltpu.VMEM((eb, tt), BF16),
                        pltpu.VMEM((D_MODEL, tt), F32)],
        compiler_params=_cparams("parallel", "arbitrary"),
        name="peer_mix",
    )(x2, hn, u_bf, vt_bf, c1, nrow, p2, rk2, g_final)


def _prep_in_weights(w_in, b_fox_f, w_gate2, b_gate):
    o = np.cumsum((0,) + IN_SIZES)
    seg = lambda i: w_in[:, int(o[i]):int(o[i + 1])]
    pad = jnp.zeros((D_MODEL, LANES - H_FOX - GLA_GATE_RANK), w_in.dtype)
    w_perm = jnp.concatenate([seg(0), seg(1), seg(2), seg(4), seg(5), seg(6), seg(8), seg(3), seg(7), pad],
                             axis=1).astype(BF16)
    b_small = jnp.zeros((1, LANES), F32).at[0, 0:H_FOX].set(b_fox_f)
    wg_pad = jnp.zeros((LANES, GLA_KW), F32).at[H_FOX:H_FOX + GLA_GATE_RANK].set(w_gate2).astype(BF16)
    return w_perm, b_small, wg_pad, b_gate.reshape(1, GLA_KW)


def kernel(x_prompt, x_sample, cache_fox_k, cache_fox_v, cache_fox_logf, state_gla, cache_mem_k, cache_mem_v, page_table, mem_prompt, g_mix, w_in, b_fox_f, w_gla_gate2, b_gla_gate, g_gla_out, w_out, g_cross, g_mem, w_mem_k, w_mem_v, w_cross_q, w_cross_o, g_ffn, peer_w_q, peer_subkeys, peer_u, peer_v, g_final):
    depth = w_in.shape[0]
    bp, lp, _ = x_prompt.shape
    bs, ls, _ = x_sample.shape
    n_mem = mem_prompt.shape[1]
    n_pool, page = cache_fox_k.shape[1:3]
    xp = x_prompt.reshape(bp * lp, D_MODEL)
    xs = x_sample.reshape(bs * ls, D_MODEL)
    vec = lambda a: a.reshape(1, -1)
    kcache_t = cache_fox_k.transpose(0, 1, 3, 4, 2).reshape(depth, n_pool, FOX_W, page)
    vcache_t = cache_fox_v.transpose(0, 1, 3, 4, 2).reshape(depth, n_pool, FOX_W, page)
    lfcache_t = cache_fox_logf.transpose(0, 1, 3, 2)
    outs = [[] for _ in range(10)]
    for l in range(depth):
        w_perm, b_small, wg_pad, b_gate = _prep_in_weights(w_in[l], b_fox_f[l], w_gla_gate2[l], b_gla_gate[l])
        w_out_bf = w_out[l].astype(BF16)
        w_cq_bf = w_cross_q[l].astype(BF16)
        w_co_bf = w_cross_o[l].astype(BF16)
        w_pq_bf = peer_w_q[l].astype(BF16)
        sk_bf = peer_subkeys[l].astype(BF16)
        u_bf, vt_bf = _peer_prep(peer_u[l], peer_v[l])
        last = l == depth - 1

        def tail(x, fox_o, gla_o, gr, mk, mv, nb, mem_layer):
            x1, qc = _mixproj(x, fox_o, gla_o, gr, vec(g_gla_out[l]), w_out_bf, vec(g_cross[l]), w_cq_bf)
            oc = _cross(qc.reshape(nb, -1, D_MODEL), mk, mv, mem_layer).reshape(-1, D_MODEL)
            x2, hn, c1, nrow, p2, rk2 = _peer_route(oc, x1, w_co_bf, vec(g_ffn[l]), w_pq_bf, sk_bf)
            return _peer_mix(x2, hn, u_bf, vt_bf, c1, nrow, p2, rk2, vec(g_final), last)

        fq_bf, fk, fv, lf, gq, gk, gv, la, gr = _inproj(xp, vec(g_mix[l]), w_perm, b_small, wg_pad, b_gate)
        qc, kc, vt, fk_t, fv_t, lf_t = _fox_prep(fq_bf, fk, fv, lf, bp, lp)
        fox_o = _fox_prompt(qc, kc, vt, bp, lp)
        gla_o, s_fin = _gla_prompt(gq, gk, gv, la, bp, lp)
        mk, mv = _memkv(mem_prompt.reshape(bp * n_mem, D_MODEL), vec(g_mem[l]), w_mem_k[l].astype(BF16),
                        w_mem_v[l].astype(BF16))
        mk3 = mk.reshape(bp, n_mem, D_MODEL)
        mv3 = mv.reshape(bp, n_mem, D_MODEL)
        xp = tail(xp, fox_o, gla_o, gr, mk3, mv3, bp, None)
        outs[0].append(fk_t.reshape(bp, H_FOX, DH_FOX, lp).transpose(0, 3, 1, 2))
        outs[1].append(fv_t.reshape(bp, H_FOX, DH_FOX, lp).transpose(0, 3, 1, 2))
        outs[2].append(lf_t.transpose(0, 2, 1))
        outs[3].append(s_fin)
        outs[4].append(mk.reshape(bp, n_mem, H_MEM, DH_MEM))
        outs[5].append(mv.reshape(bp, n_mem, H_MEM, DH_MEM))

        fq_bf, fk, fv, lf, gq, gk, gv, la, gr = _inproj(xs, vec(g_mix[l]), w_perm, b_small, wg_pad, b_gate)
        seqs = lambda a: a.reshape(bs, ls, FOX_W)
        fox_o = _fox_sample(seqs(fq_bf), seqs(fk), seqs(fv), lf.reshape(bs, ls, H_FOX).transpose(0, 2, 1),
                            kcache_t, vcache_t, lfcache_t, page_table, l).reshape(bs * ls, FOX_W)
        heads = lambda a, d: a.reshape(bs, ls, H_GLA, d).transpose(0, 2, 1, 3).reshape(bs * H_GLA, ls, d)
        gla_o, s_new = _gla_sample(heads(gq, DK_GLA), heads(gk, DK_GLA), heads(gv, DV_GLA), heads(la, DK_GLA),
                                   state_gla[l].reshape(bs * H_GLA, DK_GLA, DV_GLA))
        gla_o = gla_o.reshape(bs, H_GLA, ls, DV_GLA).transpose(0, 2, 1, 3).reshape(bs * ls, GLA_VW)
        xs = tail(xs, fox_o, gla_o, gr, cache_mem_k, cache_mem_v, bs, l)
        outs[6].append(fk.reshape(bs, ls, H_FOX, DH_FOX))
        outs[7].append(fv.reshape(bs, ls, H_FOX, DH_FOX))
        outs[8].append(lf.reshape(bs, ls, H_FOX))
        outs[9].append(s_new.reshape(bs, H_GLA, DK_GLA, DV_GLA))
    stacked = [jnp.stack(o, axis=0) for o in outs]
    return (xp.reshape(bp, lp, D_MODEL), xs.reshape(bs, ls, D_MODEL), *stacked)
```

```python
import functools

import numpy as np
import jax
import jax.numpy as jnp
from jax import lax
from jax.experimental import pallas as pl
from jax.experimental.pallas import tpu as pltpu

F32 = jnp.float32
BF16 = jnp.bfloat16

D_MODEL = 1024
H_FOX, DH_FOX = 8, 64
H_GLA, DK_GLA, DV_GLA = 4, 64, 128
GLA_GATE_RANK = 16
GLA_TAU = 16.0
GLA_CHUNK = 64
H_MEM = 4
DH_MEM = D_MODEL // H_MEM
N_KEYS = 128
H_PEER = 8
PEER_TOPK = 16
EPS = 1e-6
NEG_INF = -1e30
FOX_W = H_FOX * DH_FOX
GLA_KW = H_GLA * DK_GLA
GLA_VW = H_GLA * DV_GLA
IN_SIZES = (FOX_W, FOX_W, FOX_W, H_FOX, GLA_KW, GLA_KW, GLA_VW, GLA_GATE_RANK, GLA_VW)

LANES = 128
VMEM_LIMIT_BYTES = 56 * 2**20

TOK_TILE = 512
INPROJ_TILE = 256
FOX_TILE = 512
FOX_KTILE = 1024
GLA_GROUP = 512
PEER_TOK = 512
PEER_EBLK = 2048
PEER_SUB_I1 = 8
FOX_PAGES_PER_STEP = 8
PREP_EBLK = 512
N_CAND = 50
N_CAND_PAD = 64


def _cparams(*sem):
    return pltpu.CompilerParams(dimension_semantics=sem, vmem_limit_bytes=VMEM_LIMIT_BYTES)


def _dot(a, b):
    return jnp.dot(a, b, preferred_element_type=F32)


def _dot_nt(a, b):
    return lax.dot_general(a, b, (((1,), (1,)), ((), ())), preferred_element_type=F32)


def _split3(x):
    hi = x.astype(BF16)
    r = x - hi.astype(F32)
    mid = r.astype(BF16)
    lo = (r - mid.astype(F32)).astype(BF16)
    return hi, mid, lo


def _sel_dot_l(m01, x):
    hi, mid, lo = _split3(x)
    return (_dot(m01, hi) + _dot(m01, mid)) + _dot(m01, lo)


def _sel_dot_r(x, m01):
    hi, mid, lo = _split3(x)
    return (_dot(hi, m01) + _dot(mid, m01)) + _dot(lo, m01)


def _rms(x, g):
    return x * lax.rsqrt(jnp.mean(x * x, axis=-1, keepdims=True) + EPS) * g


def _logsig(x):
    return jnp.minimum(x, 0.0) - jnp.log1p(jnp.exp(-jnp.abs(x)))


def _full_spec(shape):
    nd = len(shape)
    return pl.BlockSpec(shape, lambda *_: (0,) * nd)


_C_FQ, _C_FK, _C_FV = 0, FOX_W, 2 * FOX_W
_C_GQ = 3 * FOX_W
_C_GK = _C_GQ + GLA_KW
_C_GV = _C_GK + GLA_KW
_C_GR = _C_GV + GLA_VW
_C_SM = _C_GR + GLA_VW
_C_END = _C_SM + LANES


def _inproj_kernel(x_ref, g_ref, w_ref, bsm_ref, wg_ref, bg_ref,
                   fq_ref, fk_ref, fv_ref, lf_ref, gq_ref, gk_ref, gv_ref, la_ref, gr_ref):
    h = _rms(x_ref[...], g_ref[...]).astype(BF16)

    def seg(lo, hi):
        return _dot(h, w_ref[:, lo:hi])

    fq_ref[...] = (seg(_C_FQ, _C_FK) * (DH_FOX ** -0.5)).astype(BF16)
    fk_ref[...] = seg(_C_FK, _C_FV)
    fv_ref[...] = seg(_C_FV, _C_GQ)
    gq_ref[...] = seg(_C_GQ, _C_GK) * (DK_GLA ** -0.5)
    gk_ref[...] = seg(_C_GK, _C_GV)
    gv_ref[...] = seg(_C_GV, _C_GR)
    gr_ref[...] = seg(_C_GR, _C_SM)
    ysm = seg(_C_SM, _C_END)
    lf_ref[...] = _logsig(ysm + bsm_ref[...])[:, 0:H_FOX]
    z = _dot(ysm.astype(BF16), wg_ref[...]) + bg_ref[...]
    la_ref[...] = _logsig(z) / GLA_TAU


def _inproj(x2d, g_mix, w_perm, b_small, wg_pad, b_gate):
    t = x2d.shape[0]
    tt = min(INPROJ_TILE, t)
    row = lambda w: pl.BlockSpec((tt, w), lambda i: (i, 0))
    outs = [(FOX_W, BF16), (FOX_W, F32), (FOX_W, F32), (H_FOX, F32),
            (GLA_KW, F32), (GLA_KW, F32), (GLA_VW, F32), (GLA_KW, F32), (GLA_VW, F32)]
    return pl.pallas_call(
        _inproj_kernel,
        grid=(t // tt,),
        in_specs=[row(D_MODEL), _full_spec(g_mix.shape), _full_spec(w_perm.shape), _full_spec(b_small.shape),
                  _full_spec(wg_pad.shape), _full_spec(b_gate.shape)],
        out_specs=[row(w) for w, _ in outs],
        out_shape=[jax.ShapeDtypeStruct((t, w), dt) for w, dt in outs],
        compiler_params=_cparams("parallel"),
        name="inproj",
    )(x2d, g_mix, w_perm, b_small, wg_pad, b_gate)


_FOX_QW = 4 * LANES
_FOX_KW = 2 * LANES
_N_HP = FOX_W // LANES


def _fox_aug_tables():
    pq = np.zeros((3, H_FOX, _N_HP * 2 * LANES), np.float32)
    oq = np.zeros((1, _N_HP * 2 * LANES), np.float32)
    pk = np.zeros((3, H_FOX, _N_HP * LANES), np.float32)
    ok = np.zeros((1, _N_HP * LANES), np.float32)
    for hp in range(_N_HP):
        for half in range(2):
            head = 2 * hp + half
            qb = (hp * 2 + half) * LANES
            kb = hp * LANES
            for j in range(3):
                pq[j, head, qb + 6 * half + j] = 1.0
                oq[0, qb + 6 * half + 3 + j] = 1.0
                pk[j, head, kb + 6 * half + 3 + j] = 1.0
                ok[0, kb + 6 * half + j] = 1.0
    return jnp.asarray(pq, BF16), jnp.asarray(oq, F32), jnp.asarray(pk, BF16), jnp.asarray(ok, F32)


def _fox_prep_kernel(q_ref, k_ref, v_ref, lf_ref, tril_ref, pq_ref, oq_ref, pk_ref, ok_ref, eye_ref,
                     qc_ref, kc_ref, vt_ref, fkt_ref, fvt_ref, lft_ref, carry_sc):
    @pl.when(pl.program_id(1) == 0)
    def _():
        carry_sc[...] = jnp.zeros_like(carry_sc)

    lf_parts = _split3(lf_ref[...])
    eye = eye_ref[...]
    lft_ref[0] = (_dot_nt(eye, lf_parts[0]) + _dot_nt(eye, lf_parts[1])) + _dot_nt(eye, lf_parts[2])
    c = _sel_dot_l(tril_ref[...], lf_ref[...]) + carry_sc[0:1, 0:H_FOX]
    carry_sc[0:1, 0:H_FOX] = c[-1:, :]
    parts = _split3(c)
    aug_q = oq_ref[...]
    aug_k = ok_ref[...]
    for j in range(3):
        aug_q = aug_q + _dot(parts[j], pq_ref[j])
        aug_k = aug_k - _dot(parts[j], pk_ref[j])
    aug_q = aug_q.astype(BF16)
    aug_k = aug_k.astype(BF16)
    lane = lax.broadcasted_iota(jnp.int32, (q_ref.shape[0], LANES), 1)
    for hp in range(_N_HP):
        q = q_ref[:, hp * LANES:(hp + 1) * LANES]
        zero = jnp.zeros_like(q)
        base = hp * _FOX_QW
        qc_ref[:, base:base + LANES] = jnp.where(lane < DH_FOX, q, zero)
        qc_ref[:, base + LANES:base + 2 * LANES] = aug_q[:, (2 * hp) * LANES:(2 * hp + 1) * LANES]
        qc_ref[:, base + 2 * LANES:base + 3 * LANES] = jnp.where(lane >= DH_FOX, q, zero)
        qc_ref[:, base + 3 * LANES:base + 4 * LANES] = aug_q[:, (2 * hp + 1) * LANES:(2 * hp + 2) * LANES]
        kc_ref[:, hp * _FOX_KW:hp * _FOX_KW + LANES] = k_ref[:, hp * LANES:(hp + 1) * LANES].astype(BF16)
        kc_ref[:, hp * _FOX_KW + LANES:(hp + 1) * _FOX_KW] = aug_k[:, hp * LANES:(hp + 1) * LANES]
    v_t = v_ref[...].T
    vt_ref[0] = v_t.astype(BF16)
    fvt_ref[0] = v_t
    fkt_ref[0] = k_ref[...].T


def _fox_prep(fq_bf, fk, fv, lf, batch, seq):
    tt = min(FOX_TILE, seq)
    nt = seq // tt
    tril = jnp.asarray(np.tril(np.ones((tt, tt), np.float32)), BF16)
    consts = [tril, *_fox_aug_tables(), jnp.eye(H_FOX, dtype=BF16)]
    row = lambda w: pl.BlockSpec((tt, w), lambda b, i: (b * nt + i, 0))
    col = lambda h: pl.BlockSpec((1, h, tt), lambda b, i: (b, 0, i))
    return pl.pallas_call(
        _fox_prep_kernel,
        grid=(batch, nt),
        in_specs=[row(FOX_W), row(FOX_W), row(FOX_W), row(H_FOX)] + [_full_spec(a.shape) for a in consts],
        out_specs=[row(_N_HP * _FOX_QW), row(_N_HP * _FOX_KW), col(FOX_W), col(FOX_W), col(FOX_W), col(H_FOX)],
        out_shape=[jax.ShapeDtypeStruct((batch * seq, _N_HP * _FOX_QW), BF16),
                   jax.ShapeDtypeStruct((batch * seq, _N_HP * _FOX_KW), BF16),
                   jax.ShapeDtypeStruct((batch, FOX_W, seq), BF16),
                   jax.ShapeDtypeStruct((batch, FOX_W, seq), F32),
                   jax.ShapeDtypeStruct((batch, FOX_W, seq), F32),
                   jax.ShapeDtypeStruct((batch, H_FOX, seq), F32)],
        scratch_shapes=[pltpu.VMEM((8, LANES), F32)],
        compiler_params=_cparams("parallel", "arbitrary"),
        name="fox_prep",
    )(fq_bf, fk, fv, lf, *consts)


def _fox_prompt_kernel(qi_tab, ki_tab, q_ref, k_ref, vt_ref, o_ref,
                       m_sc, l_sc, acc_sc):
    t = pl.program_id(2)
    qi = qi_tab[t]
    ki = ki_tab[t]
    tq = q_ref.shape[0]
    tk = k_ref.shape[0]

    @pl.when(ki == 0)
    def _():
        m_sc[...] = jnp.full_like(m_sc, NEG_INF)
        l_sc[...] = jnp.zeros_like(l_sc)
        acc_sc[...] = jnp.zeros_like(acc_sc)

    def step(masked):
        kc = k_ref[...]
        vt = vt_ref[0]
        for hh in range(2):
            s = _dot_nt(kc, q_ref[:, hh * _FOX_KW:(hh + 1) * _FOX_KW])
            if masked:
                kpos = lax.broadcasted_iota(jnp.int32, (tk, tq), 0) + ki * tk
                qpos = lax.broadcasted_iota(jnp.int32, (tk, tq), 1) + qi * tq
                s = jnp.where(kpos <= qpos, s, NEG_INF)
            m_prev = m_sc[hh:hh + 1, :]
            m_new = jnp.maximum(m_prev, jnp.max(s, axis=0, keepdims=True))
            alpha = jnp.exp(m_prev - m_new)
            p = jnp.exp(s - m_new)
            l_sc[hh:hh + 1, :] = alpha * l_sc[hh:hh + 1, :] + jnp.sum(p, axis=0, keepdims=True)
            acc_sc[hh] = alpha * acc_sc[hh] + _dot(vt, p.astype(BF16))
            m_sc[hh:hh + 1, :] = m_new

    last = (qi * tq) // tk

    @pl.when(ki < last)
    def _():
        step(False)

    @pl.when(ki == last)
    def _():
        step(True)
        o0 = acc_sc[0] / l_sc[0:1, :]
        o1 = acc_sc[1] / l_sc[1:2, :]
        dim = lax.broadcasted_iota(jnp.int32, (LANES, tq), 0)
        o_ref[...] = jnp.where(dim < DH_FOX, o0, o1).T.astype(o_ref.dtype)


def _fox_prompt(qc, kc, vt, batch, seq):
    tq = min(FOX_TILE, seq)
    tk = min(FOX_KTILE, seq)
    nq = seq // tq
    nk = seq // tk
    pairs = [(i, j) for i in range(nq) for j in range((i * tq) // tk + 1)]
    qi_tab = jnp.asarray([p[0] for p in pairs], jnp.int32)
    ki_tab = jnp.asarray([p[1] for p in pairs], jnp.int32)
    grid_spec = pltpu.PrefetchScalarGridSpec(
        num_scalar_prefetch=2,
        grid=(batch, _N_HP, len(pairs)),
        in_specs=[
            pl.BlockSpec((tq, _FOX_QW), lambda b, h, t, qt, kt: (b * nq + qt[t], h)),
            pl.BlockSpec((tk, _FOX_KW), lambda b, h, t, qt, kt: (b * nk + kt[t], h)),
            pl.BlockSpec((1, LANES, tk), lambda b, h, t, qt, kt: (b, h, kt[t])),
        ],
        out_specs=pl.BlockSpec((tq, LANES), lambda b, h, t, qt, kt: (b * nq + qt[t], h)),
        scratch_shapes=[pltpu.VMEM((2, tq), F32), pltpu.VMEM((2, tq), F32), pltpu.VMEM((2, LANES, tq), F32)],
    )
    return pl.pallas_call(
        _fox_prompt_kernel,
        grid_spec=grid_spec,
        out_shape=jax.ShapeDtypeStruct((batch * seq, FOX_W), BF16),
        compiler_params=_cparams("parallel", "parallel", "arbitrary"),
        name="fox_prompt",
    )(qi_tab, ki_tab, qc, kc, vt)


def _fox_sample_kernel(pt_ref, q_ref, *refs, n_pp):
    kc, vc, lfc = refs[0:n_pp], refs[n_pp:2 * n_pp], refs[2 * n_pp:3 * n_pp]
    kn_ref, vn_ref, lfn_ref, tri_ref, trin_ref, o_ref, qbd_sc, m_sc, l_sc, acc_sc, carry_sc = refs[3 * n_pp:]
    step = pl.program_id(1)
    ls = q_ref.shape[1]
    nr = H_FOX * ls
    page = tri_ref.shape[0]
    row_head = lax.broadcasted_iota(jnp.int32, (nr, FOX_W), 0) // ls
    lane_head = lax.broadcasted_iota(jnp.int32, (nr, FOX_W), 1) // DH_FOX

    def rows_of_heads(x):
        return jnp.broadcast_to(x[:, None, :], (H_FOX, ls, x.shape[-1])).reshape(nr, x.shape[-1])

    cn_row = rows_of_heads(_sel_dot_r(lfn_ref[0], trin_ref[...]))
    tnew = lax.broadcasted_iota(jnp.int32, (nr, ls), 1)
    tq = lax.broadcasted_iota(jnp.int32, (nr, ls), 0) % ls
    cn_col = jnp.sum(jnp.where(tnew == tq, cn_row, 0.0), axis=-1, keepdims=True)

    @pl.when(step == 0)
    def _():
        qt = jnp.concatenate([q_ref[0]] * H_FOX, axis=0)
        qbd_sc[...] = jnp.where(row_head == lane_head, qt, jnp.zeros_like(qt))
        m_sc[...] = jnp.full_like(m_sc, NEG_INF)
        l_sc[...] = jnp.zeros_like(l_sc)
        acc_sc[...] = jnp.zeros_like(acc_sc)
        carry_sc[...] = jnp.zeros_like(carry_sc)

    def online(scores, weighted_values):
        m_prev = m_sc[...]
        m_new = m_prev
        for s in scores:
            m_new = jnp.maximum(m_new, jnp.max(s, axis=-1, keepdims=True))
        alpha = jnp.exp(m_prev - m_new)
        l_new = alpha * l_sc[...]
        acc = alpha * acc_sc[...]
        for s, pv in zip(scores, weighted_values):
            pr = jnp.exp(s - m_new)
            l_new = l_new + jnp.sum(pr, axis=-1, keepdims=True)
            acc = acc + pv(pr.astype(BF16))
        m_sc[...] = m_new
        l_sc[...] = l_new
        acc_sc[...] = acc

    c_loc = _sel_dot_r(jnp.concatenate([r[0, 0] for r in lfc], axis=0), tri_ref[...])
    carry = carry_sc[...][:, 0:1]
    qbd = qbd_sc[...]
    scores = []
    for j in range(n_pp):
        c_page = c_loc[j * H_FOX:(j + 1) * H_FOX] + carry
        carry = c_page[:, page - 1:page]
        s = _dot(qbd, kc[j][0, 0].astype(BF16))
        scores.append(s + cn_col - rows_of_heads(c_page))
    carry_sc[...] = jnp.broadcast_to(carry, carry_sc.shape)
    online(scores, [lambda p, j=j: _dot_nt(p, vc[j][0, 0].astype(BF16)) for j in range(n_pp)])

    @pl.when(step == pl.num_programs(1) - 1)
    def _():
        s_new = _dot_nt(qbd, kn_ref[0].astype(BF16))
        s_new = s_new + cn_col - cn_row - rows_of_heads(jnp.broadcast_to(carry, (H_FOX, LANES)))[:, 0:1]
        s_new = jnp.where(tnew <= tq, s_new, NEG_INF)
        online([s_new], [lambda p: _dot(p, vn_ref[0].astype(BF16))])
        o = acc_sc[...] / l_sc[...]
        o = jnp.where(row_head == lane_head, o, 0.0)
        out = o[0:ls]
        for h in range(1, H_FOX):
            out = out + o[h * ls:(h + 1) * ls]
        o_ref[0] = out.astype(o_ref.dtype)


def _fox_sample(fq_bf, fk, fv, lfn_t, kcache_t, vcache_t, lfcache_t, page_table, layer):
    bs, n_pages = page_table.shape
    n_pp = min(FOX_PAGES_PER_STEP, n_pages)
    assert n_pages % n_pp == 0
    page = kcache_t.shape[3]
    ls = fq_bf.shape[1]
    nr = H_FOX * ls
    tri = jnp.asarray(np.triu(np.ones((page, page), np.float32)), BF16)
    trin = jnp.asarray(np.triu(np.ones((ls, ls), np.float32)), BF16)
    page_spec = lambda j, rows: pl.BlockSpec(
        (1, 1, rows, page), lambda b, s, pt: (layer, pt[b * n_pages + s * n_pp + j], 0, 0))
    per_seq = lambda *shape: pl.BlockSpec((1,) + shape, lambda b, s, pt: (b,) + (0,) * len(shape))
    const = lambda a: pl.BlockSpec(a.shape, lambda b, s, pt: (0,) * a.ndim)
    grid_spec = pltpu.PrefetchScalarGridSpec(
        num_scalar_prefetch=1,
        grid=(bs, n_pages // n_pp),
        in_specs=([per_seq(ls, FOX_W)] + [page_spec(j, FOX_W) for j in range(n_pp)] * 2
                  + [page_spec(j, H_FOX) for j in range(n_pp)]
                  + [per_seq(ls, FOX_W), per_seq(ls, FOX_W), per_seq(H_FOX, ls), const(tri), const(trin)]),
        out_specs=per_seq(ls, FOX_W),
        scratch_shapes=[pltpu.VMEM((nr, FOX_W), BF16), pltpu.VMEM((nr, 1), F32), pltpu.VMEM((nr, 1), F32),
                        pltpu.VMEM((nr, FOX_W), F32), pltpu.VMEM((H_FOX, LANES), F32)],
    )
    return pl.pallas_call(
        functools.partial(_fox_sample_kernel, n_pp=n_pp),
        grid_spec=grid_spec,
        out_shape=jax.ShapeDtypeStruct((bs, ls, FOX_W), BF16),
        compiler_params=_cparams("parallel", "arbitrary"),
        name="fox_sample",
    )(page_table.reshape(-1), fq_bf, *([kcache_t] * n_pp), *([vcache_t] * n_pp), *([lfcache_t] * n_pp),
      fk, fv, lfn_t, tri, trin)


def _gla_prompt_kernel(q_ref, k_ref, v_ref, la_ref, tril_ref, ones_ref, onesT_ref,
                       o_ref, sfin_ref, s_sc):
    g = pl.program_id(1)
    n_tok = q_ref.shape[0]
    n_chunks = n_tok // GLA_CHUNK

    @pl.when(g == 0)
    def _():
        s_sc[...] = jnp.zeros_like(s_sc)

    la = la_ref[...]
    b = _sel_dot_l(tril_ref[...], la)
    bl = _sel_dot_l(ones_ref[...], la)
    q_dec = q_ref[...] * jnp.exp(b)
    q_dec_bf = q_dec.astype(BF16)
    k = k_ref[...]
    kd_bf = (k * jnp.exp(-b)).astype(BF16)
    kl_t = (k * jnp.exp(bl - b)).T
    v_bf = v_ref[...].astype(BF16)
    bl_col = _sel_dot_r(la.T, onesT_ref[...])

    r = lax.broadcasted_iota(jnp.int32, (n_tok, n_tok), 0)
    c = lax.broadcasted_iota(jnp.int32, (n_tok, n_tok), 1)
    causal = (r // GLA_CHUNK == c // GLA_CHUNK) & (c <= r)
    lane_q = lax.broadcasted_iota(jnp.int32, (n_tok, GLA_KW), 1) // DK_GLA
    zero_q = jnp.zeros_like(q_dec_bf)
    for h in range(H_GLA):
        att = _dot_nt(jnp.where(lane_q == h, q_dec_bf, zero_q), kd_bf)
        att = jnp.where(causal, att, 0.0)
        o_ref[:, h * DV_GLA:(h + 1) * DV_GLA] = _dot(att.astype(BF16), v_bf[:, h * DV_GLA:(h + 1) * DV_GLA])

    row_head = lax.broadcasted_iota(jnp.int32, (GLA_KW, GLA_VW), 0) // DK_GLA
    lane_head = lax.broadcasted_iota(jnp.int32, (GLA_KW, GLA_VW), 1) // DV_GLA
    same_head = row_head == lane_head
    tok_chunk = lax.broadcasted_iota(jnp.int32, (GLA_KW, n_tok), 1) // GLA_CHUNK
    for n in range(n_chunks):
        rows = slice(n * GLA_CHUNK, (n + 1) * GLA_CHUNK)
        s_prev = s_sc[...]
        o_ref[rows, :] += _dot(q_dec_bf[rows], s_prev.astype(BF16))
        kv = _dot(jnp.where(tok_chunk == n, kl_t, 0.0).astype(BF16), v_bf)
        dec = jnp.exp(bl_col[:, n * LANES:(n + 1) * LANES])
        dec = jnp.concatenate([dec] * (GLA_VW // LANES), axis=1)
        s_sc[...] = jnp.where(same_head, dec * s_prev + kv, 0.0)

    @pl.when(g == pl.num_programs(1) - 1)
    def _():
        s = s_sc[...]
        for h in range(H_GLA):
            sfin_ref[0, h] = s[h * DK_GLA:(h + 1) * DK_GLA, h * DV_GLA:(h + 1) * DV_GLA]


def _gla_prompt(gq, gk, gv, la, batch, seq):
    grp = min(GLA_GROUP, seq)
    ng = seq // grp
    nch = grp // GLA_CHUNK
    idx = np.arange(grp)
    same = (idx[:, None] // GLA_CHUNK) == (idx[None, :] // GLA_CHUNK)
    tril = jnp.asarray((same & (idx[None, :] <= idx[:, None])).astype(np.float32), BF16)
    ones = jnp.asarray(same.astype(np.float32), BF16)
    ones_t = jnp.asarray((idx[:, None] // GLA_CHUNK == np.arange(nch * LANES)[None, :] // LANES).astype(np.float32),
                         BF16)
    row = lambda w: pl.BlockSpec((grp, w), lambda b, g: (b * ng + g, 0))
    return pl.pallas_call(
        _gla_prompt_kernel,
        grid=(batch, ng),
        in_specs=[row(GLA_KW), row(GLA_KW), row(GLA_VW), row(GLA_KW),
                  _full_spec(tril.shape), _full_spec(ones.shape), _full_spec(ones_t.shape)],
        out_specs=[row(GLA_VW), pl.BlockSpec((1, H_GLA, DK_GLA, DV_GLA), lambda b, g: (b, 0, 0, 0))],
        out_shape=[jax.ShapeDtypeStruct((batch * seq, GLA_VW), F32),
                   jax.ShapeDtypeStruct((batch, H_GLA, DK_GLA, DV_GLA), F32)],
        scratch_shapes=[pltpu.VMEM((GLA_KW, GLA_VW), F32)],
        compiler_params=_cparams("parallel", "arbitrary"),
        name="gla_prompt",
    )(gq, gk, gv, la, tril, ones, ones_t)


def _gla_sample_kernel(q_ref, k_ref, v_ref, la_ref, s0_ref, tril_ref, o_ref, s1_ref):
    n, ls, _ = q_ref.shape
    la = la_ref[...]
    tril = jnp.broadcast_to(tril_ref[...][None], (n, ls, ls))
    hi, mid, lo = _split3(la)
    bdot = lambda x: jnp.einsum('nts,nsd->ntd', tril, x, preferred_element_type=F32)
    b = (bdot(hi) + bdot(mid)) + bdot(lo)
    b_last = b[:, ls - 1:ls, :]
    q_dec = (q_ref[...] * jnp.exp(b)).astype(BF16)
    k = k_ref[...]
    kd = (k * jnp.exp(-b)).astype(BF16)
    kl = (k * jnp.exp(b_last - b)).astype(BF16)
    v = v_ref[...].astype(BF16)
    s0 = s0_ref[...]
    att = jnp.einsum('ncd,nsd->ncs', q_dec, kd, preferred_element_type=F32)
    r = lax.broadcasted_iota(jnp.int32, (n, ls, ls), 1)
    c = lax.broadcasted_iota(jnp.int32, (n, ls, ls), 2)
    att = jnp.where(c <= r, att, 0.0)
    o = jnp.einsum('ncs,nse->nce', att.astype(BF16), v, preferred_element_type=F32)
    o = o + jnp.einsum('ncd,nde->nce', q_dec, s0.astype(BF16), preferred_element_type=F32)
    o_ref[...] = o
    kv = jnp.einsum('nds,nse->nde', jnp.swapaxes(kl, 1, 2), v, preferred_element_type=F32)
    dec = jnp.swapaxes(jnp.exp(b_last), 1, 2)
    s1_ref[...] = dec * s0 + kv


def _gla_sample(gq, gk, gv, la, state):
    n, ls, _ = gq.shape
    nb = min(32, n)
    tril = jnp.asarray(np.tril(np.ones((ls, ls), np.float32)), BF16)
    blk = lambda a, c: pl.BlockSpec((nb, a, c), lambda i: (i, 0, 0))
    return pl.pallas_call(
        _gla_sample_kernel,
        grid=(n // nb,),
        in_specs=[blk(ls, DK_GLA), blk(ls, DK_GLA), blk(ls, DV_GLA), blk(ls, DK_GLA), blk(DK_GLA, DV_GLA),
                  _full_spec(tril.shape)],
        out_specs=[blk(ls, DV_GLA), blk(DK_GLA, DV_GLA)],
        out_shape=[jax.ShapeDtypeStruct((n, ls, DV_GLA), F32), jax.ShapeDtypeStruct((n, DK_GLA, DV_GLA), F32)],
        compiler_params=_cparams("parallel"),
        name="gla_sample",
    )(gq, gk, gv, la, state, tril)


def _mixproj_kernel(x_ref, fox_ref, gla_ref, gr_ref, gg_ref, wo_ref, gc_ref, wq_ref, x1_ref, q_ref):
    gla = gla_ref[...]
    gr = gr_ref[...]
    acc = _dot(fox_ref[...], wo_ref[0:FOX_W, :])
    for h in range(H_GLA):
        cols = slice(h * DV_GLA, (h + 1) * DV_GLA)
        gh = _rms(gla[:, cols], gg_ref[:, cols])
        grh = gr[:, cols]
        gh = gh * (grh * jax.nn.sigmoid(grh))
        acc = acc + _dot(gh.astype(BF16), wo_ref[FOX_W + h * DV_GLA:FOX_W + (h + 1) * DV_GLA, :])
    x1 = x_ref[...] + acc
    x1_ref[...] = x1
    h2 = _rms(x1, gc_ref[...]).astype(BF16)
    q_ref[...] = (_dot(h2, wq_ref[...]) * (DH_MEM ** -0.5)).astype(BF16)


def _mixproj(x2d, fox_o, gla_o, gr, g_gla, w_out_bf, g_cross, w_q_bf):
    t = x2d.shape[0]
    tt = min(TOK_TILE, t)
    row = lambda w: pl.BlockSpec((tt, w), lambda i: (i, 0))
    return pl.pallas_call(
        _mixproj_kernel,
        grid=(t // tt,),
        in_specs=[row(D_MODEL), row(FOX_W), row(GLA_VW), row(GLA_VW), _full_spec(g_gla.shape),
                  _full_spec(w_out_bf.shape), _full_spec(g_cross.shape), _full_spec(w_q_bf.shape)],
        out_specs=[row(D_MODEL), row(D_MODEL)],
        out_shape=[jax.ShapeDtypeStruct((t, D_MODEL), F32), jax.ShapeDtypeStruct((t, D_MODEL), BF16)],
        compiler_params=_cparams("parallel"),
        name="mixproj",
    )(x2d, fox_o, gla_o, gr, g_gla, w_out_bf, g_cross, w_q_bf)


def _memkv_kernel(m_ref, g_ref, wk_ref, wv_ref, k_ref, v_ref):
    m = _rms(m_ref[...], g_ref[...]).astype(BF16)
    k_ref[...] = _dot(m, wk_ref[...])
    v_ref[...] = _dot(m, wv_ref[...])


def _memkv(mem2d, g_mem, wk_bf, wv_bf):
    t = mem2d.shape[0]
    tt = min(TOK_TILE, t)
    row = pl.BlockSpec((tt, D_MODEL), lambda i: (i, 0))
    return pl.pallas_call(
        _memkv_kernel,
        grid=(t // tt,),
        in_specs=[row, _full_spec(g_mem.shape), _full_spec(wk_bf.shape), _full_spec(wv_bf.shape)],
        out_specs=[row, row],
        out_shape=[jax.ShapeDtypeStruct((t, D_MODEL), F32)] * 2,
        compiler_params=_cparams("parallel"),
        name="memkv",
    )(mem2d, g_mem, wk_bf, wv_bf)


def _cross_kernel(q_ref, mk_ref, mv_ref, o_ref, *, per_head_memory):
    q = q_ref[0]
    for h in range(H_MEM):
        cols = slice(h * DH_MEM, (h + 1) * DH_MEM)
        if per_head_memory:
            mk = mk_ref[0, 0, :, h, :].astype(BF16)
            mv = mv_ref[0, 0, :, h, :].astype(BF16)
        else:
            mk = mk_ref[0, :, cols].astype(BF16)
            mv = mv_ref[0, :, cols].astype(BF16)
        s = _dot_nt(q[:, cols], mk)
        e = jnp.exp(s - jnp.max(s, axis=-1, keepdims=True))
        p = e / jnp.sum(e, axis=-1, keepdims=True)
        o_ref[0, :, cols] = _dot(p.astype(BF16), mv).astype(o_ref.dtype)


def _cross(q3, mk, mv, layer=None):
    b, lq, _ = q3.shape
    tl = min(TOK_TILE, lq)
    if layer is None:
        mem_spec = pl.BlockSpec((1,) + mk.shape[1:], lambda i, j: (i, 0, 0))
    else:
        mem_spec = pl.BlockSpec((1, 1) + mk.shape[2:], lambda i, j: (layer, i, 0, 0, 0))
    return pl.pallas_call(
        functools.partial(_cross_kernel, per_head_memory=layer is not None),
        grid=(b, lq // tl),
        in_specs=[pl.BlockSpec((1, tl, D_MODEL), lambda i, j: (i, j, 0)), mem_spec, mem_spec],
        out_specs=pl.BlockSpec((1, tl, D_MODEL), lambda i, j: (i, j, 0)),
        out_shape=jax.ShapeDtypeStruct((b, lq, D_MODEL), BF16),
        compiler_params=_cparams("parallel", "parallel"),
        name="cross_attn",
    )(q3, mk, mv)


def _peer_prep_kernel(u_ref, v_ref, ub_ref, vt_ref):
    ub_ref[...] = u_ref[...].astype(BF16)
    vt_ref[...] = v_ref[...].T.astype(BF16)


def _peer_prep(u, v):
    ne = u.shape[0]
    eb = min(PREP_EBLK, ne)
    return pl.pallas_call(
        _peer_prep_kernel,
        grid=(ne // eb,),
        in_specs=[pl.BlockSpec((eb, D_MODEL), lambda i: (i, 0))] * 2,
        out_specs=[pl.BlockSpec((eb, D_MODEL), lambda i: (i, 0)), pl.BlockSpec((D_MODEL, eb), lambda i: (0, i))],
        out_shape=[jax.ShapeDtypeStruct((ne, D_MODEL), BF16), jax.ShapeDtypeStruct((D_MODEL, ne), BF16)],
        compiler_params=_cparams("parallel"),
        name="peer_prep",
    )(u, v)


def _rows_per_word():
    return 4 // jnp.dtype(BF16).itemsize


def _pack_rows(x):
    return pltpu.bitcast(x, jnp.uint32)


def _unpack_rows(x):
    return pltpu.bitcast(x, BF16)


def _top16_rows(s, break_ties):
    nrow = s.shape[0]
    rowid = lax.broadcasted_iota(jnp.int32, s.shape, 0)
    rank = jnp.full(s.shape, float(PEER_TOPK), F32)
    work = s
    vals = []
    for r in range(PEER_TOPK):
        mx = jnp.max(work, axis=0, keepdims=True)
        if break_ties:
            sel = rowid == jnp.min(jnp.where(work == mx, rowid, nrow), axis=0, keepdims=True)
        else:
            sel = work == mx
        rank = jnp.where(sel, float(r), rank)
        work = jnp.where(sel, -jnp.inf, work)
        vals.append(mx)
    return rank, jnp.concatenate(vals, axis=0)


def _ranked_16(rank):
    n = jnp.sum(jnp.where(rank < float(PEER_TOPK), 1.0, 0.0), axis=0, keepdims=True)
    return jnp.where(n == float(PEER_TOPK), 1.0, 0.0)


def _peer_route_kernel(x_ref, x1_ref, wo_ref, g_ref, wq_ref, sk_ref, ca_ref, cb_ref, cg_ref,
                       x2_ref, hn_ref, c1_ref, nrow_ref, p2_ref, rk2_ref, q_sc, s_sc):
    tt = x_ref.shape[0]
    nch = tt // LANES
    x2 = x1_ref[...] + _dot(x_ref[...], wo_ref[...])
    x2_ref[...] = x2
    hn = _rms(x2, g_ref[...]).astype(BF16)
    hn_ref[...] = hn
    q_sc[...] = _dot(hn, wq_ref[...]).astype(BF16)
    for hj in range(2 * H_PEER):
        st = _dot_nt(sk_ref[hj % 2], q_sc[:, hj * N_KEYS:(hj + 1) * N_KEYS])
        for ch in range(nch):
            s_sc[hj, ch] = st[:, ch * LANES:(ch + 1) * LANES]

    cand_row = lax.broadcasted_iota(jnp.int32, (N_CAND_PAD, LANES), 0)

    def unit(i, carry):
        h = i // nch
        ch = i % nch
        s1 = s_sc[2 * h, ch]
        s2 = s_sc[2 * h + 1, ch]

        def route(break_ties):
            rank1, v1 = _top16_rows(s1, break_ties)
            rank2, v2 = _top16_rows(s2, break_ties)
            cand = _sel_dot_l(ca_ref[...], v1) + _sel_dot_l(cb_ref[...], v2)
            cand = jnp.where(cand_row < N_CAND, cand, -jnp.inf)
            crank, _ = _top16_rows(cand, break_ties)
            chosen = crank < float(PEER_TOPK)
            n_of_a = _dot(cg_ref[...], jnp.where(chosen, 1.0, 0.0).astype(BF16))
            z = jnp.sum(jnp.where(chosen, jnp.exp(cand - cand[0:1]), 0.0), axis=0, keepdims=True)
            nrow = jnp.zeros_like(rank1)
            for a in range(PEER_TOPK):
                nrow = jnp.where(rank1 == float(a), n_of_a[a:a + 1], nrow)
            c1_ref[h, ch] = jnp.exp(s1 - v1[0:1]) / z
            nrow_ref[h, ch] = nrow
            p2_ref[h, ch] = _pack_rows(jnp.exp(s2 - v2[0:1]).astype(BF16))
            rk2_ref[h, ch] = _pack_rows(rank2.astype(BF16))
            return _ranked_16(rank1) * _ranked_16(rank2) * _ranked_16(crank)

        tie_free = route(False)

        @pl.when(jnp.min(tie_free) < 0.5)
        def _():
            route(True)

        return carry

    lax.fori_loop(0, H_PEER * nch, unit, 0)


def _peer_route(oc2d, x1, w_o_bf, g_ffn, w_q_bf, sk_bf):
    t = x1.shape[0]
    tt = min(PEER_TOK, t)
    nch = tt // LANES
    pairs = [(a, b) for a in range(PEER_TOPK) for b in range(PEER_TOPK) if (a + 1) * (b + 1) <= PEER_TOPK]
    assert len(pairs) == N_CAND
    ca = np.zeros((N_CAND_PAD, PEER_TOPK), np.float32)
    cb = np.zeros((N_CAND_PAD, PEER_TOPK), np.float32)
    cg = np.zeros((PEER_TOPK, N_CAND_PAD), np.float32)
    for i, (a, b) in enumerate(pairs):
        ca[i, a] = 1.0
        cb[i, b] = 1.0
        cg[a, i] = 1.0
    ca, cb, cg = (jnp.asarray(m, BF16) for m in (ca, cb, cg))
    row = lambda w: pl.BlockSpec((tt, w), lambda i: (i, 0))
    krows = N_KEYS // _rows_per_word()
    sel = pl.BlockSpec((H_PEER, nch, N_KEYS, LANES), lambda i: (0, i, 0, 0))
    sel_shape = jax.ShapeDtypeStruct((H_PEER, t // LANES, N_KEYS, LANES), F32)
    packed = pl.BlockSpec((H_PEER, nch, krows, LANES), lambda i: (0, i, 0, 0))
    packed_shape = jax.ShapeDtypeStruct((H_PEER, t // LANES, krows, LANES), jnp.uint32)
    return pl.pallas_call(
        _peer_route_kernel,
        grid=(t // tt,),
        in_specs=[row(D_MODEL), row(D_MODEL), _full_spec(w_o_bf.shape), _full_spec(g_ffn.shape),
                  _full_spec(w_q_bf.shape), _full_spec(sk_bf.shape),
                  _full_spec(ca.shape), _full_spec(cb.shape), _full_spec(cg.shape)],
        out_specs=[row(D_MODEL), row(D_MODEL), sel, sel, packed, packed],
        out_shape=[jax.ShapeDtypeStruct((t, D_MODEL), F32), jax.ShapeDtypeStruct((t, D_MODEL), BF16),
                   sel_shape, sel_shape, packed_shape, packed_shape],
        scratch_shapes=[pltpu.VMEM((tt, 2 * H_PEER * N_KEYS), BF16),
                        pltpu.VMEM((2 * H_PEER, nch, N_KEYS, LANES), F32)],
        compiler_params=_cparams("parallel"),
        name="peer_route",
    )(oc2d, x1, w_o_bf, g_ffn, w_q_bf, sk_bf, ca, cb, cg)


def _peer_mix_kernel(x2_ref, hn_ref, u_ref, vt_ref, c1_ref, nrow_ref, p2_ref, rk2_ref, gf_ref,
                     y_ref, a_sc, w_sc, acc_sc, *, final_norm):
    e = pl.program_id(1)
    tt = hn_ref.shape[0]
    nch = tt // LANES
    n_i1 = u_ref.shape[0] // N_KEYS

    @pl.when(e == 0)
    def _():
        acc_sc[...] = jnp.zeros_like(acc_sc)

    krows = N_KEYS // _rows_per_word()
    acc = acc_sc[...]
    for sb in range(n_i1 // PEER_SUB_I1):
        i1_lo = sb * PEER_SUB_I1
        ex = slice(i1_lo * N_KEYS, (i1_lo + PEER_SUB_I1) * N_KEYS)
        exw = slice(i1_lo * krows, (i1_lo + PEER_SUB_I1) * krows)
        a_sc[exw, :] = _pack_rows(jax.nn.gelu(_dot_nt(u_ref[ex, :], hn_ref[...])).astype(BF16))
        for ch in range(nch):
            cols = slice(ch * LANES, (ch + 1) * LANES)
            for r in range(i1_lo, i1_lo + PEER_SUB_I1):
                w = jnp.zeros((N_KEYS, LANES), BF16)
                for h in range(H_PEER):
                    wide = lambda ref: jnp.broadcast_to(ref[h, ch, r:r + 1, :], (N_KEYS, LANES)).astype(BF16)
                    p2 = _unpack_rows(p2_ref[h, ch])
                    hit = _unpack_rows(rk2_ref[h, ch]) < wide(nrow_ref)
                    w = w + jnp.where(hit, p2, jnp.zeros_like(p2)) * wide(c1_ref)
                gelu_a = _unpack_rows(a_sc[r * krows:(r + 1) * krows, cols])
                w_sc[r * N_KEYS:(r + 1) * N_KEYS, cols] = w * gelu_a
        acc = acc + _dot(vt_ref[:, ex], w_sc[ex, :])
    acc_sc[...] = acc

    @pl.when(e == pl.num_programs(1) - 1)
    def _():
        y = x2_ref[...] + acc_sc[...].T
        y_ref[...] = _rms(y, gf_ref[...]) if final_norm else y


def _peer_mix(x2, hn, u_bf, vt_bf, c1, nrow, p2, rk2, g_final, final_norm):
    t = x2.shape[0]
    tt = min(PEER_TOK, t)
    nch = tt // LANES
    ne = u_bf.shape[0]
    eb = min(PEER_EBLK, ne)
    n_i1 = eb // N_KEYS
    row = lambda w: pl.BlockSpec((tt, w), lambda i, e: (i, 0))
    part = pl.BlockSpec((H_PEER, nch, n_i1, LANES), lambda i, e: (0, i, e, 0))
    full = pl.BlockSpec((H_PEER, nch, N_KEYS // _rows_per_word(), LANES), lambda i, e: (0, i, 0, 0))
    return pl.pallas_call(
        functools.partial(_peer_mix_kernel, final_norm=final_norm),
        grid=(t // tt, ne // eb),
        in_specs=[row(D_MODEL), row(D_MODEL),
                  pl.BlockSpec((eb, D_MODEL), lambda i, e: (e, 0)),
                  pl.BlockSpec((D_MODEL, eb), lambda i, e: (0, e)),
                  part, part, full, full, pl.BlockSpec(g_final.shape, lambda i, e: (0, 0))],
        out_specs=row(D_MODEL),
        out_shape=jax.ShapeDtypeStruct((t, D_MODEL), F32),
        scratch_shapes=[pltpu.VMEM((eb // _rows_per_word(), tt), jnp.uint32), pltpu.VMEM((eb, tt), BF16),
                        pltpu.VMEM((D_MODEL, tt), F32)],
        compiler_params=_cparams("parallel", "arbitrary"),
        name="peer_mix",
    )(x2, hn, u_bf, vt_bf, c1, nrow, p2, rk2, g_final)


def _prep_in_weights(w_in, b_fox_f, w_gate2, b_gate):
    o = np.cumsum((0,) + IN_SIZES)
    seg = lambda i: w_in[:, int(o[i]):int(o[i + 1])]
    pad = jnp.zeros((D_MODEL, LANES - H_FOX - GLA_GATE_RANK), w_in.dtype)
    w_perm = jnp.concatenate([seg(0), seg(1), seg(2), seg(4), seg(5), seg(6), seg(8), seg(3), seg(7), pad],
                             axis=1).astype(BF16)
    b_small = jnp.zeros((1, LANES), F32).at[0, 0:H_FOX].set(b_fox_f)
    wg_pad = jnp.zeros((LANES, GLA_KW), F32).at[H_FOX:H_FOX + GLA_GATE_RANK].set(w_gate2).astype(BF16)
    return w_perm, b_small, wg_pad, b_gate.reshape(1, GLA_KW)


def kernel(x_prompt, x_sample, cache_fox_k, cache_fox_v, cache_fox_logf, state_gla, cache_mem_k, cache_mem_v, page_table, mem_prompt, g_mix, w_in, b_fox_f, w_gla_gate2, b_gla_gate, g_gla_out, w_out, g_cross, g_mem, w_mem_k, w_mem_v, w_cross_q, w_cross_o, g_ffn, peer_w_q, peer_subkeys, peer_u, peer_v, g_final):
    depth = w_in.shape[0]
    bp, lp, _ = x_prompt.shape
    bs, ls, _ = x_sample.shape
    n_mem = mem_prompt.shape[1]
    n_pool, page = cache_fox_k.shape[1:3]
    xp = x_prompt.reshape(bp * lp, D_MODEL)
    xs = x_sample.reshape(bs * ls, D_MODEL)
    vec = lambda a: a.reshape(1, -1)
    kcache_t = cache_fox_k.transpose(0, 1, 3, 4, 2).reshape(depth, n_pool, FOX_W, page)
    vcache_t = cache_fox_v.transpose(0, 1, 3, 4, 2).reshape(depth, n_pool, FOX_W, page)
    lfcache_t = cache_fox_logf.transpose(0, 1, 3, 2)
    outs = [[] for _ in range(10)]
    for l in range(depth):
        w_perm, b_small, wg_pad, b_gate = _prep_in_weights(w_in[l], b_fox_f[l], w_gla_gate2[l], b_gla_gate[l])
        w_out_bf = w_out[l].astype(BF16)
        w_cq_bf = w_cross_q[l].astype(BF16)
        w_co_bf = w_cross_o[l].astype(BF16)
        w_pq_bf = peer_w_q[l].astype(BF16)
        sk_bf = peer_subkeys[l].astype(BF16)
        u_bf, vt_bf = _peer_prep(peer_u[l], peer_v[l])
        last = l == depth - 1

        def tail(x, fox_o, gla_o, gr, mk, mv, nb, mem_layer):
            x1, qc = _mixproj(x, fox_o, gla_o, gr, vec(g_gla_out[l]), w_out_bf, vec(g_cross[l]), w_cq_bf)
            oc = _cross(qc.reshape(nb, -1, D_MODEL), mk, mv, mem_layer).reshape(-1, D_MODEL)
            x2, hn, c1, nrow, p2, rk2 = _peer_route(oc, x1, w_co_bf, vec(g_ffn[l]), w_pq_bf, sk_bf)
            return _peer_mix(x2, hn, u_bf, vt_bf, c1, nrow, p2, rk2, vec(g_final), last)

        fq_bf, fk, fv, lf, gq, gk, gv, la, gr = _inproj(xp, vec(g_mix[l]), w_perm, b_small, wg_pad, b_gate)
        qc, kc, vt, fk_t, fv_t, lf_t = _fox_prep(fq_bf, fk, fv, lf, bp, lp)
        fox_o = _fox_prompt(qc, kc, vt, bp, lp)
        gla_o, s_fin = _gla_prompt(gq, gk, gv, la, bp, lp)
        mk, mv = _memkv(mem_prompt.reshape(bp * n_mem, D_MODEL), vec(g_mem[l]), w_mem_k[l].astype(BF16),
                        w_mem_v[l].astype(BF16))
        mk3 = mk.reshape(bp, n_mem, D_MODEL)
        mv3 = mv.reshape(bp, n_mem, D_MODEL)
        xp = tail(xp, fox_o, gla_o, gr, mk3, mv3, bp, None)
        outs[0].append(fk_t.reshape(bp, H_FOX, DH_FOX, lp).transpose(0, 3, 1, 2))
        outs[1].append(fv_t.reshape(bp, H_FOX, DH_FOX, lp).transpose(0, 3, 1, 2))
        outs[2].append(lf_t.transpose(0, 2, 1))
        outs[3].append(s_fin)
        outs[4].append(mk.reshape(bp, n_mem, H_MEM, DH_MEM))
        outs[5].append(mv.reshape(bp, n_mem, H_MEM, DH_MEM))

        fq_bf, fk, fv, lf, gq, gk, gv, la, gr = _inproj(xs, vec(g_mix[l]), w_perm, b_small, wg_pad, b_gate)
        seqs = lambda a: a.reshape(bs, ls, FOX_W)
        fox_o = _fox_sample(seqs(fq_bf), seqs(fk), seqs(fv), lf.reshape(bs, ls, H_FOX).transpose(0, 2, 1),
                            kcache_t, vcache_t, lfcache_t, page_table, l).reshape(bs * ls, FOX_W)
        heads = lambda a, d: a.reshape(bs, ls, H_GLA, d).transpose(0, 2, 1, 3).reshape(bs * H_GLA, ls, d)
        gla_o, s_new = _gla_sample(heads(gq, DK_GLA), heads(gk, DK_GLA), heads(gv, DV_GLA), heads(la, DK_GLA),
                                   state_gla[l].reshape(bs * H_GLA, DK_GLA, DV_GLA))
        gla_o = gla_o.reshape(bs, H_GLA, ls, DV_GLA).transpose(0, 2, 1, 3).reshape(bs * ls, GLA_VW)
        xs = tail(xs, fox_o, gla_o, gr, cache_mem_k, cache_mem_v, bs, l)
        outs[6].append(fk.reshape(bs, ls, H_FOX, DH_FOX))
        outs[7].append(fv.reshape(bs, ls, H_FOX, DH_FOX))
        outs[8].append(lf.reshape(bs, ls, H_FOX))
        outs[9].append(s_new.reshape(bs, H_GLA, DK_GLA, DV_GLA))
    stacked = [jnp.stack(o, axis=0) for o in outs]
    return (xp.reshape(bp, lp, D_MODEL), xs.reshape(bs, ls, D_MODEL), *stacked)
```

```python
import functools

import numpy as np
import jax
import jax.numpy as jnp
from jax import lax
from jax.experimental import pallas as pl
from jax.experimental.pallas import tpu as pltpu

F32 = jnp.float32
BF16 = jnp.bfloat16

D_MODEL = 1024
H_FOX, DH_FOX = 8, 64
H_GLA, DK_GLA, DV_GLA = 4, 64, 128
GLA_GATE_RANK = 16
GLA_TAU = 16.0
GLA_CHUNK = 64
H_MEM = 4
DH_MEM = D_MODEL // H_MEM
N_KEYS = 128
H_PEER = 8
PEER_TOPK = 16
EPS = 1e-6
NEG_INF = -1e30
FOX_W = H_FOX * DH_FOX
GLA_KW = H_GLA * DK_GLA
GLA_VW = H_GLA * DV_GLA
IN_SIZES = (FOX_W, FOX_W, FOX_W, H_FOX, GLA_KW, GLA_KW, GLA_VW, GLA_GATE_RANK, GLA_VW)

LANES = 128
VMEM_LIMIT_BYTES = 56 * 2**20

TOK_TILE = 512
INPROJ_TILE = 256
FOX_TILE = 512
FOX_KTILE = 1024
GLA_GROUP = 512
PEER_TOK = 512
PEER_EBLK = 2048
PEER_SUB_I1 = 8
FOX_PAGES_PER_STEP = 16
PREP_EBLK = 512
N_CAND = 50
N_CAND_PAD = 64


def _cparams(*sem):
    return pltpu.CompilerParams(dimension_semantics=sem, vmem_limit_bytes=VMEM_LIMIT_BYTES)


def _dot(a, b):
    return jnp.dot(a, b, preferred_element_type=F32)


def _dot_nt(a, b):
    return lax.dot_general(a, b, (((1,), (1,)), ((), ())), preferred_element_type=F32)


def _split3(x):
    hi = x.astype(BF16)
    r = x - hi.astype(F32)
    mid = r.astype(BF16)
    lo = (r - mid.astype(F32)).astype(BF16)
    return hi, mid, lo


def _sel_dot_l(m01, x):
    hi, mid, lo = _split3(x)
    return (_dot(m01, hi) + _dot(m01, mid)) + _dot(m01, lo)


def _sel_dot_r(x, m01):
    hi, mid, lo = _split3(x)
    return (_dot(hi, m01) + _dot(mid, m01)) + _dot(lo, m01)


def _rms(x, g):
    return x * lax.rsqrt(jnp.mean(x * x, axis=-1, keepdims=True) + EPS) * g


def _logsig(x):
    return jnp.minimum(x, 0.0) - jnp.log1p(jnp.exp(-jnp.abs(x)))


def _full_spec(shape):
    nd = len(shape)
    return pl.BlockSpec(shape, lambda *_: (0,) * nd)


_C_FQ, _C_FK, _C_FV = 0, FOX_W, 2 * FOX_W
_C_GQ = 3 * FOX_W
_C_GK = _C_GQ + GLA_KW
_C_GV = _C_GK + GLA_KW
_C_GR = _C_GV + GLA_VW
_C_SM = _C_GR + GLA_VW
_C_END = _C_SM + LANES


def _inproj_kernel(x_ref, g_ref, w_ref, bsm_ref, wg_ref, bg_ref,
                   fq_ref, fk_ref, fv_ref, lf_ref, gq_ref, gk_ref, gv_ref, la_ref, gr_ref):
    h = _rms(x_ref[...], g_ref[...]).astype(BF16)

    def seg(lo, hi):
        return _dot(h, w_ref[:, lo:hi])

    fq_ref[...] = (seg(_C_FQ, _C_FK) * (DH_FOX ** -0.5)).astype(BF16)
    fk_ref[...] = seg(_C_FK, _C_FV)
    fv_ref[...] = seg(_C_FV, _C_GQ)
    gq_ref[...] = seg(_C_GQ, _C_GK) * (DK_GLA ** -0.5)
    gk_ref[...] = seg(_C_GK, _C_GV)
    gv_ref[...] = seg(_C_GV, _C_GR)
    gr_ref[...] = seg(_C_GR, _C_SM)
    ysm = seg(_C_SM, _C_END)
    lf_ref[...] = _logsig(ysm + bsm_ref[...])[:, 0:H_FOX]
    z = _dot(ysm.astype(BF16), wg_ref[...]) + bg_ref[...]
    la_ref[...] = _logsig(z) / GLA_TAU


def _inproj(x2d, g_mix, w_perm, b_small, wg_pad, b_gate):
    t = x2d.shape[0]
    tt = min(INPROJ_TILE, t)
    row = lambda w: pl.BlockSpec((tt, w), lambda i: (i, 0))
    outs = [(FOX_W, BF16), (FOX_W, F32), (FOX_W, F32), (H_FOX, F32),
            (GLA_KW, F32), (GLA_KW, F32), (GLA_VW, F32), (GLA_KW, F32), (GLA_VW, F32)]
    return pl.pallas_call(
        _inproj_kernel,
        grid=(t // tt,),
        in_specs=[row(D_MODEL), _full_spec(g_mix.shape), _full_spec(w_perm.shape), _full_spec(b_small.shape),
                  _full_spec(wg_pad.shape), _full_spec(b_gate.shape)],
        out_specs=[row(w) for w, _ in outs],
        out_shape=[jax.ShapeDtypeStruct((t, w), dt) for w, dt in outs],
        compiler_params=_cparams("parallel"),
        name="inproj",
    )(x2d, g_mix, w_perm, b_small, wg_pad, b_gate)


_FOX_QW = 4 * LANES
_FOX_KW = 2 * LANES
_N_HP = FOX_W // LANES


def _fox_aug_tables():
    pq = np.zeros((3, H_FOX, _N_HP * 2 * LANES), np.float32)
    oq = np.zeros((1, _N_HP * 2 * LANES), np.float32)
    pk = np.zeros((3, H_FOX, _N_HP * LANES), np.float32)
    ok = np.zeros((1, _N_HP * LANES), np.float32)
    for hp in range(_N_HP):
        for half in range(2):
            head = 2 * hp + half
            qb = (hp * 2 + half) * LANES
            kb = hp * LANES
            for j in range(3):
                pq[j, head, qb + 6 * half + j] = 1.0
                oq[0, qb + 6 * half + 3 + j] = 1.0
                pk[j, head, kb + 6 * half + 3 + j] = 1.0
                ok[0, kb + 6 * half + j] = 1.0
    return jnp.asarray(pq, BF16), jnp.asarray(oq, F32), jnp.asarray(pk, BF16), jnp.asarray(ok, F32)


def _fox_prep_kernel(q_ref, k_ref, v_ref, lf_ref, tril_ref, pq_ref, oq_ref, pk_ref, ok_ref, eye_ref,
                     qc_ref, kc_ref, vt_ref, fkt_ref, fvt_ref, lft_ref, carry_sc):
    @pl.when(pl.program_id(1) == 0)
    def _():
        carry_sc[...] = jnp.zeros_like(carry_sc)

    lf_parts = _split3(lf_ref[...])
    eye = eye_ref[...]
    lft_ref[0] = (_dot_nt(eye, lf_parts[0]) + _dot_nt(eye, lf_parts[1])) + _dot_nt(eye, lf_parts[2])
    c = _sel_dot_l(tril_ref[...], lf_ref[...]) + carry_sc[0:1, 0:H_FOX]
    carry_sc[0:1, 0:H_FOX] = c[-1:, :]
    parts = _split3(c)
    aug_q = oq_ref[...]
    aug_k = ok_ref[...]
    for j in range(3):
        aug_q = aug_q + _dot(parts[j], pq_ref[j])
        aug_k = aug_k - _dot(parts[j], pk_ref[j])
    aug_q = aug_q.astype(BF16)
    aug_k = aug_k.astype(BF16)
    lane = lax.broadcasted_iota(jnp.int32, (q_ref.shape[0], LANES), 1)
    for hp in range(_N_HP):
        q = q_ref[:, hp * LANES:(hp + 1) * LANES]
        zero = jnp.zeros_like(q)
        base = hp * _FOX_QW
        qc_ref[:, base:base + LANES] = jnp.where(lane < DH_FOX, q, zero)
        qc_ref[:, base + LANES:base + 2 * LANES] = aug_q[:, (2 * hp) * LANES:(2 * hp + 1) * LANES]
        qc_ref[:, base + 2 * LANES:base + 3 * LANES] = jnp.where(lane >= DH_FOX, q, zero)
        qc_ref[:, base + 3 * LANES:base + 4 * LANES] = aug_q[:, (2 * hp + 1) * LANES:(2 * hp + 2) * LANES]
        kc_ref[:, hp * _FOX_KW:hp * _FOX_KW + LANES] = k_ref[:, hp * LANES:(hp + 1) * LANES].astype(BF16)
        kc_ref[:, hp * _FOX_KW + LANES:(hp + 1) * _FOX_KW] = aug_k[:, hp * LANES:(hp + 1) * LANES]
    v_t = v_ref[...].T
    vt_ref[0] = v_t.astype(BF16)
    fvt_ref[0] = v_t
    fkt_ref[0] = k_ref[...].T


def _fox_prep(fq_bf, fk, fv, lf, batch, seq):
    tt = min(FOX_TILE, seq)
    nt = seq // tt
    tril = jnp.asarray(np.tril(np.ones((tt, tt), np.float32)), BF16)
    consts = [tril, *_fox_aug_tables(), jnp.eye(H_FOX, dtype=BF16)]
    row = lambda w: pl.BlockSpec((tt, w), lambda b, i: (b * nt + i, 0))
    col = lambda h: pl.BlockSpec((1, h, tt), lambda b, i: (b, 0, i))
    return pl.pallas_call(
        _fox_prep_kernel,
        grid=(batch, nt),
        in_specs=[row(FOX_W), row(FOX_W), row(FOX_W), row(H_FOX)] + [_full_spec(a.shape) for a in consts],
        out_specs=[row(_N_HP * _FOX_QW), row(_N_HP * _FOX_KW), col(FOX_W), col(FOX_W), col(FOX_W), col(H_FOX)],
        out_shape=[jax.ShapeDtypeStruct((batch * seq, _N_HP * _FOX_QW), BF16),
                   jax.ShapeDtypeStruct((batch * seq, _N_HP * _FOX_KW), BF16),
                   jax.ShapeDtypeStruct((batch, FOX_W, seq), BF16),
                   jax.ShapeDtypeStruct((batch, FOX_W, seq), F32),
                   jax.ShapeDtypeStruct((batch, FOX_W, seq), F32),
                   jax.ShapeDtypeStruct((batch, H_FOX, seq), F32)],
        scratch_shapes=[pltpu.VMEM((8, LANES), F32)],
        compiler_params=_cparams("parallel", "arbitrary"),
        name="fox_prep",
    )(fq_bf, fk, fv, lf, *consts)


def _fox_prompt_kernel(qi_tab, ki_tab, q_ref, k_ref, vt_ref, o_ref,
                       m_sc, l_sc, acc_sc):
    t = pl.program_id(2)
    qi = qi_tab[t]
    ki = ki_tab[t]
    tq = q_ref.shape[0]
    tk = k_ref.shape[0]

    @pl.when(ki == 0)
    def _():
        m_sc[...] = jnp.full_like(m_sc, NEG_INF)
        l_sc[...] = jnp.zeros_like(l_sc)
        acc_sc[...] = jnp.zeros_like(acc_sc)

    def step(masked):
        kc = k_ref[...]
        vt = vt_ref[0]
        for hh in range(2):
            s = _dot_nt(kc, q_ref[:, hh * _FOX_KW:(hh + 1) * _FOX_KW])
            if masked:
                kpos = lax.broadcasted_iota(jnp.int32, (tk, tq), 0) + ki * tk
                qpos = lax.broadcasted_iota(jnp.int32, (tk, tq), 1) + qi * tq
                s = jnp.where(kpos <= qpos, s, NEG_INF)
            m_prev = m_sc[hh:hh + 1, :]
            m_new = jnp.maximum(m_prev, jnp.max(s, axis=0, keepdims=True))
            alpha = jnp.exp(m_prev - m_new)
            p = jnp.exp(s - m_new)
            l_sc[hh:hh + 1, :] = alpha * l_sc[hh:hh + 1, :] + jnp.sum(p, axis=0, keepdims=True)
            acc_sc[hh] = alpha * acc_sc[hh] + _dot(vt, p.astype(BF16))
            m_sc[hh:hh + 1, :] = m_new

    last = (qi * tq) // tk

    @pl.when(ki < last)
    def _():
        step(False)

    @pl.when(ki == last)
    def _():
        step(True)
        o0 = acc_sc[0] / l_sc[0:1, :]
        o1 = acc_sc[1] / l_sc[1:2, :]
        dim = lax.broadcasted_iota(jnp.int32, (LANES, tq), 0)
        o_ref[...] = jnp.where(dim < DH_FOX, o0, o1).T.astype(o_ref.dtype)


def _fox_prompt(qc, kc, vt, batch, seq):
    tq = min(FOX_TILE, seq)
    tk = min(FOX_KTILE, seq)
    nq = seq // tq
    nk = seq // tk
    pairs = [(i, j) for i in range(nq) for j in range((i * tq) // tk + 1)]
    qi_tab = jnp.asarray([p[0] for p in pairs], jnp.int32)
    ki_tab = jnp.asarray([p[1] for p in pairs], jnp.int32)
    grid_spec = pltpu.PrefetchScalarGridSpec(
        num_scalar_prefetch=2,
        grid=(batch, _N_HP, len(pairs)),
        in_specs=[
            pl.BlockSpec((tq, _FOX_QW), lambda b, h, t, qt, kt: (b * nq + qt[t], h)),
            pl.BlockSpec((tk, _FOX_KW), lambda b, h, t, qt, kt: (b * nk + kt[t], h)),
            pl.BlockSpec((1, LANES, tk), lambda b, h, t, qt, kt: (b, h, kt[t])),
        ],
        out_specs=pl.BlockSpec((tq, LANES), lambda b, h, t, qt, kt: (b * nq + qt[t], h)),
        scratch_shapes=[pltpu.VMEM((2, tq), F32), pltpu.VMEM((2, tq), F32), pltpu.VMEM((2, LANES, tq), F32)],
    )
    return pl.pallas_call(
        _fox_prompt_kernel,
        grid_spec=grid_spec,
        out_shape=jax.ShapeDtypeStruct((batch * seq, FOX_W), BF16),
        compiler_params=_cparams("parallel", "parallel", "arbitrary"),
        name="fox_prompt",
    )(qi_tab, ki_tab, qc, kc, vt)


def _fox_sample_kernel(pt_ref, q_ref, *refs, n_pp):
    kc, vc, lfc = refs[0:n_pp], refs[n_pp:2 * n_pp], refs[2 * n_pp:3 * n_pp]
    (kn_ref, vn_ref, lfn_ref, tri_ref, trin_ref, o_ref,
     qbd_sc, m_sc, l_sc, acc_sc, carry_sc, kall_sc, vall_sc) = refs[3 * n_pp:]
    step = pl.program_id(1)
    ls = q_ref.shape[1]
    nr = H_FOX * ls
    page = tri_ref.shape[0]
    row_head = lax.broadcasted_iota(jnp.int32, (nr, FOX_W), 0) // ls
    lane_head = lax.broadcasted_iota(jnp.int32, (nr, FOX_W), 1) // DH_FOX

    def rows_of_heads(x):
        return jnp.broadcast_to(x[:, None, :], (H_FOX, ls, x.shape[-1])).reshape(nr, x.shape[-1])

    cn_row = rows_of_heads(_sel_dot_r(lfn_ref[0], trin_ref[...]))
    tnew = lax.broadcasted_iota(jnp.int32, (nr, ls), 1)
    tq = lax.broadcasted_iota(jnp.int32, (nr, ls), 0) % ls
    cn_col = jnp.sum(jnp.where(tnew == tq, cn_row, 0.0), axis=-1, keepdims=True)

    @pl.when(step == 0)
    def _():
        qt = jnp.concatenate([q_ref[0]] * H_FOX, axis=0)
        qbd_sc[...] = jnp.where(row_head == lane_head, qt, jnp.zeros_like(qt))
        m_sc[...] = jnp.full_like(m_sc, NEG_INF)
        l_sc[...] = jnp.zeros_like(l_sc)
        acc_sc[...] = jnp.zeros_like(acc_sc)
        carry_sc[...] = jnp.zeros_like(carry_sc)

    def online(scores, weighted_values):
        m_prev = m_sc[...]
        m_new = m_prev
        for s in scores:
            m_new = jnp.maximum(m_new, jnp.max(s, axis=-1, keepdims=True))
        alpha = jnp.exp(m_prev - m_new)
        l_new = alpha * l_sc[...]
        acc = alpha * acc_sc[...]
        for s, pv in zip(scores, weighted_values):
            pr = jnp.exp(s - m_new)
            l_new = l_new + jnp.sum(pr, axis=-1, keepdims=True)
            acc = acc + pv(pr.astype(BF16))
        m_sc[...] = m_new
        l_sc[...] = l_new
        acc_sc[...] = acc

    c_loc = _sel_dot_r(jnp.concatenate([r[0, 0] for r in lfc], axis=0), tri_ref[...])
    carry = carry_sc[...][:, 0:1]
    qbd = qbd_sc[...]
    c_pages = []
    for j in range(n_pp):
        c_page = c_loc[j * H_FOX:(j + 1) * H_FOX] + carry
        carry = c_page[:, page - 1:page]
        c_pages.append(rows_of_heads(c_page))
        kall_sc[:, j * page:(j + 1) * page] = kc[j][0, 0].astype(BF16)
        vall_sc[:, j * page:(j + 1) * page] = vc[j][0, 0].astype(BF16)
    carry_sc[...] = jnp.broadcast_to(carry, carry_sc.shape)
    s_all = _dot(qbd, kall_sc[...]) + cn_col - jnp.concatenate(c_pages, axis=1)
    online([s_all], [lambda p: _dot_nt(p, vall_sc[...])])

    @pl.when(step == pl.num_programs(1) - 1)
    def _():
        s_new = _dot_nt(qbd, kn_ref[0].astype(BF16))
        s_new = s_new + cn_col - cn_row - rows_of_heads(jnp.broadcast_to(carry, (H_FOX, LANES)))[:, 0:1]
        s_new = jnp.where(tnew <= tq, s_new, NEG_INF)
        online([s_new], [lambda p: _dot(p, vn_ref[0].astype(BF16))])
        o = acc_sc[...] / l_sc[...]
        o = jnp.where(row_head == lane_head, o, 0.0)
        out = o[0:ls]
        for h in range(1, H_FOX):
            out = out + o[h * ls:(h + 1) * ls]
        o_ref[0] = out.astype(o_ref.dtype)


def _fox_sample(fq_bf, fk, fv, lfn_t, kcache_t, vcache_t, lfcache_t, page_table, layer):
    bs, n_pages = page_table.shape
    n_pp = min(FOX_PAGES_PER_STEP, n_pages)
    assert n_pages % n_pp == 0
    page = kcache_t.shape[3]
    ls = fq_bf.shape[1]
    nr = H_FOX * ls
    tri = jnp.asarray(np.triu(np.ones((page, page), np.float32)), BF16)
    trin = jnp.asarray(np.triu(np.ones((ls, ls), np.float32)), BF16)
    page_spec = lambda j, rows: pl.BlockSpec(
        (1, 1, rows, page), lambda b, s, pt: (layer, pt[b * n_pages + s * n_pp + j], 0, 0))
    per_seq = lambda *shape: pl.BlockSpec((1,) + shape, lambda b, s, pt: (b,) + (0,) * len(shape))
    const = lambda a: pl.BlockSpec(a.shape, lambda b, s, pt: (0,) * a.ndim)
    grid_spec = pltpu.PrefetchScalarGridSpec(
        num_scalar_prefetch=1,
        grid=(bs, n_pages // n_pp),
        in_specs=([per_seq(ls, FOX_W)] + [page_spec(j, FOX_W) for j in range(n_pp)] * 2
                  + [page_spec(j, H_FOX) for j in range(n_pp)]
                  + [per_seq(ls, FOX_W), per_seq(ls, FOX_W), per_seq(H_FOX, ls), const(tri), const(trin)]),
        out_specs=per_seq(ls, FOX_W),
        scratch_shapes=[pltpu.VMEM((nr, FOX_W), BF16), pltpu.VMEM((nr, 1), F32), pltpu.VMEM((nr, 1), F32),
                        pltpu.VMEM((nr, FOX_W), F32), pltpu.VMEM((H_FOX, LANES), F32),
                        pltpu.VMEM((FOX_W, n_pp * page), BF16), pltpu.VMEM((FOX_W, n_pp * page), BF16)],
    )
    return pl.pallas_call(
        functools.partial(_fox_sample_kernel, n_pp=n_pp),
        grid_spec=grid_spec,
        out_shape=jax.ShapeDtypeStruct((bs, ls, FOX_W), BF16),
        compiler_params=_cparams("parallel", "arbitrary"),
        name="fox_sample",
    )(page_table.reshape(-1), fq_bf, *([kcache_t] * n_pp), *([vcache_t] * n_pp), *([lfcache_t] * n_pp),
      fk, fv, lfn_t, tri, trin)


def _gla_prompt_kernel(q_ref, k_ref, v_ref, la_ref, tril_ref, ones_ref, onesT_ref,
                       o_ref, sfin_ref, s_sc):
    g = pl.program_id(1)
    n_tok = q_ref.shape[0]
    n_chunks = n_tok // GLA_CHUNK

    @pl.when(g == 0)
    def _():
        s_sc[...] = jnp.zeros_like(s_sc)

    la = la_ref[...]
    b = _sel_dot_l(tril_ref[...], la)
    bl = _sel_dot_l(ones_ref[...], la)
    q_dec = q_ref[...] * jnp.exp(b)
    q_dec_bf = q_dec.astype(BF16)
    k = k_ref[...]
    kd_bf = (k * jnp.exp(-b)).astype(BF16)
    kl_t = (k * jnp.exp(bl - b)).T
    v_bf = v_ref[...].astype(BF16)
    bl_col = _sel_dot_r(la.T, onesT_ref[...])

    r = lax.broadcasted_iota(jnp.int32, (n_tok, n_tok), 0)
    c = lax.broadcasted_iota(jnp.int32, (n_tok, n_tok), 1)
    causal = (r // GLA_CHUNK == c // GLA_CHUNK) & (c <= r)
    lane_q = lax.broadcasted_iota(jnp.int32, (n_tok, GLA_KW), 1) // DK_GLA
    zero_q = jnp.zeros_like(q_dec_bf)
    for h in range(H_GLA):
        att = _dot_nt(jnp.where(lane_q == h, q_dec_bf, zero_q), kd_bf)
        att = jnp.where(causal, att, 0.0)
        o_ref[:, h * DV_GLA:(h + 1) * DV_GLA] = _dot(att.astype(BF16), v_bf[:, h * DV_GLA:(h + 1) * DV_GLA])

    row_head = lax.broadcasted_iota(jnp.int32, (GLA_KW, GLA_VW), 0) // DK_GLA
    lane_head = lax.broadcasted_iota(jnp.int32, (GLA_KW, GLA_VW), 1) // DV_GLA
    same_head = row_head == lane_head
    tok_chunk = lax.broadcasted_iota(jnp.int32, (GLA_KW, n_tok), 1) // GLA_CHUNK
    for n in range(n_chunks):
        rows = slice(n * GLA_CHUNK, (n + 1) * GLA_CHUNK)
        s_prev = s_sc[...]
        o_ref[rows, :] += _dot(q_dec_bf[rows], s_prev.astype(BF16))
        kv = _dot(jnp.where(tok_chunk == n, kl_t, 0.0).astype(BF16), v_bf)
        dec = jnp.exp(bl_col[:, n * LANES:(n + 1) * LANES])
        dec = jnp.concatenate([dec] * (GLA_VW // LANES), axis=1)
        s_sc[...] = jnp.where(same_head, dec * s_prev + kv, 0.0)

    @pl.when(g == pl.num_programs(1) - 1)
    def _():
        s = s_sc[...]
        for h in range(H_GLA):
            sfin_ref[0, h] = s[h * DK_GLA:(h + 1) * DK_GLA, h * DV_GLA:(h + 1) * DV_GLA]


def _gla_prompt(gq, gk, gv, la, batch, seq):
    grp = min(GLA_GROUP, seq)
    ng = seq // grp
    nch = grp // GLA_CHUNK
    idx = np.arange(grp)
    same = (idx[:, None] // GLA_CHUNK) == (idx[None, :] // GLA_CHUNK)
    tril = jnp.asarray((same & (idx[None, :] <= idx[:, None])).astype(np.float32), BF16)
    ones = jnp.asarray(same.astype(np.float32), BF16)
    ones_t = jnp.asarray((idx[:, None] // GLA_CHUNK == np.arange(nch * LANES)[None, :] // LANES).astype(np.float32),
                         BF16)
    row = lambda w: pl.BlockSpec((grp, w), lambda b, g: (b * ng + g, 0))
    return pl.pallas_call(
        _gla_prompt_kernel,
        grid=(batch, ng),
        in_specs=[row(GLA_KW), row(GLA_KW), row(GLA_VW), row(GLA_KW),
                  _full_spec(tril.shape), _full_spec(ones.shape), _full_spec(ones_t.shape)],
        out_specs=[row(GLA_VW), pl.BlockSpec((1, H_GLA, DK_GLA, DV_GLA), lambda b, g: (b, 0, 0, 0))],
        out_shape=[jax.ShapeDtypeStruct((batch * seq, GLA_VW), F32),
                   jax.ShapeDtypeStruct((batch, H_GLA, DK_GLA, DV_GLA), F32)],
        scratch_shapes=[pltpu.VMEM((GLA_KW, GLA_VW), F32)],
        compiler_params=_cparams("parallel", "arbitrary"),
        name="gla_prompt",
    )(gq, gk, gv, la, tril, ones, ones_t)


def _gla_sample_kernel(q_ref, k_ref, v_ref, la_ref, s0_ref, tril_ref, o_ref, s1_ref):
    n, ls, _ = q_ref.shape
    la = la_ref[...]
    tril = jnp.broadcast_to(tril_ref[...][None], (n, ls, ls))
    hi, mid, lo = _split3(la)
    bdot = lambda x: jnp.einsum('nts,nsd->ntd', tril, x, preferred_element_type=F32)
    b = (bdot(hi) + bdot(mid)) + bdot(lo)
    b_last = b[:, ls - 1:ls, :]
    q_dec = (q_ref[...] * jnp.exp(b)).astype(BF16)
    k = k_ref[...]
    kd = (k * jnp.exp(-b)).astype(BF16)
    kl = (k * jnp.exp(b_last - b)).astype(BF16)
    v = v_ref[...].astype(BF16)
    s0 = s0_ref[...]
    att = jnp.einsum('ncd,nsd->ncs', q_dec, kd, preferred_element_type=F32)
    r = lax.broadcasted_iota(jnp.int32, (n, ls, ls), 1)
    c = lax.broadcasted_iota(jnp.int32, (n, ls, ls), 2)
    att = jnp.where(c <= r, att, 0.0)
    o = jnp.einsum('ncs,nse->nce', att.astype(BF16), v, preferred_element_type=F32)
    o = o + jnp.einsum('ncd,nde->nce', q_dec, s0.astype(BF16), preferred_element_type=F32)
    o_ref[...] = o
    kv = jnp.einsum('nds,nse->nde', jnp.swapaxes(kl, 1, 2), v, preferred_element_type=F32)
    dec = jnp.swapaxes(jnp.exp(b_last), 1, 2)
    s1_ref[...] = dec * s0 + kv


def _gla_sample(gq, gk, gv, la, state):
    n, ls, _ = gq.shape
    nb = min(32, n)
    tril = jnp.asarray(np.tril(np.ones((ls, ls), np.float32)), BF16)
    blk = lambda a, c: pl.BlockSpec((nb, a, c), lambda i: (i, 0, 0))
    return pl.pallas_call(
        _gla_sample_kernel,
        grid=(n // nb,),
        in_specs=[blk(ls, DK_GLA), blk(ls, DK_GLA), blk(ls, DV_GLA), blk(ls, DK_GLA), blk(DK_GLA, DV_GLA),
                  _full_spec(tril.shape)],
        out_specs=[blk(ls, DV_GLA), blk(DK_GLA, DV_GLA)],
        out_shape=[jax.ShapeDtypeStruct((n, ls, DV_GLA), F32), jax.ShapeDtypeStruct((n, DK_GLA, DV_GLA), F32)],
        compiler_params=_cparams("parallel"),
        name="gla_sample",
    )(gq, gk, gv, la, state, tril)


def _mixproj_kernel(x_ref, fox_ref, gla_ref, gr_ref, gg_ref, wo_ref, gc_ref, wq_ref, x1_ref, q_ref):
    gla = gla_ref[...]
    gr = gr_ref[...]
    acc = _dot(fox_ref[...], wo_ref[0:FOX_W, :])
    for h in range(H_GLA):
        cols = slice(h * DV_GLA, (h + 1) * DV_GLA)
        gh = _rms(gla[:, cols], gg_ref[:, cols])
        grh = gr[:, cols]
        gh = gh * (grh * jax.nn.sigmoid(grh))
        acc = acc + _dot(gh.astype(BF16), wo_ref[FOX_W + h * DV_GLA:FOX_W + (h + 1) * DV_GLA, :])
    x1 = x_ref[...] + acc
    x1_ref[...] = x1
    h2 = _rms(x1, gc_ref[...]).astype(BF16)
    q_ref[...] = (_dot(h2, wq_ref[...]) * (DH_MEM ** -0.5)).astype(BF16)


def _mixproj(x2d, fox_o, gla_o, gr, g_gla, w_out_bf, g_cross, w_q_bf):
    t = x2d.shape[0]
    tt = min(TOK_TILE, t)
    row = lambda w: pl.BlockSpec((tt, w), lambda i: (i, 0))
    return pl.pallas_call(
        _mixproj_kernel,
        grid=(t // tt,),
        in_specs=[row(D_MODEL), row(FOX_W), row(GLA_VW), row(GLA_VW), _full_spec(g_gla.shape),
                  _full_spec(w_out_bf.shape), _full_spec(g_cross.shape), _full_spec(w_q_bf.shape)],
        out_specs=[row(D_MODEL), row(D_MODEL)],
        out_shape=[jax.ShapeDtypeStruct((t, D_MODEL), F32), jax.ShapeDtypeStruct((t, D_MODEL), BF16)],
        compiler_params=_cparams("parallel"),
        name="mixproj",
    )(x2d, fox_o, gla_o, gr, g_gla, w_out_bf, g_cross, w_q_bf)


def _memkv_kernel(m_ref, g_ref, wk_ref, wv_ref, k_ref, v_ref):
    m = _rms(m_ref[...], g_ref[...]).astype(BF16)
    k_ref[...] = _dot(m, wk_ref[...])
    v_ref[...] = _dot(m, wv_ref[...])


def _memkv(mem2d, g_mem, wk_bf, wv_bf):
    t = mem2d.shape[0]
    tt = min(TOK_TILE, t)
    row = pl.BlockSpec((tt, D_MODEL), lambda i: (i, 0))
    return pl.pallas_call(
        _memkv_kernel,
        grid=(t // tt,),
        in_specs=[row, _full_spec(g_mem.shape), _full_spec(wk_bf.shape), _full_spec(wv_bf.shape)],
        out_specs=[row, row],
        out_shape=[jax.ShapeDtypeStruct((t, D_MODEL), F32)] * 2,
        compiler_params=_cparams("parallel"),
        name="memkv",
    )(mem2d, g_mem, wk_bf, wv_bf)


def _cross_kernel(q_ref, mk_ref, mv_ref, o_ref, *, per_head_memory):
    q = q_ref[0]
    if per_head_memory:
        lq = q.shape[0]
        n_mem = mk_ref.shape[2]
        q_hm = jnp.concatenate([q[:, h * DH_MEM:(h + 1) * DH_MEM] for h in range(H_MEM)], axis=0)
        mk = mk_ref[0, 0].reshape(n_mem * H_MEM, DH_MEM).astype(BF16)
        mv = mv_ref[0, 0].reshape(n_mem * H_MEM, DH_MEM).astype(BF16)
        s = _dot_nt(q_hm, mk)
        row_head = lax.broadcasted_iota(jnp.int32, s.shape, 0) // lq
        key_head = lax.broadcasted_iota(jnp.int32, s.shape, 1) % H_MEM
        s = jnp.where(row_head == key_head, s, NEG_INF)
        e = jnp.exp(s - jnp.max(s, axis=-1, keepdims=True))
        p = e / jnp.sum(e, axis=-1, keepdims=True)
        o = _dot(p.astype(BF16), mv).astype(o_ref.dtype)
        for h in range(H_MEM):
            o_ref[0, :, h * DH_MEM:(h + 1) * DH_MEM] = o[h * lq:(h + 1) * lq]
        return
    for h in range(H_MEM):
        cols = slice(h * DH_MEM, (h + 1) * DH_MEM)
        mk = mk_ref[0, :, cols].astype(BF16)
        mv = mv_ref[0, :, cols].astype(BF16)
        s = _dot_nt(q[:, cols], mk)
        e = jnp.exp(s - jnp.max(s, axis=-1, keepdims=True))
        p = e / jnp.sum(e, axis=-1, keepdims=True)
        o_ref[0, :, cols] = _dot(p.astype(BF16), mv).astype(o_ref.dtype)


def _cross(q3, mk, mv, layer=None):
    b, lq, _ = q3.shape
    tl = min(TOK_TILE, lq)
    if layer is None:
        mem_spec = pl.BlockSpec((1,) + mk.shape[1:], lambda i, j: (i, 0, 0))
    else:
        mem_spec = pl.BlockSpec((1, 1) + mk.shape[2:], lambda i, j: (layer, i, 0, 0, 0))
    return pl.pallas_call(
        functools.partial(_cross_kernel, per_head_memory=layer is not None),
        grid=(b, lq // tl),
        in_specs=[pl.BlockSpec((1, tl, D_MODEL), lambda i, j: (i, j, 0)), mem_spec, mem_spec],
        out_specs=pl.BlockSpec((1, tl, D_MODEL), lambda i, j: (i, j, 0)),
        out_shape=jax.ShapeDtypeStruct((b, lq, D_MODEL), BF16),
        compiler_params=_cparams("parallel", "parallel"),
        name="cross_attn",
    )(q3, mk, mv)


def _peer_prep_kernel(u_ref, v_ref, ub_ref, vt_ref):
    ub_ref[...] = u_ref[...].astype(BF16)
    vt_ref[...] = v_ref[...].T.astype(BF16)


def _peer_prep(u, v):
    ne = u.shape[0]
    eb = min(PREP_EBLK, ne)
    return pl.pallas_call(
        _peer_prep_kernel,
        grid=(ne // eb,),
        in_specs=[pl.BlockSpec((eb, D_MODEL), lambda i: (i, 0))] * 2,
        out_specs=[pl.BlockSpec((eb, D_MODEL), lambda i: (i, 0)), pl.BlockSpec((D_MODEL, eb), lambda i: (0, i))],
        out_shape=[jax.ShapeDtypeStruct((ne, D_MODEL), BF16), jax.ShapeDtypeStruct((D_MODEL, ne), BF16)],
        compiler_params=_cparams("parallel"),
        name="peer_prep",
    )(u, v)


def _rows_per_word():
    return 4 // jnp.dtype(BF16).itemsize


def _pack_rows(x):
    return pltpu.bitcast(x, jnp.uint32)


def _unpack_rows(x):
    return pltpu.bitcast(x, BF16)


def _top16_rows(s, break_ties):
    nrow = s.shape[0]
    rowid = lax.broadcasted_iota(jnp.int32, s.shape, 0)
    rank = jnp.full(s.shape, float(PEER_TOPK), F32)
    work = s
    vals = []
    for r in range(PEER_TOPK):
        mx = jnp.max(work, axis=0, keepdims=True)
        if break_ties:
            sel = rowid == jnp.min(jnp.where(work == mx, rowid, nrow), axis=0, keepdims=True)
        else:
            sel = work == mx
        rank = jnp.where(sel, float(r), rank)
        work = jnp.where(sel, -jnp.inf, work)
        vals.append(mx)
    return rank, jnp.concatenate(vals, axis=0)


def _ranked_16(rank):
    n = jnp.sum(jnp.where(rank < float(PEER_TOPK), 1.0, 0.0), axis=0, keepdims=True)
    return jnp.where(n == float(PEER_TOPK), 1.0, 0.0)


def _peer_route_kernel(x_ref, x1_ref, wo_ref, g_ref, wq_ref, sk_ref, ca_ref, cb_ref, cg_ref,
                       x2_ref, hn_ref, c1_ref, nrow_ref, p2_ref, rk2_ref, q_sc, s_sc):
    tt = x_ref.shape[0]
    nch = tt // LANES
    x2 = x1_ref[...] + _dot(x_ref[...], wo_ref[...])
    x2_ref[...] = x2
    hn = _rms(x2, g_ref[...])
    hn_ref[...] = hn.T.astype(BF16)
    q_sc[...] = _dot(hn.astype(BF16), wq_ref[...]).astype(BF16)
    for hj in range(2 * H_PEER):
        st = _dot_nt(sk_ref[hj % 2], q_sc[:, hj * N_KEYS:(hj + 1) * N_KEYS])
        for ch in range(nch):
            s_sc[hj, ch] = st[:, ch * LANES:(ch + 1) * LANES]

    cand_row = lax.broadcasted_iota(jnp.int32, (N_CAND_PAD, LANES), 0)

    def unit(i, carry):
        h = i // nch
        ch = i % nch
        s1 = s_sc[2 * h, ch]
        s2 = s_sc[2 * h + 1, ch]

        def route(break_ties):
            rank1, v1 = _top16_rows(s1, break_ties)
            rank2, v2 = _top16_rows(s2, break_ties)
            cand = _sel_dot_l(ca_ref[...], v1) + _sel_dot_l(cb_ref[...], v2)
            cand = jnp.where(cand_row < N_CAND, cand, -jnp.inf)
            crank, _ = _top16_rows(cand, break_ties)
            chosen = crank < float(PEER_TOPK)
            n_of_a = _dot(cg_ref[...], jnp.where(chosen, 1.0, 0.0).astype(BF16))
            z = jnp.sum(jnp.where(chosen, jnp.exp(cand - cand[0:1]), 0.0), axis=0, keepdims=True)
            nrow = jnp.zeros_like(rank1)
            for a in range(PEER_TOPK):
                nrow = jnp.where(rank1 == float(a), n_of_a[a:a + 1], nrow)
            c1_ref[h, ch] = jnp.exp(s1 - v1[0:1]) / z
            nrow_ref[h, ch] = nrow
            p2_ref[h, ch] = _pack_rows(jnp.exp(s2 - v2[0:1]).astype(BF16))
            rk2_ref[h, ch] = _pack_rows(rank2.astype(BF16))
            return _ranked_16(rank1) * _ranked_16(rank2) * _ranked_16(crank)

        tie_free = route(False)

        @pl.when(jnp.min(tie_free) < 0.5)
        def _():
            route(True)

        return carry

    lax.fori_loop(0, H_PEER * nch, unit, 0)


def _peer_route(oc2d, x1, w_o_bf, g_ffn, w_q_bf, sk_bf):
    t = x1.shape[0]
    tt = min(PEER_TOK, t)
    nch = tt // LANES
    pairs = [(a, b) for a in range(PEER_TOPK) for b in range(PEER_TOPK) if (a + 1) * (b + 1) <= PEER_TOPK]
    assert len(pairs) == N_CAND
    ca = np.zeros((N_CAND_PAD, PEER_TOPK), np.float32)
    cb = np.zeros((N_CAND_PAD, PEER_TOPK), np.float32)
    cg = np.zeros((PEER_TOPK, N_CAND_PAD), np.float32)
    for i, (a, b) in enumerate(pairs):
        ca[i, a] = 1.0
        cb[i, b] = 1.0
        cg[a, i] = 1.0
    ca, cb, cg = (jnp.asarray(m, BF16) for m in (ca, cb, cg))
    row = lambda w: pl.BlockSpec((tt, w), lambda i: (i, 0))
    krows = N_KEYS // _rows_per_word()
    sel = pl.BlockSpec((H_PEER, nch, N_KEYS, LANES), lambda i: (0, i, 0, 0))
    sel_shape = jax.ShapeDtypeStruct((H_PEER, t // LANES, N_KEYS, LANES), F32)
    packed = pl.BlockSpec((H_PEER, nch, krows, LANES), lambda i: (0, i, 0, 0))
    packed_shape = jax.ShapeDtypeStruct((H_PEER, t // LANES, krows, LANES), jnp.uint32)
    return pl.pallas_call(
        _peer_route_kernel,
        grid=(t // tt,),
        in_specs=[row(D_MODEL), row(D_MODEL), _full_spec(w_o_bf.shape), _full_spec(g_ffn.shape),
                  _full_spec(w_q_bf.shape), _full_spec(sk_bf.shape),
                  _full_spec(ca.shape), _full_spec(cb.shape), _full_spec(cg.shape)],
        out_specs=[row(D_MODEL), pl.BlockSpec((D_MODEL, tt), lambda i: (0, i)), sel, sel, packed, packed],
        out_shape=[jax.ShapeDtypeStruct((t, D_MODEL), F32), jax.ShapeDtypeStruct((D_MODEL, t), BF16),
                   sel_shape, sel_shape, packed_shape, packed_shape],
        scratch_shapes=[pltpu.VMEM((tt, 2 * H_PEER * N_KEYS), BF16),
                        pltpu.VMEM((2 * H_PEER, nch, N_KEYS, LANES), F32)],
        compiler_params=_cparams("parallel"),
        name="peer_route",
    )(oc2d, x1, w_o_bf, g_ffn, w_q_bf, sk_bf, ca, cb, cg)


def _peer_mix_kernel(x2_ref, hn_ref, u_ref, vt_ref, c1_ref, nrow_ref, p2_ref, rk2_ref, gf_ref,
                     y_ref, w_sc, acc_sc, *, final_norm):
    e = pl.program_id(1)
    tt = hn_ref.shape[1]
    nch = tt // LANES
    n_i1 = u_ref.shape[0] // N_KEYS

    @pl.when(e == 0)
    def _():
        acc_sc[...] = jnp.zeros_like(acc_sc)

    krows = N_KEYS // _rows_per_word()
    acc = acc_sc[...]
    for sb in range(n_i1 // PEER_SUB_I1):
        i1_lo = sb * PEER_SUB_I1
        ex = slice(i1_lo * N_KEYS, (i1_lo + PEER_SUB_I1) * N_KEYS)
        exw = slice(i1_lo * krows, (i1_lo + PEER_SUB_I1) * krows)
        for ch in range(nch):
            cols = slice(ch * LANES, (ch + 1) * LANES)
            for r in range(i1_lo, i1_lo + PEER_SUB_I1):
                w = jnp.zeros((N_KEYS, LANES), BF16)
                for h in range(H_PEER):
                    wide = lambda ref: jnp.broadcast_to(ref[h, ch, r:r + 1, :], (N_KEYS, LANES)).astype(BF16)
                    p2 = _unpack_rows(p2_ref[h, ch])
                    hit = _unpack_rows(rk2_ref[h, ch]) < wide(nrow_ref)
                    w = w + jnp.where(hit, p2, jnp.zeros_like(p2)) * wide(c1_ref)
                w_sc[r * krows:(r + 1) * krows, cols] = _pack_rows(w)
        gelu_a = jax.nn.gelu(_dot(u_ref[ex, :], hn_ref[...])).astype(BF16)
        w_all = _unpack_rows(w_sc[exw, :]) * gelu_a
        acc = acc + _dot(vt_ref[:, ex], w_all)
    acc_sc[...] = acc

    @pl.when(e == pl.num_programs(1) - 1)
    def _():
        y = x2_ref[...] + acc_sc[...].T
        y_ref[...] = _rms(y, gf_ref[...]) if final_norm else y


def _peer_mix(x2, hn, u_bf, vt_bf, c1, nrow, p2, rk2, g_final, final_norm):
    t = x2.shape[0]
    tt = min(PEER_TOK, t)
    nch = tt // LANES
    ne = u_bf.shape[0]
    eb = min(PEER_EBLK, ne)
    n_i1 = eb // N_KEYS
    row = lambda w: pl.BlockSpec((tt, w), lambda i, e: (i, 0))
    part = pl.BlockSpec((H_PEER, nch, n_i1, LANES), lambda i, e: (0, i, e, 0))
    full = pl.BlockSpec((H_PEER, nch, N_KEYS // _rows_per_word(), LANES), lambda i, e: (0, i, 0, 0))
    return pl.pallas_call(
        functools.partial(_peer_mix_kernel, final_norm=final_norm),
        grid=(t // tt, ne // eb),
        in_specs=[row(D_MODEL), pl.BlockSpec((D_MODEL, tt), lambda i, e: (0, i)),
                  pl.BlockSpec((eb, D_MODEL), lambda i, e: (e, 0)),
                  pl.BlockSpec((D_MODEL, eb), lambda i, e: (0, e)),
                  part, part, full, full, pl.BlockSpec(g_final.shape, lambda i, e: (0, 0))],
        out_specs=row(D_MODEL),
        out_shape=jax.ShapeDtypeStruct((t, D_MODEL), F32),
        scratch_shapes=[pltpu.VMEM((eb // _rows_per_word(), tt), jnp.uint32), pltpu.VMEM((D_MODEL, tt), F32)],
        compiler_params=_cparams("parallel", "arbitrary"),
        name="peer_mix",
    )(x2, hn, u_bf, vt_bf, c1, nrow, p2, rk2, g_final)


def _prep_in_weights(w_in, b_fox_f, w_gate2, b_gate):
    o = np.cumsum((0,) + IN_SIZES)
    seg = lambda i: w_in[:, int(o[i]):int(o[i + 1])]
    pad = jnp.zeros((D_MODEL, LANES - H_FOX - GLA_GATE_RANK), w_in.dtype)
    w_perm = jnp.concatenate([seg(0), seg(1), seg(2), seg(4), seg(5), seg(6), seg(8), seg(3), seg(7), pad],
                             axis=1).astype(BF16)
    b_small = jnp.zeros((1, LANES), F32).at[0, 0:H_FOX].set(b_fox_f)
    wg_pad = jnp.zeros((LANES, GLA_KW), F32).at[H_FOX:H_FOX + GLA_GATE_RANK].set(w_gate2).astype(BF16)
    return w_perm, b_small, wg_pad, b_gate.reshape(1, GLA_KW)


def kernel(x_prompt, x_sample, cache_fox_k, cache_fox_v, cache_fox_logf, state_gla, cache_mem_k, cache_mem_v, page_table, mem_prompt, g_mix, w_in, b_fox_f, w_gla_gate2, b_gla_gate, g_gla_out, w_out, g_cross, g_mem, w_mem_k, w_mem_v, w_cross_q, w_cross_o, g_ffn, peer_w_q, peer_subkeys, peer_u, peer_v, g_final):
    depth = w_in.shape[0]
    bp, lp, _ = x_prompt.shape
    bs, ls, _ = x_sample.shape
    n_mem = mem_prompt.shape[1]
    n_pool, page = cache_fox_k.shape[1:3]
    xp = x_prompt.reshape(bp * lp, D_MODEL)
    xs = x_sample.reshape(bs * ls, D_MODEL)
    vec = lambda a: a.reshape(1, -1)
    kcache_t = cache_fox_k.transpose(0, 1, 3, 4, 2).reshape(depth, n_pool, FOX_W, page)
    vcache_t = cache_fox_v.transpose(0, 1, 3, 4, 2).reshape(depth, n_pool, FOX_W, page)
    lfcache_t = cache_fox_logf.transpose(0, 1, 3, 2)
    outs = [[] for _ in range(10)]
    for l in range(depth):
        w_perm, b_small, wg_pad, b_gate = _prep_in_weights(w_in[l], b_fox_f[l], w_gla_gate2[l], b_gla_gate[l])
        w_out_bf = w_out[l].astype(BF16)
        w_cq_bf = w_cross_q[l].astype(BF16)
        w_co_bf = w_cross_o[l].astype(BF16)
        w_pq_bf = peer_w_q[l].astype(BF16)
        sk_bf = peer_subkeys[l].astype(BF16)
        u_bf, vt_bf = _peer_prep(peer_u[l], peer_v[l])
        last = l == depth - 1

        def tail(x, fox_o, gla_o, gr, mk, mv, nb, mem_layer):
            x1, qc = _mixproj(x, fox_o, gla_o, gr, vec(g_gla_out[l]), w_out_bf, vec(g_cross[l]), w_cq_bf)
            oc = _cross(qc.reshape(nb, -1, D_MODEL), mk, mv, mem_layer).reshape(-1, D_MODEL)
            x2, hn, c1, nrow, p2, rk2 = _peer_route(oc, x1, w_co_bf, vec(g_ffn[l]), w_pq_bf, sk_bf)
            return _peer_mix(x2, hn, u_bf, vt_bf, c1, nrow, p2, rk2, vec(g_final), last)

        fq_bf, fk, fv, lf, gq, gk, gv, la, gr = _inproj(xp, vec(g_mix[l]), w_perm, b_small, wg_pad, b_gate)
        qc, kc, vt, fk_t, fv_t, lf_t = _fox_prep(fq_bf, fk, fv, lf, bp, lp)
        fox_o = _fox_prompt(qc, kc, vt, bp, lp)
        gla_o, s_fin = _gla_prompt(gq, gk, gv, la, bp, lp)
        mk, mv = _memkv(mem_prompt.reshape(bp * n_mem, D_MODEL), vec(g_mem[l]), w_mem_k[l].astype(BF16),
                        w_mem_v[l].astype(BF16))
        mk3 = mk.reshape(bp, n_mem, D_MODEL)
        mv3 = mv.reshape(bp, n_mem, D_MODEL)
        xp = tail(xp, fox_o, gla_o, gr, mk3, mv3, bp, None)
        outs[0].append(fk_t.reshape(bp, H_FOX, DH_FOX, lp).transpose(0, 3, 1, 2))
        outs[1].append(fv_t.reshape(bp, H_FOX, DH_FOX, lp).transpose(0, 3, 1, 2))
        outs[2].append(lf_t.transpose(0, 2, 1))
        outs[3].append(s_fin)
        outs[4].append(mk.reshape(bp, n_mem, H_MEM, DH_MEM))
        outs[5].append(mv.reshape(bp, n_mem, H_MEM, DH_MEM))

        fq_bf, fk, fv, lf, gq, gk, gv, la, gr = _inproj(xs, vec(g_mix[l]), w_perm, b_small, wg_pad, b_gate)
        seqs = lambda a: a.reshape(bs, ls, FOX_W)
        fox_o = _fox_sample(seqs(fq_bf), seqs(fk), seqs(fv), lf.reshape(bs, ls, H_FOX).transpose(0, 2, 1),
                            kcache_t, vcache_t, lfcache_t, page_table, l).reshape(bs * ls, FOX_W)
        heads = lambda a, d: a.reshape(bs, ls, H_GLA, d).transpose(0, 2, 1, 3).reshape(bs * H_GLA, ls, d)
        gla_o, s_new = _gla_sample(heads(gq, DK_GLA), heads(gk, DK_GLA), heads(gv, DV_GLA), heads(la, DK_GLA),
                                   state_gla[l].reshape(bs * H_GLA, DK_GLA, DV_GLA))
        gla_o = gla_o.reshape(bs, H_GLA, ls, DV_GLA).transpose(0, 2, 1, 3).reshape(bs * ls, GLA_VW)
        xs = tail(xs, fox_o, gla_o, gr, cache_mem_k, cache_mem_v, bs, l)
        outs[6].append(fk.reshape(bs, ls, H_FOX, DH_FOX))
        outs[7].append(fv.reshape(bs, ls, H_FOX, DH_FOX))
        outs[8].append(lf.reshape(bs, ls, H_FOX))
        outs[9].append(s_new.reshape(bs, H_GLA, DK_GLA, DV_GLA))
    stacked = [jnp.stack(o, axis=0) for o in outs]
    return (xp.reshape(bp, lp, D_MODEL), xs.reshape(bs, ls, D_MODEL), *stacked)
```

```python
import functools

import numpy as np
import jax
import jax.numpy as jnp
from jax import lax
from jax.experimental import pallas as pl
from jax.experimental.pallas import tpu as pltpu

F32 = jnp.float32
BF16 = jnp.bfloat16

D_MODEL = 1024
H_FOX, DH_FOX = 8, 64
H_GLA, DK_GLA, DV_GLA = 4, 64, 128
GLA_GATE_RANK = 16
GLA_TAU = 16.0
GLA_CHUNK = 64
H_MEM = 4
DH_MEM = D_MODEL // H_MEM
N_KEYS = 128
H_PEER = 8
PEER_TOPK = 16
EPS = 1e-6
NEG_INF = -1e30
_LOG2E = float(np.log2(np.e))
FOX_W = H_FOX * DH_FOX
GLA_KW = H_GLA * DK_GLA
GLA_VW = H_GLA * DV_GLA
IN_SIZES = (FOX_W, FOX_W, FOX_W, H_FOX, GLA_KW, GLA_KW, GLA_VW, GLA_GATE_RANK, GLA_VW)

LANES = 128
VMEM_LIMIT_BYTES = 56 * 2**20

TOK_TILE = 512
INPROJ_TILE = 256
FOX_TILE = 512
FOX_KTILE = 1024
FOX_HP_PER_STEP = 2
GLA_GROUP = 512
PEER_TOK = 512
PEER_EBLK = 2048
PEER_SUB_I1 = 8
FOX_PAGES_PER_STEP = 16
PREP_EBLK = 512
N_CAND = 50
N_CAND_PAD = 64


def _cparams(*sem):
    return pltpu.CompilerParams(dimension_semantics=sem, vmem_limit_bytes=VMEM_LIMIT_BYTES)


def _dot(a, b):
    return jnp.dot(a, b, preferred_element_type=F32)


def _dot_nt(a, b):
    return lax.dot_general(a, b, (((1,), (1,)), ((), ())), preferred_element_type=F32)


def _split3(x):
    hi = x.astype(BF16)
    r = x - hi.astype(F32)
    mid = r.astype(BF16)
    lo = (r - mid.astype(F32)).astype(BF16)
    return hi, mid, lo


def _sel_dot_l(m01, x):
    hi, mid, lo = _split3(x)
    return (_dot(m01, hi) + _dot(m01, mid)) + _dot(m01, lo)


def _sel_dot_r(x, m01):
    hi, mid, lo = _split3(x)
    return (_dot(hi, m01) + _dot(mid, m01)) + _dot(lo, m01)


def _rms(x, g):
    return x * lax.rsqrt(jnp.mean(x * x, axis=-1, keepdims=True) + EPS) * g


def _logsig(x):
    return jnp.minimum(x, 0.0) - jnp.log1p(jnp.exp(-jnp.abs(x)))


def _full_spec(shape):
    nd = len(shape)
    return pl.BlockSpec(shape, lambda *_: (0,) * nd)


_C_FQ, _C_FK, _C_FV = 0, FOX_W, 2 * FOX_W
_C_GQ = 3 * FOX_W
_C_GK = _C_GQ + GLA_KW
_C_GV = _C_GK + GLA_KW
_C_GR = _C_GV + GLA_VW
_C_SM = _C_GR + GLA_VW
_C_END = _C_SM + LANES


def _inproj_kernel(x_ref, g_ref, w_ref, bsm_ref, wg_ref, bg_ref,
                   fq_ref, fk_ref, fv_ref, lf_ref, gq_ref, gk_ref, gv_ref, la_ref, gr_ref, *, q_scale):
    h = _rms(x_ref[...], g_ref[...]).astype(BF16)

    def seg(lo, hi):
        return _dot(h, w_ref[:, lo:hi])

    fq_ref[...] = (seg(_C_FQ, _C_FK) * q_scale).astype(BF16)
    fk_ref[...] = seg(_C_FK, _C_FV)
    fv_ref[...] = seg(_C_FV, _C_GQ)
    gq_ref[...] = seg(_C_GQ, _C_GK) * (DK_GLA ** -0.5)
    gk_ref[...] = seg(_C_GK, _C_GV)
    gv_ref[...] = seg(_C_GV, _C_GR)
    gr_ref[...] = seg(_C_GR, _C_SM)
    ysm = seg(_C_SM, _C_END)
    lf_ref[...] = _logsig(ysm + bsm_ref[...])[:, 0:H_FOX]
    z = _dot(ysm.astype(BF16), wg_ref[...]) + bg_ref[...]
    la_ref[...] = _logsig(z) / GLA_TAU


def _inproj(x2d, g_mix, w_perm, b_small, wg_pad, b_gate, q_scale):
    t = x2d.shape[0]
    tt = min(INPROJ_TILE, t)
    row = lambda w: pl.BlockSpec((tt, w), lambda i: (i, 0))
    outs = [(FOX_W, BF16), (FOX_W, F32), (FOX_W, F32), (H_FOX, F32),
            (GLA_KW, F32), (GLA_KW, F32), (GLA_VW, F32), (GLA_KW, F32), (GLA_VW, F32)]
    return pl.pallas_call(
        functools.partial(_inproj_kernel, q_scale=q_scale),
        grid=(t // tt,),
        in_specs=[row(D_MODEL), _full_spec(g_mix.shape), _full_spec(w_perm.shape), _full_spec(b_small.shape),
                  _full_spec(wg_pad.shape), _full_spec(b_gate.shape)],
        out_specs=[row(w) for w, _ in outs],
        out_shape=[jax.ShapeDtypeStruct((t, w), dt) for w, dt in outs],
        compiler_params=_cparams("parallel"),
        name="inproj",
    )(x2d, g_mix, w_perm, b_small, wg_pad, b_gate)


_FOX_QW = 4 * LANES
_FOX_KW = 2 * LANES
_N_HP = FOX_W // LANES


def _fox_aug_tables():
    pq = np.zeros((3, H_FOX, _N_HP * 2 * LANES), np.float32)
    oq = np.zeros((1, _N_HP * 2 * LANES), np.float32)
    pk = np.zeros((3, H_FOX, _N_HP * LANES), np.float32)
    ok = np.zeros((1, _N_HP * LANES), np.float32)
    for hp in range(_N_HP):
        for half in range(2):
            head = 2 * hp + half
            qb = (hp * 2 + half) * LANES
            kb = hp * LANES
            for j in range(3):
                pq[j, head, qb + 6 * half + j] = 1.0
                oq[0, qb + 6 * half + 3 + j] = 1.0
                pk[j, head, kb + 6 * half + 3 + j] = 1.0
                ok[0, kb + 6 * half + j] = 1.0
    return jnp.asarray(pq, BF16), jnp.asarray(oq, F32), jnp.asarray(pk, BF16), jnp.asarray(ok, F32)


def _fox_prep_kernel(q_ref, k_ref, v_ref, lf_ref, tril_ref, pq_ref, oq_ref, pk_ref, ok_ref, eye_ref,
                     qc_ref, kc_ref, vt_ref, fkt_ref, fvt_ref, lft_ref, carry_sc):
    @pl.when(pl.program_id(1) == 0)
    def _():
        carry_sc[...] = jnp.zeros_like(carry_sc)

    lf_parts = _split3(lf_ref[...])
    eye = eye_ref[...]
    lft_ref[0] = (_dot_nt(eye, lf_parts[0]) + _dot_nt(eye, lf_parts[1])) + _dot_nt(eye, lf_parts[2])
    c = _sel_dot_l(tril_ref[...], lf_ref[...]) + carry_sc[0:1, 0:H_FOX]
    carry_sc[0:1, 0:H_FOX] = c[-1:, :]
    parts = _split3(c * _LOG2E)
    aug_q = oq_ref[...]
    aug_k = ok_ref[...]
    for j in range(3):
        aug_q = aug_q + _dot(parts[j], pq_ref[j])
        aug_k = aug_k - _dot(parts[j], pk_ref[j])
    aug_q = aug_q.astype(BF16)
    aug_k = aug_k.astype(BF16)
    lane = lax.broadcasted_iota(jnp.int32, (q_ref.shape[0], LANES), 1)
    for hp in range(_N_HP):
        q = q_ref[:, hp * LANES:(hp + 1) * LANES]
        zero = jnp.zeros_like(q)
        base = hp * _FOX_QW
        qc_ref[:, base:base + LANES] = jnp.where(lane < DH_FOX, q, zero)
        qc_ref[:, base + LANES:base + 2 * LANES] = aug_q[:, (2 * hp) * LANES:(2 * hp + 1) * LANES]
        qc_ref[:, base + 2 * LANES:base + 3 * LANES] = jnp.where(lane >= DH_FOX, q, zero)
        qc_ref[:, base + 3 * LANES:base + 4 * LANES] = aug_q[:, (2 * hp + 1) * LANES:(2 * hp + 2) * LANES]
        kc_ref[:, hp * _FOX_KW:hp * _FOX_KW + LANES] = k_ref[:, hp * LANES:(hp + 1) * LANES].astype(BF16)
        kc_ref[:, hp * _FOX_KW + LANES:(hp + 1) * _FOX_KW] = aug_k[:, hp * LANES:(hp + 1) * LANES]
    v_t = v_ref[...].T
    vt_ref[0] = v_t.astype(BF16)
    fvt_ref[0] = v_t
    fkt_ref[0] = k_ref[...].T


def _fox_prep(fq_bf, fk, fv, lf, batch, seq):
    tt = min(FOX_TILE, seq)
    nt = seq // tt
    tril = jnp.asarray(np.tril(np.ones((tt, tt), np.float32)), BF16)
    consts = [tril, *_fox_aug_tables(), jnp.eye(H_FOX, dtype=BF16)]
    row = lambda w: pl.BlockSpec((tt, w), lambda b, i: (b * nt + i, 0))
    col = lambda h: pl.BlockSpec((1, h, tt), lambda b, i: (b, 0, i))
    return pl.pallas_call(
        _fox_prep_kernel,
        grid=(batch, nt),
        in_specs=[row(FOX_W), row(FOX_W), row(FOX_W), row(H_FOX)] + [_full_spec(a.shape) for a in consts],
        out_specs=[row(_N_HP * _FOX_QW), row(_N_HP * _FOX_KW), col(FOX_W), col(FOX_W), col(FOX_W), col(H_FOX)],
        out_shape=[jax.ShapeDtypeStruct((batch * seq, _N_HP * _FOX_QW), BF16),
                   jax.ShapeDtypeStruct((batch * seq, _N_HP * _FOX_KW), BF16),
                   jax.ShapeDtypeStruct((batch, FOX_W, seq), BF16),
                   jax.ShapeDtypeStruct((batch, FOX_W, seq), F32),
                   jax.ShapeDtypeStruct((batch, FOX_W, seq), F32),
                   jax.ShapeDtypeStruct((batch, H_FOX, seq), F32)],
        scratch_shapes=[pltpu.VMEM((8, LANES), F32)],
        compiler_params=_cparams("parallel", "arbitrary"),
        name="fox_prep",
    )(fq_bf, fk, fv, lf, *consts)


def _fox_prompt_kernel(qi_tab, ki_tab, q_ref, k_ref, vt_ref, o_ref,
                       m_sc, l_sc, acc_sc):
    t = pl.program_id(2)
    qi = qi_tab[t]
    ki = ki_tab[t]
    tq = q_ref.shape[0]
    tk = k_ref.shape[0]

    @pl.when(ki == 0)
    def _():
        m_sc[...] = jnp.full_like(m_sc, NEG_INF)
        l_sc[...] = jnp.zeros_like(l_sc)
        acc_sc[...] = jnp.zeros_like(acc_sc)

    def step(masked):
        for hp in range(FOX_HP_PER_STEP):
            kc = k_ref[:, hp * _FOX_KW:(hp + 1) * _FOX_KW]
            vt = vt_ref[0, hp * LANES:(hp + 1) * LANES, :]
            for half in range(2):
                hh = 2 * hp + half
                s = _dot_nt(kc, q_ref[:, hh * _FOX_KW:(hh + 1) * _FOX_KW])
                if masked:
                    kpos = lax.broadcasted_iota(jnp.int32, (tk, tq), 0) + ki * tk
                    qpos = lax.broadcasted_iota(jnp.int32, (tk, tq), 1) + qi * tq
                    s = jnp.where(kpos <= qpos, s, NEG_INF)
                m_prev = m_sc[hh:hh + 1, :]
                m_new = jnp.maximum(m_prev, jnp.max(s, axis=0, keepdims=True))
                alpha = jnp.exp2(m_prev - m_new)
                p = jnp.exp2(s - m_new)
                l_sc[hh:hh + 1, :] = alpha * l_sc[hh:hh + 1, :] + jnp.sum(p, axis=0, keepdims=True)
                acc_sc[hh] = alpha * acc_sc[hh] + _dot(vt, p.astype(BF16))
                m_sc[hh:hh + 1, :] = m_new

    last = (qi * tq) // tk

    @pl.when(ki < last)
    def _():
        step(False)

    @pl.when(ki == last)
    def _():
        step(True)
        dim = lax.broadcasted_iota(jnp.int32, (LANES, tq), 0)
        for hp in range(FOX_HP_PER_STEP):
            o0 = acc_sc[2 * hp] / l_sc[2 * hp:2 * hp + 1, :]
            o1 = acc_sc[2 * hp + 1] / l_sc[2 * hp + 1:2 * hp + 2, :]
            o_ref[:, hp * LANES:(hp + 1) * LANES] = jnp.where(dim < DH_FOX, o0, o1).T.astype(o_ref.dtype)


def _fox_prompt(qc, kc, vt, batch, seq):
    tq = min(FOX_TILE, seq)
    tk = min(FOX_KTILE, seq)
    nq = seq // tq
    nk = seq // tk
    pairs = [(i, j) for i in range(nq) for j in range((i * tq) // tk + 1)]
    qi_tab = jnp.asarray([p[0] for p in pairs], jnp.int32)
    ki_tab = jnp.asarray([p[1] for p in pairs], jnp.int32)
    g = FOX_HP_PER_STEP
    grid_spec = pltpu.PrefetchScalarGridSpec(
        num_scalar_prefetch=2,
        grid=(batch, _N_HP // g, len(pairs)),
        in_specs=[
            pl.BlockSpec((tq, g * _FOX_QW), lambda b, h, t, qt, kt: (b * nq + qt[t], h)),
            pl.BlockSpec((tk, g * _FOX_KW), lambda b, h, t, qt, kt: (b * nk + kt[t], h)),
            pl.BlockSpec((1, g * LANES, tk), lambda b, h, t, qt, kt: (b, h, kt[t])),
        ],
        out_specs=pl.BlockSpec((tq, g * LANES), lambda b, h, t, qt, kt: (b * nq + qt[t], h)),
        scratch_shapes=[pltpu.VMEM((2 * g, tq), F32), pltpu.VMEM((2 * g, tq), F32),
                        pltpu.VMEM((2 * g, LANES, tq), F32)],
    )
    return pl.pallas_call(
        _fox_prompt_kernel,
        grid_spec=grid_spec,
        out_shape=jax.ShapeDtypeStruct((batch * seq, FOX_W), BF16),
        compiler_params=_cparams("parallel", "parallel", "arbitrary"),
        name="fox_prompt",
    )(qi_tab, ki_tab, qc, kc, vt)


def _fox_sample_kernel(pt_ref, q_ref, *refs, n_pp):
    kc, vc, lfc = refs[0:n_pp], refs[n_pp:2 * n_pp], refs[2 * n_pp:3 * n_pp]
    (kn_ref, vn_ref, lfn_ref, tri_ref, trin_ref, o_ref,
     qbd_sc, m_sc, l_sc, acc_sc, carry_sc, kall_sc, vall_sc) = refs[3 * n_pp:]
    step = pl.program_id(1)
    ls = q_ref.shape[1]
    nr = H_FOX * ls
    page = tri_ref.shape[0]
    row_head = lax.broadcasted_iota(jnp.int32, (nr, FOX_W), 0) // ls
    lane_head = lax.broadcasted_iota(jnp.int32, (nr, FOX_W), 1) // DH_FOX

    def rows_of_heads(x):
        return jnp.broadcast_to(x[:, None, :], (H_FOX, ls, x.shape[-1])).reshape(nr, x.shape[-1])

    cn_row = rows_of_heads(_sel_dot_r(lfn_ref[0], trin_ref[...]))
    tnew = lax.broadcasted_iota(jnp.int32, (nr, ls), 1)
    tq = lax.broadcasted_iota(jnp.int32, (nr, ls), 0) % ls
    cn_col = jnp.sum(jnp.where(tnew == tq, cn_row, 0.0), axis=-1, keepdims=True)

    @pl.when(step == 0)
    def _():
        qt = jnp.concatenate([q_ref[0]] * H_FOX, axis=0)
        qbd_sc[...] = jnp.where(row_head == lane_head, qt, jnp.zeros_like(qt))
        m_sc[...] = jnp.full_like(m_sc, NEG_INF)
        l_sc[...] = jnp.zeros_like(l_sc)
        acc_sc[...] = jnp.zeros_like(acc_sc)
        carry_sc[...] = jnp.zeros_like(carry_sc)

    def online(scores, weighted_values):
        m_prev = m_sc[...]
        m_new = m_prev
        for s in scores:
            m_new = jnp.maximum(m_new, jnp.max(s, axis=-1, keepdims=True))
        alpha = jnp.exp(m_prev - m_new)
        l_new = alpha * l_sc[...]
        acc = alpha * acc_sc[...]
        for s, pv in zip(scores, weighted_values):
            pr = jnp.exp(s - m_new)
            l_new = l_new + jnp.sum(pr, axis=-1, keepdims=True)
            acc = acc + pv(pr.astype(BF16))
        m_sc[...] = m_new
        l_sc[...] = l_new
        acc_sc[...] = acc

    c_loc = _sel_dot_r(jnp.concatenate([r[0, 0] for r in lfc], axis=0), tri_ref[...])
    carry = carry_sc[...][:, 0:1]
    qbd = qbd_sc[...]
    c_pages = []
    for j in range(n_pp):
        c_page = c_loc[j * H_FOX:(j + 1) * H_FOX] + carry
        carry = c_page[:, page - 1:page]
        c_pages.append(rows_of_heads(c_page))
        kall_sc[:, j * page:(j + 1) * page] = kc[j][0, 0].astype(BF16)
        vall_sc[:, j * page:(j + 1) * page] = vc[j][0, 0].astype(BF16)
    carry_sc[...] = jnp.broadcast_to(carry, carry_sc.shape)
    s_all = _dot(qbd, kall_sc[...]) + cn_col - jnp.concatenate(c_pages, axis=1)
    online([s_all], [lambda p: _dot_nt(p, vall_sc[...])])

    @pl.when(step == pl.num_programs(1) - 1)
    def _():
        s_new = _dot_nt(qbd, kn_ref[0].astype(BF16))
        s_new = s_new + cn_col - cn_row - rows_of_heads(jnp.broadcast_to(carry, (H_FOX, LANES)))[:, 0:1]
        s_new = jnp.where(tnew <= tq, s_new, NEG_INF)
        online([s_new], [lambda p: _dot(p, vn_ref[0].astype(BF16))])
        o = acc_sc[...] / l_sc[...]
        o = jnp.where(row_head == lane_head, o, 0.0)
        out = o[0:ls]
        for h in range(1, H_FOX):
            out = out + o[h * ls:(h + 1) * ls]
        o_ref[0] = out.astype(o_ref.dtype)


def _fox_sample(fq_bf, fk, fv, lfn_t, kcache_t, vcache_t, lfcache_t, page_table, layer):
    bs, n_pages = page_table.shape
    n_pp = min(FOX_PAGES_PER_STEP, n_pages)
    assert n_pages % n_pp == 0
    page = kcache_t.shape[3]
    ls = fq_bf.shape[1]
    nr = H_FOX * ls
    tri = jnp.asarray(np.triu(np.ones((page, page), np.float32)), BF16)
    trin = jnp.asarray(np.triu(np.ones((ls, ls), np.float32)), BF16)
    page_spec = lambda j, rows: pl.BlockSpec(
        (1, 1, rows, page), lambda b, s, pt: (layer, pt[b * n_pages + s * n_pp + j], 0, 0))
    per_seq = lambda *shape: pl.BlockSpec((1,) + shape, lambda b, s, pt: (b,) + (0,) * len(shape))
    const = lambda a: pl.BlockSpec(a.shape, lambda b, s, pt: (0,) * a.ndim)
    grid_spec = pltpu.PrefetchScalarGridSpec(
        num_scalar_prefetch=1,
        grid=(bs, n_pages // n_pp),
        in_specs=([per_seq(ls, FOX_W)] + [page_spec(j, FOX_W) for j in range(n_pp)] * 2
                  + [page_spec(j, H_FOX) for j in range(n_pp)]
                  + [per_seq(ls, FOX_W), per_seq(ls, FOX_W), per_seq(H_FOX, ls), const(tri), const(trin)]),
        out_specs=per_seq(ls, FOX_W),
        scratch_shapes=[pltpu.VMEM((nr, FOX_W), BF16), pltpu.VMEM((nr, 1), F32), pltpu.VMEM((nr, 1), F32),
                        pltpu.VMEM((nr, FOX_W), F32), pltpu.VMEM((H_FOX, LANES), F32),
                        pltpu.VMEM((FOX_W, n_pp * page), BF16), pltpu.VMEM((FOX_W, n_pp * page), BF16)],
    )
    return pl.pallas_call(
        functools.partial(_fox_sample_kernel, n_pp=n_pp),
        grid_spec=grid_spec,
        out_shape=jax.ShapeDtypeStruct((bs, ls, FOX_W), BF16),
        compiler_params=_cparams("parallel", "arbitrary"),
        name="fox_sample",
    )(page_table.reshape(-1), fq_bf, *([kcache_t] * n_pp), *([vcache_t] * n_pp), *([lfcache_t] * n_pp),
      fk, fv, lfn_t, tri, trin)


def _gla_prompt_kernel(q_ref, k_ref, v_ref, la_ref, tril_ref, ones_ref, onesT_ref,
                       o_ref, sfin_ref, s_sc):
    g = pl.program_id(1)
    n_tok = q_ref.shape[0]
    n_chunks = n_tok // GLA_CHUNK

    @pl.when(g == 0)
    def _():
        s_sc[...] = jnp.zeros_like(s_sc)

    la = la_ref[...]
    b = _sel_dot_l(tril_ref[...], la)
    bl = _sel_dot_l(ones_ref[...], la)
    q_dec = q_ref[...] * jnp.exp(b)
    q_dec_bf = q_dec.astype(BF16)
    k = k_ref[...]
    kd_bf = (k * jnp.exp(-b)).astype(BF16)
    kl_t = (k * jnp.exp(bl - b)).T
    v_bf = v_ref[...].astype(BF16)
    bl_col = _sel_dot_r(la.T, onesT_ref[...])

    r = lax.broadcasted_iota(jnp.int32, (n_tok, n_tok), 0)
    c = lax.broadcasted_iota(jnp.int32, (n_tok, n_tok), 1)
    causal = (r // GLA_CHUNK == c // GLA_CHUNK) & (c <= r)
    lane_q = lax.broadcasted_iota(jnp.int32, (n_tok, GLA_KW), 1) // DK_GLA
    zero_q = jnp.zeros_like(q_dec_bf)
    for h in range(H_GLA):
        att = _dot_nt(jnp.where(lane_q == h, q_dec_bf, zero_q), kd_bf)
        att = jnp.where(causal, att, 0.0)
        o_ref[:, h * DV_GLA:(h + 1) * DV_GLA] = _dot(att.astype(BF16), v_bf[:, h * DV_GLA:(h + 1) * DV_GLA])

    row_head = lax.broadcasted_iota(jnp.int32, (GLA_KW, GLA_VW), 0) // DK_GLA
    lane_head = lax.broadcasted_iota(jnp.int32, (GLA_KW, GLA_VW), 1) // DV_GLA
    same_head = row_head == lane_head
    tok_chunk = lax.broadcasted_iota(jnp.int32, (GLA_KW, n_tok), 1) // GLA_CHUNK
    for n in range(n_chunks):
        rows = slice(n * GLA_CHUNK, (n + 1) * GLA_CHUNK)
        s_prev = s_sc[...]
        o_ref[rows, :] += _dot(q_dec_bf[rows], s_prev.astype(BF16))
        kv = _dot(jnp.where(tok_chunk == n, kl_t, 0.0).astype(BF16), v_bf)
        dec = jnp.exp(bl_col[:, n * LANES:(n + 1) * LANES])
        dec = jnp.concatenate([dec] * (GLA_VW // LANES), axis=1)
        s_sc[...] = jnp.where(same_head, dec * s_prev + kv, 0.0)

    @pl.when(g == pl.num_programs(1) - 1)
    def _():
        s = s_sc[...]
        for h in range(H_GLA):
            sfin_ref[0, h] = s[h * DK_GLA:(h + 1) * DK_GLA, h * DV_GLA:(h + 1) * DV_GLA]


def _gla_prompt(gq, gk, gv, la, batch, seq):
    grp = min(GLA_GROUP, seq)
    ng = seq // grp
    nch = grp // GLA_CHUNK
    idx = np.arange(grp)
    same = (idx[:, None] // GLA_CHUNK) == (idx[None, :] // GLA_CHUNK)
    tril = jnp.asarray((same & (idx[None, :] <= idx[:, None])).astype(np.float32), BF16)
    ones = jnp.asarray(same.astype(np.float32), BF16)
    ones_t = jnp.asarray((idx[:, None] // GLA_CHUNK == np.arange(nch * LANES)[None, :] // LANES).astype(np.float32),
                         BF16)
    row = lambda w: pl.BlockSpec((grp, w), lambda b, g: (b * ng + g, 0))
    return pl.pallas_call(
        _gla_prompt_kernel,
        grid=(batch, ng),
        in_specs=[row(GLA_KW), row(GLA_KW), row(GLA_VW), row(GLA_KW),
                  _full_spec(tril.shape), _full_spec(ones.shape), _full_spec(ones_t.shape)],
        out_specs=[row(GLA_VW), pl.BlockSpec((1, H_GLA, DK_GLA, DV_GLA), lambda b, g: (b, 0, 0, 0))],
        out_shape=[jax.ShapeDtypeStruct((batch * seq, GLA_VW), F32),
                   jax.ShapeDtypeStruct((batch, H_GLA, DK_GLA, DV_GLA), F32)],
        scratch_shapes=[pltpu.VMEM((GLA_KW, GLA_VW), F32)],
        compiler_params=_cparams("parallel", "arbitrary"),
        name="gla_prompt",
    )(gq, gk, gv, la, tril, ones, ones_t)


def _gla_sample_kernel(q_ref, k_ref, v_ref, la_ref, s0_ref, tril_ref, o_ref, s1_ref):
    n, ls, _ = q_ref.shape
    la = la_ref[...]
    tril = jnp.broadcast_to(tril_ref[...][None], (n, ls, ls))
    hi, mid, lo = _split3(la)
    bdot = lambda x: jnp.einsum('nts,nsd->ntd', tril, x, preferred_element_type=F32)
    b = (bdot(hi) + bdot(mid)) + bdot(lo)
    b_last = b[:, ls - 1:ls, :]
    q_dec = (q_ref[...] * jnp.exp(b)).astype(BF16)
    k = k_ref[...]
    kd = (k * jnp.exp(-b)).astype(BF16)
    kl = (k * jnp.exp(b_last - b)).astype(BF16)
    v = v_ref[...].astype(BF16)
    s0 = s0_ref[...]
    att = jnp.einsum('ncd,nsd->ncs', q_dec, kd, preferred_element_type=F32)
    r = lax.broadcasted_iota(jnp.int32, (n, ls, ls), 1)
    c = lax.broadcasted_iota(jnp.int32, (n, ls, ls), 2)
    att = jnp.where(c <= r, att, 0.0)
    o = jnp.einsum('ncs,nse->nce', att.astype(BF16), v, preferred_element_type=F32)
    o = o + jnp.einsum('ncd,nde->nce', q_dec, s0.astype(BF16), preferred_element_type=F32)
    o_ref[...] = o
    kv = jnp.einsum('nds,nse->nde', jnp.swapaxes(kl, 1, 2), v, preferred_element_type=F32)
    dec = jnp.swapaxes(jnp.exp(b_last), 1, 2)
    s1_ref[...] = dec * s0 + kv


def _gla_sample(gq, gk, gv, la, state):
    n, ls, _ = gq.shape
    nb = min(32, n)
    tril = jnp.asarray(np.tril(np.ones((ls, ls), np.float32)), BF16)
    blk = lambda a, c: pl.BlockSpec((nb, a, c), lambda i: (i, 0, 0))
    return pl.pallas_call(
        _gla_sample_kernel,
        grid=(n // nb,),
        in_specs=[blk(ls, DK_GLA), blk(ls, DK_GLA), blk(ls, DV_GLA), blk(ls, DK_GLA), blk(DK_GLA, DV_GLA),
                  _full_spec(tril.shape)],
        out_specs=[blk(ls, DV_GLA), blk(DK_GLA, DV_GLA)],
        out_shape=[jax.ShapeDtypeStruct((n, ls, DV_GLA), F32), jax.ShapeDtypeStruct((n, DK_GLA, DV_GLA), F32)],
        compiler_params=_cparams("parallel"),
        name="gla_sample",
    )(gq, gk, gv, la, state, tril)


def _mixproj_kernel(x_ref, fox_ref, gla_ref, gr_ref, gg_ref, wo_ref, gc_ref, wq_ref, x1_ref, q_ref):
    gla = gla_ref[...]
    gr = gr_ref[...]
    acc = _dot(fox_ref[...], wo_ref[0:FOX_W, :])
    for h in range(H_GLA):
        cols = slice(h * DV_GLA, (h + 1) * DV_GLA)
        gh = _rms(gla[:, cols], gg_ref[:, cols])
        grh = gr[:, cols]
        gh = gh * (grh * jax.nn.sigmoid(grh))
        acc = acc + _dot(gh.astype(BF16), wo_ref[FOX_W + h * DV_GLA:FOX_W + (h + 1) * DV_GLA, :])
    x1 = x_ref[...] + acc
    x1_ref[...] = x1
    h2 = _rms(x1, gc_ref[...]).astype(BF16)
    q_ref[...] = (_dot(h2, wq_ref[...]) * (DH_MEM ** -0.5)).astype(BF16)


def _mixproj(x2d, fox_o, gla_o, gr, g_gla, w_out_bf, g_cross, w_q_bf):
    t = x2d.shape[0]
    tt = min(TOK_TILE, t)
    row = lambda w: pl.BlockSpec((tt, w), lambda i: (i, 0))
    return pl.pallas_call(
        _mixproj_kernel,
        grid=(t // tt,),
        in_specs=[row(D_MODEL), row(FOX_W), row(GLA_VW), row(GLA_VW), _full_spec(g_gla.shape),
                  _full_spec(w_out_bf.shape), _full_spec(g_cross.shape), _full_spec(w_q_bf.shape)],
        out_specs=[row(D_MODEL), row(D_MODEL)],
        out_shape=[jax.ShapeDtypeStruct((t, D_MODEL), F32), jax.ShapeDtypeStruct((t, D_MODEL), BF16)],
        compiler_params=_cparams("parallel"),
        name="mixproj",
    )(x2d, fox_o, gla_o, gr, g_gla, w_out_bf, g_cross, w_q_bf)


def _memkv_kernel(m_ref, g_ref, wk_ref, wv_ref, k_ref, v_ref):
    m = _rms(m_ref[...], g_ref[...]).astype(BF16)
    k_ref[...] = _dot(m, wk_ref[...])
    v_ref[...] = _dot(m, wv_ref[...])


def _memkv(mem2d, g_mem, wk_bf, wv_bf):
    t = mem2d.shape[0]
    tt = min(TOK_TILE, t)
    row = pl.BlockSpec((tt, D_MODEL), lambda i: (i, 0))
    return pl.pallas_call(
        _memkv_kernel,
        grid=(t // tt,),
        in_specs=[row, _full_spec(g_mem.shape), _full_spec(wk_bf.shape), _full_spec(wv_bf.shape)],
        out_specs=[row, row],
        out_shape=[jax.ShapeDtypeStruct((t, D_MODEL), F32)] * 2,
        compiler_params=_cparams("parallel"),
        name="memkv",
    )(mem2d, g_mem, wk_bf, wv_bf)


def _cross_kernel(q_ref, mk_ref, mv_ref, o_ref, *, per_head_memory):
    q = q_ref[0]
    if per_head_memory:
        lq = q.shape[0]
        n_mem = mk_ref.shape[2]
        q_hm = jnp.concatenate([q[:, h * DH_MEM:(h + 1) * DH_MEM] for h in range(H_MEM)], axis=0)
        mk = mk_ref[0, 0].reshape(n_mem * H_MEM, DH_MEM).astype(BF16)
        mv = mv_ref[0, 0].reshape(n_mem * H_MEM, DH_MEM).astype(BF16)
        s = _dot_nt(q_hm, mk)
        row_head = lax.broadcasted_iota(jnp.int32, s.shape, 0) // lq
        key_head = lax.broadcasted_iota(jnp.int32, s.shape, 1) % H_MEM
        s = jnp.where(row_head == key_head, s, NEG_INF)
        e = jnp.exp(s - jnp.max(s, axis=-1, keepdims=True))
        p = e / jnp.sum(e, axis=-1, keepdims=True)
        o = _dot(p.astype(BF16), mv).astype(o_ref.dtype)
        for h in range(H_MEM):
            o_ref[0, :, h * DH_MEM:(h + 1) * DH_MEM] = o[h * lq:(h + 1) * lq]
        return
    for h in range(H_MEM):
        cols = slice(h * DH_MEM, (h + 1) * DH_MEM)
        mk = mk_ref[0, :, cols].astype(BF16)
        mv = mv_ref[0, :, cols].astype(BF16)
        s = _dot_nt(q[:, cols], mk)
        e = jnp.exp(s - jnp.max(s, axis=-1, keepdims=True))
        p = e / jnp.sum(e, axis=-1, keepdims=True)
        o_ref[0, :, cols] = _dot(p.astype(BF16), mv).astype(o_ref.dtype)


def _cross(q3, mk, mv, layer=None):
    b, lq, _ = q3.shape
    tl = min(TOK_TILE, lq)
    if layer is None:
        mem_spec = pl.BlockSpec((1,) + mk.shape[1:], lambda i, j: (i, 0, 0))
    else:
        mem_spec = pl.BlockSpec((1, 1) + mk.shape[2:], lambda i, j: (layer, i, 0, 0, 0))
    return pl.pallas_call(
        functools.partial(_cross_kernel, per_head_memory=layer is not None),
        grid=(b, lq // tl),
        in_specs=[pl.BlockSpec((1, tl, D_MODEL), lambda i, j: (i, j, 0)), mem_spec, mem_spec],
        out_specs=pl.BlockSpec((1, tl, D_MODEL), lambda i, j: (i, j, 0)),
        out_shape=jax.ShapeDtypeStruct((b, lq, D_MODEL), BF16),
        compiler_params=_cparams("parallel", "parallel"),
        name="cross_attn",
    )(q3, mk, mv)


def _peer_prep_kernel(u_ref, v_ref, ub_ref, vt_ref):
    ub_ref[...] = u_ref[...].astype(BF16)
    vt_ref[...] = v_ref[...].T.astype(BF16)


def _peer_prep(u, v):
    ne = u.shape[0]
    eb = min(PREP_EBLK, ne)
    return pl.pallas_call(
        _peer_prep_kernel,
        grid=(ne // eb,),
        in_specs=[pl.BlockSpec((eb, D_MODEL), lambda i: (i, 0))] * 2,
        out_specs=[pl.BlockSpec((eb, D_MODEL), lambda i: (i, 0)), pl.BlockSpec((D_MODEL, eb), lambda i: (0, i))],
        out_shape=[jax.ShapeDtypeStruct((ne, D_MODEL), BF16), jax.ShapeDtypeStruct((D_MODEL, ne), BF16)],
        compiler_params=_cparams("parallel"),
        name="peer_prep",
    )(u, v)


def _rows_per_word():
    return 4 // jnp.dtype(BF16).itemsize


def _pack_rows(x):
    return pltpu.bitcast(x, jnp.uint32)


def _unpack_rows(x):
    return pltpu.bitcast(x, BF16)


def _top16_rows(s, break_ties):
    nrow = s.shape[0]
    rowid = lax.broadcasted_iota(jnp.int32, s.shape, 0)
    rank = jnp.full(s.shape, float(PEER_TOPK), F32)
    work = s
    vals = []
    for r in range(PEER_TOPK):
        mx = jnp.max(work, axis=0, keepdims=True)
        if break_ties:
            sel = rowid == jnp.min(jnp.where(work == mx, rowid, nrow), axis=0, keepdims=True)
        else:
            sel = work == mx
        rank = jnp.where(sel, float(r), rank)
        work = jnp.where(sel, -jnp.inf, work)
        vals.append(mx)
    return rank, jnp.concatenate(vals, axis=0)


def _ranked_16(rank):
    n = jnp.sum(jnp.where(rank < float(PEER_TOPK), 1.0, 0.0), axis=0, keepdims=True)
    return jnp.where(n == float(PEER_TOPK), 1.0, 0.0)


def _peer_route_kernel(x_ref, x1_ref, wo_ref, g_ref, wq_ref, sk_ref, ca_ref, cb_ref, cg_ref,
                       x2_ref, hn_ref, c1_ref, nrow_ref, p2_ref, rk2_ref, q_sc, s_sc):
    tt = x_ref.shape[0]
    nch = tt // LANES
    x2 = x1_ref[...] + _dot(x_ref[...], wo_ref[...])
    x2_ref[...] = x2
    hn = _rms(x2, g_ref[...])
    hn_ref[...] = hn.T.astype(BF16)
    q_sc[...] = _dot(hn.astype(BF16), wq_ref[...]).astype(BF16)
    for hj in range(2 * H_PEER):
        st = _dot_nt(sk_ref[hj % 2], q_sc[:, hj * N_KEYS:(hj + 1) * N_KEYS])
        for ch in range(nch):
            s_sc[hj, ch] = st[:, ch * LANES:(ch + 1) * LANES]

    cand_row = lax.broadcasted_iota(jnp.int32, (N_CAND_PAD, LANES), 0)

    per_unit = 2 if nch % 2 == 0 else 1

    def unit(i, carry):
        h = i // (nch // per_unit)
        first = (i % (nch // per_unit)) * per_unit
        tie_free = route_chunk(h, first, False)
        for d in range(1, per_unit):
            tie_free = tie_free * route_chunk(h, first + d, False)

        @pl.when(jnp.min(tie_free) < 0.5)
        def _():
            for d in range(per_unit):
                route_chunk(h, first + d, True)

        return carry

    def route_chunk(h, ch, break_ties):
        s1 = s_sc[2 * h, ch]
        s2 = s_sc[2 * h + 1, ch]
        rank1, v1 = _top16_rows(s1, break_ties)
        rank2, v2 = _top16_rows(s2, break_ties)
        cand = _sel_dot_l(ca_ref[...], v1) + _sel_dot_l(cb_ref[...], v2)
        cand = jnp.where(cand_row < N_CAND, cand, -jnp.inf)
        crank, _ = _top16_rows(cand, break_ties)
        chosen = crank < float(PEER_TOPK)
        n_of_a = _dot(cg_ref[...], jnp.where(chosen, 1.0, 0.0).astype(BF16))
        z = jnp.sum(jnp.where(chosen, jnp.exp(cand - cand[0:1]), 0.0), axis=0, keepdims=True)
        nrow = jnp.zeros_like(rank1)
        for a in range(PEER_TOPK):
            nrow = jnp.where(rank1 == float(a), n_of_a[a:a + 1], nrow)
        c1_ref[h, ch] = jnp.exp(s1 - v1[0:1]) / z
        nrow_ref[h, ch] = nrow
        p2_ref[h, ch] = _pack_rows(jnp.exp(s2 - v2[0:1]).astype(BF16))
        rk2_ref[h, ch] = _pack_rows(rank2.astype(BF16))
        return _ranked_16(rank1) * _ranked_16(rank2) * _ranked_16(crank)

    lax.fori_loop(0, H_PEER * nch // per_unit, unit, 0)


def _peer_route(oc2d, x1, w_o_bf, g_ffn, w_q_bf, sk_bf):
    t = x1.shape[0]
    tt = min(PEER_TOK, t)
    nch = tt // LANES
    pairs = [(a, b) for a in range(PEER_TOPK) for b in range(PEER_TOPK) if (a + 1) * (b + 1) <= PEER_TOPK]
    assert len(pairs) == N_CAND
    ca = np.zeros((N_CAND_PAD, PEER_TOPK), np.float32)
    cb = np.zeros((N_CAND_PAD, PEER_TOPK), np.float32)
    cg = np.zeros((PEER_TOPK, N_CAND_PAD), np.float32)
    for i, (a, b) in enumerate(pairs):
        ca[i, a] = 1.0
        cb[i, b] = 1.0
        cg[a, i] = 1.0
    ca, cb, cg = (jnp.asarray(m, BF16) for m in (ca, cb, cg))
    row = lambda w: pl.BlockSpec((tt, w), lambda i: (i, 0))
    krows = N_KEYS // _rows_per_word()
    sel = pl.BlockSpec((H_PEER, nch, N_KEYS, LANES), lambda i: (0, i, 0, 0))
    sel_shape = jax.ShapeDtypeStruct((H_PEER, t // LANES, N_KEYS, LANES), F32)
    packed = pl.BlockSpec((H_PEER, nch, krows, LANES), lambda i: (0, i, 0, 0))
    packed_shape = jax.ShapeDtypeStruct((H_PEER, t // LANES, krows, LANES), jnp.uint32)
    return pl.pallas_call(
        _peer_route_kernel,
        grid=(t // tt,),
        in_specs=[row(D_MODEL), row(D_MODEL), _full_spec(w_o_bf.shape), _full_spec(g_ffn.shape),
                  _full_spec(w_q_bf.shape), _full_spec(sk_bf.shape),
                  _full_spec(ca.shape), _full_spec(cb.shape), _full_spec(cg.shape)],
        out_specs=[row(D_MODEL), pl.BlockSpec((D_MODEL, tt), lambda i: (0, i)), sel, sel, packed, packed],
        out_shape=[jax.ShapeDtypeStruct((t, D_MODEL), F32), jax.ShapeDtypeStruct((D_MODEL, t), BF16),
                   sel_shape, sel_shape, packed_shape, packed_shape],
        scratch_shapes=[pltpu.VMEM((tt, 2 * H_PEER * N_KEYS), BF16),
                        pltpu.VMEM((2 * H_PEER, nch, N_KEYS, LANES), F32)],
        compiler_params=_cparams("parallel"),
        name="peer_route",
    )(oc2d, x1, w_o_bf, g_ffn, w_q_bf, sk_bf, ca, cb, cg)


def _peer_mix_kernel(x2_ref, hn_ref, u_ref, vt_ref, c1_ref, nrow_ref, p2_ref, rk2_ref, gf_ref,
                     y_ref, w_sc, acc_sc, *, final_norm):
    e = pl.program_id(1)
    tt = hn_ref.shape[1]
    nch = tt // LANES
    n_i1 = u_ref.shape[0] // N_KEYS

    @pl.when(e == 0)
    def _():
        acc_sc[...] = jnp.zeros_like(acc_sc)

    krows = N_KEYS // _rows_per_word()
    acc = acc_sc[...]
    for sb in range(n_i1 // PEER_SUB_I1):
        i1_lo = sb * PEER_SUB_I1
        ex = slice(i1_lo * N_KEYS, (i1_lo + PEER_SUB_I1) * N_KEYS)
        exw = slice(i1_lo * krows, (i1_lo + PEER_SUB_I1) * krows)
        for ch in range(nch):
            cols = slice(ch * LANES, (ch + 1) * LANES)
            for r in range(i1_lo, i1_lo + PEER_SUB_I1):
                w = jnp.zeros((N_KEYS, LANES), BF16)
                for h in range(H_PEER):
                    wide = lambda ref: jnp.broadcast_to(ref[h, ch, r:r + 1, :], (N_KEYS, LANES)).astype(BF16)
                    p2 = _unpack_rows(p2_ref[h, ch])
                    hit = _unpack_rows(rk2_ref[h, ch]) < wide(nrow_ref)
                    w = w + jnp.where(hit, p2, jnp.zeros_like(p2)) * wide(c1_ref)
                w_sc[r * krows:(r + 1) * krows, cols] = _pack_rows(w)
        gelu_a = jax.nn.gelu(_dot(u_ref[ex, :], hn_ref[...])).astype(BF16)
        w_all = _unpack_rows(w_sc[exw, :]) * gelu_a
        acc = acc + _dot(vt_ref[:, ex], w_all)
    acc_sc[...] = acc

    @pl.when(e == pl.num_programs(1) - 1)
    def _():
        y = x2_ref[...] + acc_sc[...].T
        y_ref[...] = _rms(y, gf_ref[...]) if final_norm else y


def _peer_mix(x2, hn, u_bf, vt_bf, c1, nrow, p2, rk2, g_final, final_norm):
    t = x2.shape[0]
    tt = min(PEER_TOK, t)
    nch = tt // LANES
    ne = u_bf.shape[0]
    eb = min(PEER_EBLK, ne)
    n_i1 = eb // N_KEYS
    row = lambda w: pl.BlockSpec((tt, w), lambda i, e: (i, 0))
    part = pl.BlockSpec((H_PEER, nch, n_i1, LANES), lambda i, e: (0, i, e, 0))
    full = pl.BlockSpec((H_PEER, nch, N_KEYS // _rows_per_word(), LANES), lambda i, e: (0, i, 0, 0))
    return pl.pallas_call(
        functools.partial(_peer_mix_kernel, final_norm=final_norm),
        grid=(t // tt, ne // eb),
        in_specs=[row(D_MODEL), pl.BlockSpec((D_MODEL, tt), lambda i, e: (0, i)),
                  pl.BlockSpec((eb, D_MODEL), lambda i, e: (e, 0)),
                  pl.BlockSpec((D_MODEL, eb), lambda i, e: (0, e)),
                  part, part, full, full, pl.BlockSpec(g_final.shape, lambda i, e: (0, 0))],
        out_specs=row(D_MODEL),
        out_shape=jax.ShapeDtypeStruct((t, D_MODEL), F32),
        scratch_shapes=[pltpu.VMEM((eb // _rows_per_word(), tt), jnp.uint32), pltpu.VMEM((D_MODEL, tt), F32)],
        compiler_params=_cparams("parallel", "arbitrary"),
        name="peer_mix",
    )(x2, hn, u_bf, vt_bf, c1, nrow, p2, rk2, g_final)


def _prep_in_weights(w_in, b_fox_f, w_gate2, b_gate):
    o = np.cumsum((0,) + IN_SIZES)
    seg = lambda i: w_in[:, int(o[i]):int(o[i + 1])]
    pad = jnp.zeros((D_MODEL, LANES - H_FOX - GLA_GATE_RANK), w_in.dtype)
    w_perm = jnp.concatenate([seg(0), seg(1), seg(2), seg(4), seg(5), seg(6), seg(8), seg(3), seg(7), pad],
                             axis=1).astype(BF16)
    b_small = jnp.zeros((1, LANES), F32).at[0, 0:H_FOX].set(b_fox_f)
    wg_pad = jnp.zeros((LANES, GLA_KW), F32).at[H_FOX:H_FOX + GLA_GATE_RANK].set(w_gate2).astype(BF16)
    return w_perm, b_small, wg_pad, b_gate.reshape(1, GLA_KW)


def kernel(x_prompt, x_sample, cache_fox_k, cache_fox_v, cache_fox_logf, state_gla, cache_mem_k, cache_mem_v, page_table, mem_prompt, g_mix, w_in, b_fox_f, w_gla_gate2, b_gla_gate, g_gla_out, w_out, g_cross, g_mem, w_mem_k, w_mem_v, w_cross_q, w_cross_o, g_ffn, peer_w_q, peer_subkeys, peer_u, peer_v, g_final):
    depth = w_in.shape[0]
    bp, lp, _ = x_prompt.shape
    bs, ls, _ = x_sample.shape
    n_mem = mem_prompt.shape[1]
    n_pool, page = cache_fox_k.shape[1:3]
    xp = x_prompt.reshape(bp * lp, D_MODEL)
    xs = x_sample.reshape(bs * ls, D_MODEL)
    vec = lambda a: a.reshape(1, -1)
    kcache_t = cache_fox_k.transpose(0, 1, 3, 4, 2).reshape(depth, n_pool, FOX_W, page)
    vcache_t = cache_fox_v.transpose(0, 1, 3, 4, 2).reshape(depth, n_pool, FOX_W, page)
    lfcache_t = cache_fox_logf.transpose(0, 1, 3, 2)
    outs = [[] for _ in range(10)]
    for l in range(depth):
        w_perm, b_small, wg_pad, b_gate = _prep_in_weights(w_in[l], b_fox_f[l], w_gla_gate2[l], b_gla_gate[l])
        w_out_bf = w_out[l].astype(BF16)
        w_cq_bf = w_cross_q[l].astype(BF16)
        w_co_bf = w_cross_o[l].astype(BF16)
        w_pq_bf = peer_w_q[l].astype(BF16)
        sk_bf = peer_subkeys[l].astype(BF16)
        u_bf, vt_bf = _peer_prep(peer_u[l], peer_v[l])
        last = l == depth - 1

        def tail(x, fox_o, gla_o, gr, mk, mv, nb, mem_layer):
            x1, qc = _mixproj(x, fox_o, gla_o, gr, vec(g_gla_out[l]), w_out_bf, vec(g_cross[l]), w_cq_bf)
            oc = _cross(qc.reshape(nb, -1, D_MODEL), mk, mv, mem_layer).reshape(-1, D_MODEL)
            x2, hn, c1, nrow, p2, rk2 = _peer_route(oc, x1, w_co_bf, vec(g_ffn[l]), w_pq_bf, sk_bf)
            return _peer_mix(x2, hn, u_bf, vt_bf, c1, nrow, p2, rk2, vec(g_final), last)

        fq_bf, fk, fv, lf, gq, gk, gv, la, gr = _inproj(xp, vec(g_mix[l]), w_perm, b_small, wg_pad, b_gate,
                                                        DH_FOX ** -0.5 * _LOG2E)
        qc, kc, vt, fk_t, fv_t, lf_t = _fox_prep(fq_bf, fk, fv, lf, bp, lp)
        fox_o = _fox_prompt(qc, kc, vt, bp, lp)
        gla_o, s_fin = _gla_prompt(gq, gk, gv, la, bp, lp)
        mk, mv = _memkv(mem_prompt.reshape(bp * n_mem, D_MODEL), vec(g_mem[l]), w_mem_k[l].astype(BF16),
                        w_mem_v[l].astype(BF16))
        mk3 = mk.reshape(bp, n_mem, D_MODEL)
        mv3 = mv.reshape(bp, n_mem, D_MODEL)
        xp = tail(xp, fox_o, gla_o, gr, mk3, mv3, bp, None)
        outs[0].append(fk_t.reshape(bp, H_FOX, DH_FOX, lp).transpose(0, 3, 1, 2))
        outs[1].append(fv_t.reshape(bp, H_FOX, DH_FOX, lp).transpose(0, 3, 1, 2))
        outs[2].append(lf_t.transpose(0, 2, 1))
        outs[3].append(s_fin)
        outs[4].append(mk.reshape(bp, n_mem, H_MEM, DH_MEM))
        outs[5].append(mv.reshape(bp, n_mem, H_MEM, DH_MEM))

        fq_bf, fk, fv, lf, gq, gk, gv, la, gr = _inproj(xs, vec(g_mix[l]), w_perm, b_small, wg_pad, b_gate,
                                                        DH_FOX ** -0.5)
        seqs = lambda a: a.reshape(bs, ls, FOX_W)
        fox_o = _fox_sample(seqs(fq_bf), seqs(fk), seqs(fv), lf.reshape(bs, ls, H_FOX).transpose(0, 2, 1),
                            kcache_t, vcache_t, lfcache_t, page_table, l).reshape(bs * ls, FOX_W)
        heads = lambda a, d: a.reshape(bs, ls, H_GLA, d).transpose(0, 2, 1, 3).reshape(bs * H_GLA, ls, d)
        gla_o, s_new = _gla_sample(heads(gq, DK_GLA), heads(gk, DK_GLA), heads(gv, DV_GLA), heads(la, DK_GLA),
                                   state_gla[l].reshape(bs * H_GLA, DK_GLA, DV_GLA))
        gla_o = gla_o.reshape(bs, H_GLA, ls, DV_GLA).transpose(0, 2, 1, 3).reshape(bs * ls, GLA_VW)
        xs = tail(xs, fox_o, gla_o, gr, cache_mem_k, cache_mem_v, bs, l)
        outs[6].append(fk.reshape(bs, ls, H_FOX, DH_FOX))
        outs[7].append(fv.reshape(bs, ls, H_FOX, DH_FOX))
        outs[8].append(lf.reshape(bs, ls, H_FOX))
        outs[9].append(s_new.reshape(bs, H_GLA, DK_GLA, DV_GLA))
    stacked = [jnp.stack(o, axis=0) for o in outs]
    return (xp.reshape(bp, lp, D_MODEL), xs.reshape(bs, ls, D_MODEL), *stacked)
```

```python
import functools

import numpy as np
import jax
import jax.numpy as jnp
from jax import lax
from jax.experimental import pallas as pl
from jax.experimental.pallas import tpu as pltpu

F32 = jnp.float32
BF16 = jnp.bfloat16

D_MODEL = 1024
H_FOX, DH_FOX = 8, 64
H_GLA, DK_GLA, DV_GLA = 4, 64, 128
GLA_GATE_RANK = 16
GLA_TAU = 16.0
GLA_CHUNK = 64
H_MEM = 4
DH_MEM = D_MODEL // H_MEM
N_KEYS = 128
H_PEER = 8
PEER_TOPK = 16
EPS = 1e-6
NEG_INF = -1e30
_LOG2E = float(np.log2(np.e))
FOX_W = H_FOX * DH_FOX
GLA_KW = H_GLA * DK_GLA
GLA_VW = H_GLA * DV_GLA
IN_SIZES = (FOX_W, FOX_W, FOX_W, H_FOX, GLA_KW, GLA_KW, GLA_VW, GLA_GATE_RANK, GLA_VW)

LANES = 128
VMEM_LIMIT_BYTES = 56 * 2**20

TOK_TILE = 512
INPROJ_TILE = 256
FOX_TILE = 512
FOX_KTILE = 1024
FOX_HP_PER_STEP = 2
GLA_GROUP = 512
PEER_TOK = 512
PEER_EBLK = 2048
PEER_SUB_I1 = 8
FOX_PAGES_PER_STEP = 16
PREP_EBLK = 512
N_CAND = 50
N_CAND_PAD = 64


def _cparams(*sem):
    return pltpu.CompilerParams(dimension_semantics=sem, vmem_limit_bytes=VMEM_LIMIT_BYTES)


def _dot(a, b):
    return jnp.dot(a, b, preferred_element_type=F32)


def _dot_nt(a, b):
    return lax.dot_general(a, b, (((1,), (1,)), ((), ())), preferred_element_type=F32)


def _split3(x):
    hi = x.astype(BF16)
    r = x - hi.astype(F32)
    mid = r.astype(BF16)
    lo = (r - mid.astype(F32)).astype(BF16)
    return hi, mid, lo


def _sel_dot_l(m01, x):
    hi, mid, lo = _split3(x)
    return (_dot(m01, hi) + _dot(m01, mid)) + _dot(m01, lo)


def _sel_dot_r(x, m01):
    hi, mid, lo = _split3(x)
    return (_dot(hi, m01) + _dot(mid, m01)) + _dot(lo, m01)


def _rms(x, g):
    return x * lax.rsqrt(jnp.mean(x * x, axis=-1, keepdims=True) + EPS) * g


def _logsig(x):
    return jnp.minimum(x, 0.0) - jnp.log1p(jnp.exp(-jnp.abs(x)))


def _full_spec(shape):
    nd = len(shape)
    return pl.BlockSpec(shape, lambda *_: (0,) * nd)


_C_FQ, _C_FK, _C_FV = 0, FOX_W, 2 * FOX_W
_C_GQ = 3 * FOX_W
_C_GK = _C_GQ + GLA_KW
_C_GV = _C_GK + GLA_KW
_C_GR = _C_GV + GLA_VW
_C_SM = _C_GR + GLA_VW
_C_END = _C_SM + LANES


def _inproj_kernel(x_ref, g_ref, w_ref, bsm_ref, wg_ref, bg_ref,
                   fq_ref, fk_ref, fv_ref, lf_ref, gq_ref, gk_ref, gv_ref, la_ref, gr_ref, *, q_scale):
    h = _rms(x_ref[...], g_ref[...]).astype(BF16)

    def seg(lo, hi):
        return _dot(h, w_ref[:, lo:hi])

    fq_ref[...] = (seg(_C_FQ, _C_FK) * q_scale).astype(BF16)
    fk_ref[...] = seg(_C_FK, _C_FV)
    fv_ref[...] = seg(_C_FV, _C_GQ)
    gq_ref[...] = seg(_C_GQ, _C_GK) * (DK_GLA ** -0.5)
    gk_ref[...] = seg(_C_GK, _C_GV)
    gv_ref[...] = seg(_C_GV, _C_GR)
    gr_ref[...] = seg(_C_GR, _C_SM)
    ysm = seg(_C_SM, _C_END)
    lf_ref[...] = _logsig(ysm + bsm_ref[...])[:, 0:H_FOX]
    z = _dot(ysm.astype(BF16), wg_ref[...]) + bg_ref[...]
    la_ref[...] = _logsig(z) / GLA_TAU


def _inproj(x2d, g_mix, w_perm, b_small, wg_pad, b_gate, q_scale):
    t = x2d.shape[0]
    tt = min(INPROJ_TILE, t)
    row = lambda w: pl.BlockSpec((tt, w), lambda i: (i, 0))
    outs = [(FOX_W, BF16), (FOX_W, F32), (FOX_W, F32), (H_FOX, F32),
            (GLA_KW, F32), (GLA_KW, F32), (GLA_VW, F32), (GLA_KW, F32), (GLA_VW, F32)]
    return pl.pallas_call(
        functools.partial(_inproj_kernel, q_scale=q_scale),
        grid=(t // tt,),
        in_specs=[row(D_MODEL), _full_spec(g_mix.shape), _full_spec(w_perm.shape), _full_spec(b_small.shape),
                  _full_spec(wg_pad.shape), _full_spec(b_gate.shape)],
        out_specs=[row(w) for w, _ in outs],
        out_shape=[jax.ShapeDtypeStruct((t, w), dt) for w, dt in outs],
        compiler_params=_cparams("parallel"),
        name="inproj",
    )(x2d, g_mix, w_perm, b_small, wg_pad, b_gate)


_FOX_QW = 4 * LANES
_FOX_KW = 2 * LANES
_N_HP = FOX_W // LANES


def _fox_aug_tables():
    pq = np.zeros((3, H_FOX, _N_HP * 2 * LANES), np.float32)
    oq = np.zeros((1, _N_HP * 2 * LANES), np.float32)
    pk = np.zeros((3, H_FOX, _N_HP * LANES), np.float32)
    ok = np.zeros((1, _N_HP * LANES), np.float32)
    for hp in range(_N_HP):
        for half in range(2):
            head = 2 * hp + half
            qb = (hp * 2 + half) * LANES
            kb = hp * LANES
            for j in range(3):
                pq[j, head, qb + 6 * half + j] = 1.0
                oq[0, qb + 6 * half + 3 + j] = 1.0
                pk[j, head, kb + 6 * half + 3 + j] = 1.0
                ok[0, kb + 6 * half + j] = 1.0
    return jnp.asarray(pq, BF16), jnp.asarray(oq, F32), jnp.asarray(pk, BF16), jnp.asarray(ok, F32)


def _fox_prep_kernel(q_ref, k_ref, v_ref, lf_ref, tril_ref, pq_ref, oq_ref, pk_ref, ok_ref, eye_ref,
                     qc_ref, kc_ref, vt_ref, fkt_ref, fvt_ref, lft_ref, carry_sc):
    @pl.when(pl.program_id(1) == 0)
    def _():
        carry_sc[...] = jnp.zeros_like(carry_sc)

    lf_parts = _split3(lf_ref[...])
    eye = eye_ref[...]
    lft_ref[0] = (_dot_nt(eye, lf_parts[0]) + _dot_nt(eye, lf_parts[1])) + _dot_nt(eye, lf_parts[2])
    c = _sel_dot_l(tril_ref[...], lf_ref[...]) + carry_sc[0:1, 0:H_FOX]
    carry_sc[0:1, 0:H_FOX] = c[-1:, :]
    parts = _split3(c * _LOG2E)
    aug_q = oq_ref[...]
    aug_k = ok_ref[...]
    for j in range(3):
        aug_q = aug_q + _dot(parts[j], pq_ref[j])
        aug_k = aug_k - _dot(parts[j], pk_ref[j])
    aug_q = aug_q.astype(BF16)
    aug_k = aug_k.astype(BF16)
    lane = lax.broadcasted_iota(jnp.int32, (q_ref.shape[0], LANES), 1)
    for hp in range(_N_HP):
        q = q_ref[:, hp * LANES:(hp + 1) * LANES]
        zero = jnp.zeros_like(q)
        base = hp * _FOX_QW
        qc_ref[:, base:base + LANES] = jnp.where(lane < DH_FOX, q, zero)
        qc_ref[:, base + LANES:base + 2 * LANES] = aug_q[:, (2 * hp) * LANES:(2 * hp + 1) * LANES]
        qc_ref[:, base + 2 * LANES:base + 3 * LANES] = jnp.where(lane >= DH_FOX, q, zero)
        qc_ref[:, base + 3 * LANES:base + 4 * LANES] = aug_q[:, (2 * hp + 1) * LANES:(2 * hp + 2) * LANES]
        kc_ref[:, hp * _FOX_KW:hp * _FOX_KW + LANES] = k_ref[:, hp * LANES:(hp + 1) * LANES].astype(BF16)
        kc_ref[:, hp * _FOX_KW + LANES:(hp + 1) * _FOX_KW] = aug_k[:, hp * LANES:(hp + 1) * LANES]
    v_t = v_ref[...].T
    vt_ref[0] = v_t.astype(BF16)
    fvt_ref[0] = v_t
    fkt_ref[0] = k_ref[...].T


def _fox_prep(fq_bf, fk, fv, lf, batch, seq):
    tt = min(FOX_TILE, seq)
    nt = seq // tt
    tril = jnp.asarray(np.tril(np.ones((tt, tt), np.float32)), BF16)
    consts = [tril, *_fox_aug_tables(), jnp.eye(H_FOX, dtype=BF16)]
    row = lambda w: pl.BlockSpec((tt, w), lambda b, i: (b * nt + i, 0))
    col = lambda h: pl.BlockSpec((1, h, tt), lambda b, i: (b, 0, i))
    return pl.pallas_call(
        _fox_prep_kernel,
        grid=(batch, nt),
        in_specs=[row(FOX_W), row(FOX_W), row(FOX_W), row(H_FOX)] + [_full_spec(a.shape) for a in consts],
        out_specs=[row(_N_HP * _FOX_QW), row(_N_HP * _FOX_KW), col(FOX_W), col(FOX_W), col(FOX_W), col(H_FOX)],
        out_shape=[jax.ShapeDtypeStruct((batch * seq, _N_HP * _FOX_QW), BF16),
                   jax.ShapeDtypeStruct((batch * seq, _N_HP * _FOX_KW), BF16),
                   jax.ShapeDtypeStruct((batch, FOX_W, seq), BF16),
                   jax.ShapeDtypeStruct((batch, FOX_W, seq), F32),
                   jax.ShapeDtypeStruct((batch, FOX_W, seq), F32),
                   jax.ShapeDtypeStruct((batch, H_FOX, seq), F32)],
        scratch_shapes=[pltpu.VMEM((8, LANES), F32)],
        compiler_params=_cparams("parallel", "arbitrary"),
        name="fox_prep",
    )(fq_bf, fk, fv, lf, *consts)


def _fox_prompt_kernel(qi_tab, ki_tab, q_ref, k_ref, vt_ref, o_ref,
                       m_sc, l_sc, acc_sc):
    t = pl.program_id(2)
    qi = qi_tab[t]
    ki = ki_tab[t]
    tq = q_ref.shape[0]
    tk = k_ref.shape[0]

    @pl.when(ki == 0)
    def _():
        m_sc[...] = jnp.full_like(m_sc, NEG_INF)
        l_sc[...] = jnp.zeros_like(l_sc)
        acc_sc[...] = jnp.zeros_like(acc_sc)

    def step(masked):
        n_hh = 2 * FOX_HP_PER_STEP

        def scores_of(hh):
            kc = k_ref[:, (hh // 2) * _FOX_KW:(hh // 2 + 1) * _FOX_KW]
            return _dot_nt(kc, q_ref[:, hh * _FOX_KW:(hh + 1) * _FOX_KW])

        def softmax_update(hh, s):
            if masked:
                kpos = lax.broadcasted_iota(jnp.int32, (tk, tq), 0) + ki * tk
                qpos = lax.broadcasted_iota(jnp.int32, (tk, tq), 1) + qi * tq
                s = jnp.where(kpos <= qpos, s, NEG_INF)
            m_prev = m_sc[hh:hh + 1, :]
            m_new = jnp.maximum(m_prev, jnp.max(s, axis=0, keepdims=True))
            alpha = jnp.exp2(m_prev - m_new)
            p = jnp.exp2(s - m_new)
            l_sc[hh:hh + 1, :] = alpha * l_sc[hh:hh + 1, :] + jnp.sum(p, axis=0, keepdims=True)
            m_sc[hh:hh + 1, :] = m_new
            return alpha, p.astype(BF16)

        def accumulate(hh, alpha, p):
            vt = vt_ref[0, (hh // 2) * LANES:(hh // 2 + 1) * LANES, :]
            acc_sc[hh] = alpha * acc_sc[hh] + _dot(vt, p)

        s_next = {0: scores_of(0)}
        if n_hh > 1:
            s_next[1] = scores_of(1)
        pending = None
        for hh in range(n_hh):
            alpha_p = softmax_update(hh, s_next.pop(hh))
            if hh + 2 < n_hh:
                s_next[hh + 2] = scores_of(hh + 2)
            if pending is not None:
                accumulate(*pending)
            pending = (hh, *alpha_p)
        accumulate(*pending)

    last = (qi * tq) // tk

    @pl.when(ki < last)
    def _():
        step(False)

    @pl.when(ki == last)
    def _():
        step(True)
        dim = lax.broadcasted_iota(jnp.int32, (LANES, tq), 0)
        for hp in range(FOX_HP_PER_STEP):
            o0 = acc_sc[2 * hp] / l_sc[2 * hp:2 * hp + 1, :]
            o1 = acc_sc[2 * hp + 1] / l_sc[2 * hp + 1:2 * hp + 2, :]
            o_ref[:, hp * LANES:(hp + 1) * LANES] = jnp.where(dim < DH_FOX, o0, o1).T.astype(o_ref.dtype)


def _fox_prompt(qc, kc, vt, batch, seq):
    tq = min(FOX_TILE, seq)
    tk = min(FOX_KTILE, seq)
    nq = seq // tq
    nk = seq // tk
    pairs = [(i, j) for i in range(nq) for j in range((i * tq) // tk + 1)]
    qi_tab = jnp.asarray([p[0] for p in pairs], jnp.int32)
    ki_tab = jnp.asarray([p[1] for p in pairs], jnp.int32)
    g = FOX_HP_PER_STEP
    grid_spec = pltpu.PrefetchScalarGridSpec(
        num_scalar_prefetch=2,
        grid=(batch, _N_HP // g, len(pairs)),
        in_specs=[
            pl.BlockSpec((tq, g * _FOX_QW), lambda b, h, t, qt, kt: (b * nq + qt[t], h)),
            pl.BlockSpec((tk, g * _FOX_KW), lambda b, h, t, qt, kt: (b * nk + kt[t], h)),
            pl.BlockSpec((1, g * LANES, tk), lambda b, h, t, qt, kt: (b, h, kt[t])),
        ],
        out_specs=pl.BlockSpec((tq, g * LANES), lambda b, h, t, qt, kt: (b * nq + qt[t], h)),
        scratch_shapes=[pltpu.VMEM((2 * g, tq), F32), pltpu.VMEM((2 * g, tq), F32),
                        pltpu.VMEM((2 * g, LANES, tq), F32)],
    )
    return pl.pallas_call(
        _fox_prompt_kernel,
        grid_spec=grid_spec,
        out_shape=jax.ShapeDtypeStruct((batch * seq, FOX_W), BF16),
        compiler_params=_cparams("parallel", "parallel", "arbitrary"),
        name="fox_prompt",
    )(qi_tab, ki_tab, qc, kc, vt)


def _fox_sample_kernel(pt_ref, q_ref, *refs, n_pp):
    kc, vc, lfc = refs[0:n_pp], refs[n_pp:2 * n_pp], refs[2 * n_pp:3 * n_pp]
    (kn_ref, vn_ref, lfn_ref, tri_ref, trin_ref, o_ref,
     qbd_sc, m_sc, l_sc, acc_sc, carry_sc, kall_sc, vall_sc) = refs[3 * n_pp:]
    step = pl.program_id(1)
    ls = q_ref.shape[1]
    nr = H_FOX * ls
    page = tri_ref.shape[0]
    row_head = lax.broadcasted_iota(jnp.int32, (nr, FOX_W), 0) // ls
    lane_head = lax.broadcasted_iota(jnp.int32, (nr, FOX_W), 1) // DH_FOX

    def rows_of_heads(x):
        return jnp.broadcast_to(x[:, None, :], (H_FOX, ls, x.shape[-1])).reshape(nr, x.shape[-1])

    cn_row = rows_of_heads(_sel_dot_r(lfn_ref[0], trin_ref[...]))
    tnew = lax.broadcasted_iota(jnp.int32, (nr, ls), 1)
    tq = lax.broadcasted_iota(jnp.int32, (nr, ls), 0) % ls
    cn_col = jnp.sum(jnp.where(tnew == tq, cn_row, 0.0), axis=-1, keepdims=True)

    @pl.when(step == 0)
    def _():
        qt = jnp.concatenate([q_ref[0]] * H_FOX, axis=0)
        qbd_sc[...] = jnp.where(row_head == lane_head, qt, jnp.zeros_like(qt))
        m_sc[...] = jnp.full_like(m_sc, NEG_INF)
        l_sc[...] = jnp.zeros_like(l_sc)
        acc_sc[...] = jnp.zeros_like(acc_sc)
        carry_sc[...] = jnp.zeros_like(carry_sc)

    def online(scores, weighted_values):
        m_prev = m_sc[...]
        m_new = m_prev
        for s in scores:
            m_new = jnp.maximum(m_new, jnp.max(s, axis=-1, keepdims=True))
        alpha = jnp.exp(m_prev - m_new)
        l_new = alpha * l_sc[...]
        acc = alpha * acc_sc[...]
        for s, pv in zip(scores, weighted_values):
            pr = jnp.exp(s - m_new)
            l_new = l_new + jnp.sum(pr, axis=-1, keepdims=True)
            acc = acc + pv(pr.astype(BF16))
        m_sc[...] = m_new
        l_sc[...] = l_new
        acc_sc[...] = acc

    c_loc = _sel_dot_r(jnp.concatenate([r[0, 0] for r in lfc], axis=0), tri_ref[...])
    carry = carry_sc[...][:, 0:1]
    qbd = qbd_sc[...]
    c_pages = []
    for j in range(n_pp):
        c_page = c_loc[j * H_FOX:(j + 1) * H_FOX] + carry
        carry = c_page[:, page - 1:page]
        c_pages.append(rows_of_heads(c_page))
        kall_sc[:, j * page:(j + 1) * page] = kc[j][0, 0].astype(BF16)
        vall_sc[:, j * page:(j + 1) * page] = vc[j][0, 0].astype(BF16)
    carry_sc[...] = jnp.broadcast_to(carry, carry_sc.shape)
    s_all = _dot(qbd, kall_sc[...]) + cn_col - jnp.concatenate(c_pages, axis=1)
    online([s_all], [lambda p: _dot_nt(p, vall_sc[...])])

    @pl.when(step == pl.num_programs(1) - 1)
    def _():
        s_new = _dot_nt(qbd, kn_ref[0].astype(BF16))
        s_new = s_new + cn_col - cn_row - rows_of_heads(jnp.broadcast_to(carry, (H_FOX, LANES)))[:, 0:1]
        s_new = jnp.where(tnew <= tq, s_new, NEG_INF)
        online([s_new], [lambda p: _dot(p, vn_ref[0].astype(BF16))])
        o = acc_sc[...] / l_sc[...]
        o = jnp.where(row_head == lane_head, o, 0.0)
        out = o[0:ls]
        for h in range(1, H_FOX):
            out = out + o[h * ls:(h + 1) * ls]
        o_ref[0] = out.astype(o_ref.dtype)


def _fox_sample(fq_bf, fk, fv, lfn_t, kcache_t, vcache_t, lfcache_t, page_table, layer):
    bs, n_pages = page_table.shape
    n_pp = min(FOX_PAGES_PER_STEP, n_pages)
    assert n_pages % n_pp == 0
    page = kcache_t.shape[3]
    ls = fq_bf.shape[1]
    nr = H_FOX * ls
    tri = jnp.asarray(np.triu(np.ones((page, page), np.float32)), BF16)
    trin = jnp.asarray(np.triu(np.ones((ls, ls), np.float32)), BF16)
    page_spec = lambda j, rows: pl.BlockSpec(
        (1, 1, rows, page), lambda b, s, pt: (layer, pt[b * n_pages + s * n_pp + j], 0, 0))
    per_seq = lambda *shape: pl.BlockSpec((1,) + shape, lambda b, s, pt: (b,) + (0,) * len(shape))
    const = lambda a: pl.BlockSpec(a.shape, lambda b, s, pt: (0,) * a.ndim)
    grid_spec = pltpu.PrefetchScalarGridSpec(
        num_scalar_prefetch=1,
        grid=(bs, n_pages // n_pp),
        in_specs=([per_seq(ls, FOX_W)] + [page_spec(j, FOX_W) for j in range(n_pp)] * 2
                  + [page_spec(j, H_FOX) for j in range(n_pp)]
                  + [per_seq(ls, FOX_W), per_seq(ls, FOX_W), per_seq(H_FOX, ls), const(tri), const(trin)]),
        out_specs=per_seq(ls, FOX_W),
        scratch_shapes=[pltpu.VMEM((nr, FOX_W), BF16), pltpu.VMEM((nr, 1), F32), pltpu.VMEM((nr, 1), F32),
                        pltpu.VMEM((nr, FOX_W), F32), pltpu.VMEM((H_FOX, LANES), F32),
                        pltpu.VMEM((FOX_W, n_pp * page), BF16), pltpu.VMEM((FOX_W, n_pp * page), BF16)],
    )
    return pl.pallas_call(
        functools.partial(_fox_sample_kernel, n_pp=n_pp),
        grid_spec=grid_spec,
        out_shape=jax.ShapeDtypeStruct((bs, ls, FOX_W), BF16),
        compiler_params=_cparams("parallel", "arbitrary"),
        name="fox_sample",
    )(page_table.reshape(-1), fq_bf, *([kcache_t] * n_pp), *([vcache_t] * n_pp), *([lfcache_t] * n_pp),
      fk, fv, lfn_t, tri, trin)


def _gla_prompt_kernel(q_ref, k_ref, v_ref, la_ref, tril_ref, ones_ref, onesT_ref,
                       o_ref, sfin_ref, s_sc):
    g = pl.program_id(1)
    n_tok = q_ref.shape[0]
    n_chunks = n_tok // GLA_CHUNK

    @pl.when(g == 0)
    def _():
        s_sc[...] = jnp.zeros_like(s_sc)

    la = la_ref[...]
    b = _sel_dot_l(tril_ref[...], la)
    bl = _sel_dot_l(ones_ref[...], la)
    q_dec = q_ref[...] * jnp.exp(b)
    q_dec_bf = q_dec.astype(BF16)
    k = k_ref[...]
    kd_bf = (k * jnp.exp(-b)).astype(BF16)
    kl_t = (k * jnp.exp(bl - b)).T
    v_bf = v_ref[...].astype(BF16)
    bl_col = _sel_dot_r(la.T, onesT_ref[...])

    r = lax.broadcasted_iota(jnp.int32, (n_tok, n_tok), 0)
    c = lax.broadcasted_iota(jnp.int32, (n_tok, n_tok), 1)
    causal = (r // GLA_CHUNK == c // GLA_CHUNK) & (c <= r)
    lane_q = lax.broadcasted_iota(jnp.int32, (n_tok, GLA_KW), 1) // DK_GLA
    zero_q = jnp.zeros_like(q_dec_bf)
    for h in range(H_GLA):
        att = _dot_nt(jnp.where(lane_q == h, q_dec_bf, zero_q), kd_bf)
        att = jnp.where(causal, att, 0.0)
        o_ref[:, h * DV_GLA:(h + 1) * DV_GLA] = _dot(att.astype(BF16), v_bf[:, h * DV_GLA:(h + 1) * DV_GLA])

    row_head = lax.broadcasted_iota(jnp.int32, (GLA_KW, GLA_VW), 0) // DK_GLA
    lane_head = lax.broadcasted_iota(jnp.int32, (GLA_KW, GLA_VW), 1) // DV_GLA
    same_head = row_head == lane_head
    tok_chunk = lax.broadcasted_iota(jnp.int32, (GLA_KW, n_tok), 1) // GLA_CHUNK
    for n in range(n_chunks):
        rows = slice(n * GLA_CHUNK, (n + 1) * GLA_CHUNK)
        s_prev = s_sc[...]
        o_ref[rows, :] += _dot(q_dec_bf[rows], s_prev.astype(BF16))
        kv = _dot(jnp.where(tok_chunk == n, kl_t, 0.0).astype(BF16), v_bf)
        dec = jnp.exp(bl_col[:, n * LANES:(n + 1) * LANES])
        dec = jnp.concatenate([dec] * (GLA_VW // LANES), axis=1)
        s_sc[...] = jnp.where(same_head, dec * s_prev + kv, 0.0)

    @pl.when(g == pl.num_programs(1) - 1)
    def _():
        s = s_sc[...]
        for h in range(H_GLA):
            sfin_ref[0, h] = s[h * DK_GLA:(h + 1) * DK_GLA, h * DV_GLA:(h + 1) * DV_GLA]


def _gla_prompt(gq, gk, gv, la, batch, seq):
    grp = min(GLA_GROUP, seq)
    ng = seq // grp
    nch = grp // GLA_CHUNK
    idx = np.arange(grp)
    same = (idx[:, None] // GLA_CHUNK) == (idx[None, :] // GLA_CHUNK)
    tril = jnp.asarray((same & (idx[None, :] <= idx[:, None])).astype(np.float32), BF16)
    ones = jnp.asarray(same.astype(np.float32), BF16)
    ones_t = jnp.asarray((idx[:, None] // GLA_CHUNK == np.arange(nch * LANES)[None, :] // LANES).astype(np.float32),
                         BF16)
    row = lambda w: pl.BlockSpec((grp, w), lambda b, g: (b * ng + g, 0))
    return pl.pallas_call(
        _gla_prompt_kernel,
        grid=(batch, ng),
        in_specs=[row(GLA_KW), row(GLA_KW), row(GLA_VW), row(GLA_KW),
                  _full_spec(tril.shape), _full_spec(ones.shape), _full_spec(ones_t.shape)],
        out_specs=[row(GLA_VW), pl.BlockSpec((1, H_GLA, DK_GLA, DV_GLA), lambda b, g: (b, 0, 0, 0))],
        out_shape=[jax.ShapeDtypeStruct((batch * seq, GLA_VW), F32),
                   jax.ShapeDtypeStruct((batch, H_GLA, DK_GLA, DV_GLA), F32)],
        scratch_shapes=[pltpu.VMEM((GLA_KW, GLA_VW), F32)],
        compiler_params=_cparams("parallel", "arbitrary"),
        name="gla_prompt",
    )(gq, gk, gv, la, tril, ones, ones_t)


def _gla_sample_kernel(q_ref, k_ref, v_ref, la_ref, s0_ref, tril_ref, o_ref, s1_ref):
    n, ls, _ = q_ref.shape
    la = la_ref[...]
    tril = jnp.broadcast_to(tril_ref[...][None], (n, ls, ls))
    hi, mid, lo = _split3(la)
    bdot = lambda x: jnp.einsum('nts,nsd->ntd', tril, x, preferred_element_type=F32)
    b = (bdot(hi) + bdot(mid)) + bdot(lo)
    b_last = b[:, ls - 1:ls, :]
    q_dec = (q_ref[...] * jnp.exp(b)).astype(BF16)
    k = k_ref[...]
    kd = (k * jnp.exp(-b)).astype(BF16)
    kl = (k * jnp.exp(b_last - b)).astype(BF16)
    v = v_ref[...].astype(BF16)
    s0 = s0_ref[...]
    att = jnp.einsum('ncd,nsd->ncs', q_dec, kd, preferred_element_type=F32)
    r = lax.broadcasted_iota(jnp.int32, (n, ls, ls), 1)
    c = lax.broadcasted_iota(jnp.int32, (n, ls, ls), 2)
    att = jnp.where(c <= r, att, 0.0)
    o = jnp.einsum('ncs,nse->nce', att.astype(BF16), v, preferred_element_type=F32)
    o = o + jnp.einsum('ncd,nde->nce', q_dec, s0.astype(BF16), preferred_element_type=F32)
    o_ref[...] = o
    kv = jnp.einsum('nds,nse->nde', jnp.swapaxes(kl, 1, 2), v, preferred_element_type=F32)
    dec = jnp.swapaxes(jnp.exp(b_last), 1, 2)
    s1_ref[...] = dec * s0 + kv


def _gla_sample(gq, gk, gv, la, state):
    n, ls, _ = gq.shape
    nb = min(32, n)
    tril = jnp.asarray(np.tril(np.ones((ls, ls), np.float32)), BF16)
    blk = lambda a, c: pl.BlockSpec((nb, a, c), lambda i: (i, 0, 0))
    return pl.pallas_call(
        _gla_sample_kernel,
        grid=(n // nb,),
        in_specs=[blk(ls, DK_GLA), blk(ls, DK_GLA), blk(ls, DV_GLA), blk(ls, DK_GLA), blk(DK_GLA, DV_GLA),
                  _full_spec(tril.shape)],
        out_specs=[blk(ls, DV_GLA), blk(DK_GLA, DV_GLA)],
        out_shape=[jax.ShapeDtypeStruct((n, ls, DV_GLA), F32), jax.ShapeDtypeStruct((n, DK_GLA, DV_GLA), F32)],
        compiler_params=_cparams("parallel"),
        name="gla_sample",
    )(gq, gk, gv, la, state, tril)


def _mixproj_kernel(x_ref, fox_ref, gla_ref, gr_ref, gg_ref, wo_ref, gc_ref, wq_ref, x1_ref, q_ref):
    gla = gla_ref[...]
    gr = gr_ref[...]
    acc = _dot(fox_ref[...], wo_ref[0:FOX_W, :])
    for h in range(H_GLA):
        cols = slice(h * DV_GLA, (h + 1) * DV_GLA)
        gh = _rms(gla[:, cols], gg_ref[:, cols])
        grh = gr[:, cols]
        gh = gh * (grh * jax.nn.sigmoid(grh))
        acc = acc + _dot(gh.astype(BF16), wo_ref[FOX_W + h * DV_GLA:FOX_W + (h + 1) * DV_GLA, :])
    x1 = x_ref[...] + acc
    x1_ref[...] = x1
    h2 = _rms(x1, gc_ref[...]).astype(BF16)
    q_ref[...] = (_dot(h2, wq_ref[...]) * (DH_MEM ** -0.5)).astype(BF16)


def _mixproj(x2d, fox_o, gla_o, gr, g_gla, w_out_bf, g_cross, w_q_bf):
    t = x2d.shape[0]
    tt = min(TOK_TILE, t)
    row = lambda w: pl.BlockSpec((tt, w), lambda i: (i, 0))
    return pl.pallas_call(
        _mixproj_kernel,
        grid=(t // tt,),
        in_specs=[row(D_MODEL), row(FOX_W), row(GLA_VW), row(GLA_VW), _full_spec(g_gla.shape),
                  _full_spec(w_out_bf.shape), _full_spec(g_cross.shape), _full_spec(w_q_bf.shape)],
        out_specs=[row(D_MODEL), row(D_MODEL)],
        out_shape=[jax.ShapeDtypeStruct((t, D_MODEL), F32), jax.ShapeDtypeStruct((t, D_MODEL), BF16)],
        compiler_params=_cparams("parallel"),
        name="mixproj",
    )(x2d, fox_o, gla_o, gr, g_gla, w_out_bf, g_cross, w_q_bf)


def _memkv_kernel(m_ref, g_ref, wk_ref, wv_ref, k_ref, v_ref):
    m = _rms(m_ref[...], g_ref[...]).astype(BF16)
    k_ref[...] = _dot(m, wk_ref[...])
    v_ref[...] = _dot(m, wv_ref[...])


def _memkv(mem2d, g_mem, wk_bf, wv_bf):
    t = mem2d.shape[0]
    tt = min(TOK_TILE, t)
    row = pl.BlockSpec((tt, D_MODEL), lambda i: (i, 0))
    return pl.pallas_call(
        _memkv_kernel,
        grid=(t // tt,),
        in_specs=[row, _full_spec(g_mem.shape), _full_spec(wk_bf.shape), _full_spec(wv_bf.shape)],
        out_specs=[row, row],
        out_shape=[jax.ShapeDtypeStruct((t, D_MODEL), F32)] * 2,
        compiler_params=_cparams("parallel"),
        name="memkv",
    )(mem2d, g_mem, wk_bf, wv_bf)


def _cross_kernel(q_ref, mk_ref, mv_ref, o_ref, *, per_head_memory):
    q = q_ref[0]
    if per_head_memory:
        lq = q.shape[0]
        n_mem = mk_ref.shape[2]
        q_hm = jnp.concatenate([q[:, h * DH_MEM:(h + 1) * DH_MEM] for h in range(H_MEM)], axis=0)
        mk = mk_ref[0, 0].reshape(n_mem * H_MEM, DH_MEM).astype(BF16)
        mv = mv_ref[0, 0].reshape(n_mem * H_MEM, DH_MEM).astype(BF16)
        s = _dot_nt(q_hm, mk)
        row_head = lax.broadcasted_iota(jnp.int32, s.shape, 0) // lq
        key_head = lax.broadcasted_iota(jnp.int32, s.shape, 1) % H_MEM
        s = jnp.where(row_head == key_head, s, NEG_INF)
        e = jnp.exp(s - jnp.max(s, axis=-1, keepdims=True))
        p = e / jnp.sum(e, axis=-1, keepdims=True)
        o = _dot(p.astype(BF16), mv).astype(o_ref.dtype)
        for h in range(H_MEM):
            o_ref[0, :, h * DH_MEM:(h + 1) * DH_MEM] = o[h * lq:(h + 1) * lq]
        return
    for h in range(H_MEM):
        cols = slice(h * DH_MEM, (h + 1) * DH_MEM)
        mk = mk_ref[0, :, cols].astype(BF16)
        mv = mv_ref[0, :, cols].astype(BF16)
        s = _dot_nt(q[:, cols], mk)
        e = jnp.exp(s - jnp.max(s, axis=-1, keepdims=True))
        p = e / jnp.sum(e, axis=-1, keepdims=True)
        o_ref[0, :, cols] = _dot(p.astype(BF16), mv).astype(o_ref.dtype)


def _cross(q3, mk, mv, layer=None):
    b, lq, _ = q3.shape
    tl = min(TOK_TILE, lq)
    if layer is None:
        mem_spec = pl.BlockSpec((1,) + mk.shape[1:], lambda i, j: (i, 0, 0))
    else:
        mem_spec = pl.BlockSpec((1, 1) + mk.shape[2:], lambda i, j: (layer, i, 0, 0, 0))
    return pl.pallas_call(
        functools.partial(_cross_kernel, per_head_memory=layer is not None),
        grid=(b, lq // tl),
        in_specs=[pl.BlockSpec((1, tl, D_MODEL), lambda i, j: (i, j, 0)), mem_spec, mem_spec],
        out_specs=pl.BlockSpec((1, tl, D_MODEL), lambda i, j: (i, j, 0)),
        out_shape=jax.ShapeDtypeStruct((b, lq, D_MODEL), BF16),
        compiler_params=_cparams("parallel", "parallel"),
        name="cross_attn",
    )(q3, mk, mv)


def _peer_prep_kernel(u_ref, v_ref, ub_ref, vt_ref):
    ub_ref[...] = u_ref[...].astype(BF16)
    vt_ref[...] = v_ref[...].T.astype(BF16)


def _peer_prep(u, v):
    ne = u.shape[0]
    eb = min(PREP_EBLK, ne)
    return pl.pallas_call(
        _peer_prep_kernel,
        grid=(ne // eb,),
        in_specs=[pl.BlockSpec((eb, D_MODEL), lambda i: (i, 0))] * 2,
        out_specs=[pl.BlockSpec((eb, D_MODEL), lambda i: (i, 0)), pl.BlockSpec((D_MODEL, eb), lambda i: (0, i))],
        out_shape=[jax.ShapeDtypeStruct((ne, D_MODEL), BF16), jax.ShapeDtypeStruct((D_MODEL, ne), BF16)],
        compiler_params=_cparams("parallel"),
        name="peer_prep",
    )(u, v)


def _rows_per_word():
    return 4 // jnp.dtype(BF16).itemsize


def _pack_rows(x):
    return pltpu.bitcast(x, jnp.uint32)


def _unpack_rows(x):
    return pltpu.bitcast(x, BF16)


def _top16_rows(s, break_ties):
    nrow = s.shape[0]
    rowid = lax.broadcasted_iota(jnp.int32, s.shape, 0)
    rank = jnp.full(s.shape, float(PEER_TOPK), F32)
    work = s
    vals = []
    for r in range(PEER_TOPK):
        mx = jnp.max(work, axis=0, keepdims=True)
        if break_ties:
            sel = rowid == jnp.min(jnp.where(work == mx, rowid, nrow), axis=0, keepdims=True)
        else:
            sel = work == mx
        rank = jnp.where(sel, float(r), rank)
        work = jnp.where(sel, -jnp.inf, work)
        vals.append(mx)
    return rank, jnp.concatenate(vals, axis=0)


def _ranked_16(rank):
    n = jnp.sum(jnp.where(rank < float(PEER_TOPK), 1.0, 0.0), axis=0, keepdims=True)
    return jnp.where(n == float(PEER_TOPK), 1.0, 0.0)


def _peer_route_kernel(x_ref, x1_ref, wo_ref, g_ref, wq_ref, sk_ref, ca_ref, cb_ref, cg_ref,
                       x2_ref, hn_ref, c1_ref, nrow_ref, p2_ref, rk2_ref, q_sc, s_sc):
    tt = x_ref.shape[0]
    nch = tt // LANES
    x2 = x1_ref[...] + _dot(x_ref[...], wo_ref[...])
    x2_ref[...] = x2
    hn = _rms(x2, g_ref[...])
    hn_ref[...] = hn.T.astype(BF16)
    q_sc[...] = _dot(hn.astype(BF16), wq_ref[...]).astype(BF16)
    for hj in range(2 * H_PEER):
        st = _dot_nt(sk_ref[hj % 2], q_sc[:, hj * N_KEYS:(hj + 1) * N_KEYS])
        for ch in range(nch):
            s_sc[hj, ch] = st[:, ch * LANES:(ch + 1) * LANES]

    cand_row = lax.broadcasted_iota(jnp.int32, (N_CAND_PAD, LANES), 0)

    per_unit = 2 if nch % 2 == 0 else 1

    def unit(i, carry):
        h = i // (nch // per_unit)
        first = (i % (nch // per_unit)) * per_unit
        tie_free = route_chunk(h, first, False)
        for d in range(1, per_unit):
            tie_free = tie_free * route_chunk(h, first + d, False)

        @pl.when(jnp.min(tie_free) < 0.5)
        def _():
            for d in range(per_unit):
                route_chunk(h, first + d, True)

        return carry

    def route_chunk(h, ch, break_ties):
        s1 = s_sc[2 * h, ch]
        s2 = s_sc[2 * h + 1, ch]
        rank1, v1 = _top16_rows(s1, break_ties)
        rank2, v2 = _top16_rows(s2, break_ties)
        cand = _sel_dot_l(ca_ref[...], v1) + _sel_dot_l(cb_ref[...], v2)
        cand = jnp.where(cand_row < N_CAND, cand, -jnp.inf)
        crank, _ = _top16_rows(cand, break_ties)
        chosen = crank < float(PEER_TOPK)
        n_of_a = _dot(cg_ref[...], jnp.where(chosen, 1.0, 0.0).astype(BF16))
        z = jnp.sum(jnp.where(chosen, jnp.exp(cand - cand[0:1]), 0.0), axis=0, keepdims=True)
        nrow = jnp.zeros_like(rank1)
        for a in range(PEER_TOPK):
            nrow = jnp.where(rank1 == float(a), n_of_a[a:a + 1], nrow)
        c1_ref[h, ch] = jnp.exp(s1 - v1[0:1]) / z
        nrow_ref[h, ch] = nrow
        p2_ref[h, ch] = _pack_rows(jnp.exp(s2 - v2[0:1]).astype(BF16))
        rk2_ref[h, ch] = _pack_rows(rank2.astype(BF16))
        return _ranked_16(rank1) * _ranked_16(rank2) * _ranked_16(crank)

    lax.fori_loop(0, H_PEER * nch // per_unit, unit, 0)


def _peer_route(oc2d, x1, w_o_bf, g_ffn, w_q_bf, sk_bf):
    t = x1.shape[0]
    tt = min(PEER_TOK, t)
    nch = tt // LANES
    pairs = [(a, b) for a in range(PEER_TOPK) for b in range(PEER_TOPK) if (a + 1) * (b + 1) <= PEER_TOPK]
    assert len(pairs) == N_CAND
    ca = np.zeros((N_CAND_PAD, PEER_TOPK), np.float32)
    cb = np.zeros((N_CAND_PAD, PEER_TOPK), np.float32)
    cg = np.zeros((PEER_TOPK, N_CAND_PAD), np.float32)
    for i, (a, b) in enumerate(pairs):
        ca[i, a] = 1.0
        cb[i, b] = 1.0
        cg[a, i] = 1.0
    ca, cb, cg = (jnp.asarray(m, BF16) for m in (ca, cb, cg))
    row = lambda w: pl.BlockSpec((tt, w), lambda i: (i, 0))
    krows = N_KEYS // _rows_per_word()
    sel = pl.BlockSpec((H_PEER, nch, N_KEYS, LANES), lambda i: (0, i, 0, 0))
    sel_shape = jax.ShapeDtypeStruct((H_PEER, t // LANES, N_KEYS, LANES), F32)
    packed = pl.BlockSpec((H_PEER, nch, krows, LANES), lambda i: (0, i, 0, 0))
    packed_shape = jax.ShapeDtypeStruct((H_PEER, t // LANES, krows, LANES), jnp.uint32)
    return pl.pallas_call(
        _peer_route_kernel,
        grid=(t // tt,),
        in_specs=[row(D_MODEL), row(D_MODEL), _full_spec(w_o_bf.shape), _full_spec(g_ffn.shape),
                  _full_spec(w_q_bf.shape), _full_spec(sk_bf.shape),
                  _full_spec(ca.shape), _full_spec(cb.shape), _full_spec(cg.shape)],
        out_specs=[row(D_MODEL), pl.BlockSpec((D_MODEL, tt), lambda i: (0, i)), sel, sel, packed, packed],
        out_shape=[jax.ShapeDtypeStruct((t, D_MODEL), F32), jax.ShapeDtypeStruct((D_MODEL, t), BF16),
                   sel_shape, sel_shape, packed_shape, packed_shape],
        scratch_shapes=[pltpu.VMEM((tt, 2 * H_PEER * N_KEYS), BF16),
                        pltpu.VMEM((2 * H_PEER, nch, N_KEYS, LANES), F32)],
        compiler_params=_cparams("parallel"),
        name="peer_route",
    )(oc2d, x1, w_o_bf, g_ffn, w_q_bf, sk_bf, ca, cb, cg)


def _peer_mix_kernel(x2_ref, hn_ref, u_ref, vt_ref, c1_ref, nrow_ref, p2_ref, rk2_ref, gf_ref,
                     y_ref, w_sc, acc_sc, *, final_norm):
    e = pl.program_id(1)
    tt = hn_ref.shape[1]
    nch = tt // LANES
    n_i1 = u_ref.shape[0] // N_KEYS

    @pl.when(e == 0)
    def _():
        acc_sc[...] = jnp.zeros_like(acc_sc)

    krows = N_KEYS // _rows_per_word()
    acc = acc_sc[...]
    for sb in range(n_i1 // PEER_SUB_I1):
        i1_lo = sb * PEER_SUB_I1
        ex = slice(i1_lo * N_KEYS, (i1_lo + PEER_SUB_I1) * N_KEYS)
        exw = slice(i1_lo * krows, (i1_lo + PEER_SUB_I1) * krows)
        for ch in range(nch):
            cols = slice(ch * LANES, (ch + 1) * LANES)
            for r in range(i1_lo, i1_lo + PEER_SUB_I1):
                w = jnp.zeros((N_KEYS, LANES), BF16)
                for h in range(H_PEER):
                    wide = lambda ref: jnp.broadcast_to(ref[h, ch, r:r + 1, :], (N_KEYS, LANES)).astype(BF16)
                    p2 = _unpack_rows(p2_ref[h, ch])
                    hit = _unpack_rows(rk2_ref[h, ch]) < wide(nrow_ref)
                    w = w + jnp.where(hit, p2, jnp.zeros_like(p2)) * wide(c1_ref)
                w_sc[r * krows:(r + 1) * krows, cols] = _pack_rows(w)
        gelu_a = jax.nn.gelu(_dot(u_ref[ex, :], hn_ref[...])).astype(BF16)
        w_all = _unpack_rows(w_sc[exw, :]) * gelu_a
        acc = acc + _dot(vt_ref[:, ex], w_all)
    acc_sc[...] = acc

    @pl.when(e == pl.num_programs(1) - 1)
    def _():
        y = x2_ref[...] + acc_sc[...].T
        y_ref[...] = _rms(y, gf_ref[...]) if final_norm else y


def _peer_mix(x2, hn, u_bf, vt_bf, c1, nrow, p2, rk2, g_final, final_norm):
    t = x2.shape[0]
    tt = min(PEER_TOK, t)
    nch = tt // LANES
    ne = u_bf.shape[0]
    eb = min(PEER_EBLK, ne)
    n_i1 = eb // N_KEYS
    row = lambda w: pl.BlockSpec((tt, w), lambda i, e: (i, 0))
    part = pl.BlockSpec((H_PEER, nch, n_i1, LANES), lambda i, e: (0, i, e, 0))
    full = pl.BlockSpec((H_PEER, nch, N_KEYS // _rows_per_word(), LANES), lambda i, e: (0, i, 0, 0))
    return pl.pallas_call(
        functools.partial(_peer_mix_kernel, final_norm=final_norm),
        grid=(t // tt, ne // eb),
        in_specs=[row(D_MODEL), pl.BlockSpec((D_MODEL, tt), lambda i, e: (0, i)),
                  pl.BlockSpec((eb, D_MODEL), lambda i, e: (e, 0)),
                  pl.BlockSpec((D_MODEL, eb), lambda i, e: (0, e)),
                  part, part, full, full, pl.BlockSpec(g_final.shape, lambda i, e: (0, 0))],
        out_specs=row(D_MODEL),
        out_shape=jax.ShapeDtypeStruct((t, D_MODEL), F32),
        scratch_shapes=[pltpu.VMEM((eb // _rows_per_word(), tt), jnp.uint32), pltpu.VMEM((D_MODEL, tt), F32)],
        compiler_params=_cparams("parallel", "arbitrary"),
        name="peer_mix",
    )(x2, hn, u_bf, vt_bf, c1, nrow, p2, rk2, g_final)


def _prep_in_weights(w_in, b_fox_f, w_gate2, b_gate):
    o = np.cumsum((0,) + IN_SIZES)
    seg = lambda i: w_in[:, int(o[i]):int(o[i + 1])]
    pad = jnp.zeros((D_MODEL, LANES - H_FOX - GLA_GATE_RANK), w_in.dtype)
    w_perm = jnp.concatenate([seg(0), seg(1), seg(2), seg(4), seg(5), seg(6), seg(8), seg(3), seg(7), pad],
                             axis=1).astype(BF16)
    b_small = jnp.zeros((1, LANES), F32).at[0, 0:H_FOX].set(b_fox_f)
    wg_pad = jnp.zeros((LANES, GLA_KW), F32).at[H_FOX:H_FOX + GLA_GATE_RANK].set(w_gate2).astype(BF16)
    return w_perm, b_small, wg_pad, b_gate.reshape(1, GLA_KW)


def kernel(x_prompt, x_sample, cache_fox_k, cache_fox_v, cache_fox_logf, state_gla, cache_mem_k, cache_mem_v, page_table, mem_prompt, g_mix, w_in, b_fox_f, w_gla_gate2, b_gla_gate, g_gla_out, w_out, g_cross, g_mem, w_mem_k, w_mem_v, w_cross_q, w_cross_o, g_ffn, peer_w_q, peer_subkeys, peer_u, peer_v, g_final):
    depth = w_in.shape[0]
    bp, lp, _ = x_prompt.shape
    bs, ls, _ = x_sample.shape
    n_mem = mem_prompt.shape[1]
    n_pool, page = cache_fox_k.shape[1:3]
    xp = x_prompt.reshape(bp * lp, D_MODEL)
    xs = x_sample.reshape(bs * ls, D_MODEL)
    vec = lambda a: a.reshape(1, -1)
    kcache_t = cache_fox_k.transpose(0, 1, 3, 4, 2).reshape(depth, n_pool, FOX_W, page)
    vcache_t = cache_fox_v.transpose(0, 1, 3, 4, 2).reshape(depth, n_pool, FOX_W, page)
    lfcache_t = cache_fox_logf.transpose(0, 1, 3, 2)
    outs = [[] for _ in range(10)]
    for l in range(depth):
        w_perm, b_small, wg_pad, b_gate = _prep_in_weights(w_in[l], b_fox_f[l], w_gla_gate2[l], b_gla_gate[l])
        w_out_bf = w_out[l].astype(BF16)
        w_cq_bf = w_cross_q[l].astype(BF16)
        w_co_bf = w_cross_o[l].astype(BF16)
        w_pq_bf = peer_w_q[l].astype(BF16)
        sk_bf = peer_subkeys[l].astype(BF16)
        u_bf, vt_bf = _peer_prep(peer_u[l], peer_v[l])
        last = l == depth - 1

        def tail(x, fox_o, gla_o, gr, mk, mv, nb, mem_layer):
            x1, qc = _mixproj(x, fox_o, gla_o, gr, vec(g_gla_out[l]), w_out_bf, vec(g_cross[l]), w_cq_bf)
            oc = _cross(qc.reshape(nb, -1, D_MODEL), mk, mv, mem_layer).reshape(-1, D_MODEL)
            x2, hn, c1, nrow, p2, rk2 = _peer_route(oc, x1, w_co_bf, vec(g_ffn[l]), w_pq_bf, sk_bf)
            return _peer_mix(x2, hn, u_bf, vt_bf, c1, nrow, p2, rk2, vec(g_final), last)

        fq_bf, fk, fv, lf, gq, gk, gv, la, gr = _inproj(xp, vec(g_mix[l]), w_perm, b_small, wg_pad, b_gate,
                                                        DH_FOX ** -0.5 * _LOG2E)
        qc, kc, vt, fk_t, fv_t, lf_t = _fox_prep(fq_bf, fk, fv, lf, bp, lp)
        fox_o = _fox_prompt(qc, kc, vt, bp, lp)
        gla_o, s_fin = _gla_prompt(gq, gk, gv, la, bp, lp)
        mk, mv = _memkv(mem_prompt.reshape(bp * n_mem, D_MODEL), vec(g_mem[l]), w_mem_k[l].astype(BF16),
                        w_mem_v[l].astype(BF16))
        mk3 = mk.reshape(bp, n_mem, D_MODEL)
        mv3 = mv.reshape(bp, n_mem, D_MODEL)
        xp = tail(xp, fox_o, gla_o, gr, mk3, mv3, bp, None)
        outs[0].append(fk_t.reshape(bp, H_FOX, DH_FOX, lp).transpose(0, 3, 1, 2))
        outs[1].append(fv_t.reshape(bp, H_FOX, DH_FOX, lp).transpose(0, 3, 1, 2))
        outs[2].append(lf_t.transpose(0, 2, 1))
        outs[3].append(s_fin)
        outs[4].append(mk.reshape(bp, n_mem, H_MEM, DH_MEM))
        outs[5].append(mv.reshape(bp, n_mem, H_MEM, DH_MEM))

        fq_bf, fk, fv, lf, gq, gk, gv, la, gr = _inproj(xs, vec(g_mix[l]), w_perm, b_small, wg_pad, b_gate,
                                                        DH_FOX ** -0.5)
        seqs = lambda a: a.reshape(bs, ls, FOX_W)
        fox_o = _fox_sample(seqs(fq_bf), seqs(fk), seqs(fv), lf.reshape(bs, ls, H_FOX).transpose(0, 2, 1),
                            kcache_t, vcache_t, lfcache_t, page_table, l).reshape(bs * ls, FOX_W)
        heads = lambda a, d: a.reshape(bs, ls, H_GLA, d).transpose(0, 2, 1, 3).reshape(bs * H_GLA, ls, d)
        gla_o, s_new = _gla_sample(heads(gq, DK_GLA), heads(gk, DK_GLA), heads(gv, DV_GLA), heads(la, DK_GLA),
                                   state_gla[l].reshape(bs * H_GLA, DK_GLA, DV_GLA))
        gla_o = gla_o.reshape(bs, H_GLA, ls, DV_GLA).transpose(0, 2, 1, 3).reshape(bs * ls, GLA_VW)
        xs = tail(xs, fox_o, gla_o, gr, cache_mem_k, cache_mem_v, bs, l)
        outs[6].append(fk.reshape(bs, ls, H_FOX, DH_FOX))
        outs[7].append(fv.reshape(bs, ls, H_FOX, DH_FOX))
        outs[8].append(lf.reshape(bs, ls, H_FOX))
        outs[9].append(s_new.reshape(bs, H_GLA, DK_GLA, DV_GLA))
    stacked = [jnp.stack(o, axis=0) for o in outs]
    return (xp.reshape(bp, lp, D_MODEL), xs.reshape(bs, ls, D_MODEL), *stacked)
```

```python
import functools

import numpy as np
import jax
import jax.numpy as jnp
from jax import lax
from jax.experimental import pallas as pl
from jax.experimental.pallas import tpu as pltpu

F32 = jnp.float32
BF16 = jnp.bfloat16

D_MODEL = 1024
H_FOX, DH_FOX = 8, 64
H_GLA, DK_GLA, DV_GLA = 4, 64, 128
GLA_GATE_RANK = 16
GLA_TAU = 16.0
GLA_CHUNK = 64
H_MEM = 4
DH_MEM = D_MODEL // H_MEM
N_KEYS = 128
H_PEER = 8
PEER_TOPK = 16
EPS = 1e-6
NEG_INF = -1e30
_LOG2E = float(np.log2(np.e))
FOX_W = H_FOX * DH_FOX
GLA_KW = H_GLA * DK_GLA
GLA_VW = H_GLA * DV_GLA
IN_SIZES = (FOX_W, FOX_W, FOX_W, H_FOX, GLA_KW, GLA_KW, GLA_VW, GLA_GATE_RANK, GLA_VW)

LANES = 128
VMEM_LIMIT_BYTES = 56 * 2**20

TOK_TILE = 512
INPROJ_TILE = 256
FOX_TILE = 512
FOX_KTILE = 1024
FOX_HP_PER_STEP = 2
GLA_GROUP = 512
PEER_TOK = 512
PEER_EBLK = 2048
PEER_CHUNK_I1 = 4
PEER_SUB_I1 = 8
FOX_PAGES_PER_STEP = 16
PREP_EBLK = 512
N_CAND = 50
N_CAND_PAD = 64


def _cparams(*sem):
    return pltpu.CompilerParams(dimension_semantics=sem, vmem_limit_bytes=VMEM_LIMIT_BYTES)


def _dot(a, b):
    return jnp.dot(a, b, preferred_element_type=F32)


def _dot_nt(a, b):
    return lax.dot_general(a, b, (((1,), (1,)), ((), ())), preferred_element_type=F32)


def _split3(x):
    hi = x.astype(BF16)
    r = x - hi.astype(F32)
    mid = r.astype(BF16)
    lo = (r - mid.astype(F32)).astype(BF16)
    return hi, mid, lo


def _sel_dot_l(m01, x):
    hi, mid, lo = _split3(x)
    return (_dot(m01, hi) + _dot(m01, mid)) + _dot(m01, lo)


def _sel_dot_r(x, m01):
    hi, mid, lo = _split3(x)
    return (_dot(hi, m01) + _dot(mid, m01)) + _dot(lo, m01)


def _rms(x, g):
    return x * lax.rsqrt(jnp.mean(x * x, axis=-1, keepdims=True) + EPS) * g


def _logsig(x):
    return jnp.minimum(x, 0.0) - jnp.log1p(jnp.exp(-jnp.abs(x)))


def _full_spec(shape):
    nd = len(shape)
    return pl.BlockSpec(shape, lambda *_: (0,) * nd)


_C_FQ, _C_FK, _C_FV = 0, FOX_W, 2 * FOX_W
_C_GQ = 3 * FOX_W
_C_GK = _C_GQ + GLA_KW
_C_GV = _C_GK + GLA_KW
_C_GR = _C_GV + GLA_VW
_C_SM = _C_GR + GLA_VW
_C_END = _C_SM + LANES


def _inproj_kernel(x_ref, g_ref, w_ref, bsm_ref, wg_ref, bg_ref,
                   fq_ref, fk_ref, fv_ref, lf_ref, gq_ref, gk_ref, gv_ref, la_ref, gr_ref, *, q_scale):
    h = _rms(x_ref[...], g_ref[...]).astype(BF16)

    def seg(lo, hi):
        return _dot(h, w_ref[:, lo:hi])

    fq_ref[...] = (seg(_C_FQ, _C_FK) * q_scale).astype(BF16)
    fk_ref[...] = seg(_C_FK, _C_FV)
    fv_ref[...] = seg(_C_FV, _C_GQ)
    gq_ref[...] = seg(_C_GQ, _C_GK) * (DK_GLA ** -0.5)
    gk_ref[...] = seg(_C_GK, _C_GV)
    gv_ref[...] = seg(_C_GV, _C_GR)
    gr_ref[...] = seg(_C_GR, _C_SM)
    ysm = seg(_C_SM, _C_END)
    lf_ref[...] = _logsig(ysm + bsm_ref[...])[:, 0:H_FOX]
    z = _dot(ysm.astype(BF16), wg_ref[...]) + bg_ref[...]
    la_ref[...] = _logsig(z) / GLA_TAU


def _inproj(x2d, g_mix, w_perm, b_small, wg_pad, b_gate, q_scale):
    t = x2d.shape[0]
    tt = min(INPROJ_TILE, t)
    row = lambda w: pl.BlockSpec((tt, w), lambda i: (i, 0))
    outs = [(FOX_W, BF16), (FOX_W, F32), (FOX_W, F32), (H_FOX, F32),
            (GLA_KW, F32), (GLA_KW, F32), (GLA_VW, F32), (GLA_KW, F32), (GLA_VW, F32)]
    return pl.pallas_call(
        functools.partial(_inproj_kernel, q_scale=q_scale),
        grid=(t // tt,),
        in_specs=[row(D_MODEL), _full_spec(g_mix.shape), _full_spec(w_perm.shape), _full_spec(b_small.shape),
                  _full_spec(wg_pad.shape), _full_spec(b_gate.shape)],
        out_specs=[row(w) for w, _ in outs],
        out_shape=[jax.ShapeDtypeStruct((t, w), dt) for w, dt in outs],
        compiler_params=_cparams("parallel"),
        name="inproj",
    )(x2d, g_mix, w_perm, b_small, wg_pad, b_gate)


_FOX_QW = 4 * LANES
_FOX_KW = 2 * LANES
_N_HP = FOX_W // LANES


def _fox_aug_tables():
    pq = np.zeros((3, H_FOX, _N_HP * 2 * LANES), np.float32)
    oq = np.zeros((1, _N_HP * 2 * LANES), np.float32)
    pk = np.zeros((3, H_FOX, _N_HP * LANES), np.float32)
    ok = np.zeros((1, _N_HP * LANES), np.float32)
    for hp in range(_N_HP):
        for half in range(2):
            head = 2 * hp + half
            qb = (hp * 2 + half) * LANES
            kb = hp * LANES
            for j in range(3):
                pq[j, head, qb + 6 * half + j] = 1.0
                oq[0, qb + 6 * half + 3 + j] = 1.0
                pk[j, head, kb + 6 * half + 3 + j] = 1.0
                ok[0, kb + 6 * half + j] = 1.0
    return jnp.asarray(pq, BF16), jnp.asarray(oq, F32), jnp.asarray(pk, BF16), jnp.asarray(ok, F32)


def _fox_prep_kernel(q_ref, k_ref, v_ref, lf_ref, tril_ref, pq_ref, oq_ref, pk_ref, ok_ref, eye_ref,
                     qc_ref, kc_ref, vt_ref, fkt_ref, fvt_ref, lft_ref, carry_sc):
    @pl.when(pl.program_id(1) == 0)
    def _():
        carry_sc[...] = jnp.zeros_like(carry_sc)

    lf_parts = _split3(lf_ref[...])
    eye = eye_ref[...]
    lft_ref[0] = (_dot_nt(eye, lf_parts[0]) + _dot_nt(eye, lf_parts[1])) + _dot_nt(eye, lf_parts[2])
    c = _sel_dot_l(tril_ref[...], lf_ref[...]) + carry_sc[0:1, 0:H_FOX]
    carry_sc[0:1, 0:H_FOX] = c[-1:, :]
    parts = _split3(c * _LOG2E)
    aug_q = oq_ref[...]
    aug_k = ok_ref[...]
    for j in range(3):
        aug_q = aug_q + _dot(parts[j], pq_ref[j])
        aug_k = aug_k - _dot(parts[j], pk_ref[j])
    aug_q = aug_q.astype(BF16)
    aug_k = aug_k.astype(BF16)
    lane = lax.broadcasted_iota(jnp.int32, (q_ref.shape[0], LANES), 1)
    for hp in range(_N_HP):
        q = q_ref[:, hp * LANES:(hp + 1) * LANES]
        zero = jnp.zeros_like(q)
        base = hp * _FOX_QW
        qc_ref[:, base:base + LANES] = jnp.where(lane < DH_FOX, q, zero)
        qc_ref[:, base + LANES:base + 2 * LANES] = aug_q[:, (2 * hp) * LANES:(2 * hp + 1) * LANES]
        qc_ref[:, base + 2 * LANES:base + 3 * LANES] = jnp.where(lane >= DH_FOX, q, zero)
        qc_ref[:, base + 3 * LANES:base + 4 * LANES] = aug_q[:, (2 * hp + 1) * LANES:(2 * hp + 2) * LANES]
        kc_ref[:, hp * _FOX_KW:hp * _FOX_KW + LANES] = k_ref[:, hp * LANES:(hp + 1) * LANES].astype(BF16)
        kc_ref[:, hp * _FOX_KW + LANES:(hp + 1) * _FOX_KW] = aug_k[:, hp * LANES:(hp + 1) * LANES]
    v_t = v_ref[...].T
    vt_ref[0] = v_t.astype(BF16)
    fvt_ref[0] = v_t
    fkt_ref[0] = k_ref[...].T


def _fox_prep(fq_bf, fk, fv, lf, batch, seq):
    tt = min(FOX_TILE, seq)
    nt = seq // tt
    tril = jnp.asarray(np.tril(np.ones((tt, tt), np.float32)), BF16)
    consts = [tril, *_fox_aug_tables(), jnp.eye(H_FOX, dtype=BF16)]
    row = lambda w: pl.BlockSpec((tt, w), lambda b, i: (b * nt + i, 0))
    col = lambda h: pl.BlockSpec((1, h, tt), lambda b, i: (b, 0, i))
    return pl.pallas_call(
        _fox_prep_kernel,
        grid=(batch, nt),
        in_specs=[row(FOX_W), row(FOX_W), row(FOX_W), row(H_FOX)] + [_full_spec(a.shape) for a in consts],
        out_specs=[row(_N_HP * _FOX_QW), row(_N_HP * _FOX_KW), col(FOX_W), col(FOX_W), col(FOX_W), col(H_FOX)],
        out_shape=[jax.ShapeDtypeStruct((batch * seq, _N_HP * _FOX_QW), BF16),
                   jax.ShapeDtypeStruct((batch * seq, _N_HP * _FOX_KW), BF16),
                   jax.ShapeDtypeStruct((batch, FOX_W, seq), BF16),
                   jax.ShapeDtypeStruct((batch, FOX_W, seq), F32),
                   jax.ShapeDtypeStruct((batch, FOX_W, seq), F32),
                   jax.ShapeDtypeStruct((batch, H_FOX, seq), F32)],
        scratch_shapes=[pltpu.VMEM((8, LANES), F32)],
        compiler_params=_cparams("parallel", "arbitrary"),
        name="fox_prep",
    )(fq_bf, fk, fv, lf, *consts)


def _fox_prompt_kernel(qi_tab, ki_tab, q_ref, k_ref, vt_ref, o_ref,
                       m_sc, l_sc, acc_sc):
    t = pl.program_id(2)
    qi = qi_tab[t]
    ki = ki_tab[t]
    tq = q_ref.shape[0]
    tk = k_ref.shape[0]

    @pl.when(ki == 0)
    def _():
        m_sc[...] = jnp.full_like(m_sc, NEG_INF)
        l_sc[...] = jnp.zeros_like(l_sc)
        acc_sc[...] = jnp.zeros_like(acc_sc)

    def step(masked):
        n_hh = 2 * FOX_HP_PER_STEP

        def scores_of(hh):
            kc = k_ref[:, (hh // 2) * _FOX_KW:(hh // 2 + 1) * _FOX_KW]
            return _dot_nt(kc, q_ref[:, hh * _FOX_KW:(hh + 1) * _FOX_KW])

        def softmax_update(hh, s):
            if masked:
                kpos = lax.broadcasted_iota(jnp.int32, (tk, tq), 0) + ki * tk
                qpos = lax.broadcasted_iota(jnp.int32, (tk, tq), 1) + qi * tq
                s = jnp.where(kpos <= qpos, s, NEG_INF)
            m_prev = m_sc[hh:hh + 1, :]
            m_new = jnp.maximum(m_prev, jnp.max(s, axis=0, keepdims=True))
            alpha = jnp.exp2(m_prev - m_new)
            p = jnp.exp2(s - m_new)
            l_sc[hh:hh + 1, :] = alpha * l_sc[hh:hh + 1, :] + jnp.sum(p, axis=0, keepdims=True)
            m_sc[hh:hh + 1, :] = m_new
            return alpha, p.astype(BF16)

        def accumulate(hh, alpha, p):
            vt = vt_ref[0, (hh // 2) * LANES:(hh // 2 + 1) * LANES, :]
            acc_sc[hh] = alpha * acc_sc[hh] + _dot(vt, p)

        s_next = {0: scores_of(0)}
        if n_hh > 1:
            s_next[1] = scores_of(1)
        pending = None
        for hh in range(n_hh):
            alpha_p = softmax_update(hh, s_next.pop(hh))
            if hh + 2 < n_hh:
                s_next[hh + 2] = scores_of(hh + 2)
            if pending is not None:
                accumulate(*pending)
            pending = (hh, *alpha_p)
        accumulate(*pending)

    last = (qi * tq) // tk

    @pl.when(ki < last)
    def _():
        step(False)

    @pl.when(ki == last)
    def _():
        step(True)
        dim = lax.broadcasted_iota(jnp.int32, (LANES, tq), 0)
        for hp in range(FOX_HP_PER_STEP):
            o0 = acc_sc[2 * hp] / l_sc[2 * hp:2 * hp + 1, :]
            o1 = acc_sc[2 * hp + 1] / l_sc[2 * hp + 1:2 * hp + 2, :]
            o_ref[:, hp * LANES:(hp + 1) * LANES] = jnp.where(dim < DH_FOX, o0, o1).T.astype(o_ref.dtype)


def _fox_prompt(qc, kc, vt, batch, seq):
    tq = min(FOX_TILE, seq)
    tk = min(FOX_KTILE, seq)
    nq = seq // tq
    nk = seq // tk
    pairs = [(i, j) for i in range(nq) for j in range((i * tq) // tk + 1)]
    qi_tab = jnp.asarray([p[0] for p in pairs], jnp.int32)
    ki_tab = jnp.asarray([p[1] for p in pairs], jnp.int32)
    g = FOX_HP_PER_STEP
    grid_spec = pltpu.PrefetchScalarGridSpec(
        num_scalar_prefetch=2,
        grid=(batch, _N_HP // g, len(pairs)),
        in_specs=[
            pl.BlockSpec((tq, g * _FOX_QW), lambda b, h, t, qt, kt: (b * nq + qt[t], h)),
            pl.BlockSpec((tk, g * _FOX_KW), lambda b, h, t, qt, kt: (b * nk + kt[t], h)),
            pl.BlockSpec((1, g * LANES, tk), lambda b, h, t, qt, kt: (b, h, kt[t])),
        ],
        out_specs=pl.BlockSpec((tq, g * LANES), lambda b, h, t, qt, kt: (b * nq + qt[t], h)),
        scratch_shapes=[pltpu.VMEM((2 * g, tq), F32), pltpu.VMEM((2 * g, tq), F32),
                        pltpu.VMEM((2 * g, LANES, tq), F32)],
    )
    return pl.pallas_call(
        _fox_prompt_kernel,
        grid_spec=grid_spec,
        out_shape=jax.ShapeDtypeStruct((batch * seq, FOX_W), BF16),
        compiler_params=_cparams("parallel", "parallel", "arbitrary"),
        name="fox_prompt",
    )(qi_tab, ki_tab, qc, kc, vt)


def _fox_sample_kernel(pt_ref, q_ref, *refs, n_pp):
    kc, vc, lfc = refs[0:n_pp], refs[n_pp:2 * n_pp], refs[2 * n_pp:3 * n_pp]
    (kn_ref, vn_ref, lfn_ref, tri_ref, trin_ref, o_ref,
     qbd_sc, m_sc, l_sc, acc_sc, carry_sc, kall_sc, vall_sc) = refs[3 * n_pp:]
    step = pl.program_id(1)
    ls = q_ref.shape[1]
    nr = H_FOX * ls
    page = tri_ref.shape[0]
    row_head = lax.broadcasted_iota(jnp.int32, (nr, FOX_W), 0) // ls
    lane_head = lax.broadcasted_iota(jnp.int32, (nr, FOX_W), 1) // DH_FOX

    def rows_of_heads(x):
        return jnp.broadcast_to(x[:, None, :], (H_FOX, ls, x.shape[-1])).reshape(nr, x.shape[-1])

    cn_row = rows_of_heads(_sel_dot_r(lfn_ref[0], trin_ref[...]))
    tnew = lax.broadcasted_iota(jnp.int32, (nr, ls), 1)
    tq = lax.broadcasted_iota(jnp.int32, (nr, ls), 0) % ls
    cn_col = jnp.sum(jnp.where(tnew == tq, cn_row, 0.0), axis=-1, keepdims=True)

    @pl.when(step == 0)
    def _():
        qt = jnp.concatenate([q_ref[0]] * H_FOX, axis=0)
        qbd_sc[...] = jnp.where(row_head == lane_head, qt, jnp.zeros_like(qt))
        m_sc[...] = jnp.full_like(m_sc, NEG_INF)
        l_sc[...] = jnp.zeros_like(l_sc)
        acc_sc[...] = jnp.zeros_like(acc_sc)
        carry_sc[...] = jnp.zeros_like(carry_sc)

    def online(scores, weighted_values):
        m_prev = m_sc[...]
        m_new = m_prev
        for s in scores:
            m_new = jnp.maximum(m_new, jnp.max(s, axis=-1, keepdims=True))
        alpha = jnp.exp(m_prev - m_new)
        l_new = alpha * l_sc[...]
        acc = alpha * acc_sc[...]
        for s, pv in zip(scores, weighted_values):
            pr = jnp.exp(s - m_new)
            l_new = l_new + jnp.sum(pr, axis=-1, keepdims=True)
            acc = acc + pv(pr.astype(BF16))
        m_sc[...] = m_new
        l_sc[...] = l_new
        acc_sc[...] = acc

    c_loc = _sel_dot_r(jnp.concatenate([r[0, 0] for r in lfc], axis=0), tri_ref[...])
    carry = carry_sc[...][:, 0:1]
    qbd = qbd_sc[...]
    c_pages = []
    for j in range(n_pp):
        c_page = c_loc[j * H_FOX:(j + 1) * H_FOX] + carry
        carry = c_page[:, page - 1:page]
        c_pages.append(rows_of_heads(c_page))
        kall_sc[:, j * page:(j + 1) * page] = kc[j][0, 0].astype(BF16)
        vall_sc[:, j * page:(j + 1) * page] = vc[j][0, 0].astype(BF16)
    carry_sc[...] = jnp.broadcast_to(carry, carry_sc.shape)
    s_all = _dot(qbd, kall_sc[...]) + cn_col - jnp.concatenate(c_pages, axis=1)
    online([s_all], [lambda p: _dot_nt(p, vall_sc[...])])

    @pl.when(step == pl.num_programs(1) - 1)
    def _():
        s_new = _dot_nt(qbd, kn_ref[0].astype(BF16))
        s_new = s_new + cn_col - cn_row - rows_of_heads(jnp.broadcast_to(carry, (H_FOX, LANES)))[:, 0:1]
        s_new = jnp.where(tnew <= tq, s_new, NEG_INF)
        online([s_new], [lambda p: _dot(p, vn_ref[0].astype(BF16))])
        o = acc_sc[...] / l_sc[...]
        o = jnp.where(row_head == lane_head, o, 0.0)
        out = o[0:ls]
        for h in range(1, H_FOX):
            out = out + o[h * ls:(h + 1) * ls]
        o_ref[0] = out.astype(o_ref.dtype)


def _fox_sample(fq_bf, fk, fv, lfn_t, kcache_t, vcache_t, lfcache_t, page_table, layer):
    bs, n_pages = page_table.shape
    n_pp = min(FOX_PAGES_PER_STEP, n_pages)
    assert n_pages % n_pp == 0
    page = kcache_t.shape[3]
    ls = fq_bf.shape[1]
    nr = H_FOX * ls
    tri = jnp.asarray(np.triu(np.ones((page, page), np.float32)), BF16)
    trin = jnp.asarray(np.triu(np.ones((ls, ls), np.float32)), BF16)
    page_spec = lambda j, rows: pl.BlockSpec(
        (1, 1, rows, page), lambda b, s, pt: (layer, pt[b * n_pages + s * n_pp + j], 0, 0))
    per_seq = lambda *shape: pl.BlockSpec((1,) + shape, lambda b, s, pt: (b,) + (0,) * len(shape))
    const = lambda a: pl.BlockSpec(a.shape, lambda b, s, pt: (0,) * a.ndim)
    grid_spec = pltpu.PrefetchScalarGridSpec(
        num_scalar_prefetch=1,
        grid=(bs, n_pages // n_pp),
        in_specs=([per_seq(ls, FOX_W)] + [page_spec(j, FOX_W) for j in range(n_pp)] * 2
                  + [page_spec(j, H_FOX) for j in range(n_pp)]
                  + [per_seq(ls, FOX_W), per_seq(ls, FOX_W), per_seq(H_FOX, ls), const(tri), const(trin)]),
        out_specs=per_seq(ls, FOX_W),
        scratch_shapes=[pltpu.VMEM((nr, FOX_W), BF16), pltpu.VMEM((nr, 1), F32), pltpu.VMEM((nr, 1), F32),
                        pltpu.VMEM((nr, FOX_W), F32), pltpu.VMEM((H_FOX, LANES), F32),
                        pltpu.VMEM((FOX_W, n_pp * page), BF16), pltpu.VMEM((FOX_W, n_pp * page), BF16)],
    )
    return pl.pallas_call(
        functools.partial(_fox_sample_kernel, n_pp=n_pp),
        grid_spec=grid_spec,
        out_shape=jax.ShapeDtypeStruct((bs, ls, FOX_W), BF16),
        compiler_params=_cparams("parallel", "arbitrary"),
        name="fox_sample",
    )(page_table.reshape(-1), fq_bf, *([kcache_t] * n_pp), *([vcache_t] * n_pp), *([lfcache_t] * n_pp),
      fk, fv, lfn_t, tri, trin)


def _gla_prompt_kernel(q_ref, k_ref, v_ref, la_ref, tril_ref, ones_ref, onesT_ref,
                       o_ref, sfin_ref, s_sc):
    g = pl.program_id(1)
    n_tok = q_ref.shape[0]
    n_chunks = n_tok // GLA_CHUNK

    @pl.when(g == 0)
    def _():
        s_sc[...] = jnp.zeros_like(s_sc)

    la = la_ref[...]
    b = _sel_dot_l(tril_ref[...], la)
    bl = _sel_dot_l(ones_ref[...], la)
    q_dec = q_ref[...] * jnp.exp(b)
    q_dec_bf = q_dec.astype(BF16)
    k = k_ref[...]
    kd_bf = (k * jnp.exp(-b)).astype(BF16)
    kl_t = (k * jnp.exp(bl - b)).T
    v_bf = v_ref[...].astype(BF16)
    bl_col = _sel_dot_r(la.T, onesT_ref[...])

    r = lax.broadcasted_iota(jnp.int32, (n_tok, n_tok), 0)
    c = lax.broadcasted_iota(jnp.int32, (n_tok, n_tok), 1)
    causal = (r // GLA_CHUNK == c // GLA_CHUNK) & (c <= r)
    lane_q = lax.broadcasted_iota(jnp.int32, (n_tok, GLA_KW), 1) // DK_GLA
    zero_q = jnp.zeros_like(q_dec_bf)
    for h in range(H_GLA):
        att = _dot_nt(jnp.where(lane_q == h, q_dec_bf, zero_q), kd_bf)
        att = jnp.where(causal, att, 0.0)
        o_ref[:, h * DV_GLA:(h + 1) * DV_GLA] = _dot(att.astype(BF16), v_bf[:, h * DV_GLA:(h + 1) * DV_GLA])

    row_head = lax.broadcasted_iota(jnp.int32, (GLA_KW, GLA_VW), 0) // DK_GLA
    lane_head = lax.broadcasted_iota(jnp.int32, (GLA_KW, GLA_VW), 1) // DV_GLA
    same_head = row_head == lane_head
    tok_chunk = lax.broadcasted_iota(jnp.int32, (GLA_KW, n_tok), 1) // GLA_CHUNK
    for n in range(n_chunks):
        rows = slice(n * GLA_CHUNK, (n + 1) * GLA_CHUNK)
        s_prev = s_sc[...]
        o_ref[rows, :] += _dot(q_dec_bf[rows], s_prev.astype(BF16))
        kv = _dot(jnp.where(tok_chunk == n, kl_t, 0.0).astype(BF16), v_bf)
        dec = jnp.exp(bl_col[:, n * LANES:(n + 1) * LANES])
        dec = jnp.concatenate([dec] * (GLA_VW // LANES), axis=1)
        s_sc[...] = jnp.where(same_head, dec * s_prev + kv, 0.0)

    @pl.when(g == pl.num_programs(1) - 1)
    def _():
        s = s_sc[...]
        for h in range(H_GLA):
            sfin_ref[0, h] = s[h * DK_GLA:(h + 1) * DK_GLA, h * DV_GLA:(h + 1) * DV_GLA]


def _gla_prompt(gq, gk, gv, la, batch, seq):
    grp = min(GLA_GROUP, seq)
    ng = seq // grp
    nch = grp // GLA_CHUNK
    idx = np.arange(grp)
    same = (idx[:, None] // GLA_CHUNK) == (idx[None, :] // GLA_CHUNK)
    tril = jnp.asarray((same & (idx[None, :] <= idx[:, None])).astype(np.float32), BF16)
    ones = jnp.asarray(same.astype(np.float32), BF16)
    ones_t = jnp.asarray((idx[:, None] // GLA_CHUNK == np.arange(nch * LANES)[None, :] // LANES).astype(np.float32),
                         BF16)
    row = lambda w: pl.BlockSpec((grp, w), lambda b, g: (b * ng + g, 0))
    return pl.pallas_call(
        _gla_prompt_kernel,
        grid=(batch, ng),
        in_specs=[row(GLA_KW), row(GLA_KW), row(GLA_VW), row(GLA_KW),
                  _full_spec(tril.shape), _full_spec(ones.shape), _full_spec(ones_t.shape)],
        out_specs=[row(GLA_VW), pl.BlockSpec((1, H_GLA, DK_GLA, DV_GLA), lambda b, g: (b, 0, 0, 0))],
        out_shape=[jax.ShapeDtypeStruct((batch * seq, GLA_VW), F32),
                   jax.ShapeDtypeStruct((batch, H_GLA, DK_GLA, DV_GLA), F32)],
        scratch_shapes=[pltpu.VMEM((GLA_KW, GLA_VW), F32)],
        compiler_params=_cparams("parallel", "arbitrary"),
        name="gla_prompt",
    )(gq, gk, gv, la, tril, ones, ones_t)


def _gla_sample_kernel(q_ref, k_ref, v_ref, la_ref, s0_ref, tril_ref, o_ref, s1_ref):
    n, ls, _ = q_ref.shape
    la = la_ref[...]
    tril = jnp.broadcast_to(tril_ref[...][None], (n, ls, ls))
    hi, mid, lo = _split3(la)
    bdot = lambda x: jnp.einsum('nts,nsd->ntd', tril, x, preferred_element_type=F32)
    b = (bdot(hi) + bdot(mid)) + bdot(lo)
    b_last = b[:, ls - 1:ls, :]
    q_dec = (q_ref[...] * jnp.exp(b)).astype(BF16)
    k = k_ref[...]
    kd = (k * jnp.exp(-b)).astype(BF16)
    kl = (k * jnp.exp(b_last - b)).astype(BF16)
    v = v_ref[...].astype(BF16)
    s0 = s0_ref[...]
    att = jnp.einsum('ncd,nsd->ncs', q_dec, kd, preferred_element_type=F32)
    r = lax.broadcasted_iota(jnp.int32, (n, ls, ls), 1)
    c = lax.broadcasted_iota(jnp.int32, (n, ls, ls), 2)
    att = jnp.where(c <= r, att, 0.0)
    o = jnp.einsum('ncs,nse->nce', att.astype(BF16), v, preferred_element_type=F32)
    o = o + jnp.einsum('ncd,nde->nce', q_dec, s0.astype(BF16), preferred_element_type=F32)
    o_ref[...] = o
    kv = jnp.einsum('nds,nse->nde', jnp.swapaxes(kl, 1, 2), v, preferred_element_type=F32)
    dec = jnp.swapaxes(jnp.exp(b_last), 1, 2)
    s1_ref[...] = dec * s0 + kv


def _gla_sample(gq, gk, gv, la, state):
    n, ls, _ = gq.shape
    nb = min(32, n)
    tril = jnp.asarray(np.tril(np.ones((ls, ls), np.float32)), BF16)
    blk = lambda a, c: pl.BlockSpec((nb, a, c), lambda i: (i, 0, 0))
    return pl.pallas_call(
        _gla_sample_kernel,
        grid=(n // nb,),
        in_specs=[blk(ls, DK_GLA), blk(ls, DK_GLA), blk(ls, DV_GLA), blk(ls, DK_GLA), blk(DK_GLA, DV_GLA),
                  _full_spec(tril.shape)],
        out_specs=[blk(ls, DV_GLA), blk(DK_GLA, DV_GLA)],
        out_shape=[jax.ShapeDtypeStruct((n, ls, DV_GLA), F32), jax.ShapeDtypeStruct((n, DK_GLA, DV_GLA), F32)],
        compiler_params=_cparams("parallel"),
        name="gla_sample",
    )(gq, gk, gv, la, state, tril)


def _mixproj_kernel(x_ref, fox_ref, gla_ref, gr_ref, gg_ref, wo_ref, gc_ref, wq_ref, x1_ref, q_ref):
    gla = gla_ref[...]
    gr = gr_ref[...]
    acc = _dot(fox_ref[...], wo_ref[0:FOX_W, :])
    for h in range(H_GLA):
        cols = slice(h * DV_GLA, (h + 1) * DV_GLA)
        gh = _rms(gla[:, cols], gg_ref[:, cols])
        grh = gr[:, cols]
        gh = gh * (grh * jax.nn.sigmoid(grh))
        acc = acc + _dot(gh.astype(BF16), wo_ref[FOX_W + h * DV_GLA:FOX_W + (h + 1) * DV_GLA, :])
    x1 = x_ref[...] + acc
    x1_ref[...] = x1
    h2 = _rms(x1, gc_ref[...]).astype(BF16)
    q_ref[...] = (_dot(h2, wq_ref[...]) * (DH_MEM ** -0.5)).astype(BF16)


def _mixproj(x2d, fox_o, gla_o, gr, g_gla, w_out_bf, g_cross, w_q_bf):
    t = x2d.shape[0]
    tt = min(TOK_TILE, t)
    row = lambda w: pl.BlockSpec((tt, w), lambda i: (i, 0))
    return pl.pallas_call(
        _mixproj_kernel,
        grid=(t // tt,),
        in_specs=[row(D_MODEL), row(FOX_W), row(GLA_VW), row(GLA_VW), _full_spec(g_gla.shape),
                  _full_spec(w_out_bf.shape), _full_spec(g_cross.shape), _full_spec(w_q_bf.shape)],
        out_specs=[row(D_MODEL), row(D_MODEL)],
        out_shape=[jax.ShapeDtypeStruct((t, D_MODEL), F32), jax.ShapeDtypeStruct((t, D_MODEL), BF16)],
        compiler_params=_cparams("parallel"),
        name="mixproj",
    )(x2d, fox_o, gla_o, gr, g_gla, w_out_bf, g_cross, w_q_bf)


def _memkv_kernel(m_ref, g_ref, wk_ref, wv_ref, k_ref, v_ref):
    m = _rms(m_ref[...], g_ref[...]).astype(BF16)
    k_ref[...] = _dot(m, wk_ref[...])
    v_ref[...] = _dot(m, wv_ref[...])


def _memkv(mem2d, g_mem, wk_bf, wv_bf):
    t = mem2d.shape[0]
    tt = min(TOK_TILE, t)
    row = pl.BlockSpec((tt, D_MODEL), lambda i: (i, 0))
    return pl.pallas_call(
        _memkv_kernel,
        grid=(t // tt,),
        in_specs=[row, _full_spec(g_mem.shape), _full_spec(wk_bf.shape), _full_spec(wv_bf.shape)],
        out_specs=[row, row],
        out_shape=[jax.ShapeDtypeStruct((t, D_MODEL), F32)] * 2,
        compiler_params=_cparams("parallel"),
        name="memkv",
    )(mem2d, g_mem, wk_bf, wv_bf)


def _cross_kernel(q_ref, mk_ref, mv_ref, o_ref, *, per_head_memory):
    q = q_ref[0]
    if per_head_memory:
        lq = q.shape[0]
        n_mem = mk_ref.shape[2]
        q_hm = jnp.concatenate([q[:, h * DH_MEM:(h + 1) * DH_MEM] for h in range(H_MEM)], axis=0)
        mk = mk_ref[0, 0].reshape(n_mem * H_MEM, DH_MEM).astype(BF16)
        mv = mv_ref[0, 0].reshape(n_mem * H_MEM, DH_MEM).astype(BF16)
        s = _dot_nt(q_hm, mk)
        row_head = lax.broadcasted_iota(jnp.int32, s.shape, 0) // lq
        key_head = lax.broadcasted_iota(jnp.int32, s.shape, 1) % H_MEM
        s = jnp.where(row_head == key_head, s, NEG_INF)
        e = jnp.exp(s - jnp.max(s, axis=-1, keepdims=True))
        p = e / jnp.sum(e, axis=-1, keepdims=True)
        o = _dot(p.astype(BF16), mv).astype(o_ref.dtype)
        for h in range(H_MEM):
            o_ref[0, :, h * DH_MEM:(h + 1) * DH_MEM] = o[h * lq:(h + 1) * lq]
        return
    for h in range(H_MEM):
        cols = slice(h * DH_MEM, (h + 1) * DH_MEM)
        mk = mk_ref[0, :, cols].astype(BF16)
        mv = mv_ref[0, :, cols].astype(BF16)
        s = _dot_nt(q[:, cols], mk)
        e = jnp.exp(s - jnp.max(s, axis=-1, keepdims=True))
        p = e / jnp.sum(e, axis=-1, keepdims=True)
        o_ref[0, :, cols] = _dot(p.astype(BF16), mv).astype(o_ref.dtype)


def _cross(q3, mk, mv, layer=None):
    b, lq, _ = q3.shape
    tl = min(TOK_TILE, lq)
    if layer is None:
        mem_spec = pl.BlockSpec((1,) + mk.shape[1:], lambda i, j: (i, 0, 0))
    else:
        mem_spec = pl.BlockSpec((1, 1) + mk.shape[2:], lambda i, j: (layer, i, 0, 0, 0))
    return pl.pallas_call(
        functools.partial(_cross_kernel, per_head_memory=layer is not None),
        grid=(b, lq // tl),
        in_specs=[pl.BlockSpec((1, tl, D_MODEL), lambda i, j: (i, j, 0)), mem_spec, mem_spec],
        out_specs=pl.BlockSpec((1, tl, D_MODEL), lambda i, j: (i, j, 0)),
        out_shape=jax.ShapeDtypeStruct((b, lq, D_MODEL), BF16),
        compiler_params=_cparams("parallel", "parallel"),
        name="cross_attn",
    )(q3, mk, mv)


def _peer_prep_kernel(u_ref, v_ref, ub_ref, vt_ref):
    ub_ref[...] = u_ref[...].astype(BF16)
    vt_ref[...] = v_ref[...].T.astype(BF16)


def _peer_prep(u, v):
    ne = u.shape[0]
    eb = min(PREP_EBLK, ne)
    return pl.pallas_call(
        _peer_prep_kernel,
        grid=(ne // eb,),
        in_specs=[pl.BlockSpec((eb, D_MODEL), lambda i: (i, 0))] * 2,
        out_specs=[pl.BlockSpec((eb, D_MODEL), lambda i: (i, 0)), pl.BlockSpec((D_MODEL, eb), lambda i: (0, i))],
        out_shape=[jax.ShapeDtypeStruct((ne, D_MODEL), BF16), jax.ShapeDtypeStruct((D_MODEL, ne), BF16)],
        compiler_params=_cparams("parallel"),
        name="peer_prep",
    )(u, v)


def _rows_per_word():
    return 4 // jnp.dtype(BF16).itemsize


def _pack_rows(x):
    return pltpu.bitcast(x, jnp.uint32)


def _unpack_rows(x):
    return pltpu.bitcast(x, BF16)


def _top16_rows(s, break_ties):
    nrow = s.shape[0]
    rowid = lax.broadcasted_iota(jnp.int32, s.shape, 0)
    rank = jnp.full(s.shape, float(PEER_TOPK), F32)
    work = s
    vals = []
    for r in range(PEER_TOPK):
        mx = jnp.max(work, axis=0, keepdims=True)
        if break_ties:
            sel = rowid == jnp.min(jnp.where(work == mx, rowid, nrow), axis=0, keepdims=True)
        else:
            sel = work == mx
        rank = jnp.where(sel, float(r), rank)
        work = jnp.where(sel, -jnp.inf, work)
        vals.append(mx)
    return rank, jnp.concatenate(vals, axis=0)


def _ranked_16(rank):
    n = jnp.sum(jnp.where(rank < float(PEER_TOPK), 1.0, 0.0), axis=0, keepdims=True)
    return jnp.where(n == float(PEER_TOPK), 1.0, 0.0)


def _peer_route_kernel(x_ref, x1_ref, wo_ref, g_ref, wq_ref, sk_ref, ca_ref, cb_ref, cg_ref,
                       x2_ref, hn_ref, c1_ref, nrow_ref, p2_ref, rk2_ref, q_sc, s_sc):
    tt = x_ref.shape[0]
    nch = tt // LANES
    x2 = x1_ref[...] + _dot(x_ref[...], wo_ref[...])
    x2_ref[...] = x2
    hn = _rms(x2, g_ref[...])
    hn_ref[...] = hn.T.astype(BF16)
    q_sc[...] = _dot(hn.astype(BF16), wq_ref[...]).astype(BF16)
    for hj in range(2 * H_PEER):
        st = _dot_nt(sk_ref[hj % 2], q_sc[:, hj * N_KEYS:(hj + 1) * N_KEYS])
        for ch in range(nch):
            s_sc[hj, ch] = st[:, ch * LANES:(ch + 1) * LANES]

    cand_row = lax.broadcasted_iota(jnp.int32, (N_CAND_PAD, LANES), 0)

    per_unit = 2 if nch % 2 == 0 else 1

    def unit(i, carry):
        h = i // (nch // per_unit)
        first = (i % (nch // per_unit)) * per_unit
        tie_free = route_chunk(h, first, False)
        for d in range(1, per_unit):
            tie_free = tie_free * route_chunk(h, first + d, False)

        @pl.when(jnp.min(tie_free) < 0.5)
        def _():
            for d in range(per_unit):
                route_chunk(h, first + d, True)

        return carry

    def route_chunk(h, ch, break_ties):
        s1 = s_sc[2 * h, ch]
        s2 = s_sc[2 * h + 1, ch]
        rank1, v1 = _top16_rows(s1, break_ties)
        rank2, v2 = _top16_rows(s2, break_ties)
        cand = _sel_dot_l(ca_ref[...], v1) + _sel_dot_l(cb_ref[...], v2)
        cand = jnp.where(cand_row < N_CAND, cand, -jnp.inf)
        crank, _ = _top16_rows(cand, break_ties)
        chosen = crank < float(PEER_TOPK)
        n_of_a = _dot(cg_ref[...], jnp.where(chosen, 1.0, 0.0).astype(BF16))
        z = jnp.sum(jnp.where(chosen, jnp.exp(cand - cand[0:1]), 0.0), axis=0, keepdims=True)
        nrow = jnp.zeros_like(rank1)
        for a in range(PEER_TOPK):
            nrow = jnp.where(rank1 == float(a), n_of_a[a:a + 1], nrow)
        c1_ref[h, ch] = jnp.exp(s1 - v1[0:1]) / z
        nrow_ref[h, ch] = nrow
        p2_ref[h, ch] = _pack_rows(jnp.exp(s2 - v2[0:1]).astype(BF16))
        rk2_ref[h, ch] = _pack_rows(rank2.astype(BF16))
        return _ranked_16(rank1) * _ranked_16(rank2) * _ranked_16(crank)

    lax.fori_loop(0, H_PEER * nch // per_unit, unit, 0)


def _peer_route(oc2d, x1, w_o_bf, g_ffn, w_q_bf, sk_bf):
    t = x1.shape[0]
    tt = min(PEER_TOK, t)
    nch = tt // LANES
    pairs = [(a, b) for a in range(PEER_TOPK) for b in range(PEER_TOPK) if (a + 1) * (b + 1) <= PEER_TOPK]
    assert len(pairs) == N_CAND
    ca = np.zeros((N_CAND_PAD, PEER_TOPK), np.float32)
    cb = np.zeros((N_CAND_PAD, PEER_TOPK), np.float32)
    cg = np.zeros((PEER_TOPK, N_CAND_PAD), np.float32)
    for i, (a, b) in enumerate(pairs):
        ca[i, a] = 1.0
        cb[i, b] = 1.0
        cg[a, i] = 1.0
    ca, cb, cg = (jnp.asarray(m, BF16) for m in (ca, cb, cg))
    row = lambda w: pl.BlockSpec((tt, w), lambda i: (i, 0))
    krows = N_KEYS // _rows_per_word()
    sel = pl.BlockSpec((H_PEER, nch, N_KEYS, LANES), lambda i: (0, i, 0, 0))
    sel_shape = jax.ShapeDtypeStruct((H_PEER, t // LANES, N_KEYS, LANES), F32)
    packed = pl.BlockSpec((H_PEER, nch, krows, LANES), lambda i: (0, i, 0, 0))
    packed_shape = jax.ShapeDtypeStruct((H_PEER, t // LANES, krows, LANES), jnp.uint32)
    return pl.pallas_call(
        _peer_route_kernel,
        grid=(t // tt,),
        in_specs=[row(D_MODEL), row(D_MODEL), _full_spec(w_o_bf.shape), _full_spec(g_ffn.shape),
                  _full_spec(w_q_bf.shape), _full_spec(sk_bf.shape),
                  _full_spec(ca.shape), _full_spec(cb.shape), _full_spec(cg.shape)],
        out_specs=[row(D_MODEL), pl.BlockSpec((D_MODEL, tt), lambda i: (0, i)), sel, sel, packed, packed],
        out_shape=[jax.ShapeDtypeStruct((t, D_MODEL), F32), jax.ShapeDtypeStruct((D_MODEL, t), BF16),
                   sel_shape, sel_shape, packed_shape, packed_shape],
        scratch_shapes=[pltpu.VMEM((tt, 2 * H_PEER * N_KEYS), BF16),
                        pltpu.VMEM((2 * H_PEER, nch, N_KEYS, LANES), F32)],
        compiler_params=_cparams("parallel"),
        name="peer_route",
    )(oc2d, x1, w_o_bf, g_ffn, w_q_bf, sk_bf, ca, cb, cg)


def _peer_mix_kernel(x2_ref, hn_ref, u_ref, vt_ref, c1_ref, nrow_ref, p2_ref, rk2_ref, gf_ref,
                     y_ref, a_sc, w_sc, acc_sc, *, final_norm):
    e = pl.program_id(1)
    tt = hn_ref.shape[1]
    nch = tt // LANES
    n_i1 = u_ref.shape[0] // N_KEYS

    @pl.when(e == 0)
    def _():
        acc_sc[...] = jnp.zeros_like(acc_sc)

    krows = N_KEYS // _rows_per_word()
    acc = acc_sc[...]
    def activations(c):
        lo = c * PEER_CHUNK_I1
        ex = slice(lo * N_KEYS, (lo + PEER_CHUNK_I1) * N_KEYS)
        a_sc[lo * krows:(lo + PEER_CHUNK_I1) * krows, :] = _pack_rows(
            jax.nn.gelu(_dot(u_ref[ex, :], hn_ref[...])).astype(BF16))

    def weights(c):
        lo = c * PEER_CHUNK_I1
        for ch in range(nch):
            cols = slice(ch * LANES, (ch + 1) * LANES)
            for r in range(lo, lo + PEER_CHUNK_I1):
                w = jnp.zeros((N_KEYS, LANES), BF16)
                for h in range(H_PEER):
                    wide = lambda ref: jnp.broadcast_to(ref[h, ch, r:r + 1, :], (N_KEYS, LANES)).astype(BF16)
                    p2 = _unpack_rows(p2_ref[h, ch])
                    hit = _unpack_rows(rk2_ref[h, ch]) < wide(nrow_ref)
                    w = w + jnp.where(hit, p2, jnp.zeros_like(p2)) * wide(c1_ref)
                w_sc[r * N_KEYS:(r + 1) * N_KEYS, cols] = w * _unpack_rows(a_sc[r * krows:(r + 1) * krows, cols])

    n_chunks = n_i1 // PEER_CHUNK_I1
    per_sub = PEER_SUB_I1 // PEER_CHUNK_I1
    for c in range(min(2, n_chunks)):
        activations(c)
    for c in range(n_chunks):
        weights(c)
        if c + 2 < n_chunks:
            activations(c + 2)
        if (c + 1) % per_sub == 0:
            ex = slice((c + 1 - per_sub) * PEER_CHUNK_I1 * N_KEYS, (c + 1) * PEER_CHUNK_I1 * N_KEYS)
            acc = acc + _dot(vt_ref[:, ex], w_sc[ex, :])
    acc_sc[...] = acc

    @pl.when(e == pl.num_programs(1) - 1)
    def _():
        y = x2_ref[...] + acc_sc[...].T
        y_ref[...] = _rms(y, gf_ref[...]) if final_norm else y


def _peer_mix(x2, hn, u_bf, vt_bf, c1, nrow, p2, rk2, g_final, final_norm):
    t = x2.shape[0]
    tt = min(PEER_TOK, t)
    nch = tt // LANES
    ne = u_bf.shape[0]
    eb = min(PEER_EBLK, ne)
    n_i1 = eb // N_KEYS
    row = lambda w: pl.BlockSpec((tt, w), lambda i, e: (i, 0))
    part = pl.BlockSpec((H_PEER, nch, n_i1, LANES), lambda i, e: (0, i, e, 0))
    full = pl.BlockSpec((H_PEER, nch, N_KEYS // _rows_per_word(), LANES), lambda i, e: (0, i, 0, 0))
    return pl.pallas_call(
        functools.partial(_peer_mix_kernel, final_norm=final_norm),
        grid=(t // tt, ne // eb),
        in_specs=[row(D_MODEL), pl.BlockSpec((D_MODEL, tt), lambda i, e: (0, i)),
                  pl.BlockSpec((eb, D_MODEL), lambda i, e: (e, 0)),
                  pl.BlockSpec((D_MODEL, eb), lambda i, e: (0, e)),
                  part, part, full, full, pl.BlockSpec(g_final.shape, lambda i, e: (0, 0))],
        out_specs=row(D_MODEL),
        out_shape=jax.ShapeDtypeStruct((t, D_MODEL), F32),
        scratch_shapes=[pltpu.VMEM((eb // _rows_per_word(), tt), jnp.uint32), pltpu.VMEM((eb, tt), BF16),
                        pltpu.VMEM((D_MODEL, tt), F32)],
        compiler_params=_cparams("parallel", "arbitrary"),
        name="peer_mix",
    )(x2, hn, u_bf, vt_bf, c1, nrow, p2, rk2, g_final)


def _prep_in_weights(w_in, b_fox_f, w_gate2, b_gate):
    o = np.cumsum((0,) + IN_SIZES)
    seg = lambda i: w_in[:, int(o[i]):int(o[i + 1])]
    pad = jnp.zeros((D_MODEL, LANES - H_FOX - GLA_GATE_RANK), w_in.dtype)
    w_perm = jnp.concatenate([seg(0), seg(1), seg(2), seg(4), seg(5), seg(6), seg(8), seg(3), seg(7), pad],
                             axis=1).astype(BF16)
    b_small = jnp.zeros((1, LANES), F32).at[0, 0:H_FOX].set(b_fox_f)
    wg_pad = jnp.zeros((LANES, GLA_KW), F32).at[H_FOX:H_FOX + GLA_GATE_RANK].set(w_gate2).astype(BF16)
    return w_perm, b_small, wg_pad, b_gate.reshape(1, GLA_KW)


def kernel(x_prompt, x_sample, cache_fox_k, cache_fox_v, cache_fox_logf, state_gla, cache_mem_k, cache_mem_v, page_table, mem_prompt, g_mix, w_in, b_fox_f, w_gla_gate2, b_gla_gate, g_gla_out, w_out, g_cross, g_mem, w_mem_k, w_mem_v, w_cross_q, w_cross_o, g_ffn, peer_w_q, peer_subkeys, peer_u, peer_v, g_final):
    depth = w_in.shape[0]
    bp, lp, _ = x_prompt.shape
    bs, ls, _ = x_sample.shape
    n_mem = mem_prompt.shape[1]
    n_pool, page = cache_fox_k.shape[1:3]
    xp = x_prompt.reshape(bp * lp, D_MODEL)
    xs = x_sample.reshape(bs * ls, D_MODEL)
    vec = lambda a: a.reshape(1, -1)
    kcache_t = cache_fox_k.transpose(0, 1, 3, 4, 2).reshape(depth, n_pool, FOX_W, page)
    vcache_t = cache_fox_v.transpose(0, 1, 3, 4, 2).reshape(depth, n_pool, FOX_W, page)
    lfcache_t = cache_fox_logf.transpose(0, 1, 3, 2)
    outs = [[] for _ in range(10)]
    for l in range(depth):
        w_perm, b_small, wg_pad, b_gate = _prep_in_weights(w_in[l], b_fox_f[l], w_gla_gate2[l], b_gla_gate[l])
        w_out_bf = w_out[l].astype(BF16)
        w_cq_bf = w_cross_q[l].astype(BF16)
        w_co_bf = w_cross_o[l].astype(BF16)
        w_pq_bf = peer_w_q[l].astype(BF16)
        sk_bf = peer_subkeys[l].astype(BF16)
        u_bf, vt_bf = _peer_prep(peer_u[l], peer_v[l])
        last = l == depth - 1

        def tail(x, fox_o, gla_o, gr, mk, mv, nb, mem_layer):
            x1, qc = _mixproj(x, fox_o, gla_o, gr, vec(g_gla_out[l]), w_out_bf, vec(g_cross[l]), w_cq_bf)
            oc = _cross(qc.reshape(nb, -1, D_MODEL), mk, mv, mem_layer).reshape(-1, D_MODEL)
            x2, hn, c1, nrow, p2, rk2 = _peer_route(oc, x1, w_co_bf, vec(g_ffn[l]), w_pq_bf, sk_bf)
            return _peer_mix(x2, hn, u_bf, vt_bf, c1, nrow, p2, rk2, vec(g_final), last)

        fq_bf, fk, fv, lf, gq, gk, gv, la, gr = _inproj(xp, vec(g_mix[l]), w_perm, b_small, wg_pad, b_gate,
                                                        DH_FOX ** -0.5 * _LOG2E)
        qc, kc, vt, fk_t, fv_t, lf_t = _fox_prep(fq_bf, fk, fv, lf, bp, lp)
        fox_o = _fox_prompt(qc, kc, vt, bp, lp)
        gla_o, s_fin = _gla_prompt(gq, gk, gv, la, bp, lp)
        mk, mv = _memkv(mem_prompt.reshape(bp * n_mem, D_MODEL), vec(g_mem[l]), w_mem_k[l].astype(BF16),
                        w_mem_v[l].astype(BF16))
        mk3 = mk.reshape(bp, n_mem, D_MODEL)
        mv3 = mv.reshape(bp, n_mem, D_MODEL)
        xp = tail(xp, fox_o, gla_o, gr, mk3, mv3, bp, None)
        outs[0].append(fk_t.reshape(bp, H_FOX, DH_FOX, lp).transpose(0, 3, 1, 2))
        outs[1].append(fv_t.reshape(bp, H_FOX, DH_FOX, lp).transpose(0, 3, 1, 2))
        outs[2].append(lf_t.transpose(0, 2, 1))
        outs[3].append(s_fin)
        outs[4].append(mk.reshape(bp, n_mem, H_MEM, DH_MEM))
        outs[5].append(mv.reshape(bp, n_mem, H_MEM, DH_MEM))

        fq_bf, fk, fv, lf, gq, gk, gv, la, gr = _inproj(xs, vec(g_mix[l]), w_perm, b_small, wg_pad, b_gate,
                                                        DH_FOX ** -0.5)
        seqs = lambda a: a.reshape(bs, ls, FOX_W)
        fox_o = _fox_sample(seqs(fq_bf), seqs(fk), seqs(fv), lf.reshape(bs, ls, H_FOX).transpose(0, 2, 1),
                            kcache_t, vcache_t, lfcache_t, page_table, l).reshape(bs * ls, FOX_W)
        heads = lambda a, d: a.reshape(bs, ls, H_GLA, d).transpose(0, 2, 1, 3).reshape(bs * H_GLA, ls, d)
        gla_o, s_new = _gla_sample(heads(gq, DK_GLA), heads(gk, DK_GLA), heads(gv, DV_GLA), heads(la, DK_GLA),
                                   state_gla[l].reshape(bs * H_GLA, DK_GLA, DV_GLA))
        gla_o = gla_o.reshape(bs, H_GLA, ls, DV_GLA).transpose(0, 2, 1, 3).reshape(bs * ls, GLA_VW)
        xs = tail(xs, fox_o, gla_o, gr, cache_mem_k, cache_mem_v, bs, l)
        outs[6].append(fk.reshape(bs, ls, H_FOX, DH_FOX))
        outs[7].append(fv.reshape(bs, ls, H_FOX, DH_FOX))
        outs[8].append(lf.reshape(bs, ls, H_FOX))
        outs[9].append(s_new.reshape(bs, H_GLA, DK_GLA, DV_GLA))
    stacked = [jnp.stack(o, axis=0) for o in outs]
    return (xp.reshape(bp, lp, D_MODEL), xs.reshape(bs, ls, D_MODEL), *stacked)
```

```python
import functools

import numpy as np
import jax
import jax.numpy as jnp
from jax import lax
from jax.experimental import pallas as pl
from jax.experimental.pallas import tpu as pltpu

F32 = jnp.float32
BF16 = jnp.bfloat16

D_MODEL = 1024
H_FOX, DH_FOX = 8, 64
H_GLA, DK_GLA, DV_GLA = 4, 64, 128
GLA_GATE_RANK = 16
GLA_TAU = 16.0
GLA_CHUNK = 64
H_MEM = 4
DH_MEM = D_MODEL // H_MEM
N_KEYS = 128
H_PEER = 8
PEER_TOPK = 16
EPS = 1e-6
NEG_INF = -1e30
_LOG2E = float(np.log2(np.e))
FOX_W = H_FOX * DH_FOX
GLA_KW = H_GLA * DK_GLA
GLA_VW = H_GLA * DV_GLA
IN_SIZES = (FOX_W, FOX_W, FOX_W, H_FOX, GLA_KW, GLA_KW, GLA_VW, GLA_GATE_RANK, GLA_VW)

LANES = 128
VMEM_LIMIT_BYTES = 56 * 2**20

TOK_TILE = 512
INPROJ_TILE = 512
FOX_TILE = 512
FOX_KTILE = 1024
FOX_HP_PER_STEP = 2
GLA_GROUP = 512
PEER_TOK = 512
PEER_EBLK = 2048
PEER_CHUNK_I1 = 4
PEER_SUB_I1 = 8
FOX_PAGES_PER_STEP = 16
PREP_EBLK = 512
N_CAND = 50
N_CAND_PAD = 64


def _cparams(*sem):
    return pltpu.CompilerParams(dimension_semantics=sem, vmem_limit_bytes=VMEM_LIMIT_BYTES)


def _dot(a, b):
    return jnp.dot(a, b, preferred_element_type=F32)


def _dot_nt(a, b):
    return lax.dot_general(a, b, (((1,), (1,)), ((), ())), preferred_element_type=F32)


def _split3(x):
    hi = x.astype(BF16)
    r = x - hi.astype(F32)
    mid = r.astype(BF16)
    lo = (r - mid.astype(F32)).astype(BF16)
    return hi, mid, lo


def _sel_dot_l(m01, x):
    hi, mid, lo = _split3(x)
    return (_dot(m01, hi) + _dot(m01, mid)) + _dot(m01, lo)


def _sel_dot_r(x, m01):
    hi, mid, lo = _split3(x)
    return (_dot(hi, m01) + _dot(mid, m01)) + _dot(lo, m01)


def _rms(x, g):
    return x * lax.rsqrt(jnp.mean(x * x, axis=-1, keepdims=True) + EPS) * g


def _logsig(x):
    return jnp.minimum(x, 0.0) - jnp.log1p(jnp.exp(-jnp.abs(x)))


def _full_spec(shape):
    nd = len(shape)
    return pl.BlockSpec(shape, lambda *_: (0,) * nd)


_C_FQ, _C_FK, _C_FV = 0, FOX_W, 2 * FOX_W
_C_GQ = 3 * FOX_W
_C_GK = _C_GQ + GLA_KW
_C_GV = _C_GK + GLA_KW
_C_GR = _C_GV + GLA_VW
_C_SM = _C_GR + GLA_VW
_C_END = _C_SM + LANES


def _inproj_kernel(x_ref, g_ref, w_ref, bsm_ref, wg_ref, bg_ref,
                   fq_ref, fk_ref, fv_ref, lf_ref, gq_ref, gk_ref, gv_ref, la_ref, gr_ref, *, q_scale):
    h = _rms(x_ref[...], g_ref[...]).astype(BF16)

    def seg(lo, hi):
        return _dot(h, w_ref[:, lo:hi])

    fq_ref[...] = (seg(_C_FQ, _C_FK) * q_scale).astype(BF16)
    fk_ref[...] = seg(_C_FK, _C_FV)
    fv_ref[...] = seg(_C_FV, _C_GQ)
    gq_ref[...] = seg(_C_GQ, _C_GK) * (DK_GLA ** -0.5)
    gk_ref[...] = seg(_C_GK, _C_GV)
    gv_ref[...] = seg(_C_GV, _C_GR)
    gr_ref[...] = seg(_C_GR, _C_SM)
    ysm = seg(_C_SM, _C_END)
    lf_ref[...] = _logsig(ysm + bsm_ref[...])[:, 0:H_FOX]
    z = _dot(ysm.astype(BF16), wg_ref[...]) + bg_ref[...]
    la_ref[...] = _logsig(z) / GLA_TAU


def _inproj(x2d, g_mix, w_perm, b_small, wg_pad, b_gate, q_scale):
    t = x2d.shape[0]
    tt = min(INPROJ_TILE, t)
    row = lambda w: pl.BlockSpec((tt, w), lambda i: (i, 0))
    outs = [(FOX_W, BF16), (FOX_W, F32), (FOX_W, F32), (H_FOX, F32),
            (GLA_KW, F32), (GLA_KW, F32), (GLA_VW, F32), (GLA_KW, F32), (GLA_VW, F32)]
    return pl.pallas_call(
        functools.partial(_inproj_kernel, q_scale=q_scale),
        grid=(t // tt,),
        in_specs=[row(D_MODEL), _full_spec(g_mix.shape), _full_spec(w_perm.shape), _full_spec(b_small.shape),
                  _full_spec(wg_pad.shape), _full_spec(b_gate.shape)],
        out_specs=[row(w) for w, _ in outs],
        out_shape=[jax.ShapeDtypeStruct((t, w), dt) for w, dt in outs],
        compiler_params=_cparams("parallel"),
        name="inproj",
    )(x2d, g_mix, w_perm, b_small, wg_pad, b_gate)


_FOX_QW = 4 * LANES
_FOX_KW = 2 * LANES
_N_HP = FOX_W // LANES


def _fox_aug_tables():
    pq = np.zeros((3, H_FOX, _N_HP * 2 * LANES), np.float32)
    oq = np.zeros((1, _N_HP * 2 * LANES), np.float32)
    pk = np.zeros((3, H_FOX, _N_HP * LANES), np.float32)
    ok = np.zeros((1, _N_HP * LANES), np.float32)
    for hp in range(_N_HP):
        for half in range(2):
            head = 2 * hp + half
            qb = (hp * 2 + half) * LANES
            kb = hp * LANES
            for j in range(3):
                pq[j, head, qb + 6 * half + j] = 1.0
                oq[0, qb + 6 * half + 3 + j] = 1.0
                pk[j, head, kb + 6 * half + 3 + j] = 1.0
                ok[0, kb + 6 * half + j] = 1.0
    return jnp.asarray(pq, BF16), jnp.asarray(oq, F32), jnp.asarray(pk, BF16), jnp.asarray(ok, F32)


def _fox_prep_kernel(q_ref, k_ref, v_ref, lf_ref, tril_ref, pq_ref, oq_ref, pk_ref, ok_ref, eye_ref,
                     qc_ref, kc_ref, vt_ref, fkt_ref, fvt_ref, lft_ref, carry_sc):
    @pl.when(pl.program_id(1) == 0)
    def _():
        carry_sc[...] = jnp.zeros_like(carry_sc)

    lf_parts = _split3(lf_ref[...])
    eye = eye_ref[...]
    lft_ref[0] = (_dot_nt(eye, lf_parts[0]) + _dot_nt(eye, lf_parts[1])) + _dot_nt(eye, lf_parts[2])
    c = _sel_dot_l(tril_ref[...], lf_ref[...]) + carry_sc[0:1, 0:H_FOX]
    carry_sc[0:1, 0:H_FOX] = c[-1:, :]
    parts = _split3(c * _LOG2E)
    aug_q = oq_ref[...]
    aug_k = ok_ref[...]
    for j in range(3):
        aug_q = aug_q + _dot(parts[j], pq_ref[j])
        aug_k = aug_k - _dot(parts[j], pk_ref[j])
    aug_q = aug_q.astype(BF16)
    aug_k = aug_k.astype(BF16)
    lane = lax.broadcasted_iota(jnp.int32, (q_ref.shape[0], LANES), 1)
    for hp in range(_N_HP):
        q = q_ref[:, hp * LANES:(hp + 1) * LANES]
        zero = jnp.zeros_like(q)
        base = hp * _FOX_QW
        qc_ref[:, base:base + LANES] = jnp.where(lane < DH_FOX, q, zero)
        qc_ref[:, base + LANES:base + 2 * LANES] = aug_q[:, (2 * hp) * LANES:(2 * hp + 1) * LANES]
        qc_ref[:, base + 2 * LANES:base + 3 * LANES] = jnp.where(lane >= DH_FOX, q, zero)
        qc_ref[:, base + 3 * LANES:base + 4 * LANES] = aug_q[:, (2 * hp + 1) * LANES:(2 * hp + 2) * LANES]
        kc_ref[:, hp * _FOX_KW:hp * _FOX_KW + LANES] = k_ref[:, hp * LANES:(hp + 1) * LANES].astype(BF16)
        kc_ref[:, hp * _FOX_KW + LANES:(hp + 1) * _FOX_KW] = aug_k[:, hp * LANES:(hp + 1) * LANES]
    v_t = v_ref[...].T
    vt_ref[0] = v_t.astype(BF16)
    fvt_ref[0] = v_t
    fkt_ref[0] = k_ref[...].T


def _fox_prep(fq_bf, fk, fv, lf, batch, seq):
    tt = min(FOX_TILE, seq)
    nt = seq // tt
    tril = jnp.asarray(np.tril(np.ones((tt, tt), np.float32)), BF16)
    consts = [tril, *_fox_aug_tables(), jnp.eye(H_FOX, dtype=BF16)]
    row = lambda w: pl.BlockSpec((tt, w), lambda b, i: (b * nt + i, 0))
    col = lambda h: pl.BlockSpec((1, h, tt), lambda b, i: (b, 0, i))
    return pl.pallas_call(
        _fox_prep_kernel,
        grid=(batch, nt),
        in_specs=[row(FOX_W), row(FOX_W), row(FOX_W), row(H_FOX)] + [_full_spec(a.shape) for a in consts],
        out_specs=[row(_N_HP * _FOX_QW), row(_N_HP * _FOX_KW), col(FOX_W), col(FOX_W), col(FOX_W), col(H_FOX)],
        out_shape=[jax.ShapeDtypeStruct((batch * seq, _N_HP * _FOX_QW), BF16),
                   jax.ShapeDtypeStruct((batch * seq, _N_HP * _FOX_KW), BF16),
                   jax.ShapeDtypeStruct((batch, FOX_W, seq), BF16),
                   jax.ShapeDtypeStruct((batch, FOX_W, seq), F32),
                   jax.ShapeDtypeStruct((batch, FOX_W, seq), F32),
                   jax.ShapeDtypeStruct((batch, H_FOX, seq), F32)],
        scratch_shapes=[pltpu.VMEM((8, LANES), F32)],
        compiler_params=_cparams("parallel", "arbitrary"),
        name="fox_prep",
    )(fq_bf, fk, fv, lf, *consts)


def _fox_prompt_kernel(qi_tab, ki_tab, q_ref, k_ref, vt_ref, o_ref,
                       m_sc, l_sc, acc_sc):
    t = pl.program_id(2)
    qi = qi_tab[t]
    ki = ki_tab[t]
    tq = q_ref.shape[0]
    tk = k_ref.shape[0]

    @pl.when(ki == 0)
    def _():
        m_sc[...] = jnp.full_like(m_sc, NEG_INF)
        l_sc[...] = jnp.zeros_like(l_sc)
        acc_sc[...] = jnp.zeros_like(acc_sc)

    def step(masked):
        n_hh = 2 * FOX_HP_PER_STEP

        def scores_of(hh):
            kc = k_ref[:, (hh // 2) * _FOX_KW:(hh // 2 + 1) * _FOX_KW]
            return _dot_nt(kc, q_ref[:, hh * _FOX_KW:(hh + 1) * _FOX_KW])

        def softmax_update(hh, s):
            if masked:
                kpos = lax.broadcasted_iota(jnp.int32, (tk, tq), 0) + ki * tk
                qpos = lax.broadcasted_iota(jnp.int32, (tk, tq), 1) + qi * tq
                s = jnp.where(kpos <= qpos, s, NEG_INF)
            m_prev = m_sc[hh:hh + 1, :]
            m_new = jnp.maximum(m_prev, jnp.max(s, axis=0, keepdims=True))
            alpha = jnp.exp2(m_prev - m_new)
            p = jnp.exp2(s - m_new)
            l_sc[hh:hh + 1, :] = alpha * l_sc[hh:hh + 1, :] + jnp.sum(p, axis=0, keepdims=True)
            m_sc[hh:hh + 1, :] = m_new
            return alpha, p.astype(BF16)

        def accumulate(hh, alpha, p):
            vt = vt_ref[0, (hh // 2) * LANES:(hh // 2 + 1) * LANES, :]
            acc_sc[hh] = alpha * acc_sc[hh] + _dot(vt, p)

        s_next = {0: scores_of(0)}
        if n_hh > 1:
            s_next[1] = scores_of(1)
        pending = None
        for hh in range(n_hh):
            alpha_p = softmax_update(hh, s_next.pop(hh))
            if hh + 2 < n_hh:
                s_next[hh + 2] = scores_of(hh + 2)
            if pending is not None:
                accumulate(*pending)
            pending = (hh, *alpha_p)
        accumulate(*pending)

    last = (qi * tq) // tk

    @pl.when(ki < last)
    def _():
        step(False)

    @pl.when(ki == last)
    def _():
        step(True)
        dim = lax.broadcasted_iota(jnp.int32, (LANES, tq), 0)
        for hp in range(FOX_HP_PER_STEP):
            o0 = acc_sc[2 * hp] / l_sc[2 * hp:2 * hp + 1, :]
            o1 = acc_sc[2 * hp + 1] / l_sc[2 * hp + 1:2 * hp + 2, :]
            o_ref[:, hp * LANES:(hp + 1) * LANES] = jnp.where(dim < DH_FOX, o0, o1).T.astype(o_ref.dtype)


def _fox_prompt(qc, kc, vt, batch, seq):
    tq = min(FOX_TILE, seq)
    tk = min(FOX_KTILE, seq)
    nq = seq // tq
    nk = seq // tk
    pairs = [(i, j) for i in range(nq) for j in range((i * tq) // tk + 1)]
    qi_tab = jnp.asarray([p[0] for p in pairs], jnp.int32)
    ki_tab = jnp.asarray([p[1] for p in pairs], jnp.int32)
    g = FOX_HP_PER_STEP
    grid_spec = pltpu.PrefetchScalarGridSpec(
        num_scalar_prefetch=2,
        grid=(batch, _N_HP // g, len(pairs)),
        in_specs=[
            pl.BlockSpec((tq, g * _FOX_QW), lambda b, h, t, qt, kt: (b * nq + qt[t], h)),
            pl.BlockSpec((tk, g * _FOX_KW), lambda b, h, t, qt, kt: (b * nk + kt[t], h)),
            pl.BlockSpec((1, g * LANES, tk), lambda b, h, t, qt, kt: (b, h, kt[t])),
        ],
        out_specs=pl.BlockSpec((tq, g * LANES), lambda b, h, t, qt, kt: (b * nq + qt[t], h)),
        scratch_shapes=[pltpu.VMEM((2 * g, tq), F32), pltpu.VMEM((2 * g, tq), F32),
                        pltpu.VMEM((2 * g, LANES, tq), F32)],
    )
    return pl.pallas_call(
        _fox_prompt_kernel,
        grid_spec=grid_spec,
        out_shape=jax.ShapeDtypeStruct((batch * seq, FOX_W), BF16),
        compiler_params=_cparams("parallel", "parallel", "arbitrary"),
        name="fox_prompt",
    )(qi_tab, ki_tab, qc, kc, vt)


def _fox_sample_kernel(pt_ref, q_ref, *refs, n_pp):
    kc, vc, lfc = refs[0:n_pp], refs[n_pp:2 * n_pp], refs[2 * n_pp:3 * n_pp]
    (kn_ref, vn_ref, lfn_ref, tri_ref, trin_ref, o_ref,
     qbd_sc, m_sc, l_sc, acc_sc, carry_sc, kall_sc, vall_sc) = refs[3 * n_pp:]
    step = pl.program_id(1)
    ls = q_ref.shape[1]
    nr = H_FOX * ls
    page = tri_ref.shape[0]
    row_head = lax.broadcasted_iota(jnp.int32, (nr, FOX_W), 0) // ls
    lane_head = lax.broadcasted_iota(jnp.int32, (nr, FOX_W), 1) // DH_FOX

    def rows_of_heads(x):
        return jnp.broadcast_to(x[:, None, :], (H_FOX, ls, x.shape[-1])).reshape(nr, x.shape[-1])

    cn_row = rows_of_heads(_sel_dot_r(lfn_ref[0], trin_ref[...]))
    tnew = lax.broadcasted_iota(jnp.int32, (nr, ls), 1)
    tq = lax.broadcasted_iota(jnp.int32, (nr, ls), 0) % ls
    cn_col = jnp.sum(jnp.where(tnew == tq, cn_row, 0.0), axis=-1, keepdims=True)

    @pl.when(step == 0)
    def _():
        qt = jnp.concatenate([q_ref[0]] * H_FOX, axis=0)
        qbd_sc[...] = jnp.where(row_head == lane_head, qt, jnp.zeros_like(qt))
        m_sc[...] = jnp.full_like(m_sc, NEG_INF)
        l_sc[...] = jnp.zeros_like(l_sc)
        acc_sc[...] = jnp.zeros_like(acc_sc)
        carry_sc[...] = jnp.zeros_like(carry_sc)

    def online(scores, weighted_values):
        m_prev = m_sc[...]
        m_new = m_prev
        for s in scores:
            m_new = jnp.maximum(m_new, jnp.max(s, axis=-1, keepdims=True))
        alpha = jnp.exp(m_prev - m_new)
        l_new = alpha * l_sc[...]
        acc = alpha * acc_sc[...]
        for s, pv in zip(scores, weighted_values):
            pr = jnp.exp(s - m_new)
            l_new = l_new + jnp.sum(pr, axis=-1, keepdims=True)
            acc = acc + pv(pr.astype(BF16))
        m_sc[...] = m_new
        l_sc[...] = l_new
        acc_sc[...] = acc

    c_loc = _sel_dot_r(jnp.concatenate([r[0, 0] for r in lfc], axis=0), tri_ref[...])
    carry = carry_sc[...][:, 0:1]
    qbd = qbd_sc[...]
    c_pages = []
    for j in range(n_pp):
        c_page = c_loc[j * H_FOX:(j + 1) * H_FOX] + carry
        carry = c_page[:, page - 1:page]
        c_pages.append(rows_of_heads(c_page))
        kall_sc[:, j * page:(j + 1) * page] = kc[j][0, 0].astype(BF16)
        vall_sc[:, j * page:(j + 1) * page] = vc[j][0, 0].astype(BF16)
    carry_sc[...] = jnp.broadcast_to(carry, carry_sc.shape)
    s_all = _dot(qbd, kall_sc[...]) + cn_col - jnp.concatenate(c_pages, axis=1)
    online([s_all], [lambda p: _dot_nt(p, vall_sc[...])])

    @pl.when(step == pl.num_programs(1) - 1)
    def _():
        s_new = _dot_nt(qbd, kn_ref[0].astype(BF16))
        s_new = s_new + cn_col - cn_row - rows_of_heads(jnp.broadcast_to(carry, (H_FOX, LANES)))[:, 0:1]
        s_new = jnp.where(tnew <= tq, s_new, NEG_INF)
        online([s_new], [lambda p: _dot(p, vn_ref[0].astype(BF16))])
        o = acc_sc[...] / l_sc[...]
        o = jnp.where(row_head == lane_head, o, 0.0)
        out = o[0:ls]
        for h in range(1, H_FOX):
            out = out + o[h * ls:(h + 1) * ls]
        o_ref[0] = out.astype(o_ref.dtype)


def _fox_sample(fq_bf, fk, fv, lfn_t, kcache_t, vcache_t, lfcache_t, page_table, layer):
    bs, n_pages = page_table.shape
    n_pp = min(FOX_PAGES_PER_STEP, n_pages)
    assert n_pages % n_pp == 0
    page = kcache_t.shape[3]
    ls = fq_bf.shape[1]
    nr = H_FOX * ls
    tri = jnp.asarray(np.triu(np.ones((page, page), np.float32)), BF16)
    trin = jnp.asarray(np.triu(np.ones((ls, ls), np.float32)), BF16)
    page_spec = lambda j, rows: pl.BlockSpec(
        (1, 1, rows, page), lambda b, s, pt: (layer, pt[b * n_pages + s * n_pp + j], 0, 0))
    per_seq = lambda *shape: pl.BlockSpec((1,) + shape, lambda b, s, pt: (b,) + (0,) * len(shape))
    const = lambda a: pl.BlockSpec(a.shape, lambda b, s, pt: (0,) * a.ndim)
    grid_spec = pltpu.PrefetchScalarGridSpec(
        num_scalar_prefetch=1,
        grid=(bs, n_pages // n_pp),
        in_specs=([per_seq(ls, FOX_W)] + [page_spec(j, FOX_W) for j in range(n_pp)] * 2
                  + [page_spec(j, H_FOX) for j in range(n_pp)]
                  + [per_seq(ls, FOX_W), per_seq(ls, FOX_W), per_seq(H_FOX, ls), const(tri), const(trin)]),
        out_specs=per_seq(ls, FOX_W),
        scratch_shapes=[pltpu.VMEM((nr, FOX_W), BF16), pltpu.VMEM((nr, 1), F32), pltpu.VMEM((nr, 1), F32),
                        pltpu.VMEM((nr, FOX_W), F32), pltpu.VMEM((H_FOX, LANES), F32),
                        pltpu.VMEM((FOX_W, n_pp * page), BF16), pltpu.VMEM((FOX_W, n_pp * page), BF16)],
    )
    return pl.pallas_call(
        functools.partial(_fox_sample_kernel, n_pp=n_pp),
        grid_spec=grid_spec,
        out_shape=jax.ShapeDtypeStruct((bs, ls, FOX_W), BF16),
        compiler_params=_cparams("parallel", "arbitrary"),
        name="fox_sample",
    )(page_table.reshape(-1), fq_bf, *([kcache_t] * n_pp), *([vcache_t] * n_pp), *([lfcache_t] * n_pp),
      fk, fv, lfn_t, tri, trin)


def _gla_prompt_kernel(q_ref, k_ref, v_ref, la_ref, tril_ref, ones_ref, onesT_ref,
                       o_ref, sfin_ref, s_sc):
    g = pl.program_id(1)
    n_tok = q_ref.shape[0]
    n_chunks = n_tok // GLA_CHUNK

    @pl.when(g == 0)
    def _():
        s_sc[...] = jnp.zeros_like(s_sc)

    la = la_ref[...]
    b = _sel_dot_l(tril_ref[...], la)
    bl = _sel_dot_l(ones_ref[...], la)
    q_dec = q_ref[...] * jnp.exp(b)
    q_dec_bf = q_dec.astype(BF16)
    k = k_ref[...]
    kd_bf = (k * jnp.exp(-b)).astype(BF16)
    kl_t = (k * jnp.exp(bl - b)).T
    v_bf = v_ref[...].astype(BF16)
    bl_col = _sel_dot_r(la.T, onesT_ref[...])

    r = lax.broadcasted_iota(jnp.int32, (n_tok, n_tok), 0)
    c = lax.broadcasted_iota(jnp.int32, (n_tok, n_tok), 1)
    causal = (r // GLA_CHUNK == c // GLA_CHUNK) & (c <= r)
    lane_q = lax.broadcasted_iota(jnp.int32, (n_tok, GLA_KW), 1) // DK_GLA
    zero_q = jnp.zeros_like(q_dec_bf)
    for h in range(H_GLA):
        att = _dot_nt(jnp.where(lane_q == h, q_dec_bf, zero_q), kd_bf)
        att = jnp.where(causal, att, 0.0)
        o_ref[:, h * DV_GLA:(h + 1) * DV_GLA] = _dot(att.astype(BF16), v_bf[:, h * DV_GLA:(h + 1) * DV_GLA])

    row_head = lax.broadcasted_iota(jnp.int32, (GLA_KW, GLA_VW), 0) // DK_GLA
    lane_head = lax.broadcasted_iota(jnp.int32, (GLA_KW, GLA_VW), 1) // DV_GLA
    same_head = row_head == lane_head
    tok_chunk = lax.broadcasted_iota(jnp.int32, (GLA_KW, n_tok), 1) // GLA_CHUNK
    for n in range(n_chunks):
        rows = slice(n * GLA_CHUNK, (n + 1) * GLA_CHUNK)
        s_prev = s_sc[...]
        o_ref[rows, :] += _dot(q_dec_bf[rows], s_prev.astype(BF16))
        kv = _dot(jnp.where(tok_chunk == n, kl_t, 0.0).astype(BF16), v_bf)
        dec = jnp.exp(bl_col[:, n * LANES:(n + 1) * LANES])
        dec = jnp.concatenate([dec] * (GLA_VW // LANES), axis=1)
        s_sc[...] = jnp.where(same_head, dec * s_prev + kv, 0.0)

    @pl.when(g == pl.num_programs(1) - 1)
    def _():
        s = s_sc[...]
        for h in range(H_GLA):
            sfin_ref[0, h] = s[h * DK_GLA:(h + 1) * DK_GLA, h * DV_GLA:(h + 1) * DV_GLA]


def _gla_prompt(gq, gk, gv, la, batch, seq):
    grp = min(GLA_GROUP, seq)
    ng = seq // grp
    nch = grp // GLA_CHUNK
    idx = np.arange(grp)
    same = (idx[:, None] // GLA_CHUNK) == (idx[None, :] // GLA_CHUNK)
    tril = jnp.asarray((same & (idx[None, :] <= idx[:, None])).astype(np.float32), BF16)
    ones = jnp.asarray(same.astype(np.float32), BF16)
    ones_t = jnp.asarray((idx[:, None] // GLA_CHUNK == np.arange(nch * LANES)[None, :] // LANES).astype(np.float32),
                         BF16)
    row = lambda w: pl.BlockSpec((grp, w), lambda b, g: (b * ng + g, 0))
    return pl.pallas_call(
        _gla_prompt_kernel,
        grid=(batch, ng),
        in_specs=[row(GLA_KW), row(GLA_KW), row(GLA_VW), row(GLA_KW),
                  _full_spec(tril.shape), _full_spec(ones.shape), _full_spec(ones_t.shape)],
        out_specs=[row(GLA_VW), pl.BlockSpec((1, H_GLA, DK_GLA, DV_GLA), lambda b, g: (b, 0, 0, 0))],
        out_shape=[jax.ShapeDtypeStruct((batch * seq, GLA_VW), F32),
                   jax.ShapeDtypeStruct((batch, H_GLA, DK_GLA, DV_GLA), F32)],
        scratch_shapes=[pltpu.VMEM((GLA_KW, GLA_VW), F32)],
        compiler_params=_cparams("parallel", "arbitrary"),
        name="gla_prompt",
    )(gq, gk, gv, la, tril, ones, ones_t)


def _gla_sample_kernel(q_ref, k_ref, v_ref, la_ref, s0_ref, tril_ref, o_ref, s1_ref):
    n, ls, _ = q_ref.shape
    la = la_ref[...]
    tril = jnp.broadcast_to(tril_ref[...][None], (n, ls, ls))
    hi, mid, lo = _split3(la)
    bdot = lambda x: jnp.einsum('nts,nsd->ntd', tril, x, preferred_element_type=F32)
    b = (bdot(hi) + bdot(mid)) + bdot(lo)
    b_last = b[:, ls - 1:ls, :]
    q_dec = (q_ref[...] * jnp.exp(b)).astype(BF16)
    k = k_ref[...]
    kd = (k * jnp.exp(-b)).astype(BF16)
    kl = (k * jnp.exp(b_last - b)).astype(BF16)
    v = v_ref[...].astype(BF16)
    s0 = s0_ref[...]
    att = jnp.einsum('ncd,nsd->ncs', q_dec, kd, preferred_element_type=F32)
    r = lax.broadcasted_iota(jnp.int32, (n, ls, ls), 1)
    c = lax.broadcasted_iota(jnp.int32, (n, ls, ls), 2)
    att = jnp.where(c <= r, att, 0.0)
    o = jnp.einsum('ncs,nse->nce', att.astype(BF16), v, preferred_element_type=F32)
    o = o + jnp.einsum('ncd,nde->nce', q_dec, s0.astype(BF16), preferred_element_type=F32)
    o_ref[...] = o
    kv = jnp.einsum('nds,nse->nde', jnp.swapaxes(kl, 1, 2), v, preferred_element_type=F32)
    dec = jnp.swapaxes(jnp.exp(b_last), 1, 2)
    s1_ref[...] = dec * s0 + kv


def _gla_sample(gq, gk, gv, la, state):
    n, ls, _ = gq.shape
    nb = min(32, n)
    tril = jnp.asarray(np.tril(np.ones((ls, ls), np.float32)), BF16)
    blk = lambda a, c: pl.BlockSpec((nb, a, c), lambda i: (i, 0, 0))
    return pl.pallas_call(
        _gla_sample_kernel,
        grid=(n // nb,),
        in_specs=[blk(ls, DK_GLA), blk(ls, DK_GLA), blk(ls, DV_GLA), blk(ls, DK_GLA), blk(DK_GLA, DV_GLA),
                  _full_spec(tril.shape)],
        out_specs=[blk(ls, DV_GLA), blk(DK_GLA, DV_GLA)],
        out_shape=[jax.ShapeDtypeStruct((n, ls, DV_GLA), F32), jax.ShapeDtypeStruct((n, DK_GLA, DV_GLA), F32)],
        compiler_params=_cparams("parallel"),
        name="gla_sample",
    )(gq, gk, gv, la, state, tril)


def _mixproj_kernel(x_ref, fox_ref, gla_ref, gr_ref, gg_ref, wo_ref, gc_ref, wq_ref, x1_ref, q_ref):
    gla = gla_ref[...]
    gr = gr_ref[...]
    acc = _dot(fox_ref[...], wo_ref[0:FOX_W, :])
    for h in range(H_GLA):
        cols = slice(h * DV_GLA, (h + 1) * DV_GLA)
        gh = _rms(gla[:, cols], gg_ref[:, cols])
        grh = gr[:, cols]
        gh = gh * (grh * jax.nn.sigmoid(grh))
        acc = acc + _dot(gh.astype(BF16), wo_ref[FOX_W + h * DV_GLA:FOX_W + (h + 1) * DV_GLA, :])
    x1 = x_ref[...] + acc
    x1_ref[...] = x1
    h2 = _rms(x1, gc_ref[...]).astype(BF16)
    q_ref[...] = (_dot(h2, wq_ref[...]) * (DH_MEM ** -0.5)).astype(BF16)


def _mixproj(x2d, fox_o, gla_o, gr, g_gla, w_out_bf, g_cross, w_q_bf):
    t = x2d.shape[0]
    tt = min(TOK_TILE, t)
    row = lambda w: pl.BlockSpec((tt, w), lambda i: (i, 0))
    return pl.pallas_call(
        _mixproj_kernel,
        grid=(t // tt,),
        in_specs=[row(D_MODEL), row(FOX_W), row(GLA_VW), row(GLA_VW), _full_spec(g_gla.shape),
                  _full_spec(w_out_bf.shape), _full_spec(g_cross.shape), _full_spec(w_q_bf.shape)],
        out_specs=[row(D_MODEL), row(D_MODEL)],
        out_shape=[jax.ShapeDtypeStruct((t, D_MODEL), F32), jax.ShapeDtypeStruct((t, D_MODEL), BF16)],
        compiler_params=_cparams("parallel"),
        name="mixproj",
    )(x2d, fox_o, gla_o, gr, g_gla, w_out_bf, g_cross, w_q_bf)


def _memkv_kernel(m_ref, g_ref, wk_ref, wv_ref, k_ref, v_ref):
    m = _rms(m_ref[...], g_ref[...]).astype(BF16)
    k_ref[...] = _dot(m, wk_ref[...])
    v_ref[...] = _dot(m, wv_ref[...])


def _memkv(mem2d, g_mem, wk_bf, wv_bf):
    t = mem2d.shape[0]
    tt = min(TOK_TILE, t)
    row = pl.BlockSpec((tt, D_MODEL), lambda i: (i, 0))
    return pl.pallas_call(
        _memkv_kernel,
        grid=(t // tt,),
        in_specs=[row, _full_spec(g_mem.shape), _full_spec(wk_bf.shape), _full_spec(wv_bf.shape)],
        out_specs=[row, row],
        out_shape=[jax.ShapeDtypeStruct((t, D_MODEL), F32)] * 2,
        compiler_params=_cparams("parallel"),
        name="memkv",
    )(mem2d, g_mem, wk_bf, wv_bf)


def _cross_kernel(q_ref, mk_ref, mv_ref, o_ref, *, per_head_memory):
    q = q_ref[0]
    if per_head_memory:
        lq = q.shape[0]
        n_mem = mk_ref.shape[2]
        q_hm = jnp.concatenate([q[:, h * DH_MEM:(h + 1) * DH_MEM] for h in range(H_MEM)], axis=0)
        mk = mk_ref[0, 0].reshape(n_mem * H_MEM, DH_MEM).astype(BF16)
        mv = mv_ref[0, 0].reshape(n_mem * H_MEM, DH_MEM).astype(BF16)
        s = _dot_nt(q_hm, mk)
        row_head = lax.broadcasted_iota(jnp.int32, s.shape, 0) // lq
        key_head = lax.broadcasted_iota(jnp.int32, s.shape, 1) % H_MEM
        s = jnp.where(row_head == key_head, s, NEG_INF)
        e = jnp.exp(s - jnp.max(s, axis=-1, keepdims=True))
        p = e / jnp.sum(e, axis=-1, keepdims=True)
        o = _dot(p.astype(BF16), mv).astype(o_ref.dtype)
        for h in range(H_MEM):
            o_ref[0, :, h * DH_MEM:(h + 1) * DH_MEM] = o[h * lq:(h + 1) * lq]
        return
    for h in range(H_MEM):
        cols = slice(h * DH_MEM, (h + 1) * DH_MEM)
        mk = mk_ref[0, :, cols].astype(BF16)
        mv = mv_ref[0, :, cols].astype(BF16)
        s = _dot_nt(q[:, cols], mk)
        e = jnp.exp(s - jnp.max(s, axis=-1, keepdims=True))
        p = e / jnp.sum(e, axis=-1, keepdims=True)
        o_ref[0, :, cols] = _dot(p.astype(BF16), mv).astype(o_ref.dtype)


def _cross(q3, mk, mv, layer=None):
    b, lq, _ = q3.shape
    tl = min(TOK_TILE, lq)
    if layer is None:
        mem_spec = pl.BlockSpec((1,) + mk.shape[1:], lambda i, j: (i, 0, 0))
    else:
        mem_spec = pl.BlockSpec((1, 1) + mk.shape[2:], lambda i, j: (layer, i, 0, 0, 0))
    return pl.pallas_call(
        functools.partial(_cross_kernel, per_head_memory=layer is not None),
        grid=(b, lq // tl),
        in_specs=[pl.BlockSpec((1, tl, D_MODEL), lambda i, j: (i, j, 0)), mem_spec, mem_spec],
        out_specs=pl.BlockSpec((1, tl, D_MODEL), lambda i, j: (i, j, 0)),
        out_shape=jax.ShapeDtypeStruct((b, lq, D_MODEL), BF16),
        compiler_params=_cparams("parallel", "parallel"),
        name="cross_attn",
    )(q3, mk, mv)


def _peer_prep_kernel(u_ref, v_ref, ub_ref, vt_ref):
    ub_ref[...] = u_ref[...].astype(BF16)
    vt_ref[...] = v_ref[...].T.astype(BF16)


def _peer_prep(u, v):
    ne = u.shape[0]
    eb = min(PREP_EBLK, ne)
    return pl.pallas_call(
        _peer_prep_kernel,
        grid=(ne // eb,),
        in_specs=[pl.BlockSpec((eb, D_MODEL), lambda i: (i, 0))] * 2,
        out_specs=[pl.BlockSpec((eb, D_MODEL), lambda i: (i, 0)), pl.BlockSpec((D_MODEL, eb), lambda i: (0, i))],
        out_shape=[jax.ShapeDtypeStruct((ne, D_MODEL), BF16), jax.ShapeDtypeStruct((D_MODEL, ne), BF16)],
        compiler_params=_cparams("parallel"),
        name="peer_prep",
    )(u, v)


def _rows_per_word():
    return 4 // jnp.dtype(BF16).itemsize


def _pack_rows(x):
    return pltpu.bitcast(x, jnp.uint32)


def _unpack_rows(x):
    return pltpu.bitcast(x, BF16)


def _top16_rows(s, break_ties):
    nrow = s.shape[0]
    rowid = lax.broadcasted_iota(jnp.int32, s.shape, 0)
    rank = jnp.full(s.shape, float(PEER_TOPK), F32)
    work = s
    vals = []
    for r in range(PEER_TOPK):
        mx = jnp.max(work, axis=0, keepdims=True)
        if break_ties:
            sel = rowid == jnp.min(jnp.where(work == mx, rowid, nrow), axis=0, keepdims=True)
        else:
            sel = work == mx
        rank = jnp.where(sel, float(r), rank)
        work = jnp.where(sel, -jnp.inf, work)
        vals.append(mx)
    return rank, jnp.concatenate(vals, axis=0)


def _ranked_16(rank):
    n = jnp.sum(jnp.where(rank < float(PEER_TOPK), 1.0, 0.0), axis=0, keepdims=True)
    return jnp.where(n == float(PEER_TOPK), 1.0, 0.0)


def _peer_route_kernel(x_ref, x1_ref, wo_ref, g_ref, wq_ref, sk_ref, ca_ref, cb_ref, cg_ref,
                       x2_ref, hn_ref, c1_ref, nrow_ref, p2_ref, rk2_ref, q_sc, s_sc):
    tt = x_ref.shape[0]
    nch = tt // LANES
    x2 = x1_ref[...] + _dot(x_ref[...], wo_ref[...])
    x2_ref[...] = x2
    hn = _rms(x2, g_ref[...])
    hn_ref[...] = hn.T.astype(BF16)
    q_sc[...] = _dot(hn.astype(BF16), wq_ref[...]).astype(BF16)
    for hj in range(2 * H_PEER):
        st = _dot_nt(sk_ref[hj % 2], q_sc[:, hj * N_KEYS:(hj + 1) * N_KEYS])
        for ch in range(nch):
            s_sc[hj, ch] = st[:, ch * LANES:(ch + 1) * LANES]

    cand_row = lax.broadcasted_iota(jnp.int32, (N_CAND_PAD, LANES), 0)

    per_unit = 2 if nch % 2 == 0 else 1

    def unit(i, carry):
        h = i // (nch // per_unit)
        first = (i % (nch // per_unit)) * per_unit
        def route(break_ties):
            halves = [rank_halves(h, first + d, break_ties) for d in range(per_unit)]
            tie_free = select_pairs(h, first, break_ties, *halves[0])
            for d in range(1, per_unit):
                tie_free = tie_free * select_pairs(h, first + d, break_ties, *halves[d])
            return tie_free

        tie_free = route(False)

        @pl.when(jnp.min(tie_free) < 0.5)
        def _():
            route(True)

        return carry

    def rank_halves(h, ch, break_ties):
        s1 = s_sc[2 * h, ch]
        s2 = s_sc[2 * h + 1, ch]
        return (s1, s2) + _top16_rows(s1, break_ties) + _top16_rows(s2, break_ties)

    def select_pairs(h, ch, break_ties, s1, s2, rank1, v1, rank2, v2):
        cand = _sel_dot_l(ca_ref[...], v1) + _sel_dot_l(cb_ref[...], v2)
        cand = jnp.where(cand_row < N_CAND, cand, -jnp.inf)
        crank, _ = _top16_rows(cand, break_ties)
        chosen = crank < float(PEER_TOPK)
        n_of_a = _dot(cg_ref[...], jnp.where(chosen, 1.0, 0.0).astype(BF16))
        z = jnp.sum(jnp.where(chosen, jnp.exp(cand - cand[0:1]), 0.0), axis=0, keepdims=True)
        nrow = jnp.zeros_like(rank1)
        for a in range(PEER_TOPK):
            nrow = jnp.where(rank1 == float(a), n_of_a[a:a + 1], nrow)
        c1_ref[h, ch] = jnp.exp(s1 - v1[0:1]) / z
        nrow_ref[h, ch] = nrow
        p2_ref[h, ch] = _pack_rows(jnp.exp(s2 - v2[0:1]).astype(BF16))
        rk2_ref[h, ch] = _pack_rows(rank2.astype(BF16))
        return _ranked_16(rank1) * _ranked_16(rank2) * _ranked_16(crank)

    lax.fori_loop(0, H_PEER * nch // per_unit, unit, 0)


def _peer_route(oc2d, x1, w_o_bf, g_ffn, w_q_bf, sk_bf):
    t = x1.shape[0]
    tt = min(PEER_TOK, t)
    nch = tt // LANES
    pairs = [(a, b) for a in range(PEER_TOPK) for b in range(PEER_TOPK) if (a + 1) * (b + 1) <= PEER_TOPK]
    assert len(pairs) == N_CAND
    ca = np.zeros((N_CAND_PAD, PEER_TOPK), np.float32)
    cb = np.zeros((N_CAND_PAD, PEER_TOPK), np.float32)
    cg = np.zeros((PEER_TOPK, N_CAND_PAD), np.float32)
    for i, (a, b) in enumerate(pairs):
        ca[i, a] = 1.0
        cb[i, b] = 1.0
        cg[a, i] = 1.0
    ca, cb, cg = (jnp.asarray(m, BF16) for m in (ca, cb, cg))
    row = lambda w: pl.BlockSpec((tt, w), lambda i: (i, 0))
    krows = N_KEYS // _rows_per_word()
    sel = pl.BlockSpec((H_PEER, nch, N_KEYS, LANES), lambda i: (0, i, 0, 0))
    sel_shape = jax.ShapeDtypeStruct((H_PEER, t // LANES, N_KEYS, LANES), F32)
    packed = pl.BlockSpec((H_PEER, nch, krows, LANES), lambda i: (0, i, 0, 0))
    packed_shape = jax.ShapeDtypeStruct((H_PEER, t // LANES, krows, LANES), jnp.uint32)
    return pl.pallas_call(
        _peer_route_kernel,
        grid=(t // tt,),
        in_specs=[row(D_MODEL), row(D_MODEL), _full_spec(w_o_bf.shape), _full_spec(g_ffn.shape),
                  _full_spec(w_q_bf.shape), _full_spec(sk_bf.shape),
                  _full_spec(ca.shape), _full_spec(cb.shape), _full_spec(cg.shape)],
        out_specs=[row(D_MODEL), pl.BlockSpec((D_MODEL, tt), lambda i: (0, i)), sel, sel, packed, packed],
        out_shape=[jax.ShapeDtypeStruct((t, D_MODEL), F32), jax.ShapeDtypeStruct((D_MODEL, t), BF16),
                   sel_shape, sel_shape, packed_shape, packed_shape],
        scratch_shapes=[pltpu.VMEM((tt, 2 * H_PEER * N_KEYS), BF16),
                        pltpu.VMEM((2 * H_PEER, nch, N_KEYS, LANES), F32)],
        compiler_params=_cparams("parallel"),
        name="peer_route",
    )(oc2d, x1, w_o_bf, g_ffn, w_q_bf, sk_bf, ca, cb, cg)


def _peer_mix_kernel(x2_ref, hn_ref, u_ref, vt_ref, c1_ref, nrow_ref, p2_ref, rk2_ref, gf_ref,
                     y_ref, a_sc, w_sc, acc_sc, *, final_norm):
    e = pl.program_id(1)
    tt = hn_ref.shape[1]
    nch = tt // LANES
    n_i1 = u_ref.shape[0] // N_KEYS

    @pl.when(e == 0)
    def _():
        acc_sc[...] = jnp.zeros_like(acc_sc)

    krows = N_KEYS // _rows_per_word()
    acc = acc_sc[...]
    def activations(c):
        lo = c * PEER_CHUNK_I1
        ex = slice(lo * N_KEYS, (lo + PEER_CHUNK_I1) * N_KEYS)
        a_sc[lo * krows:(lo + PEER_CHUNK_I1) * krows, :] = _pack_rows(
            jax.nn.gelu(_dot(u_ref[ex, :], hn_ref[...])).astype(BF16))

    def weights(c):
        lo = c * PEER_CHUNK_I1
        for ch in range(nch):
            cols = slice(ch * LANES, (ch + 1) * LANES)
            for r in range(lo, lo + PEER_CHUNK_I1):
                w = jnp.zeros((N_KEYS, LANES), BF16)
                for h in range(H_PEER):
                    wide = lambda ref: jnp.broadcast_to(ref[h, ch, r:r + 1, :], (N_KEYS, LANES)).astype(BF16)
                    p2 = _unpack_rows(p2_ref[h, ch])
                    hit = _unpack_rows(rk2_ref[h, ch]) < wide(nrow_ref)
                    w = w + jnp.where(hit, p2, jnp.zeros_like(p2)) * wide(c1_ref)
                w_sc[r * N_KEYS:(r + 1) * N_KEYS, cols] = w * _unpack_rows(a_sc[r * krows:(r + 1) * krows, cols])

    n_chunks = n_i1 // PEER_CHUNK_I1
    per_sub = PEER_SUB_I1 // PEER_CHUNK_I1
    for c in range(min(2, n_chunks)):
        activations(c)
    for c in range(n_chunks):
        weights(c)
        if c + 2 < n_chunks:
            activations(c + 2)
        if (c + 1) % per_sub == 0:
            ex = slice((c + 1 - per_sub) * PEER_CHUNK_I1 * N_KEYS, (c + 1) * PEER_CHUNK_I1 * N_KEYS)
            acc = acc + _dot(vt_ref[:, ex], w_sc[ex, :])
    acc_sc[...] = acc

    @pl.when(e == pl.num_programs(1) - 1)
    def _():
        y = x2_ref[...] + acc_sc[...].T
        y_ref[...] = _rms(y, gf_ref[...]) if final_norm else y


def _peer_mix(x2, hn, u_bf, vt_bf, c1, nrow, p2, rk2, g_final, final_norm):
    t = x2.shape[0]
    tt = min(PEER_TOK, t)
    nch = tt // LANES
    ne = u_bf.shape[0]
    eb = min(PEER_EBLK, ne)
    n_i1 = eb // N_KEYS
    row = lambda w: pl.BlockSpec((tt, w), lambda i, e: (i, 0))
    part = pl.BlockSpec((H_PEER, nch, n_i1, LANES), lambda i, e: (0, i, e, 0))
    full = pl.BlockSpec((H_PEER, nch, N_KEYS // _rows_per_word(), LANES), lambda i, e: (0, i, 0, 0))
    return pl.pallas_call(
        functools.partial(_peer_mix_kernel, final_norm=final_norm),
        grid=(t // tt, ne // eb),
        in_specs=[row(D_MODEL), pl.BlockSpec((D_MODEL, tt), lambda i, e: (0, i)),
                  pl.BlockSpec((eb, D_MODEL), lambda i, e: (e, 0)),
                  pl.BlockSpec((D_MODEL, eb), lambda i, e: (0, e)),
                  part, part, full, full, pl.BlockSpec(g_final.shape, lambda i, e: (0, 0))],
        out_specs=row(D_MODEL),
        out_shape=jax.ShapeDtypeStruct((t, D_MODEL), F32),
        scratch_shapes=[pltpu.VMEM((eb // _rows_per_word(), tt), jnp.uint32), pltpu.VMEM((eb, tt), BF16),
                        pltpu.VMEM((D_MODEL, tt), F32)],
        compiler_params=_cparams("parallel", "arbitrary"),
        name="peer_mix",
    )(x2, hn, u_bf, vt_bf, c1, nrow, p2, rk2, g_final)


def _prep_in_weights(w_in, b_fox_f, w_gate2, b_gate):
    o = np.cumsum((0,) + IN_SIZES)
    seg = lambda i: w_in[:, int(o[i]):int(o[i + 1])]
    pad = jnp.zeros((D_MODEL, LANES - H_FOX - GLA_GATE_RANK), w_in.dtype)
    w_perm = jnp.concatenate([seg(0), seg(1), seg(2), seg(4), seg(5), seg(6), seg(8), seg(3), seg(7), pad],
                             axis=1).astype(BF16)
    b_small = jnp.zeros((1, LANES), F32).at[0, 0:H_FOX].set(b_fox_f)
    wg_pad = jnp.zeros((LANES, GLA_KW), F32).at[H_FOX:H_FOX + GLA_GATE_RANK].set(w_gate2).astype(BF16)
    return w_perm, b_small, wg_pad, b_gate.reshape(1, GLA_KW)


def kernel(x_prompt, x_sample, cache_fox_k, cache_fox_v, cache_fox_logf, state_gla, cache_mem_k, cache_mem_v, page_table, mem_prompt, g_mix, w_in, b_fox_f, w_gla_gate2, b_gla_gate, g_gla_out, w_out, g_cross, g_mem, w_mem_k, w_mem_v, w_cross_q, w_cross_o, g_ffn, peer_w_q, peer_subkeys, peer_u, peer_v, g_final):
    depth = w_in.shape[0]
    bp, lp, _ = x_prompt.shape
    bs, ls, _ = x_sample.shape
    n_mem = mem_prompt.shape[1]
    n_pool, page = cache_fox_k.shape[1:3]
    xp = x_prompt.reshape(bp * lp, D_MODEL)
    xs = x_sample.reshape(bs * ls, D_MODEL)
    vec = lambda a: a.reshape(1, -1)
    kcache_t = cache_fox_k.transpose(0, 1, 3, 4, 2).reshape(depth, n_pool, FOX_W, page)
    vcache_t = cache_fox_v.transpose(0, 1, 3, 4, 2).reshape(depth, n_pool, FOX_W, page)
    lfcache_t = cache_fox_logf.transpose(0, 1, 3, 2)
    outs = [[] for _ in range(10)]
    for l in range(depth):
        w_perm, b_small, wg_pad, b_gate = _prep_in_weights(w_in[l], b_fox_f[l], w_gla_gate2[l], b_gla_gate[l])
        w_out_bf = w_out[l].astype(BF16)
        w_cq_bf = w_cross_q[l].astype(BF16)
        w_co_bf = w_cross_o[l].astype(BF16)
        w_pq_bf = peer_w_q[l].astype(BF16)
        sk_bf = peer_subkeys[l].astype(BF16)
        u_bf, vt_bf = _peer_prep(peer_u[l], peer_v[l])
        last = l == depth - 1

        def tail(x, fox_o, gla_o, gr, mk, mv, nb, mem_layer):
            x1, qc = _mixproj(x, fox_o, gla_o, gr, vec(g_gla_out[l]), w_out_bf, vec(g_cross[l]), w_cq_bf)
            oc = _cross(qc.reshape(nb, -1, D_MODEL), mk, mv, mem_layer).reshape(-1, D_MODEL)
            x2, hn, c1, nrow, p2, rk2 = _peer_route(oc, x1, w_co_bf, vec(g_ffn[l]), w_pq_bf, sk_bf)
            return _peer_mix(x2, hn, u_bf, vt_bf, c1, nrow, p2, rk2, vec(g_final), last)

        fq_bf, fk, fv, lf, gq, gk, gv, la, gr = _inproj(xp, vec(g_mix[l]), w_perm, b_small, wg_pad, b_gate,
                                                        DH_FOX ** -0.5 * _LOG2E)
        qc, kc, vt, fk_t, fv_t, lf_t = _fox_prep(fq_bf, fk, fv, lf, bp, lp)
        fox_o = _fox_prompt(qc, kc, vt, bp, lp)
        gla_o, s_fin = _gla_prompt(gq, gk, gv, la, bp, lp)
        mk, mv = _memkv(mem_prompt.reshape(bp * n_mem, D_MODEL), vec(g_mem[l]), w_mem_k[l].astype(BF16),
                        w_mem_v[l].astype(BF16))
        mk3 = mk.reshape(bp, n_mem, D_MODEL)
        mv3 = mv.reshape(bp, n_mem, D_MODEL)
        xp = tail(xp, fox_o, gla_o, gr, mk3, mv3, bp, None)
        outs[0].append(fk_t.reshape(bp, H_FOX, DH_FOX, lp).transpose(0, 3, 1, 2))
        outs[1].append(fv_t.reshape(bp, H_FOX, DH_FOX, lp).transpose(0, 3, 1, 2))
        outs[2].append(lf_t.transpose(0, 2, 1))
        outs[3].append(s_fin)
        outs[4].append(mk.reshape(bp, n_mem, H_MEM, DH_MEM))
        outs[5].append(mv.reshape(bp, n_mem, H_MEM, DH_MEM))

        fq_bf, fk, fv, lf, gq, gk, gv, la, gr = _inproj(xs, vec(g_mix[l]), w_perm, b_small, wg_pad, b_gate,
                                                        DH_FOX ** -0.5)
        seqs = lambda a: a.reshape(bs, ls, FOX_W)
        fox_o = _fox_sample(seqs(fq_bf), seqs(fk), seqs(fv), lf.reshape(bs, ls, H_FOX).transpose(0, 2, 1),
                            kcache_t, vcache_t, lfcache_t, page_table, l).reshape(bs * ls, FOX_W)
        heads = lambda a, d: a.reshape(bs, ls, H_GLA, d).transpose(0, 2, 1, 3).reshape(bs * H_GLA, ls, d)
        gla_o, s_new = _gla_sample(heads(gq, DK_GLA), heads(gk, DK_GLA), heads(gv, DV_GLA), heads(la, DK_GLA),
                                   state_gla[l].reshape(bs * H_GLA, DK_GLA, DV_GLA))
        gla_o = gla_o.reshape(bs, H_GLA, ls, DV_GLA).transpose(0, 2, 1, 3).reshape(bs * ls, GLA_VW)
        xs = tail(xs, fox_o, gla_o, gr, cache_mem_k, cache_mem_v, bs, l)
        outs[6].append(fk.reshape(bs, ls, H_FOX, DH_FOX))
        outs[7].append(fv.reshape(bs, ls, H_FOX, DH_FOX))
        outs[8].append(lf.reshape(bs, ls, H_FOX))
        outs[9].append(s_new.reshape(bs, H_GLA, DK_GLA, DV_GLA))
    stacked = [jnp.stack(o, axis=0) for o in outs]
    return (xp.reshape(bp, lp, D_MODEL), xs.reshape(bs, ls, D_MODEL), *stacked)
```

```python
import functools

import numpy as np
import jax
import jax.numpy as jnp
from jax import lax
from jax.experimental import pallas as pl
from jax.experimental.pallas import tpu as pltpu

F32 = jnp.float32
BF16 = jnp.bfloat16

D_MODEL = 1024
H_FOX, DH_FOX = 8, 64
H_GLA, DK_GLA, DV_GLA = 4, 64, 128
GLA_GATE_RANK = 16
GLA_TAU = 16.0
GLA_CHUNK = 64
H_MEM = 4
DH_MEM = D_MODEL // H_MEM
N_KEYS = 128
H_PEER = 8
PEER_TOPK = 16
EPS = 1e-6
NEG_INF = -1e30
_LOG2E = float(np.log2(np.e))
FOX_W = H_FOX * DH_FOX
GLA_KW = H_GLA * DK_GLA
GLA_VW = H_GLA * DV_GLA
IN_SIZES = (FOX_W, FOX_W, FOX_W, H_FOX, GLA_KW, GLA_KW, GLA_VW, GLA_GATE_RANK, GLA_VW)

LANES = 128
VMEM_LIMIT_BYTES = 56 * 2**20

TOK_TILE = 512
INPROJ_TILE = 512
FOX_TILE = 512
FOX_KTILE = 1024
FOX_HP_PER_STEP = 4
GLA_GROUP = 512
PEER_TOK = 512
PEER_EBLK = 2048
PEER_CHUNK_I1 = 4
PEER_SUB_I1 = 8
FOX_PAGES_PER_STEP = 16
PREP_EBLK = 512
N_CAND = 50
N_CAND_PAD = 64


def _cparams(*sem):
    return pltpu.CompilerParams(dimension_semantics=sem, vmem_limit_bytes=VMEM_LIMIT_BYTES)


def _dot(a, b):
    return jnp.dot(a, b, preferred_element_type=F32)


def _dot_nt(a, b):
    return lax.dot_general(a, b, (((1,), (1,)), ((), ())), preferred_element_type=F32)


def _split3(x):
    hi = x.astype(BF16)
    r = x - hi.astype(F32)
    mid = r.astype(BF16)
    lo = (r - mid.astype(F32)).astype(BF16)
    return hi, mid, lo


def _sel_dot_l(m01, x):
    hi, mid, lo = _split3(x)
    return (_dot(m01, hi) + _dot(m01, mid)) + _dot(m01, lo)


def _sel_dot_r(x, m01):
    hi, mid, lo = _split3(x)
    return (_dot(hi, m01) + _dot(mid, m01)) + _dot(lo, m01)


def _rms(x, g):
    return x * lax.rsqrt(jnp.mean(x * x, axis=-1, keepdims=True) + EPS) * g


def _logsig(x):
    return jnp.minimum(x, 0.0) - jnp.log1p(jnp.exp(-jnp.abs(x)))


def _full_spec(shape):
    nd = len(shape)
    return pl.BlockSpec(shape, lambda *_: (0,) * nd)


_C_FQ, _C_FK, _C_FV = 0, FOX_W, 2 * FOX_W
_C_GQ = 3 * FOX_W
_C_GK = _C_GQ + GLA_KW
_C_GV = _C_GK + GLA_KW
_C_GR = _C_GV + GLA_VW
_C_SM = _C_GR + GLA_VW
_C_END = _C_SM + LANES


def _inproj_kernel(x_ref, g_ref, w_ref, bsm_ref, wg_ref, bg_ref,
                   fq_ref, fk_ref, fv_ref, lf_ref, gq_ref, gk_ref, gv_ref, la_ref, gr_ref, *, q_scale):
    h = _rms(x_ref[...], g_ref[...]).astype(BF16)

    def seg(lo, hi):
        return _dot(h, w_ref[:, lo:hi])

    fq_ref[...] = (seg(_C_FQ, _C_FK) * q_scale).astype(BF16)
    fk_ref[...] = seg(_C_FK, _C_FV)
    fv_ref[...] = seg(_C_FV, _C_GQ)
    gq_ref[...] = seg(_C_GQ, _C_GK) * (DK_GLA ** -0.5)
    gk_ref[...] = seg(_C_GK, _C_GV)
    gv_ref[...] = seg(_C_GV, _C_GR)
    gr_ref[...] = seg(_C_GR, _C_SM)
    ysm = seg(_C_SM, _C_END)
    lf_ref[...] = _logsig(ysm + bsm_ref[...])[:, 0:H_FOX]
    z = _dot(ysm.astype(BF16), wg_ref[...]) + bg_ref[...]
    la_ref[...] = _logsig(z) / GLA_TAU


def _inproj(x2d, g_mix, w_perm, b_small, wg_pad, b_gate, q_scale):
    t = x2d.shape[0]
    tt = min(INPROJ_TILE, t)
    row = lambda w: pl.BlockSpec((tt, w), lambda i: (i, 0))
    outs = [(FOX_W, BF16), (FOX_W, F32), (FOX_W, F32), (H_FOX, F32),
            (GLA_KW, F32), (GLA_KW, F32), (GLA_VW, F32), (GLA_KW, F32), (GLA_VW, F32)]
    return pl.pallas_call(
        functools.partial(_inproj_kernel, q_scale=q_scale),
        grid=(t // tt,),
        in_specs=[row(D_MODEL), _full_spec(g_mix.shape), _full_spec(w_perm.shape), _full_spec(b_small.shape),
                  _full_spec(wg_pad.shape), _full_spec(b_gate.shape)],
        out_specs=[row(w) for w, _ in outs],
        out_shape=[jax.ShapeDtypeStruct((t, w), dt) for w, dt in outs],
        compiler_params=_cparams("parallel"),
        name="inproj",
    )(x2d, g_mix, w_perm, b_small, wg_pad, b_gate)


_FOX_QW = 4 * LANES
_FOX_KW = 2 * LANES
_N_HP = FOX_W // LANES


def _fox_aug_tables():
    pq = np.zeros((3, H_FOX, _N_HP * 2 * LANES), np.float32)
    oq = np.zeros((1, _N_HP * 2 * LANES), np.float32)
    pk = np.zeros((3, H_FOX, _N_HP * LANES), np.float32)
    ok = np.zeros((1, _N_HP * LANES), np.float32)
    for hp in range(_N_HP):
        for half in range(2):
            head = 2 * hp + half
            qb = (hp * 2 + half) * LANES
            kb = hp * LANES
            for j in range(3):
                pq[j, head, qb + 6 * half + j] = 1.0
                oq[0, qb + 6 * half + 3 + j] = 1.0
                pk[j, head, kb + 6 * half + 3 + j] = 1.0
                ok[0, kb + 6 * half + j] = 1.0
    return jnp.asarray(pq, BF16), jnp.asarray(oq, F32), jnp.asarray(pk, BF16), jnp.asarray(ok, F32)


def _fox_prep_kernel(q_ref, k_ref, v_ref, lf_ref, tril_ref, pq_ref, oq_ref, pk_ref, ok_ref, eye_ref,
                     qc_ref, kc_ref, vt_ref, fkt_ref, fvt_ref, lft_ref, carry_sc):
    @pl.when(pl.program_id(1) == 0)
    def _():
        carry_sc[...] = jnp.zeros_like(carry_sc)

    lf_parts = _split3(lf_ref[...])
    eye = eye_ref[...]
    lft_ref[0] = (_dot_nt(eye, lf_parts[0]) + _dot_nt(eye, lf_parts[1])) + _dot_nt(eye, lf_parts[2])
    c = _sel_dot_l(tril_ref[...], lf_ref[...]) + carry_sc[0:1, 0:H_FOX]
    carry_sc[0:1, 0:H_FOX] = c[-1:, :]
    parts = _split3(c * _LOG2E)
    aug_q = oq_ref[...]
    aug_k = ok_ref[...]
    for j in range(3):
        aug_q = aug_q + _dot(parts[j], pq_ref[j])
        aug_k = aug_k - _dot(parts[j], pk_ref[j])
    aug_q = aug_q.astype(BF16)
    aug_k = aug_k.astype(BF16)
    lane = lax.broadcasted_iota(jnp.int32, (q_ref.shape[0], LANES), 1)
    for hp in range(_N_HP):
        q = q_ref[:, hp * LANES:(hp + 1) * LANES]
        zero = jnp.zeros_like(q)
        base = hp * _FOX_QW
        qc_ref[:, base:base + LANES] = jnp.where(lane < DH_FOX, q, zero)
        qc_ref[:, base + LANES:base + 2 * LANES] = aug_q[:, (2 * hp) * LANES:(2 * hp + 1) * LANES]
        qc_ref[:, base + 2 * LANES:base + 3 * LANES] = jnp.where(lane >= DH_FOX, q, zero)
        qc_ref[:, base + 3 * LANES:base + 4 * LANES] = aug_q[:, (2 * hp + 1) * LANES:(2 * hp + 2) * LANES]
        kc_ref[:, hp * _FOX_KW:hp * _FOX_KW + LANES] = k_ref[:, hp * LANES:(hp + 1) * LANES].astype(BF16)
        kc_ref[:, hp * _FOX_KW + LANES:(hp + 1) * _FOX_KW] = aug_k[:, hp * LANES:(hp + 1) * LANES]
    v_t = v_ref[...].T
    vt_ref[0] = v_t.astype(BF16)
    fvt_ref[0] = v_t
    fkt_ref[0] = k_ref[...].T


def _fox_prep(fq_bf, fk, fv, lf, batch, seq):
    tt = min(FOX_TILE, seq)
    nt = seq // tt
    tril = jnp.asarray(np.tril(np.ones((tt, tt), np.float32)), BF16)
    consts = [tril, *_fox_aug_tables(), jnp.eye(H_FOX, dtype=BF16)]
    row = lambda w: pl.BlockSpec((tt, w), lambda b, i: (b * nt + i, 0))
    col = lambda h: pl.BlockSpec((1, h, tt), lambda b, i: (b, 0, i))
    return pl.pallas_call(
        _fox_prep_kernel,
        grid=(batch, nt),
        in_specs=[row(FOX_W), row(FOX_W), row(FOX_W), row(H_FOX)] + [_full_spec(a.shape) for a in consts],
        out_specs=[row(_N_HP * _FOX_QW), row(_N_HP * _FOX_KW), col(FOX_W), col(FOX_W), col(FOX_W), col(H_FOX)],
        out_shape=[jax.ShapeDtypeStruct((batch * seq, _N_HP * _FOX_QW), BF16),
                   jax.ShapeDtypeStruct((batch * seq, _N_HP * _FOX_KW), BF16),
                   jax.ShapeDtypeStruct((batch, FOX_W, seq), BF16),
                   jax.ShapeDtypeStruct((batch, FOX_W, seq), F32),
                   jax.ShapeDtypeStruct((batch, FOX_W, seq), F32),
                   jax.ShapeDtypeStruct((batch, H_FOX, seq), F32)],
        scratch_shapes=[pltpu.VMEM((8, LANES), F32)],
        compiler_params=_cparams("parallel", "arbitrary"),
        name="fox_prep",
    )(fq_bf, fk, fv, lf, *consts)


def _fox_prompt_kernel(qi_tab, ki_tab, q_ref, k_ref, vt_ref, o_ref,
                       m_sc, l_sc, acc_sc):
    t = pl.program_id(2)
    qi = qi_tab[t]
    ki = ki_tab[t]
    tq = q_ref.shape[0]
    tk = k_ref.shape[0]

    @pl.when(ki == 0)
    def _():
        m_sc[...] = jnp.full_like(m_sc, NEG_INF)
        l_sc[...] = jnp.zeros_like(l_sc)
        acc_sc[...] = jnp.zeros_like(acc_sc)

    def step(masked):
        n_hh = 2 * FOX_HP_PER_STEP

        def scores_of(hh):
            kc = k_ref[:, (hh // 2) * _FOX_KW:(hh // 2 + 1) * _FOX_KW]
            return _dot_nt(kc, q_ref[:, hh * _FOX_KW:(hh + 1) * _FOX_KW])

        def softmax_update(hh, s):
            if masked:
                kpos = lax.broadcasted_iota(jnp.int32, (tk, tq), 0) + ki * tk
                qpos = lax.broadcasted_iota(jnp.int32, (tk, tq), 1) + qi * tq
                s = jnp.where(kpos <= qpos, s, NEG_INF)
            m_prev = m_sc[hh:hh + 1, :]
            m_new = jnp.maximum(m_prev, jnp.max(s, axis=0, keepdims=True))
            alpha = jnp.exp2(m_prev - m_new)
            p = jnp.exp2(s - m_new)
            l_sc[hh:hh + 1, :] = alpha * l_sc[hh:hh + 1, :] + jnp.sum(p, axis=0, keepdims=True)
            m_sc[hh:hh + 1, :] = m_new
            return alpha, p.astype(BF16)

        def accumulate(hh, alpha, p):
            vt = vt_ref[0, (hh // 2) * LANES:(hh // 2 + 1) * LANES, :]
            acc_sc[hh] = alpha * acc_sc[hh] + _dot(vt, p)

        s_next = {0: scores_of(0)}
        if n_hh > 1:
            s_next[1] = scores_of(1)
        pending = None
        for hh in range(n_hh):
            alpha_p = softmax_update(hh, s_next.pop(hh))
            if hh + 2 < n_hh:
                s_next[hh + 2] = scores_of(hh + 2)
            if pending is not None:
                accumulate(*pending)
            pending = (hh, *alpha_p)
        accumulate(*pending)

    last = (qi * tq) // tk

    @pl.when(ki < last)
    def _():
        step(False)

    @pl.when(ki == last)
    def _():
        step(True)
        dim = lax.broadcasted_iota(jnp.int32, (LANES, tq), 0)
        for hp in range(FOX_HP_PER_STEP):
            o0 = acc_sc[2 * hp] / l_sc[2 * hp:2 * hp + 1, :]
            o1 = acc_sc[2 * hp + 1] / l_sc[2 * hp + 1:2 * hp + 2, :]
            o_ref[:, hp * LANES:(hp + 1) * LANES] = jnp.where(dim < DH_FOX, o0, o1).T.astype(o_ref.dtype)


def _fox_prompt(qc, kc, vt, batch, seq):
    tq = min(FOX_TILE, seq)
    tk = min(FOX_KTILE, seq)
    nq = seq // tq
    nk = seq // tk
    pairs = [(i, j) for i in range(nq) for j in range((i * tq) // tk + 1)]
    qi_tab = jnp.asarray([p[0] for p in pairs], jnp.int32)
    ki_tab = jnp.asarray([p[1] for p in pairs], jnp.int32)
    g = FOX_HP_PER_STEP
    grid_spec = pltpu.PrefetchScalarGridSpec(
        num_scalar_prefetch=2,
        grid=(batch, _N_HP // g, len(pairs)),
        in_specs=[
            pl.BlockSpec((tq, g * _FOX_QW), lambda b, h, t, qt, kt: (b * nq + qt[t], h)),
            pl.BlockSpec((tk, g * _FOX_KW), lambda b, h, t, qt, kt: (b * nk + kt[t], h)),
            pl.BlockSpec((1, g * LANES, tk), lambda b, h, t, qt, kt: (b, h, kt[t])),
        ],
        out_specs=pl.BlockSpec((tq, g * LANES), lambda b, h, t, qt, kt: (b * nq + qt[t], h)),
        scratch_shapes=[pltpu.VMEM((2 * g, tq), F32), pltpu.VMEM((2 * g, tq), F32),
                        pltpu.VMEM((2 * g, LANES, tq), F32)],
    )
    return pl.pallas_call(
        _fox_prompt_kernel,
        grid_spec=grid_spec,
        out_shape=jax.ShapeDtypeStruct((batch * seq, FOX_W), BF16),
        compiler_params=_cparams("parallel", "parallel", "arbitrary"),
        name="fox_prompt",
    )(qi_tab, ki_tab, qc, kc, vt)


def _fox_sample_kernel(pt_ref, q_ref, *refs, n_pp):
    kc, vc, lfc = refs[0:n_pp], refs[n_pp:2 * n_pp], refs[2 * n_pp:3 * n_pp]
    (kn_ref, vn_ref, lfn_ref, tri_ref, trin_ref, o_ref,
     qbd_sc, m_sc, l_sc, acc_sc, carry_sc, kall_sc, vall_sc) = refs[3 * n_pp:]
    step = pl.program_id(1)
    ls = q_ref.shape[1]
    nr = H_FOX * ls
    page = tri_ref.shape[0]
    row_head = lax.broadcasted_iota(jnp.int32, (nr, FOX_W), 0) // ls
    lane_head = lax.broadcasted_iota(jnp.int32, (nr, FOX_W), 1) // DH_FOX

    def rows_of_heads(x):
        return jnp.broadcast_to(x[:, None, :], (H_FOX, ls, x.shape[-1])).reshape(nr, x.shape[-1])

    cn_row = rows_of_heads(_sel_dot_r(lfn_ref[0], trin_ref[...]))
    tnew = lax.broadcasted_iota(jnp.int32, (nr, ls), 1)
    tq = lax.broadcasted_iota(jnp.int32, (nr, ls), 0) % ls
    cn_col = jnp.sum(jnp.where(tnew == tq, cn_row, 0.0), axis=-1, keepdims=True)

    @pl.when(step == 0)
    def _():
        qt = jnp.concatenate([q_ref[0]] * H_FOX, axis=0)
        qbd_sc[...] = jnp.where(row_head == lane_head, qt, jnp.zeros_like(qt))
        m_sc[...] = jnp.full_like(m_sc, NEG_INF)
        l_sc[...] = jnp.zeros_like(l_sc)
        acc_sc[...] = jnp.zeros_like(acc_sc)
        carry_sc[...] = jnp.zeros_like(carry_sc)

    def online(scores, weighted_values):
        m_prev = m_sc[...]
        m_new = m_prev
        for s in scores:
            m_new = jnp.maximum(m_new, jnp.max(s, axis=-1, keepdims=True))
        alpha = jnp.exp(m_prev - m_new)
        l_new = alpha * l_sc[...]
        acc = alpha * acc_sc[...]
        for s, pv in zip(scores, weighted_values):
            pr = jnp.exp(s - m_new)
            l_new = l_new + jnp.sum(pr, axis=-1, keepdims=True)
            acc = acc + pv(pr.astype(BF16))
        m_sc[...] = m_new
        l_sc[...] = l_new
        acc_sc[...] = acc

    c_loc = _sel_dot_r(jnp.concatenate([r[0, 0] for r in lfc], axis=0), tri_ref[...])
    carry = carry_sc[...][:, 0:1]
    qbd = qbd_sc[...]
    c_pages = []
    for j in range(n_pp):
        c_page = c_loc[j * H_FOX:(j + 1) * H_FOX] + carry
        carry = c_page[:, page - 1:page]
        c_pages.append(rows_of_heads(c_page))
        kall_sc[:, j * page:(j + 1) * page] = kc[j][0, 0].astype(BF16)
        vall_sc[:, j * page:(j + 1) * page] = vc[j][0, 0].astype(BF16)
    carry_sc[...] = jnp.broadcast_to(carry, carry_sc.shape)
    s_all = _dot(qbd, kall_sc[...]) + cn_col - jnp.concatenate(c_pages, axis=1)
    online([s_all], [lambda p: _dot_nt(p, vall_sc[...])])

    @pl.when(step == pl.num_programs(1) - 1)
    def _():
        s_new = _dot_nt(qbd, kn_ref[0].astype(BF16))
        s_new = s_new + cn_col - cn_row - rows_of_heads(jnp.broadcast_to(carry, (H_FOX, LANES)))[:, 0:1]
        s_new = jnp.where(tnew <= tq, s_new, NEG_INF)
        online([s_new], [lambda p: _dot(p, vn_ref[0].astype(BF16))])
        o = acc_sc[...] / l_sc[...]
        o = jnp.where(row_head == lane_head, o, 0.0)
        out = o[0:ls]
        for h in range(1, H_FOX):
            out = out + o[h * ls:(h + 1) * ls]
        o_ref[0] = out.astype(o_ref.dtype)


def _fox_sample(fq_bf, fk, fv, lfn_t, kcache_t, vcache_t, lfcache_t, page_table, layer):
    bs, n_pages = page_table.shape
    n_pp = min(FOX_PAGES_PER_STEP, n_pages)
    assert n_pages % n_pp == 0
    page = kcache_t.shape[3]
    ls = fq_bf.shape[1]
    nr = H_FOX * ls
    tri = jnp.asarray(np.triu(np.ones((page, page), np.float32)), BF16)
    trin = jnp.asarray(np.triu(np.ones((ls, ls), np.float32)), BF16)
    page_spec = lambda j, rows: pl.BlockSpec(
        (1, 1, rows, page), lambda b, s, pt: (layer, pt[b * n_pages + s * n_pp + j], 0, 0))
    per_seq = lambda *shape: pl.BlockSpec((1,) + shape, lambda b, s, pt: (b,) + (0,) * len(shape))
    const = lambda a: pl.BlockSpec(a.shape, lambda b, s, pt: (0,) * a.ndim)
    grid_spec = pltpu.PrefetchScalarGridSpec(
        num_scalar_prefetch=1,
        grid=(bs, n_pages // n_pp),
        in_specs=([per_seq(ls, FOX_W)] + [page_spec(j, FOX_W) for j in range(n_pp)] * 2
                  + [page_spec(j, H_FOX) for j in range(n_pp)]
                  + [per_seq(ls, FOX_W), per_seq(ls, FOX_W), per_seq(H_FOX, ls), const(tri), const(trin)]),
        out_specs=per_seq(ls, FOX_W),
        scratch_shapes=[pltpu.VMEM((nr, FOX_W), BF16), pltpu.VMEM((nr, 1), F32), pltpu.VMEM((nr, 1), F32),
                        pltpu.VMEM((nr, FOX_W), F32), pltpu.VMEM((H_FOX, LANES), F32),
                        pltpu.VMEM((FOX_W, n_pp * page), BF16), pltpu.VMEM((FOX_W, n_pp * page), BF16)],
    )
    return pl.pallas_call(
        functools.partial(_fox_sample_kernel, n_pp=n_pp),
        grid_spec=grid_spec,
        out_shape=jax.ShapeDtypeStruct((bs, ls, FOX_W), BF16),
        compiler_params=_cparams("parallel", "arbitrary"),
        name="fox_sample",
    )(page_table.reshape(-1), fq_bf, *([kcache_t] * n_pp), *([vcache_t] * n_pp), *([lfcache_t] * n_pp),
      fk, fv, lfn_t, tri, trin)


def _gla_prompt_kernel(q_ref, k_ref, v_ref, la_ref, tril_ref, ones_ref, onesT_ref,
                       o_ref, sfin_ref, s_sc):
    g = pl.program_id(1)
    n_tok = q_ref.shape[0]
    n_chunks = n_tok // GLA_CHUNK

    @pl.when(g == 0)
    def _():
        s_sc[...] = jnp.zeros_like(s_sc)

    la = la_ref[...]
    b = _sel_dot_l(tril_ref[...], la)
    bl = _sel_dot_l(ones_ref[...], la)
    q_dec = q_ref[...] * jnp.exp(b)
    q_dec_bf = q_dec.astype(BF16)
    k = k_ref[...]
    kd_bf = (k * jnp.exp(-b)).astype(BF16)
    kl_t = (k * jnp.exp(bl - b)).T
    v_bf = v_ref[...].astype(BF16)
    bl_col = _sel_dot_r(la.T, onesT_ref[...])

    r = lax.broadcasted_iota(jnp.int32, (n_tok, n_tok), 0)
    c = lax.broadcasted_iota(jnp.int32, (n_tok, n_tok), 1)
    causal = (r // GLA_CHUNK == c // GLA_CHUNK) & (c <= r)
    lane_q = lax.broadcasted_iota(jnp.int32, (n_tok, GLA_KW), 1) // DK_GLA
    zero_q = jnp.zeros_like(q_dec_bf)
    for h in range(H_GLA):
        att = _dot_nt(jnp.where(lane_q == h, q_dec_bf, zero_q), kd_bf)
        att = jnp.where(causal, att, 0.0)
        o_ref[:, h * DV_GLA:(h + 1) * DV_GLA] = _dot(att.astype(BF16), v_bf[:, h * DV_GLA:(h + 1) * DV_GLA])

    row_head = lax.broadcasted_iota(jnp.int32, (GLA_KW, GLA_VW), 0) // DK_GLA
    lane_head = lax.broadcasted_iota(jnp.int32, (GLA_KW, GLA_VW), 1) // DV_GLA
    same_head = row_head == lane_head
    tok_chunk = lax.broadcasted_iota(jnp.int32, (GLA_KW, n_tok), 1) // GLA_CHUNK
    for n in range(n_chunks):
        rows = slice(n * GLA_CHUNK, (n + 1) * GLA_CHUNK)
        s_prev = s_sc[...]
        o_ref[rows, :] += _dot(q_dec_bf[rows], s_prev.astype(BF16))
        kv = _dot(jnp.where(tok_chunk == n, kl_t, 0.0).astype(BF16), v_bf)
        dec = jnp.exp(bl_col[:, n * LANES:(n + 1) * LANES])
        dec = jnp.concatenate([dec] * (GLA_VW // LANES), axis=1)
        s_sc[...] = jnp.where(same_head, dec * s_prev + kv, 0.0)

    @pl.when(g == pl.num_programs(1) - 1)
    def _():
        s = s_sc[...]
        for h in range(H_GLA):
            sfin_ref[0, h] = s[h * DK_GLA:(h + 1) * DK_GLA, h * DV_GLA:(h + 1) * DV_GLA]


def _gla_prompt(gq, gk, gv, la, batch, seq):
    grp = min(GLA_GROUP, seq)
    ng = seq // grp
    nch = grp // GLA_CHUNK
    idx = np.arange(grp)
    same = (idx[:, None] // GLA_CHUNK) == (idx[None, :] // GLA_CHUNK)
    tril = jnp.asarray((same & (idx[None, :] <= idx[:, None])).astype(np.float32), BF16)
    ones = jnp.asarray(same.astype(np.float32), BF16)
    ones_t = jnp.asarray((idx[:, None] // GLA_CHUNK == np.arange(nch * LANES)[None, :] // LANES).astype(np.float32),
                         BF16)
    row = lambda w: pl.BlockSpec((grp, w), lambda b, g: (b * ng + g, 0))
    return pl.pallas_call(
        _gla_prompt_kernel,
        grid=(batch, ng),
        in_specs=[row(GLA_KW), row(GLA_KW), row(GLA_VW), row(GLA_KW),
                  _full_spec(tril.shape), _full_spec(ones.shape), _full_spec(ones_t.shape)],
        out_specs=[row(GLA_VW), pl.BlockSpec((1, H_GLA, DK_GLA, DV_GLA), lambda b, g: (b, 0, 0, 0))],
        out_shape=[jax.ShapeDtypeStruct((batch * seq, GLA_VW), F32),
                   jax.ShapeDtypeStruct((batch, H_GLA, DK_GLA, DV_GLA), F32)],
        scratch_shapes=[pltpu.VMEM((GLA_KW, GLA_VW), F32)],
        compiler_params=_cparams("parallel", "arbitrary"),
        name="gla_prompt",
    )(gq, gk, gv, la, tril, ones, ones_t)


def _gla_sample_kernel(q_ref, k_ref, v_ref, la_ref, s0_ref, tril_ref, o_ref, s1_ref):
    n, ls, _ = q_ref.shape
    la = la_ref[...]
    tril = jnp.broadcast_to(tril_ref[...][None], (n, ls, ls))
    hi, mid, lo = _split3(la)
    bdot = lambda x: jnp.einsum('nts,nsd->ntd', tril, x, preferred_element_type=F32)
    b = (bdot(hi) + bdot(mid)) + bdot(lo)
    b_last = b[:, ls - 1:ls, :]
    q_dec = (q_ref[...] * jnp.exp(b)).astype(BF16)
    k = k_ref[...]
    kd = (k * jnp.exp(-b)).astype(BF16)
    kl = (k * jnp.exp(b_last - b)).astype(BF16)
    v = v_ref[...].astype(BF16)
    s0 = s0_ref[...]
    att = jnp.einsum('ncd,nsd->ncs', q_dec, kd, preferred_element_type=F32)
    r = lax.broadcasted_iota(jnp.int32, (n, ls, ls), 1)
    c = lax.broadcasted_iota(jnp.int32, (n, ls, ls), 2)
    att = jnp.where(c <= r, att, 0.0)
    o = jnp.einsum('ncs,nse->nce', att.astype(BF16), v, preferred_element_type=F32)
    o = o + jnp.einsum('ncd,nde->nce', q_dec, s0.astype(BF16), preferred_element_type=F32)
    o_ref[...] = o
    kv = jnp.einsum('nds,nse->nde', jnp.swapaxes(kl, 1, 2), v, preferred_element_type=F32)
    dec = jnp.swapaxes(jnp.exp(b_last), 1, 2)
    s1_ref[...] = dec * s0 + kv


def _gla_sample(gq, gk, gv, la, state):
    n, ls, _ = gq.shape
    nb = min(32, n)
    tril = jnp.asarray(np.tril(np.ones((ls, ls), np.float32)), BF16)
    blk = lambda a, c: pl.BlockSpec((nb, a, c), lambda i: (i, 0, 0))
    return pl.pallas_call(
        _gla_sample_kernel,
        grid=(n // nb,),
        in_specs=[blk(ls, DK_GLA), blk(ls, DK_GLA), blk(ls, DV_GLA), blk(ls, DK_GLA), blk(DK_GLA, DV_GLA),
                  _full_spec(tril.shape)],
        out_specs=[blk(ls, DV_GLA), blk(DK_GLA, DV_GLA)],
        out_shape=[jax.ShapeDtypeStruct((n, ls, DV_GLA), F32), jax.ShapeDtypeStruct((n, DK_GLA, DV_GLA), F32)],
        compiler_params=_cparams("parallel"),
        name="gla_sample",
    )(gq, gk, gv, la, state, tril)


def _mixproj_kernel(x_ref, fox_ref, gla_ref, gr_ref, gg_ref, wo_ref, gc_ref, wq_ref, x1_ref, q_ref):
    gla = gla_ref[...]
    gr = gr_ref[...]
    acc = _dot(fox_ref[...], wo_ref[0:FOX_W, :])
    for h in range(H_GLA):
        cols = slice(h * DV_GLA, (h + 1) * DV_GLA)
        gh = _rms(gla[:, cols], gg_ref[:, cols])
        grh = gr[:, cols]
        gh = gh * (grh * jax.nn.sigmoid(grh))
        acc = acc + _dot(gh.astype(BF16), wo_ref[FOX_W + h * DV_GLA:FOX_W + (h + 1) * DV_GLA, :])
    x1 = x_ref[...] + acc
    x1_ref[...] = x1
    h2 = _rms(x1, gc_ref[...]).astype(BF16)
    q_ref[...] = (_dot(h2, wq_ref[...]) * (DH_MEM ** -0.5)).astype(BF16)


def _mixproj(x2d, fox_o, gla_o, gr, g_gla, w_out_bf, g_cross, w_q_bf):
    t = x2d.shape[0]
    tt = min(TOK_TILE, t)
    row = lambda w: pl.BlockSpec((tt, w), lambda i: (i, 0))
    return pl.pallas_call(
        _mixproj_kernel,
        grid=(t // tt,),
        in_specs=[row(D_MODEL), row(FOX_W), row(GLA_VW), row(GLA_VW), _full_spec(g_gla.shape),
                  _full_spec(w_out_bf.shape), _full_spec(g_cross.shape), _full_spec(w_q_bf.shape)],
        out_specs=[row(D_MODEL), row(D_MODEL)],
        out_shape=[jax.ShapeDtypeStruct((t, D_MODEL), F32), jax.ShapeDtypeStruct((t, D_MODEL), BF16)],
        compiler_params=_cparams("parallel"),
        name="mixproj",
    )(x2d, fox_o, gla_o, gr, g_gla, w_out_bf, g_cross, w_q_bf)


def _memkv_kernel(m_ref, g_ref, wk_ref, wv_ref, k_ref, v_ref):
    m = _rms(m_ref[...], g_ref[...]).astype(BF16)
    k_ref[...] = _dot(m, wk_ref[...])
    v_ref[...] = _dot(m, wv_ref[...])


def _memkv(mem2d, g_mem, wk_bf, wv_bf):
    t = mem2d.shape[0]
    tt = min(TOK_TILE, t)
    row = pl.BlockSpec((tt, D_MODEL), lambda i: (i, 0))
    return pl.pallas_call(
        _memkv_kernel,
        grid=(t // tt,),
        in_specs=[row, _full_spec(g_mem.shape), _full_spec(wk_bf.shape), _full_spec(wv_bf.shape)],
        out_specs=[row, row],
        out_shape=[jax.ShapeDtypeStruct((t, D_MODEL), F32)] * 2,
        compiler_params=_cparams("parallel"),
        name="memkv",
    )(mem2d, g_mem, wk_bf, wv_bf)


def _cross_kernel(q_ref, mk_ref, mv_ref, o_ref, *, per_head_memory):
    q = q_ref[0]
    if per_head_memory:
        lq = q.shape[0]
        n_mem = mk_ref.shape[2]
        q_hm = jnp.concatenate([q[:, h * DH_MEM:(h + 1) * DH_MEM] for h in range(H_MEM)], axis=0)
        mk = mk_ref[0, 0].reshape(n_mem * H_MEM, DH_MEM).astype(BF16)
        mv = mv_ref[0, 0].reshape(n_mem * H_MEM, DH_MEM).astype(BF16)
        s = _dot_nt(q_hm, mk)
        row_head = lax.broadcasted_iota(jnp.int32, s.shape, 0) // lq
        key_head = lax.broadcasted_iota(jnp.int32, s.shape, 1) % H_MEM
        s = jnp.where(row_head == key_head, s, NEG_INF)
        e = jnp.exp(s - jnp.max(s, axis=-1, keepdims=True))
        p = e / jnp.sum(e, axis=-1, keepdims=True)
        o = _dot(p.astype(BF16), mv).astype(o_ref.dtype)
        for h in range(H_MEM):
            o_ref[0, :, h * DH_MEM:(h + 1) * DH_MEM] = o[h * lq:(h + 1) * lq]
        return
    for h in range(H_MEM):
        cols = slice(h * DH_MEM, (h + 1) * DH_MEM)
        mk = mk_ref[0, :, cols].astype(BF16)
        mv = mv_ref[0, :, cols].astype(BF16)
        s = _dot_nt(q[:, cols], mk)
        e = jnp.exp(s - jnp.max(s, axis=-1, keepdims=True))
        p = e / jnp.sum(e, axis=-1, keepdims=True)
        o_ref[0, :, cols] = _dot(p.astype(BF16), mv).astype(o_ref.dtype)


def _cross(q3, mk, mv, layer=None):
    b, lq, _ = q3.shape
    tl = min(TOK_TILE, lq)
    if layer is None:
        mem_spec = pl.BlockSpec((1,) + mk.shape[1:], lambda i, j: (i, 0, 0))
    else:
        mem_spec = pl.BlockSpec((1, 1) + mk.shape[2:], lambda i, j: (layer, i, 0, 0, 0))
    return pl.pallas_call(
        functools.partial(_cross_kernel, per_head_memory=layer is not None),
        grid=(b, lq // tl),
        in_specs=[pl.BlockSpec((1, tl, D_MODEL), lambda i, j: (i, j, 0)), mem_spec, mem_spec],
        out_specs=pl.BlockSpec((1, tl, D_MODEL), lambda i, j: (i, j, 0)),
        out_shape=jax.ShapeDtypeStruct((b, lq, D_MODEL), BF16),
        compiler_params=_cparams("parallel", "parallel"),
        name="cross_attn",
    )(q3, mk, mv)


def _peer_prep_kernel(u_ref, v_ref, ub_ref, vt_ref):
    ub_ref[...] = u_ref[...].astype(BF16)
    vt_ref[...] = v_ref[...].T.astype(BF16)


def _peer_prep(u, v):
    ne = u.shape[0]
    eb = min(PREP_EBLK, ne)
    return pl.pallas_call(
        _peer_prep_kernel,
        grid=(ne // eb,),
        in_specs=[pl.BlockSpec((eb, D_MODEL), lambda i: (i, 0))] * 2,
        out_specs=[pl.BlockSpec((eb, D_MODEL), lambda i: (i, 0)), pl.BlockSpec((D_MODEL, eb), lambda i: (0, i))],
        out_shape=[jax.ShapeDtypeStruct((ne, D_MODEL), BF16), jax.ShapeDtypeStruct((D_MODEL, ne), BF16)],
        compiler_params=_cparams("parallel"),
        name="peer_prep",
    )(u, v)


def _rows_per_word():
    return 4 // jnp.dtype(BF16).itemsize


def _pack_rows(x):
    return pltpu.bitcast(x, jnp.uint32)


def _unpack_rows(x):
    return pltpu.bitcast(x, BF16)


def _top16_rows(s, break_ties):
    nrow = s.shape[0]
    rowid = lax.broadcasted_iota(jnp.int32, s.shape, 0)
    rank = jnp.full(s.shape, float(PEER_TOPK), F32)
    work = s
    vals = []
    for r in range(PEER_TOPK):
        mx = jnp.max(work, axis=0, keepdims=True)
        if break_ties:
            sel = rowid == jnp.min(jnp.where(work == mx, rowid, nrow), axis=0, keepdims=True)
        else:
            sel = work == mx
        rank = jnp.where(sel, float(r), rank)
        work = jnp.where(sel, -jnp.inf, work)
        vals.append(mx)
    return rank, jnp.concatenate(vals, axis=0)


def _ranked_16(rank):
    n = jnp.sum(jnp.where(rank < float(PEER_TOPK), 1.0, 0.0), axis=0, keepdims=True)
    return jnp.where(n == float(PEER_TOPK), 1.0, 0.0)


def _peer_route_kernel(x_ref, x1_ref, wo_ref, g_ref, wq_ref, sk_ref, ca_ref, cb_ref, cg_ref,
                       x2_ref, hn_ref, c1_ref, nrow_ref, p2_ref, rk2_ref, q_sc, s_sc):
    tt = x_ref.shape[0]
    nch = tt // LANES
    x2 = x1_ref[...] + _dot(x_ref[...], wo_ref[...])
    x2_ref[...] = x2
    hn = _rms(x2, g_ref[...])
    hn_ref[...] = hn.T.astype(BF16)
    q_sc[...] = _dot(hn.astype(BF16), wq_ref[...]).astype(BF16)
    for hj in range(2 * H_PEER):
        st = _dot_nt(sk_ref[hj % 2], q_sc[:, hj * N_KEYS:(hj + 1) * N_KEYS])
        for ch in range(nch):
            s_sc[hj, ch] = st[:, ch * LANES:(ch + 1) * LANES]

    cand_row = lax.broadcasted_iota(jnp.int32, (N_CAND_PAD, LANES), 0)

    per_unit = 2 if nch % 2 == 0 else 1

    def unit(i, carry):
        h = i // (nch // per_unit)
        first = (i % (nch // per_unit)) * per_unit
        def route(break_ties):
            halves = [rank_halves(h, first + d, break_ties) for d in range(per_unit)]
            tie_free = select_pairs(h, first, break_ties, *halves[0])
            for d in range(1, per_unit):
                tie_free = tie_free * select_pairs(h, first + d, break_ties, *halves[d])
            return tie_free

        tie_free = route(False)

        @pl.when(jnp.min(tie_free) < 0.5)
        def _():
            route(True)

        return carry

    def rank_halves(h, ch, break_ties):
        s1 = s_sc[2 * h, ch]
        s2 = s_sc[2 * h + 1, ch]
        return (s1, s2) + _top16_rows(s1, break_ties) + _top16_rows(s2, break_ties)

    def select_pairs(h, ch, break_ties, s1, s2, rank1, v1, rank2, v2):
        cand = _sel_dot_l(ca_ref[...], v1) + _sel_dot_l(cb_ref[...], v2)
        cand = jnp.where(cand_row < N_CAND, cand, -jnp.inf)
        crank, _ = _top16_rows(cand, break_ties)
        chosen = crank < float(PEER_TOPK)
        n_of_a = _dot(cg_ref[...], jnp.where(chosen, 1.0, 0.0).astype(BF16))
        z = jnp.sum(jnp.where(chosen, jnp.exp(cand - cand[0:1]), 0.0), axis=0, keepdims=True)
        nrow = jnp.zeros_like(rank1)
        for a in range(PEER_TOPK):
            nrow = jnp.where(rank1 == float(a), n_of_a[a:a + 1], nrow)
        c1_ref[h, ch] = jnp.exp(s1 - v1[0:1]) / z
        nrow_ref[h, ch] = nrow
        p2_ref[h, ch] = _pack_rows(jnp.exp(s2 - v2[0:1]).astype(BF16))
        rk2_ref[h, ch] = _pack_rows(rank2.astype(BF16))
        return _ranked_16(rank1) * _ranked_16(rank2) * _ranked_16(crank)

    lax.fori_loop(0, H_PEER * nch // per_unit, unit, 0)


def _peer_route(oc2d, x1, w_o_bf, g_ffn, w_q_bf, sk_bf):
    t = x1.shape[0]
    tt = min(PEER_TOK, t)
    nch = tt // LANES
    pairs = [(a, b) for a in range(PEER_TOPK) for b in range(PEER_TOPK) if (a + 1) * (b + 1) <= PEER_TOPK]
    assert len(pairs) == N_CAND
    ca = np.zeros((N_CAND_PAD, PEER_TOPK), np.float32)
    cb = np.zeros((N_CAND_PAD, PEER_TOPK), np.float32)
    cg = np.zeros((PEER_TOPK, N_CAND_PAD), np.float32)
    for i, (a, b) in enumerate(pairs):
        ca[i, a] = 1.0
        cb[i, b] = 1.0
        cg[a, i] = 1.0
    ca, cb, cg = (jnp.asarray(m, BF16) for m in (ca, cb, cg))
    row = lambda w: pl.BlockSpec((tt, w), lambda i: (i, 0))
    krows = N_KEYS // _rows_per_word()
    sel = pl.BlockSpec((H_PEER, nch, N_KEYS, LANES), lambda i: (0, i, 0, 0))
    sel_shape = jax.ShapeDtypeStruct((H_PEER, t // LANES, N_KEYS, LANES), F32)
    packed = pl.BlockSpec((H_PEER, nch, krows, LANES), lambda i: (0, i, 0, 0))
    packed_shape = jax.ShapeDtypeStruct((H_PEER, t // LANES, krows, LANES), jnp.uint32)
    return pl.pallas_call(
        _peer_route_kernel,
        grid=(t // tt,),
        in_specs=[row(D_MODEL), row(D_MODEL), _full_spec(w_o_bf.shape), _full_spec(g_ffn.shape),
                  _full_spec(w_q_bf.shape), _full_spec(sk_bf.shape),
                  _full_spec(ca.shape), _full_spec(cb.shape), _full_spec(cg.shape)],
        out_specs=[row(D_MODEL), pl.BlockSpec((D_MODEL, tt), lambda i: (0, i)), sel, sel, packed, packed],
        out_shape=[jax.ShapeDtypeStruct((t, D_MODEL), F32), jax.ShapeDtypeStruct((D_MODEL, t), BF16),
                   sel_shape, sel_shape, packed_shape, packed_shape],
        scratch_shapes=[pltpu.VMEM((tt, 2 * H_PEER * N_KEYS), BF16),
                        pltpu.VMEM((2 * H_PEER, nch, N_KEYS, LANES), F32)],
        compiler_params=_cparams("parallel"),
        name="peer_route",
    )(oc2d, x1, w_o_bf, g_ffn, w_q_bf, sk_bf, ca, cb, cg)


def _peer_mix_kernel(x2_ref, hn_ref, u_ref, vt_ref, c1_ref, nrow_ref, p2_ref, rk2_ref, gf_ref,
                     y_ref, a_sc, w_sc, acc_sc, *, final_norm):
    e = pl.program_id(1)
    tt = hn_ref.shape[1]
    nch = tt // LANES
    n_i1 = u_ref.shape[0] // N_KEYS

    @pl.when(e == 0)
    def _():
        acc_sc[...] = jnp.zeros_like(acc_sc)

    krows = N_KEYS // _rows_per_word()
    acc = acc_sc[...]
    def activations(c):
        lo = c * PEER_CHUNK_I1
        ex = slice(lo * N_KEYS, (lo + PEER_CHUNK_I1) * N_KEYS)
        a_sc[lo * krows:(lo + PEER_CHUNK_I1) * krows, :] = _pack_rows(
            jax.nn.gelu(_dot(u_ref[ex, :], hn_ref[...])).astype(BF16))

    def weights(c):
        lo = c * PEER_CHUNK_I1
        for ch in range(nch):
            cols = slice(ch * LANES, (ch + 1) * LANES)
            for r in range(lo, lo + PEER_CHUNK_I1):
                w = jnp.zeros((N_KEYS, LANES), BF16)
                for h in range(H_PEER):
                    wide = lambda ref: jnp.broadcast_to(ref[h, ch, r:r + 1, :], (N_KEYS, LANES)).astype(BF16)
                    p2 = _unpack_rows(p2_ref[h, ch])
                    hit = _unpack_rows(rk2_ref[h, ch]) < wide(nrow_ref)
                    w = w + jnp.where(hit, p2, jnp.zeros_like(p2)) * wide(c1_ref)
                w_sc[r * N_KEYS:(r + 1) * N_KEYS, cols] = w * _unpack_rows(a_sc[r * krows:(r + 1) * krows, cols])

    n_chunks = n_i1 // PEER_CHUNK_I1
    per_sub = PEER_SUB_I1 // PEER_CHUNK_I1
    for c in range(min(2, n_chunks)):
        activations(c)
    for c in range(n_chunks):
        weights(c)
        if c + 2 < n_chunks:
            activations(c + 2)
        if (c + 1) % per_sub == 0:
            ex = slice((c + 1 - per_sub) * PEER_CHUNK_I1 * N_KEYS, (c + 1) * PEER_CHUNK_I1 * N_KEYS)
            acc = acc + _dot(vt_ref[:, ex], w_sc[ex, :])
    acc_sc[...] = acc

    @pl.when(e == pl.num_programs(1) - 1)
    def _():
        y = x2_ref[...] + acc_sc[...].T
        y_ref[...] = _rms(y, gf_ref[...]) if final_norm else y


def _peer_mix(x2, hn, u_bf, vt_bf, c1, nrow, p2, rk2, g_final, final_norm):
    t = x2.shape[0]
    tt = min(PEER_TOK, t)
    nch = tt // LANES
    ne = u_bf.shape[0]
    eb = min(PEER_EBLK, ne)
    n_i1 = eb // N_KEYS
    row = lambda w: pl.BlockSpec((tt, w), lambda i, e: (i, 0))
    part = pl.BlockSpec((H_PEER, nch, n_i1, LANES), lambda i, e: (0, i, e, 0))
    full = pl.BlockSpec((H_PEER, nch, N_KEYS // _rows_per_word(), LANES), lambda i, e: (0, i, 0, 0))
    return pl.pallas_call(
        functools.partial(_peer_mix_kernel, final_norm=final_norm),
        grid=(t // tt, ne // eb),
        in_specs=[row(D_MODEL), pl.BlockSpec((D_MODEL, tt), lambda i, e: (0, i)),
                  pl.BlockSpec((eb, D_MODEL), lambda i, e: (e, 0)),
                  pl.BlockSpec((D_MODEL, eb), lambda i, e: (0, e)),
                  part, part, full, full, pl.BlockSpec(g_final.shape, lambda i, e: (0, 0))],
        out_specs=row(D_MODEL),
        out_shape=jax.ShapeDtypeStruct((t, D_MODEL), F32),
        scratch_shapes=[pltpu.VMEM((eb // _rows_per_word(), tt), jnp.uint32), pltpu.VMEM((eb, tt), BF16),
                        pltpu.VMEM((D_MODEL, tt), F32)],
        compiler_params=_cparams("parallel", "arbitrary"),
        name="peer_mix",
    )(x2, hn, u_bf, vt_bf, c1, nrow, p2, rk2, g_final)


def _prep_in_weights(w_in, b_fox_f, w_gate2, b_gate):
    o = np.cumsum((0,) + IN_SIZES)
    seg = lambda i: w_in[:, int(o[i]):int(o[i + 1])]
    pad = jnp.zeros((D_MODEL, LANES - H_FOX - GLA_GATE_RANK), w_in.dtype)
    w_perm = jnp.concatenate([seg(0), seg(1), seg(2), seg(4), seg(5), seg(6), seg(8), seg(3), seg(7), pad],
                             axis=1).astype(BF16)
    b_small = jnp.zeros((1, LANES), F32).at[0, 0:H_FOX].set(b_fox_f)
    wg_pad = jnp.zeros((LANES, GLA_KW), F32).at[H_FOX:H_FOX + GLA_GATE_RANK].set(w_gate2).astype(BF16)
    return w_perm, b_small, wg_pad, b_gate.reshape(1, GLA_KW)


def kernel(x_prompt, x_sample, cache_fox_k, cache_fox_v, cache_fox_logf, state_gla, cache_mem_k, cache_mem_v, page_table, mem_prompt, g_mix, w_in, b_fox_f, w_gla_gate2, b_gla_gate, g_gla_out, w_out, g_cross, g_mem, w_mem_k, w_mem_v, w_cross_q, w_cross_o, g_ffn, peer_w_q, peer_subkeys, peer_u, peer_v, g_final):
    depth = w_in.shape[0]
    bp, lp, _ = x_prompt.shape
    bs, ls, _ = x_sample.shape
    n_mem = mem_prompt.shape[1]
    n_pool, page = cache_fox_k.shape[1:3]
    xp = x_prompt.reshape(bp * lp, D_MODEL)
    xs = x_sample.reshape(bs * ls, D_MODEL)
    vec = lambda a: a.reshape(1, -1)
    kcache_t = cache_fox_k.transpose(0, 1, 3, 4, 2).reshape(depth, n_pool, FOX_W, page)
    vcache_t = cache_fox_v.transpose(0, 1, 3, 4, 2).reshape(depth, n_pool, FOX_W, page)
    lfcache_t = cache_fox_logf.transpose(0, 1, 3, 2)
    outs = [[] for _ in range(10)]
    for l in range(depth):
        w_perm, b_small, wg_pad, b_gate = _prep_in_weights(w_in[l], b_fox_f[l], w_gla_gate2[l], b_gla_gate[l])
        w_out_bf = w_out[l].astype(BF16)
        w_cq_bf = w_cross_q[l].astype(BF16)
        w_co_bf = w_cross_o[l].astype(BF16)
        w_pq_bf = peer_w_q[l].astype(BF16)
        sk_bf = peer_subkeys[l].astype(BF16)
        u_bf, vt_bf = _peer_prep(peer_u[l], peer_v[l])
        last = l == depth - 1

        def tail(x, fox_o, gla_o, gr, mk, mv, nb, mem_layer):
            x1, qc = _mixproj(x, fox_o, gla_o, gr, vec(g_gla_out[l]), w_out_bf, vec(g_cross[l]), w_cq_bf)
            oc = _cross(qc.reshape(nb, -1, D_MODEL), mk, mv, mem_layer).reshape(-1, D_MODEL)
            x2, hn, c1, nrow, p2, rk2 = _peer_route(oc, x1, w_co_bf, vec(g_ffn[l]), w_pq_bf, sk_bf)
            return _peer_mix(x2, hn, u_bf, vt_bf, c1, nrow, p2, rk2, vec(g_final), last)

        fq_bf, fk, fv, lf, gq, gk, gv, la, gr = _inproj(xp, vec(g_mix[l]), w_perm, b_small, wg_pad, b_gate,
                                                        DH_FOX ** -0.5 * _LOG2E)
        qc, kc, vt, fk_t, fv_t, lf_t = _fox_prep(fq_bf, fk, fv, lf, bp, lp)
        fox_o = _fox_prompt(qc, kc, vt, bp, lp)
        gla_o, s_fin = _gla_prompt(gq, gk, gv, la, bp, lp)
        mk, mv = _memkv(mem_prompt.reshape(bp * n_mem, D_MODEL), vec(g_mem[l]), w_mem_k[l].astype(BF16),
                        w_mem_v[l].astype(BF16))
        mk3 = mk.reshape(bp, n_mem, D_MODEL)
        mv3 = mv.reshape(bp, n_mem, D_MODEL)
        xp = tail(xp, fox_o, gla_o, gr, mk3, mv3, bp, None)
        outs[0].append(fk_t.reshape(bp, H_FOX, DH_FOX, lp).transpose(0, 3, 1, 2))
        outs[1].append(fv_t.reshape(bp, H_FOX, DH_FOX, lp).transpose(0, 3, 1, 2))
        outs[2].append(lf_t.transpose(0, 2, 1))
        outs[3].append(s_fin)
        outs[4].append(mk.reshape(bp, n_mem, H_MEM, DH_MEM))
        outs[5].append(mv.reshape(bp, n_mem, H_MEM, DH_MEM))

        fq_bf, fk, fv, lf, gq, gk, gv, la, gr = _inproj(xs, vec(g_mix[l]), w_perm, b_small, wg_pad, b_gate,
                                                        DH_FOX ** -0.5)
        seqs = lambda a: a.reshape(bs, ls, FOX_W)
        fox_o = _fox_sample(seqs(fq_bf), seqs(fk), seqs(fv), lf.reshape(bs, ls, H_FOX).transpose(0, 2, 1),
                            kcache_t, vcache_t, lfcache_t, page_table, l).reshape(bs * ls, FOX_W)
        heads = lambda a, d: a.reshape(bs, ls, H_GLA, d).transpose(0, 2, 1, 3).reshape(bs * H_GLA, ls, d)
        gla_o, s_new = _gla_sample(heads(gq, DK_GLA), heads(gk, DK_GLA), heads(gv, DV_GLA), heads(la, DK_GLA),
                                   state_gla[l].reshape(bs * H_GLA, DK_GLA, DV_GLA))
        gla_o = gla_o.reshape(bs, H_GLA, ls, DV_GLA).transpose(0, 2, 1, 3).reshape(bs * ls, GLA_VW)
        xs = tail(xs, fox_o, gla_o, gr, cache_mem_k, cache_mem_v, bs, l)
        outs[6].append(fk.reshape(bs, ls, H_FOX, DH_FOX))
        outs[7].append(fv.reshape(bs, ls, H_FOX, DH_FOX))
        outs[8].append(lf.reshape(bs, ls, H_FOX))
        outs[9].append(s_new.reshape(bs, H_GLA, DK_GLA, DV_GLA))
    stacked = [jnp.stack(o, axis=0) for o in outs]
    return (xp.reshape(bp, lp, D_MODEL), xs.reshape(bs, ls, D_MODEL), *stacked)
```

```python
import functools

import numpy as np
import jax
import jax.numpy as jnp
from jax import lax
from jax.experimental import pallas as pl
from jax.experimental.pallas import tpu as pltpu

F32 = jnp.float32
BF16 = jnp.bfloat16

D_MODEL = 1024
H_FOX, DH_FOX = 8, 64
H_GLA, DK_GLA, DV_GLA = 4, 64, 128
GLA_GATE_RANK = 16
GLA_TAU = 16.0
GLA_CHUNK = 64
H_MEM = 4
DH_MEM = D_MODEL // H_MEM
N_KEYS = 128
H_PEER = 8
PEER_TOPK = 16
EPS = 1e-6
NEG_INF = -1e30
_LOG2E = float(np.log2(np.e))
FOX_W = H_FOX * DH_FOX
GLA_KW = H_GLA * DK_GLA
GLA_VW = H_GLA * DV_GLA
IN_SIZES = (FOX_W, FOX_W, FOX_W, H_FOX, GLA_KW, GLA_KW, GLA_VW, GLA_GATE_RANK, GLA_VW)

LANES = 128
VMEM_LIMIT_BYTES = 56 * 2**20

TOK_TILE = 512
INPROJ_TILE = 512
FOX_TILE = 512
FOX_KTILE = 1024
FOX_HP_PER_STEP = 4
GLA_GROUP = 512
PEER_TOK = 512
PEER_MIX_TOK = 1024
PEER_EBLK = 1024
PEER_CHUNK_I1 = 4
PEER_SUB_I1 = 8
FOX_PAGES_PER_STEP = 16
PREP_EBLK = 512
N_CAND = 50
N_CAND_PAD = 64


def _cparams(*sem):
    return pltpu.CompilerParams(dimension_semantics=sem, vmem_limit_bytes=VMEM_LIMIT_BYTES)


def _dot(a, b):
    return jnp.dot(a, b, preferred_element_type=F32)


def _dot_nt(a, b):
    return lax.dot_general(a, b, (((1,), (1,)), ((), ())), preferred_element_type=F32)


def _split3(x):
    hi = x.astype(BF16)
    r = x - hi.astype(F32)
    mid = r.astype(BF16)
    lo = (r - mid.astype(F32)).astype(BF16)
    return hi, mid, lo


def _sel_dot_l(m01, x):
    hi, mid, lo = _split3(x)
    return (_dot(m01, hi) + _dot(m01, mid)) + _dot(m01, lo)


def _sel_dot_r(x, m01):
    hi, mid, lo = _split3(x)
    return (_dot(hi, m01) + _dot(mid, m01)) + _dot(lo, m01)


def _rms(x, g):
    return x * lax.rsqrt(jnp.mean(x * x, axis=-1, keepdims=True) + EPS) * g


def _logsig(x):
    return jnp.minimum(x, 0.0) - jnp.log1p(jnp.exp(-jnp.abs(x)))


def _full_spec(shape):
    nd = len(shape)
    return pl.BlockSpec(shape, lambda *_: (0,) * nd)


_C_FQ, _C_FK, _C_FV = 0, FOX_W, 2 * FOX_W
_C_GQ = 3 * FOX_W
_C_GK = _C_GQ + GLA_KW
_C_GV = _C_GK + GLA_KW
_C_GR = _C_GV + GLA_VW
_C_SM = _C_GR + GLA_VW
_C_END = _C_SM + LANES


def _inproj_kernel(x_ref, g_ref, w_ref, bsm_ref, wg_ref, bg_ref,
                   fq_ref, fk_ref, fv_ref, lf_ref, gq_ref, gk_ref, gv_ref, la_ref, gr_ref, *, q_scale):
    h = _rms(x_ref[...], g_ref[...]).astype(BF16)

    def seg(lo, hi):
        return _dot(h, w_ref[:, lo:hi])

    fq_ref[...] = (seg(_C_FQ, _C_FK) * q_scale).astype(BF16)
    fk_ref[...] = seg(_C_FK, _C_FV)
    fv_ref[...] = seg(_C_FV, _C_GQ)
    gq_ref[...] = seg(_C_GQ, _C_GK) * (DK_GLA ** -0.5)
    gk_ref[...] = seg(_C_GK, _C_GV)
    gv_ref[...] = seg(_C_GV, _C_GR)
    gr_ref[...] = seg(_C_GR, _C_SM)
    ysm = seg(_C_SM, _C_END)
    lf_ref[...] = _logsig(ysm + bsm_ref[...])[:, 0:H_FOX]
    z = _dot(ysm.astype(BF16), wg_ref[...]) + bg_ref[...]
    la_ref[...] = _logsig(z) / GLA_TAU


def _inproj(x2d, g_mix, w_perm, b_small, wg_pad, b_gate, q_scale):
    t = x2d.shape[0]
    tt = min(INPROJ_TILE, t)
    row = lambda w: pl.BlockSpec((tt, w), lambda i: (i, 0))
    outs = [(FOX_W, BF16), (FOX_W, F32), (FOX_W, F32), (H_FOX, F32),
            (GLA_KW, F32), (GLA_KW, F32), (GLA_VW, F32), (GLA_KW, F32), (GLA_VW, F32)]
    return pl.pallas_call(
        functools.partial(_inproj_kernel, q_scale=q_scale),
        grid=(t // tt,),
        in_specs=[row(D_MODEL), _full_spec(g_mix.shape), _full_spec(w_perm.shape), _full_spec(b_small.shape),
                  _full_spec(wg_pad.shape), _full_spec(b_gate.shape)],
        out_specs=[row(w) for w, _ in outs],
        out_shape=[jax.ShapeDtypeStruct((t, w), dt) for w, dt in outs],
        compiler_params=_cparams("parallel"),
        name="inproj",
    )(x2d, g_mix, w_perm, b_small, wg_pad, b_gate)


_FOX_QW = 4 * LANES
_FOX_KW = 2 * LANES
_N_HP = FOX_W // LANES


def _fox_aug_tables():
    pq = np.zeros((3, H_FOX, _N_HP * 2 * LANES), np.float32)
    oq = np.zeros((1, _N_HP * 2 * LANES), np.float32)
    pk = np.zeros((3, H_FOX, _N_HP * LANES), np.float32)
    ok = np.zeros((1, _N_HP * LANES), np.float32)
    for hp in range(_N_HP):
        for half in range(2):
            head = 2 * hp + half
            qb = (hp * 2 + half) * LANES
            kb = hp * LANES
            for j in range(3):
                pq[j, head, qb + 6 * half + j] = 1.0
                oq[0, qb + 6 * half + 3 + j] = 1.0
                pk[j, head, kb + 6 * half + 3 + j] = 1.0
                ok[0, kb + 6 * half + j] = 1.0
    return jnp.asarray(pq, BF16), jnp.asarray(oq, F32), jnp.asarray(pk, BF16), jnp.asarray(ok, F32)


def _fox_prep_kernel(q_ref, k_ref, v_ref, lf_ref, tril_ref, pq_ref, oq_ref, pk_ref, ok_ref, eye_ref,
                     qc_ref, kc_ref, vt_ref, fkt_ref, fvt_ref, lft_ref, carry_sc):
    @pl.when(pl.program_id(1) == 0)
    def _():
        carry_sc[...] = jnp.zeros_like(carry_sc)

    lf_parts = _split3(lf_ref[...])
    eye = eye_ref[...]
    lft_ref[0] = (_dot_nt(eye, lf_parts[0]) + _dot_nt(eye, lf_parts[1])) + _dot_nt(eye, lf_parts[2])
    c = _sel_dot_l(tril_ref[...], lf_ref[...]) + carry_sc[0:1, 0:H_FOX]
    carry_sc[0:1, 0:H_FOX] = c[-1:, :]
    parts = _split3(c * _LOG2E)
    aug_q = oq_ref[...]
    aug_k = ok_ref[...]
    for j in range(3):
        aug_q = aug_q + _dot(parts[j], pq_ref[j])
        aug_k = aug_k - _dot(parts[j], pk_ref[j])
    aug_q = aug_q.astype(BF16)
    aug_k = aug_k.astype(BF16)
    lane = lax.broadcasted_iota(jnp.int32, (q_ref.shape[0], LANES), 1)
    for hp in range(_N_HP):
        q = q_ref[:, hp * LANES:(hp + 1) * LANES]
        zero = jnp.zeros_like(q)
        base = hp * _FOX_QW
        qc_ref[:, base:base + LANES] = jnp.where(lane < DH_FOX, q, zero)
        qc_ref[:, base + LANES:base + 2 * LANES] = aug_q[:, (2 * hp) * LANES:(2 * hp + 1) * LANES]
        qc_ref[:, base + 2 * LANES:base + 3 * LANES] = jnp.where(lane >= DH_FOX, q, zero)
        qc_ref[:, base + 3 * LANES:base + 4 * LANES] = aug_q[:, (2 * hp + 1) * LANES:(2 * hp + 2) * LANES]
        kc_ref[:, hp * _FOX_KW:hp * _FOX_KW + LANES] = k_ref[:, hp * LANES:(hp + 1) * LANES].astype(BF16)
        kc_ref[:, hp * _FOX_KW + LANES:(hp + 1) * _FOX_KW] = aug_k[:, hp * LANES:(hp + 1) * LANES]
    v_t = v_ref[...].T
    vt_ref[0] = v_t.astype(BF16)
    fvt_ref[0] = v_t
    fkt_ref[0] = k_ref[...].T


def _fox_prep(fq_bf, fk, fv, lf, batch, seq):
    tt = min(FOX_TILE, seq)
    nt = seq // tt
    tril = jnp.asarray(np.tril(np.ones((tt, tt), np.float32)), BF16)
    consts = [tril, *_fox_aug_tables(), jnp.eye(H_FOX, dtype=BF16)]
    row = lambda w: pl.BlockSpec((tt, w), lambda b, i: (b * nt + i, 0))
    col = lambda h: pl.BlockSpec((1, h, tt), lambda b, i: (b, 0, i))
    return pl.pallas_call(
        _fox_prep_kernel,
        grid=(batch, nt),
        in_specs=[row(FOX_W), row(FOX_W), row(FOX_W), row(H_FOX)] + [_full_spec(a.shape) for a in consts],
        out_specs=[row(_N_HP * _FOX_QW), row(_N_HP * _FOX_KW), col(FOX_W), col(FOX_W), col(FOX_W), col(H_FOX)],
        out_shape=[jax.ShapeDtypeStruct((batch * seq, _N_HP * _FOX_QW), BF16),
                   jax.ShapeDtypeStruct((batch * seq, _N_HP * _FOX_KW), BF16),
                   jax.ShapeDtypeStruct((batch, FOX_W, seq), BF16),
                   jax.ShapeDtypeStruct((batch, FOX_W, seq), F32),
                   jax.ShapeDtypeStruct((batch, FOX_W, seq), F32),
                   jax.ShapeDtypeStruct((batch, H_FOX, seq), F32)],
        scratch_shapes=[pltpu.VMEM((8, LANES), F32)],
        compiler_params=_cparams("parallel", "arbitrary"),
        name="fox_prep",
    )(fq_bf, fk, fv, lf, *consts)


def _fox_prompt_kernel(qi_tab, ki_tab, q_ref, k_ref, vt_ref, o_ref,
                       m_sc, l_sc, acc_sc):
    t = pl.program_id(2)
    qi = qi_tab[t]
    ki = ki_tab[t]
    tq = q_ref.shape[0]
    tk = k_ref.shape[0]

    @pl.when(ki == 0)
    def _():
        m_sc[...] = jnp.full_like(m_sc, NEG_INF)
        l_sc[...] = jnp.zeros_like(l_sc)
        acc_sc[...] = jnp.zeros_like(acc_sc)

    def step(masked):
        n_hh = 2 * FOX_HP_PER_STEP

        def scores_of(hh):
            kc = k_ref[:, (hh // 2) * _FOX_KW:(hh // 2 + 1) * _FOX_KW]
            return _dot_nt(kc, q_ref[:, hh * _FOX_KW:(hh + 1) * _FOX_KW])

        def softmax_update(hh, s):
            if masked:
                kpos = lax.broadcasted_iota(jnp.int32, (tk, tq), 0) + ki * tk
                qpos = lax.broadcasted_iota(jnp.int32, (tk, tq), 1) + qi * tq
                s = jnp.where(kpos <= qpos, s, NEG_INF)
            m_prev = m_sc[hh:hh + 1, :]
            m_new = jnp.maximum(m_prev, jnp.max(s, axis=0, keepdims=True))
            alpha = jnp.exp2(m_prev - m_new)
            p = jnp.exp2(s - m_new)
            l_sc[hh:hh + 1, :] = alpha * l_sc[hh:hh + 1, :] + jnp.sum(p, axis=0, keepdims=True)
            m_sc[hh:hh + 1, :] = m_new
            return alpha, p.astype(BF16)

        def accumulate(hh, alpha, p):
            vt = vt_ref[0, (hh // 2) * LANES:(hh // 2 + 1) * LANES, :]
            acc_sc[hh] = alpha * acc_sc[hh] + _dot(vt, p)

        s_next = {0: scores_of(0)}
        if n_hh > 1:
            s_next[1] = scores_of(1)
        pending = None
        for hh in range(n_hh):
            alpha_p = softmax_update(hh, s_next.pop(hh))
            if hh + 2 < n_hh:
                s_next[hh + 2] = scores_of(hh + 2)
            if pending is not None:
                accumulate(*pending)
            pending = (hh, *alpha_p)
        accumulate(*pending)

    last = (qi * tq) // tk

    @pl.when(ki < last)
    def _():
        step(False)

    @pl.when(ki == last)
    def _():
        step(True)
        dim = lax.broadcasted_iota(jnp.int32, (LANES, tq), 0)
        for hp in range(FOX_HP_PER_STEP):
            o0 = acc_sc[2 * hp] / l_sc[2 * hp:2 * hp + 1, :]
            o1 = acc_sc[2 * hp + 1] / l_sc[2 * hp + 1:2 * hp + 2, :]
            o_ref[:, hp * LANES:(hp + 1) * LANES] = jnp.where(dim < DH_FOX, o0, o1).T.astype(o_ref.dtype)


def _fox_prompt(qc, kc, vt, batch, seq):
    tq = min(FOX_TILE, seq)
    tk = min(FOX_KTILE, seq)
    nq = seq // tq
    nk = seq // tk
    pairs = [(i, j) for i in range(nq) for j in range((i * tq) // tk + 1)]
    qi_tab = jnp.asarray([p[0] for p in pairs], jnp.int32)
    ki_tab = jnp.asarray([p[1] for p in pairs], jnp.int32)
    g = FOX_HP_PER_STEP
    grid_spec = pltpu.PrefetchScalarGridSpec(
        num_scalar_prefetch=2,
        grid=(batch, _N_HP // g, len(pairs)),
        in_specs=[
            pl.BlockSpec((tq, g * _FOX_QW), lambda b, h, t, qt, kt: (b * nq + qt[t], h)),
            pl.BlockSpec((tk, g * _FOX_KW), lambda b, h, t, qt, kt: (b * nk + kt[t], h)),
            pl.BlockSpec((1, g * LANES, tk), lambda b, h, t, qt, kt: (b, h, kt[t])),
        ],
        out_specs=pl.BlockSpec((tq, g * LANES), lambda b, h, t, qt, kt: (b * nq + qt[t], h)),
        scratch_shapes=[pltpu.VMEM((2 * g, tq), F32), pltpu.VMEM((2 * g, tq), F32),
                        pltpu.VMEM((2 * g, LANES, tq), F32)],
    )
    return pl.pallas_call(
        _fox_prompt_kernel,
        grid_spec=grid_spec,
        out_shape=jax.ShapeDtypeStruct((batch * seq, FOX_W), BF16),
        compiler_params=_cparams("parallel", "parallel", "arbitrary"),
        name="fox_prompt",
    )(qi_tab, ki_tab, qc, kc, vt)


def _fox_sample_kernel(pt_ref, q_ref, *refs, n_pp):
    kc, vc, lfc = refs[0:n_pp], refs[n_pp:2 * n_pp], refs[2 * n_pp:3 * n_pp]
    (kn_ref, vn_ref, lfn_ref, tri_ref, trin_ref, o_ref,
     qbd_sc, m_sc, l_sc, acc_sc, carry_sc, kall_sc, vall_sc) = refs[3 * n_pp:]
    step = pl.program_id(1)
    ls = q_ref.shape[1]
    nr = H_FOX * ls
    page = tri_ref.shape[0]
    row_head = lax.broadcasted_iota(jnp.int32, (nr, FOX_W), 0) // ls
    lane_head = lax.broadcasted_iota(jnp.int32, (nr, FOX_W), 1) // DH_FOX

    def rows_of_heads(x):
        return jnp.broadcast_to(x[:, None, :], (H_FOX, ls, x.shape[-1])).reshape(nr, x.shape[-1])

    cn_row = rows_of_heads(_sel_dot_r(lfn_ref[0], trin_ref[...]))
    tnew = lax.broadcasted_iota(jnp.int32, (nr, ls), 1)
    tq = lax.broadcasted_iota(jnp.int32, (nr, ls), 0) % ls
    cn_col = jnp.sum(jnp.where(tnew == tq, cn_row, 0.0), axis=-1, keepdims=True)

    @pl.when(step == 0)
    def _():
        qt = jnp.concatenate([q_ref[0]] * H_FOX, axis=0)
        qbd_sc[...] = jnp.where(row_head == lane_head, qt, jnp.zeros_like(qt))
        m_sc[...] = jnp.full_like(m_sc, NEG_INF)
        l_sc[...] = jnp.zeros_like(l_sc)
        acc_sc[...] = jnp.zeros_like(acc_sc)
        carry_sc[...] = jnp.zeros_like(carry_sc)

    def online(scores, weighted_values):
        m_prev = m_sc[...]
        m_new = m_prev
        for s in scores:
            m_new = jnp.maximum(m_new, jnp.max(s, axis=-1, keepdims=True))
        alpha = jnp.exp(m_prev - m_new)
        l_new = alpha * l_sc[...]
        acc = alpha * acc_sc[...]
        for s, pv in zip(scores, weighted_values):
            pr = jnp.exp(s - m_new)
            l_new = l_new + jnp.sum(pr, axis=-1, keepdims=True)
            acc = acc + pv(pr.astype(BF16))
        m_sc[...] = m_new
        l_sc[...] = l_new
        acc_sc[...] = acc

    c_loc = _sel_dot_r(jnp.concatenate([r[0, 0] for r in lfc], axis=0), tri_ref[...])
    carry = carry_sc[...][:, 0:1]
    qbd = qbd_sc[...]
    c_pages = []
    for j in range(n_pp):
        c_page = c_loc[j * H_FOX:(j + 1) * H_FOX] + carry
        carry = c_page[:, page - 1:page]
        c_pages.append(rows_of_heads(c_page))
        kall_sc[:, j * page:(j + 1) * page] = kc[j][0, 0].astype(BF16)
        vall_sc[:, j * page:(j + 1) * page] = vc[j][0, 0].astype(BF16)
    carry_sc[...] = jnp.broadcast_to(carry, carry_sc.shape)
    s_all = _dot(qbd, kall_sc[...]) + cn_col - jnp.concatenate(c_pages, axis=1)
    online([s_all], [lambda p: _dot_nt(p, vall_sc[...])])

    @pl.when(step == pl.num_programs(1) - 1)
    def _():
        s_new = _dot_nt(qbd, kn_ref[0].astype(BF16))
        s_new = s_new + cn_col - cn_row - rows_of_heads(jnp.broadcast_to(carry, (H_FOX, LANES)))[:, 0:1]
        s_new = jnp.where(tnew <= tq, s_new, NEG_INF)
        online([s_new], [lambda p: _dot(p, vn_ref[0].astype(BF16))])
        o = acc_sc[...] / l_sc[...]
        o = jnp.where(row_head == lane_head, o, 0.0)
        out = o[0:ls]
        for h in range(1, H_FOX):
            out = out + o[h * ls:(h + 1) * ls]
        o_ref[0] = out.astype(o_ref.dtype)


def _fox_sample(fq_bf, fk, fv, lfn_t, kcache_t, vcache_t, lfcache_t, page_table, layer):
    bs, n_pages = page_table.shape
    n_pp = min(FOX_PAGES_PER_STEP, n_pages)
    assert n_pages % n_pp == 0
    page = kcache_t.shape[3]
    ls = fq_bf.shape[1]
    nr = H_FOX * ls
    tri = jnp.asarray(np.triu(np.ones((page, page), np.float32)), BF16)
    trin = jnp.asarray(np.triu(np.ones((ls, ls), np.float32)), BF16)
    page_spec = lambda j, rows: pl.BlockSpec(
        (1, 1, rows, page), lambda b, s, pt: (layer, pt[b * n_pages + s * n_pp + j], 0, 0))
    per_seq = lambda *shape: pl.BlockSpec((1,) + shape, lambda b, s, pt: (b,) + (0,) * len(shape))
    const = lambda a: pl.BlockSpec(a.shape, lambda b, s, pt: (0,) * a.ndim)
    grid_spec = pltpu.PrefetchScalarGridSpec(
        num_scalar_prefetch=1,
        grid=(bs, n_pages // n_pp),
        in_specs=([per_seq(ls, FOX_W)] + [page_spec(j, FOX_W) for j in range(n_pp)] * 2
                  + [page_spec(j, H_FOX) for j in range(n_pp)]
                  + [per_seq(ls, FOX_W), per_seq(ls, FOX_W), per_seq(H_FOX, ls), const(tri), const(trin)]),
        out_specs=per_seq(ls, FOX_W),
        scratch_shapes=[pltpu.VMEM((nr, FOX_W), BF16), pltpu.VMEM((nr, 1), F32), pltpu.VMEM((nr, 1), F32),
                        pltpu.VMEM((nr, FOX_W), F32), pltpu.VMEM((H_FOX, LANES), F32),
                        pltpu.VMEM((FOX_W, n_pp * page), BF16), pltpu.VMEM((FOX_W, n_pp * page), BF16)],
    )
    return pl.pallas_call(
        functools.partial(_fox_sample_kernel, n_pp=n_pp),
        grid_spec=grid_spec,
        out_shape=jax.ShapeDtypeStruct((bs, ls, FOX_W), BF16),
        compiler_params=_cparams("parallel", "arbitrary"),
        name="fox_sample",
    )(page_table.reshape(-1), fq_bf, *([kcache_t] * n_pp), *([vcache_t] * n_pp), *([lfcache_t] * n_pp),
      fk, fv, lfn_t, tri, trin)


def _gla_prompt_kernel(q_ref, k_ref, v_ref, la_ref, tril_ref, ones_ref, onesT_ref,
                       o_ref, sfin_ref, s_sc):
    g = pl.program_id(1)
    n_tok = q_ref.shape[0]
    n_chunks = n_tok // GLA_CHUNK

    @pl.when(g == 0)
    def _():
        s_sc[...] = jnp.zeros_like(s_sc)

    la = la_ref[...]
    b = _sel_dot_l(tril_ref[...], la)
    bl = _sel_dot_l(ones_ref[...], la)
    q_dec = q_ref[...] * jnp.exp(b)
    q_dec_bf = q_dec.astype(BF16)
    k = k_ref[...]
    kd_bf = (k * jnp.exp(-b)).astype(BF16)
    kl_t = (k * jnp.exp(bl - b)).T
    v_bf = v_ref[...].astype(BF16)
    bl_col = _sel_dot_r(la.T, onesT_ref[...])

    r = lax.broadcasted_iota(jnp.int32, (n_tok, n_tok), 0)
    c = lax.broadcasted_iota(jnp.int32, (n_tok, n_tok), 1)
    causal = (r // GLA_CHUNK == c // GLA_CHUNK) & (c <= r)
    lane_q = lax.broadcasted_iota(jnp.int32, (n_tok, GLA_KW), 1) // DK_GLA
    zero_q = jnp.zeros_like(q_dec_bf)
    for h in range(H_GLA):
        att = _dot_nt(jnp.where(lane_q == h, q_dec_bf, zero_q), kd_bf)
        att = jnp.where(causal, att, 0.0)
        o_ref[:, h * DV_GLA:(h + 1) * DV_GLA] = _dot(att.astype(BF16), v_bf[:, h * DV_GLA:(h + 1) * DV_GLA])

    row_head = lax.broadcasted_iota(jnp.int32, (GLA_KW, GLA_VW), 0) // DK_GLA
    lane_head = lax.broadcasted_iota(jnp.int32, (GLA_KW, GLA_VW), 1) // DV_GLA
    same_head = row_head == lane_head
    tok_chunk = lax.broadcasted_iota(jnp.int32, (GLA_KW, n_tok), 1) // GLA_CHUNK
    for n in range(n_chunks):
        rows = slice(n * GLA_CHUNK, (n + 1) * GLA_CHUNK)
        s_prev = s_sc[...]
        o_ref[rows, :] += _dot(q_dec_bf[rows], s_prev.astype(BF16))
        kv = _dot(jnp.where(tok_chunk == n, kl_t, 0.0).astype(BF16), v_bf)
        dec = jnp.exp(bl_col[:, n * LANES:(n + 1) * LANES])
        dec = jnp.concatenate([dec] * (GLA_VW // LANES), axis=1)
        s_sc[...] = jnp.where(same_head, dec * s_prev + kv, 0.0)

    @pl.when(g == pl.num_programs(1) - 1)
    def _():
        s = s_sc[...]
        for h in range(H_GLA):
            sfin_ref[0, h] = s[h * DK_GLA:(h + 1) * DK_GLA, h * DV_GLA:(h + 1) * DV_GLA]


def _gla_prompt(gq, gk, gv, la, batch, seq):
    grp = min(GLA_GROUP, seq)
    ng = seq // grp
    nch = grp // GLA_CHUNK
    idx = np.arange(grp)
    same = (idx[:, None] // GLA_CHUNK) == (idx[None, :] // GLA_CHUNK)
    tril = jnp.asarray((same & (idx[None, :] <= idx[:, None])).astype(np.float32), BF16)
    ones = jnp.asarray(same.astype(np.float32), BF16)
    ones_t = jnp.asarray((idx[:, None] // GLA_CHUNK == np.arange(nch * LANES)[None, :] // LANES).astype(np.float32),
                         BF16)
    row = lambda w: pl.BlockSpec((grp, w), lambda b, g: (b * ng + g, 0))
    return pl.pallas_call(
        _gla_prompt_kernel,
        grid=(batch, ng),
        in_specs=[row(GLA_KW), row(GLA_KW), row(GLA_VW), row(GLA_KW),
                  _full_spec(tril.shape), _full_spec(ones.shape), _full_spec(ones_t.shape)],
        out_specs=[row(GLA_VW), pl.BlockSpec((1, H_GLA, DK_GLA, DV_GLA), lambda b, g: (b, 0, 0, 0))],
        out_shape=[jax.ShapeDtypeStruct((batch * seq, GLA_VW), F32),
                   jax.ShapeDtypeStruct((batch, H_GLA, DK_GLA, DV_GLA), F32)],
        scratch_shapes=[pltpu.VMEM((GLA_KW, GLA_VW), F32)],
        compiler_params=_cparams("parallel", "arbitrary"),
        name="gla_prompt",
    )(gq, gk, gv, la, tril, ones, ones_t)


def _gla_sample_kernel(q_ref, k_ref, v_ref, la_ref, s0_ref, tril_ref, o_ref, s1_ref):
    n, ls, _ = q_ref.shape
    la = la_ref[...]
    tril = jnp.broadcast_to(tril_ref[...][None], (n, ls, ls))
    hi, mid, lo = _split3(la)
    bdot = lambda x: jnp.einsum('nts,nsd->ntd', tril, x, preferred_element_type=F32)
    b = (bdot(hi) + bdot(mid)) + bdot(lo)
    b_last = b[:, ls - 1:ls, :]
    q_dec = (q_ref[...] * jnp.exp(b)).astype(BF16)
    k = k_ref[...]
    kd = (k * jnp.exp(-b)).astype(BF16)
    kl = (k * jnp.exp(b_last - b)).astype(BF16)
    v = v_ref[...].astype(BF16)
    s0 = s0_ref[...]
    att = jnp.einsum('ncd,nsd->ncs', q_dec, kd, preferred_element_type=F32)
    r = lax.broadcasted_iota(jnp.int32, (n, ls, ls), 1)
    c = lax.broadcasted_iota(jnp.int32, (n, ls, ls), 2)
    att = jnp.where(c <= r, att, 0.0)
    o = jnp.einsum('ncs,nse->nce', att.astype(BF16), v, preferred_element_type=F32)
    o = o + jnp.einsum('ncd,nde->nce', q_dec, s0.astype(BF16), preferred_element_type=F32)
    o_ref[...] = o
    kv = jnp.einsum('nds,nse->nde', jnp.swapaxes(kl, 1, 2), v, preferred_element_type=F32)
    dec = jnp.swapaxes(jnp.exp(b_last), 1, 2)
    s1_ref[...] = dec * s0 + kv


def _gla_sample(gq, gk, gv, la, state):
    n, ls, _ = gq.shape
    nb = min(32, n)
    tril = jnp.asarray(np.tril(np.ones((ls, ls), np.float32)), BF16)
    blk = lambda a, c: pl.BlockSpec((nb, a, c), lambda i: (i, 0, 0))
    return pl.pallas_call(
        _gla_sample_kernel,
        grid=(n // nb,),
        in_specs=[blk(ls, DK_GLA), blk(ls, DK_GLA), blk(ls, DV_GLA), blk(ls, DK_GLA), blk(DK_GLA, DV_GLA),
                  _full_spec(tril.shape)],
        out_specs=[blk(ls, DV_GLA), blk(DK_GLA, DV_GLA)],
        out_shape=[jax.ShapeDtypeStruct((n, ls, DV_GLA), F32), jax.ShapeDtypeStruct((n, DK_GLA, DV_GLA), F32)],
        compiler_params=_cparams("parallel"),
        name="gla_sample",
    )(gq, gk, gv, la, state, tril)


def _mixproj_kernel(x_ref, fox_ref, gla_ref, gr_ref, gg_ref, wo_ref, gc_ref, wq_ref, x1_ref, q_ref):
    gla = gla_ref[...]
    gr = gr_ref[...]
    acc = _dot(fox_ref[...], wo_ref[0:FOX_W, :])
    for h in range(H_GLA):
        cols = slice(h * DV_GLA, (h + 1) * DV_GLA)
        gh = _rms(gla[:, cols], gg_ref[:, cols])
        grh = gr[:, cols]
        gh = gh * (grh * jax.nn.sigmoid(grh))
        acc = acc + _dot(gh.astype(BF16), wo_ref[FOX_W + h * DV_GLA:FOX_W + (h + 1) * DV_GLA, :])
    x1 = x_ref[...] + acc
    x1_ref[...] = x1
    h2 = _rms(x1, gc_ref[...]).astype(BF16)
    q_ref[...] = (_dot(h2, wq_ref[...]) * (DH_MEM ** -0.5)).astype(BF16)


def _mixproj(x2d, fox_o, gla_o, gr, g_gla, w_out_bf, g_cross, w_q_bf):
    t = x2d.shape[0]
    tt = min(TOK_TILE, t)
    row = lambda w: pl.BlockSpec((tt, w), lambda i: (i, 0))
    return pl.pallas_call(
        _mixproj_kernel,
        grid=(t // tt,),
        in_specs=[row(D_MODEL), row(FOX_W), row(GLA_VW), row(GLA_VW), _full_spec(g_gla.shape),
                  _full_spec(w_out_bf.shape), _full_spec(g_cross.shape), _full_spec(w_q_bf.shape)],
        out_specs=[row(D_MODEL), row(D_MODEL)],
        out_shape=[jax.ShapeDtypeStruct((t, D_MODEL), F32), jax.ShapeDtypeStruct((t, D_MODEL), BF16)],
        compiler_params=_cparams("parallel"),
        name="mixproj",
    )(x2d, fox_o, gla_o, gr, g_gla, w_out_bf, g_cross, w_q_bf)


def _memkv_kernel(m_ref, g_ref, wk_ref, wv_ref, k_ref, v_ref):
    m = _rms(m_ref[...], g_ref[...]).astype(BF16)
    k_ref[...] = _dot(m, wk_ref[...])
    v_ref[...] = _dot(m, wv_ref[...])


def _memkv(mem2d, g_mem, wk_bf, wv_bf):
    t = mem2d.shape[0]
    tt = min(TOK_TILE, t)
    row = pl.BlockSpec((tt, D_MODEL), lambda i: (i, 0))
    return pl.pallas_call(
        _memkv_kernel,
        grid=(t // tt,),
        in_specs=[row, _full_spec(g_mem.shape), _full_spec(wk_bf.shape), _full_spec(wv_bf.shape)],
        out_specs=[row, row],
        out_shape=[jax.ShapeDtypeStruct((t, D_MODEL), F32)] * 2,
        compiler_params=_cparams("parallel"),
        name="memkv",
    )(mem2d, g_mem, wk_bf, wv_bf)


def _cross_kernel(q_ref, mk_ref, mv_ref, o_ref, *, per_head_memory):
    q = q_ref[0]
    if per_head_memory:
        lq = q.shape[0]
        n_mem = mk_ref.shape[2]
        q_hm = jnp.concatenate([q[:, h * DH_MEM:(h + 1) * DH_MEM] for h in range(H_MEM)], axis=0)
        mk = mk_ref[0, 0].reshape(n_mem * H_MEM, DH_MEM).astype(BF16)
        mv = mv_ref[0, 0].reshape(n_mem * H_MEM, DH_MEM).astype(BF16)
        s = _dot_nt(q_hm, mk)
        row_head = lax.broadcasted_iota(jnp.int32, s.shape, 0) // lq
        key_head = lax.broadcasted_iota(jnp.int32, s.shape, 1) % H_MEM
        s = jnp.where(row_head == key_head, s, NEG_INF)
        e = jnp.exp(s - jnp.max(s, axis=-1, keepdims=True))
        p = e / jnp.sum(e, axis=-1, keepdims=True)
        o = _dot(p.astype(BF16), mv).astype(o_ref.dtype)
        for h in range(H_MEM):
            o_ref[0, :, h * DH_MEM:(h + 1) * DH_MEM] = o[h * lq:(h + 1) * lq]
        return
    for h in range(H_MEM):
        cols = slice(h * DH_MEM, (h + 1) * DH_MEM)
        mk = mk_ref[0, :, cols].astype(BF16)
        mv = mv_ref[0, :, cols].astype(BF16)
        s = _dot_nt(q[:, cols], mk)
        e = jnp.exp(s - jnp.max(s, axis=-1, keepdims=True))
        p = e / jnp.sum(e, axis=-1, keepdims=True)
        o_ref[0, :, cols] = _dot(p.astype(BF16), mv).astype(o_ref.dtype)


def _cross(q3, mk, mv, layer=None):
    b, lq, _ = q3.shape
    tl = min(TOK_TILE, lq)
    if layer is None:
        mem_spec = pl.BlockSpec((1,) + mk.shape[1:], lambda i, j: (i, 0, 0))
    else:
        mem_spec = pl.BlockSpec((1, 1) + mk.shape[2:], lambda i, j: (layer, i, 0, 0, 0))
    return pl.pallas_call(
        functools.partial(_cross_kernel, per_head_memory=layer is not None),
        grid=(b, lq // tl),
        in_specs=[pl.BlockSpec((1, tl, D_MODEL), lambda i, j: (i, j, 0)), mem_spec, mem_spec],
        out_specs=pl.BlockSpec((1, tl, D_MODEL), lambda i, j: (i, j, 0)),
        out_shape=jax.ShapeDtypeStruct((b, lq, D_MODEL), BF16),
        compiler_params=_cparams("parallel", "parallel"),
        name="cross_attn",
    )(q3, mk, mv)


def _peer_prep_kernel(u_ref, v_ref, ub_ref, vt_ref):
    ub_ref[...] = u_ref[...].astype(BF16)
    vt_ref[...] = v_ref[...].T.astype(BF16)


def _peer_prep(u, v):
    ne = u.shape[0]
    eb = min(PREP_EBLK, ne)
    return pl.pallas_call(
        _peer_prep_kernel,
        grid=(ne // eb,),
        in_specs=[pl.BlockSpec((eb, D_MODEL), lambda i: (i, 0))] * 2,
        out_specs=[pl.BlockSpec((eb, D_MODEL), lambda i: (i, 0)), pl.BlockSpec((D_MODEL, eb), lambda i: (0, i))],
        out_shape=[jax.ShapeDtypeStruct((ne, D_MODEL), BF16), jax.ShapeDtypeStruct((D_MODEL, ne), BF16)],
        compiler_params=_cparams("parallel"),
        name="peer_prep",
    )(u, v)


def _rows_per_word():
    return 4 // jnp.dtype(BF16).itemsize


def _pack_rows(x):
    return pltpu.bitcast(x, jnp.uint32)


def _unpack_rows(x):
    return pltpu.bitcast(x, BF16)


def _top16_rows(s, break_ties):
    nrow = s.shape[0]
    rowid = lax.broadcasted_iota(jnp.int32, s.shape, 0)
    rank = jnp.full(s.shape, float(PEER_TOPK), F32)
    work = s
    vals = []
    for r in range(PEER_TOPK):
        mx = jnp.max(work, axis=0, keepdims=True)
        if break_ties:
            sel = rowid == jnp.min(jnp.where(work == mx, rowid, nrow), axis=0, keepdims=True)
        else:
            sel = work == mx
        rank = jnp.where(sel, float(r), rank)
        work = jnp.where(sel, -jnp.inf, work)
        vals.append(mx)
    return rank, jnp.concatenate(vals, axis=0)


def _ranked_16(rank):
    n = jnp.sum(jnp.where(rank < float(PEER_TOPK), 1.0, 0.0), axis=0, keepdims=True)
    return jnp.where(n == float(PEER_TOPK), 1.0, 0.0)


def _peer_route_kernel(x_ref, x1_ref, wo_ref, g_ref, wq_ref, sk_ref, ca_ref, cb_ref, cg_ref,
                       x2_ref, hn_ref, c1_ref, nrow_ref, p2_ref, rk2_ref, q_sc, s_sc):
    tt = x_ref.shape[0]
    nch = tt // LANES
    x2 = x1_ref[...] + _dot(x_ref[...], wo_ref[...])
    x2_ref[...] = x2
    hn = _rms(x2, g_ref[...])
    hn_ref[...] = hn.T.astype(BF16)
    q_sc[...] = _dot(hn.astype(BF16), wq_ref[...]).astype(BF16)
    for hj in range(2 * H_PEER):
        st = _dot_nt(sk_ref[hj % 2], q_sc[:, hj * N_KEYS:(hj + 1) * N_KEYS])
        for ch in range(nch):
            s_sc[hj, ch] = st[:, ch * LANES:(ch + 1) * LANES]

    cand_row = lax.broadcasted_iota(jnp.int32, (N_CAND_PAD, LANES), 0)

    per_unit = 2 if nch % 2 == 0 else 1

    def unit(i, carry):
        h = i // (nch // per_unit)
        first = (i % (nch // per_unit)) * per_unit
        def route(break_ties):
            halves = [rank_halves(h, first + d, break_ties) for d in range(per_unit)]
            tie_free = select_pairs(h, first, break_ties, *halves[0])
            for d in range(1, per_unit):
                tie_free = tie_free * select_pairs(h, first + d, break_ties, *halves[d])
            return tie_free

        tie_free = route(False)

        @pl.when(jnp.min(tie_free) < 0.5)
        def _():
            route(True)

        return carry

    def rank_halves(h, ch, break_ties):
        s1 = s_sc[2 * h, ch]
        s2 = s_sc[2 * h + 1, ch]
        return (s1, s2) + _top16_rows(s1, break_ties) + _top16_rows(s2, break_ties)

    def select_pairs(h, ch, break_ties, s1, s2, rank1, v1, rank2, v2):
        cand = _sel_dot_l(ca_ref[...], v1) + _sel_dot_l(cb_ref[...], v2)
        cand = jnp.where(cand_row < N_CAND, cand, -jnp.inf)
        crank, _ = _top16_rows(cand, break_ties)
        chosen = crank < float(PEER_TOPK)
        n_of_a = _dot(cg_ref[...], jnp.where(chosen, 1.0, 0.0).astype(BF16))
        z = jnp.sum(jnp.where(chosen, jnp.exp(cand - cand[0:1]), 0.0), axis=0, keepdims=True)
        nrow = jnp.zeros_like(rank1)
        for a in range(PEER_TOPK):
            nrow = jnp.where(rank1 == float(a), n_of_a[a:a + 1], nrow)
        c1_ref[h, ch] = jnp.exp(s1 - v1[0:1]) / z
        nrow_ref[h, ch] = nrow
        p2_ref[h, ch] = _pack_rows(jnp.exp(s2 - v2[0:1]).astype(BF16))
        rk2_ref[h, ch] = _pack_rows(rank2.astype(BF16))
        return _ranked_16(rank1) * _ranked_16(rank2) * _ranked_16(crank)

    lax.fori_loop(0, H_PEER * nch // per_unit, unit, 0)


def _peer_route(oc2d, x1, w_o_bf, g_ffn, w_q_bf, sk_bf):
    t = x1.shape[0]
    tt = min(PEER_TOK, t)
    nch = tt // LANES
    pairs = [(a, b) for a in range(PEER_TOPK) for b in range(PEER_TOPK) if (a + 1) * (b + 1) <= PEER_TOPK]
    assert len(pairs) == N_CAND
    ca = np.zeros((N_CAND_PAD, PEER_TOPK), np.float32)
    cb = np.zeros((N_CAND_PAD, PEER_TOPK), np.float32)
    cg = np.zeros((PEER_TOPK, N_CAND_PAD), np.float32)
    for i, (a, b) in enumerate(pairs):
        ca[i, a] = 1.0
        cb[i, b] = 1.0
        cg[a, i] = 1.0
    ca, cb, cg = (jnp.asarray(m, BF16) for m in (ca, cb, cg))
    row = lambda w: pl.BlockSpec((tt, w), lambda i: (i, 0))
    krows = N_KEYS // _rows_per_word()
    sel = pl.BlockSpec((H_PEER, nch, N_KEYS, LANES), lambda i: (0, i, 0, 0))
    sel_shape = jax.ShapeDtypeStruct((H_PEER, t // LANES, N_KEYS, LANES), F32)
    packed = pl.BlockSpec((H_PEER, nch, krows, LANES), lambda i: (0, i, 0, 0))
    packed_shape = jax.ShapeDtypeStruct((H_PEER, t // LANES, krows, LANES), jnp.uint32)
    return pl.pallas_call(
        _peer_route_kernel,
        grid=(t // tt,),
        in_specs=[row(D_MODEL), row(D_MODEL), _full_spec(w_o_bf.shape), _full_spec(g_ffn.shape),
                  _full_spec(w_q_bf.shape), _full_spec(sk_bf.shape),
                  _full_spec(ca.shape), _full_spec(cb.shape), _full_spec(cg.shape)],
        out_specs=[row(D_MODEL), pl.BlockSpec((D_MODEL, tt), lambda i: (0, i)), sel, sel, packed, packed],
        out_shape=[jax.ShapeDtypeStruct((t, D_MODEL), F32), jax.ShapeDtypeStruct((D_MODEL, t), BF16),
                   sel_shape, sel_shape, packed_shape, packed_shape],
        scratch_shapes=[pltpu.VMEM((tt, 2 * H_PEER * N_KEYS), BF16),
                        pltpu.VMEM((2 * H_PEER, nch, N_KEYS, LANES), F32)],
        compiler_params=_cparams("parallel"),
        name="peer_route",
    )(oc2d, x1, w_o_bf, g_ffn, w_q_bf, sk_bf, ca, cb, cg)


def _peer_mix_kernel(x2_ref, hn_ref, u_ref, vt_ref, c1_ref, nrow_ref, p2_ref, rk2_ref, gf_ref,
                     y_ref, a_sc, w_sc, acc_sc, *, final_norm):
    e = pl.program_id(1)
    tt = hn_ref.shape[1]
    nch = tt // LANES
    n_i1 = u_ref.shape[0] // N_KEYS

    @pl.when(e == 0)
    def _():
        acc_sc[...] = jnp.zeros_like(acc_sc)

    krows = N_KEYS // _rows_per_word()
    acc = acc_sc[...]
    def activations(c):
        lo = c * PEER_CHUNK_I1
        ex = slice(lo * N_KEYS, (lo + PEER_CHUNK_I1) * N_KEYS)
        a_sc[lo * krows:(lo + PEER_CHUNK_I1) * krows, :] = _pack_rows(
            jax.nn.gelu(_dot(u_ref[ex, :], hn_ref[...])).astype(BF16))

    def weights(c):
        lo = c * PEER_CHUNK_I1
        for ch in range(nch):
            cols = slice(ch * LANES, (ch + 1) * LANES)
            for r in range(lo, lo + PEER_CHUNK_I1):
                w = jnp.zeros((N_KEYS, LANES), BF16)
                for h in range(H_PEER):
                    wide = lambda ref: jnp.broadcast_to(ref[h, ch, r:r + 1, :], (N_KEYS, LANES)).astype(BF16)
                    p2 = _unpack_rows(p2_ref[h, ch])
                    hit = _unpack_rows(rk2_ref[h, ch]) < wide(nrow_ref)
                    w = w + jnp.where(hit, p2, jnp.zeros_like(p2)) * wide(c1_ref)
                w_sc[r * N_KEYS:(r + 1) * N_KEYS, cols] = w * _unpack_rows(a_sc[r * krows:(r + 1) * krows, cols])

    n_chunks = n_i1 // PEER_CHUNK_I1
    per_sub = PEER_SUB_I1 // PEER_CHUNK_I1
    for c in range(min(2, n_chunks)):
        activations(c)
    for c in range(n_chunks):
        weights(c)
        if c + 2 < n_chunks:
            activations(c + 2)
        if (c + 1) % per_sub == 0:
            ex = slice((c + 1 - per_sub) * PEER_CHUNK_I1 * N_KEYS, (c + 1) * PEER_CHUNK_I1 * N_KEYS)
            acc = acc + _dot(vt_ref[:, ex], w_sc[ex, :])
    acc_sc[...] = acc

    @pl.when(e == pl.num_programs(1) - 1)
    def _():
        y = x2_ref[...] + acc_sc[...].T
        y_ref[...] = _rms(y, gf_ref[...]) if final_norm else y


def _peer_mix(x2, hn, u_bf, vt_bf, c1, nrow, p2, rk2, g_final, final_norm):
    t = x2.shape[0]
    tt = min(PEER_MIX_TOK, t)
    nch = tt // LANES
    ne = u_bf.shape[0]
    eb = min(PEER_EBLK, ne)
    n_i1 = eb // N_KEYS
    row = lambda w: pl.BlockSpec((tt, w), lambda i, e: (i, 0))
    part = pl.BlockSpec((H_PEER, nch, n_i1, LANES), lambda i, e: (0, i, e, 0))
    full = pl.BlockSpec((H_PEER, nch, N_KEYS // _rows_per_word(), LANES), lambda i, e: (0, i, 0, 0))
    return pl.pallas_call(
        functools.partial(_peer_mix_kernel, final_norm=final_norm),
        grid=(t // tt, ne // eb),
        in_specs=[row(D_MODEL), pl.BlockSpec((D_MODEL, tt), lambda i, e: (0, i)),
                  pl.BlockSpec((eb, D_MODEL), lambda i, e: (e, 0)),
                  pl.BlockSpec((D_MODEL, eb), lambda i, e: (0, e)),
                  part, part, full, full, pl.BlockSpec(g_final.shape, lambda i, e: (0, 0))],
        out_specs=row(D_MODEL),
        out_shape=jax.ShapeDtypeStruct((t, D_MODEL), F32),
        scratch_shapes=[pltpu.VMEM((eb // _rows_per_word(), tt), jnp.uint32), pltpu.VMEM((eb, tt), BF16),
                        pltpu.VMEM((D_MODEL, tt), F32)],
        compiler_params=_cparams("parallel", "arbitrary"),
        name="peer_mix",
    )(x2, hn, u_bf, vt_bf, c1, nrow, p2, rk2, g_final)


def _prep_in_weights(w_in, b_fox_f, w_gate2, b_gate):
    o = np.cumsum((0,) + IN_SIZES)
    seg = lambda i: w_in[:, int(o[i]):int(o[i + 1])]
    pad = jnp.zeros((D_MODEL, LANES - H_FOX - GLA_GATE_RANK), w_in.dtype)
    w_perm = jnp.concatenate([seg(0), seg(1), seg(2), seg(4), seg(5), seg(6), seg(8), seg(3), seg(7), pad],
                             axis=1).astype(BF16)
    b_small = jnp.zeros((1, LANES), F32).at[0, 0:H_FOX].set(b_fox_f)
    wg_pad = jnp.zeros((LANES, GLA_KW), F32).at[H_FOX:H_FOX + GLA_GATE_RANK].set(w_gate2).astype(BF16)
    return w_perm, b_small, wg_pad, b_gate.reshape(1, GLA_KW)


def kernel(x_prompt, x_sample, cache_fox_k, cache_fox_v, cache_fox_logf, state_gla, cache_mem_k, cache_mem_v, page_table, mem_prompt, g_mix, w_in, b_fox_f, w_gla_gate2, b_gla_gate, g_gla_out, w_out, g_cross, g_mem, w_mem_k, w_mem_v, w_cross_q, w_cross_o, g_ffn, peer_w_q, peer_subkeys, peer_u, peer_v, g_final):
    depth = w_in.shape[0]
    bp, lp, _ = x_prompt.shape
    bs, ls, _ = x_sample.shape
    n_mem = mem_prompt.shape[1]
    n_pool, page = cache_fox_k.shape[1:3]
    xp = x_prompt.reshape(bp * lp, D_MODEL)
    xs = x_sample.reshape(bs * ls, D_MODEL)
    vec = lambda a: a.reshape(1, -1)
    kcache_t = cache_fox_k.transpose(0, 1, 3, 4, 2).reshape(depth, n_pool, FOX_W, page)
    vcache_t = cache_fox_v.transpose(0, 1, 3, 4, 2).reshape(depth, n_pool, FOX_W, page)
    lfcache_t = cache_fox_logf.transpose(0, 1, 3, 2)
    outs = [[] for _ in range(10)]
    for l in range(depth):
        w_perm, b_small, wg_pad, b_gate = _prep_in_weights(w_in[l], b_fox_f[l], w_gla_gate2[l], b_gla_gate[l])
        w_out_bf = w_out[l].astype(BF16)
        w_cq_bf = w_cross_q[l].astype(BF16)
        w_co_bf = w_cross_o[l].astype(BF16)
        w_pq_bf = peer_w_q[l].astype(BF16)
        sk_bf = peer_subkeys[l].astype(BF16)
        u_bf, vt_bf = _peer_prep(peer_u[l], peer_v[l])
        last = l == depth - 1

        def tail(x, fox_o, gla_o, gr, mk, mv, nb, mem_layer):
            x1, qc = _mixproj(x, fox_o, gla_o, gr, vec(g_gla_out[l]), w_out_bf, vec(g_cross[l]), w_cq_bf)
            oc = _cross(qc.reshape(nb, -1, D_MODEL), mk, mv, mem_layer).reshape(-1, D_MODEL)
            x2, hn, c1, nrow, p2, rk2 = _peer_route(oc, x1, w_co_bf, vec(g_ffn[l]), w_pq_bf, sk_bf)
            return _peer_mix(x2, hn, u_bf, vt_bf, c1, nrow, p2, rk2, vec(g_final), last)

        fq_bf, fk, fv, lf, gq, gk, gv, la, gr = _inproj(xp, vec(g_mix[l]), w_perm, b_small, wg_pad, b_gate,
                                                        DH_FOX ** -0.5 * _LOG2E)
        qc, kc, vt, fk_t, fv_t, lf_t = _fox_prep(fq_bf, fk, fv, lf, bp, lp)
        fox_o = _fox_prompt(qc, kc, vt, bp, lp)
        gla_o, s_fin = _gla_prompt(gq, gk, gv, la, bp, lp)
        mk, mv = _memkv(mem_prompt.reshape(bp * n_mem, D_MODEL), vec(g_mem[l]), w_mem_k[l].astype(BF16),
                        w_mem_v[l].astype(BF16))
        mk3 = mk.reshape(bp, n_mem, D_MODEL)
        mv3 = mv.reshape(bp, n_mem, D_MODEL)
        xp = tail(xp, fox_o, gla_o, gr, mk3, mv3, bp, None)
        outs[0].append(fk_t.reshape(bp, H_FOX, DH_FOX, lp).transpose(0, 3, 1, 2))
        outs[1].append(fv_t.reshape(bp, H_FOX, DH_FOX, lp).transpose(0, 3, 1, 2))
        outs[2].append(lf_t.transpose(0, 2, 1))
        outs[3].append(s_fin)
        outs[4].append(mk.reshape(bp, n_mem, H_MEM, DH_MEM))
        outs[5].append(mv.reshape(bp, n_mem, H_MEM, DH_MEM))

        fq_bf, fk, fv, lf, gq, gk, gv, la, gr = _inproj(xs, vec(g_mix[l]), w_perm, b_small, wg_pad, b_gate,
                                                        DH_FOX ** -0.5)
        seqs = lambda a: a.reshape(bs, ls, FOX_W)
        fox_o = _fox_sample(seqs(fq_bf), seqs(fk), seqs(fv), lf.reshape(bs, ls, H_FOX).transpose(0, 2, 1),
                            kcache_t, vcache_t, lfcache_t, page_table, l).reshape(bs * ls, FOX_W)
        heads = lambda a, d: a.reshape(bs, ls, H_GLA, d).transpose(0, 2, 1, 3).reshape(bs * H_GLA, ls, d)
        gla_o, s_new = _gla_sample(heads(gq, DK_GLA), heads(gk, DK_GLA), heads(gv, DV_GLA), heads(la, DK_GLA),
                                   state_gla[l].reshape(bs * H_GLA, DK_GLA, DV_GLA))
        gla_o = gla_o.reshape(bs, H_GLA, ls, DV_GLA).transpose(0, 2, 1, 3).reshape(bs * ls, GLA_VW)
        xs = tail(xs, fox_o, gla_o, gr, cache_mem_k, cache_mem_v, bs, l)
        outs[6].append(fk.reshape(bs, ls, H_FOX, DH_FOX))
        outs[7].append(fv.reshape(bs, ls, H_FOX, DH_FOX))
        outs[8].append(lf.reshape(bs, ls, H_FOX))
        outs[9].append(s_new.reshape(bs, H_GLA, DK_GLA, DV_GLA))
    stacked = [jnp.stack(o, axis=0) for o in outs]
    return (xp.reshape(bp, lp, D_MODEL), xs.reshape(bs, ls, D_MODEL), *stacked)
```

```python
import functools

import numpy as np
import jax
import jax.numpy as jnp
from jax import lax
from jax.experimental import pallas as pl
from jax.experimental.pallas import tpu as pltpu

F32 = jnp.float32
BF16 = jnp.bfloat16

D_MODEL = 1024
H_FOX, DH_FOX = 8, 64
H_GLA, DK_GLA, DV_GLA = 4, 64, 128
GLA_GATE_RANK = 16
GLA_TAU = 16.0
GLA_CHUNK = 64
H_MEM = 4
DH_MEM = D_MODEL // H_MEM
N_KEYS = 128
H_PEER = 8
PEER_TOPK = 16
EPS = 1e-6
NEG_INF = -1e30
_LOG2E = float(np.log2(np.e))
FOX_W = H_FOX * DH_FOX
GLA_KW = H_GLA * DK_GLA
GLA_VW = H_GLA * DV_GLA
IN_SIZES = (FOX_W, FOX_W, FOX_W, H_FOX, GLA_KW, GLA_KW, GLA_VW, GLA_GATE_RANK, GLA_VW)

LANES = 128
VMEM_LIMIT_BYTES = 56 * 2**20

TOK_TILE = 512
INPROJ_TILE = 512
FOX_TILE = 512
FOX_KTILE = 1024
FOX_HP_PER_STEP = 4
GLA_GROUP = 512
PEER_TOK = 512
PEER_EBLK = 2048
PEER_CHUNK_I1 = 4
PEER_SUB_I1 = 8
FOX_PAGES_PER_STEP = 16
PREP_EBLK = 512
N_CAND = 50
N_CAND_PAD = 64


def _cparams(*sem):
    return pltpu.CompilerParams(dimension_semantics=sem, vmem_limit_bytes=VMEM_LIMIT_BYTES)


def _dot(a, b):
    return jnp.dot(a, b, preferred_element_type=F32)


def _dot_nt(a, b):
    return lax.dot_general(a, b, (((1,), (1,)), ((), ())), preferred_element_type=F32)


def _split3(x):
    hi = x.astype(BF16)
    r = x - hi.astype(F32)
    mid = r.astype(BF16)
    lo = (r - mid.astype(F32)).astype(BF16)
    return hi, mid, lo


def _sel_dot_l(m01, x):
    hi, mid, lo = _split3(x)
    return (_dot(m01, hi) + _dot(m01, mid)) + _dot(m01, lo)


def _sel_dot_r(x, m01):
    hi, mid, lo = _split3(x)
    return (_dot(hi, m01) + _dot(mid, m01)) + _dot(lo, m01)


def _rms(x, g):
    return x * lax.rsqrt(jnp.mean(x * x, axis=-1, keepdims=True) + EPS) * g


def _logsig(x):
    return jnp.minimum(x, 0.0) - jnp.log1p(jnp.exp(-jnp.abs(x)))


def _full_spec(shape):
    nd = len(shape)
    return pl.BlockSpec(shape, lambda *_: (0,) * nd)


_C_FQ, _C_FK, _C_FV = 0, FOX_W, 2 * FOX_W
_C_GQ = 3 * FOX_W
_C_GK = _C_GQ + GLA_KW
_C_GV = _C_GK + GLA_KW
_C_GR = _C_GV + GLA_VW
_C_SM = _C_GR + GLA_VW
_C_END = _C_SM + LANES


def _inproj_kernel(x_ref, g_ref, w_ref, bsm_ref, wg_ref, bg_ref,
                   fq_ref, fk_ref, fv_ref, lf_ref, gq_ref, gk_ref, gv_ref, la_ref, gr_ref, *, q_scale):
    h = _rms(x_ref[...], g_ref[...]).astype(BF16)

    def seg(lo, hi):
        return _dot(h, w_ref[:, lo:hi])

    fq_ref[...] = (seg(_C_FQ, _C_FK) * q_scale).astype(BF16)
    fk_ref[...] = seg(_C_FK, _C_FV)
    fv_ref[...] = seg(_C_FV, _C_GQ)
    gq_ref[...] = seg(_C_GQ, _C_GK) * (DK_GLA ** -0.5)
    gk_ref[...] = seg(_C_GK, _C_GV)
    gv_ref[...] = seg(_C_GV, _C_GR)
    gr_ref[...] = seg(_C_GR, _C_SM)
    ysm = seg(_C_SM, _C_END)
    lf_ref[...] = _logsig(ysm + bsm_ref[...])[:, 0:H_FOX]
    z = _dot(ysm.astype(BF16), wg_ref[...]) + bg_ref[...]
    la_ref[...] = _logsig(z) / GLA_TAU


def _inproj(x2d, g_mix, w_perm, b_small, wg_pad, b_gate, q_scale):
    t = x2d.shape[0]
    tt = min(INPROJ_TILE, t)
    row = lambda w: pl.BlockSpec((tt, w), lambda i: (i, 0))
    outs = [(FOX_W, BF16), (FOX_W, F32), (FOX_W, F32), (H_FOX, F32),
            (GLA_KW, F32), (GLA_KW, F32), (GLA_VW, F32), (GLA_KW, F32), (GLA_VW, F32)]
    return pl.pallas_call(
        functools.partial(_inproj_kernel, q_scale=q_scale),
        grid=(t // tt,),
        in_specs=[row(D_MODEL), _full_spec(g_mix.shape), _full_spec(w_perm.shape), _full_spec(b_small.shape),
                  _full_spec(wg_pad.shape), _full_spec(b_gate.shape)],
        out_specs=[row(w) for w, _ in outs],
        out_shape=[jax.ShapeDtypeStruct((t, w), dt) for w, dt in outs],
        compiler_params=_cparams("parallel"),
        name="inproj",
    )(x2d, g_mix, w_perm, b_small, wg_pad, b_gate)


_FOX_QW = 4 * LANES
_FOX_KW = 2 * LANES
_N_HP = FOX_W // LANES


def _fox_aug_tables():
    pq = np.zeros((3, H_FOX, _N_HP * 2 * LANES), np.float32)
    oq = np.zeros((1, _N_HP * 2 * LANES), np.float32)
    pk = np.zeros((3, H_FOX, _N_HP * LANES), np.float32)
    ok = np.zeros((1, _N_HP * LANES), np.float32)
    for hp in range(_N_HP):
        for half in range(2):
            head = 2 * hp + half
            qb = (hp * 2 + half) * LANES
            kb = hp * LANES
            for j in range(3):
                pq[j, head, qb + 6 * half + j] = 1.0
                oq[0, qb + 6 * half + 3 + j] = 1.0
                pk[j, head, kb + 6 * half + 3 + j] = 1.0
                ok[0, kb + 6 * half + j] = 1.0
    return jnp.asarray(pq, BF16), jnp.asarray(oq, F32), jnp.asarray(pk, BF16), jnp.asarray(ok, F32)


def _fox_prep_kernel(q_ref, k_ref, v_ref, lf_ref, tril_ref, pq_ref, oq_ref, pk_ref, ok_ref, eye_ref,
                     qc_ref, kc_ref, vt_ref, fkt_ref, fvt_ref, lft_ref, carry_sc):
    @pl.when(pl.program_id(1) == 0)
    def _():
        carry_sc[...] = jnp.zeros_like(carry_sc)

    lf_parts = _split3(lf_ref[...])
    eye = eye_ref[...]
    lft_ref[0] = (_dot_nt(eye, lf_parts[0]) + _dot_nt(eye, lf_parts[1])) + _dot_nt(eye, lf_parts[2])
    c = _sel_dot_l(tril_ref[...], lf_ref[...]) + carry_sc[0:1, 0:H_FOX]
    carry_sc[0:1, 0:H_FOX] = c[-1:, :]
    parts = _split3(c * _LOG2E)
    aug_q = oq_ref[...]
    aug_k = ok_ref[...]
    for j in range(3):
        aug_q = aug_q + _dot(parts[j], pq_ref[j])
        aug_k = aug_k - _dot(parts[j], pk_ref[j])
    aug_q = aug_q.astype(BF16)
    aug_k = aug_k.astype(BF16)
    lane = lax.broadcasted_iota(jnp.int32, (q_ref.shape[0], LANES), 1)
    for hp in range(_N_HP):
        q = q_ref[:, hp * LANES:(hp + 1) * LANES]
        zero = jnp.zeros_like(q)
        base = hp * _FOX_QW
        qc_ref[:, base:base + LANES] = jnp.where(lane < DH_FOX, q, zero)
        qc_ref[:, base + LANES:base + 2 * LANES] = aug_q[:, (2 * hp) * LANES:(2 * hp + 1) * LANES]
        qc_ref[:, base + 2 * LANES:base + 3 * LANES] = jnp.where(lane >= DH_FOX, q, zero)
        qc_ref[:, base + 3 * LANES:base + 4 * LANES] = aug_q[:, (2 * hp + 1) * LANES:(2 * hp + 2) * LANES]
        kc_ref[:, hp * _FOX_KW:hp * _FOX_KW + LANES] = k_ref[:, hp * LANES:(hp + 1) * LANES].astype(BF16)
        kc_ref[:, hp * _FOX_KW + LANES:(hp + 1) * _FOX_KW] = aug_k[:, hp * LANES:(hp + 1) * LANES]
    v_t = v_ref[...].T
    vt_ref[0] = v_t.astype(BF16)
    fvt_ref[0] = v_t
    fkt_ref[0] = k_ref[...].T


def _fox_prep(fq_bf, fk, fv, lf, batch, seq):
    tt = min(FOX_TILE, seq)
    nt = seq // tt
    tril = jnp.asarray(np.tril(np.ones((tt, tt), np.float32)), BF16)
    consts = [tril, *_fox_aug_tables(), jnp.eye(H_FOX, dtype=BF16)]
    row = lambda w: pl.BlockSpec((tt, w), lambda b, i: (b * nt + i, 0))
    col = lambda h: pl.BlockSpec((1, h, tt), lambda b, i: (b, 0, i))
    return pl.pallas_call(
        _fox_prep_kernel,
        grid=(batch, nt),
        in_specs=[row(FOX_W), row(FOX_W), row(FOX_W), row(H_FOX)] + [_full_spec(a.shape) for a in consts],
        out_specs=[row(_N_HP * _FOX_QW), row(_N_HP * _FOX_KW), col(FOX_W), col(FOX_W), col(FOX_W), col(H_FOX)],
        out_shape=[jax.ShapeDtypeStruct((batch * seq, _N_HP * _FOX_QW), BF16),
                   jax.ShapeDtypeStruct((batch * seq, _N_HP * _FOX_KW), BF16),
                   jax.ShapeDtypeStruct((batch, FOX_W, seq), BF16),
                   jax.ShapeDtypeStruct((batch, FOX_W, seq), F32),
                   jax.ShapeDtypeStruct((batch, FOX_W, seq), F32),
                   jax.ShapeDtypeStruct((batch, H_FOX, seq), F32)],
        scratch_shapes=[pltpu.VMEM((8, LANES), F32)],
        compiler_params=_cparams("parallel", "arbitrary"),
        name="fox_prep",
    )(fq_bf, fk, fv, lf, *consts)


def _fox_prompt_kernel(qi_tab, ki_tab, q_ref, k_ref, vt_ref, o_ref,
                       m_sc, l_sc, acc_sc):
    t = pl.program_id(2)
    qi = qi_tab[t]
    ki = ki_tab[t]
    tq = q_ref.shape[0]
    tk = k_ref.shape[0]

    @pl.when(ki == 0)
    def _():
        m_sc[...] = jnp.full_like(m_sc, NEG_INF)
        l_sc[...] = jnp.zeros_like(l_sc)
        acc_sc[...] = jnp.zeros_like(acc_sc)

    def step(masked):
        n_hh = 2 * FOX_HP_PER_STEP

        def scores_of(hh):
            kc = k_ref[:, (hh // 2) * _FOX_KW:(hh // 2 + 1) * _FOX_KW]
            return _dot_nt(kc, q_ref[:, hh * _FOX_KW:(hh + 1) * _FOX_KW])

        def softmax_update(hh, s):
            if masked:
                kpos = lax.broadcasted_iota(jnp.int32, (tk, tq), 0) + ki * tk
                qpos = lax.broadcasted_iota(jnp.int32, (tk, tq), 1) + qi * tq
                s = jnp.where(kpos <= qpos, s, NEG_INF)
            m_prev = m_sc[hh:hh + 1, :]
            m_new = jnp.maximum(m_prev, jnp.max(s, axis=0, keepdims=True))
            alpha = jnp.exp2(m_prev - m_new)
            p = jnp.exp2(s - m_new)
            l_sc[hh:hh + 1, :] = alpha * l_sc[hh:hh + 1, :] + jnp.sum(p, axis=0, keepdims=True)
            m_sc[hh:hh + 1, :] = m_new
            return alpha, p.astype(BF16)

        def accumulate(hh, alpha, p):
            vt = vt_ref[0, (hh // 2) * LANES:(hh // 2 + 1) * LANES, :]
            acc_sc[hh] = alpha * acc_sc[hh] + _dot(vt, p)

        s_next = {0: scores_of(0)}
        if n_hh > 1:
            s_next[1] = scores_of(1)
        pending = None
        for hh in range(n_hh):
            alpha_p = softmax_update(hh, s_next.pop(hh))
            if hh + 2 < n_hh:
                s_next[hh + 2] = scores_of(hh + 2)
            if pending is not None:
                accumulate(*pending)
            pending = (hh, *alpha_p)
        accumulate(*pending)

    last = (qi * tq) // tk

    @pl.when(ki < last)
    def _():
        step(False)

    @pl.when(ki == last)
    def _():
        step(True)
        dim = lax.broadcasted_iota(jnp.int32, (LANES, tq), 0)
        for hp in range(FOX_HP_PER_STEP):
            o0 = acc_sc[2 * hp] / l_sc[2 * hp:2 * hp + 1, :]
            o1 = acc_sc[2 * hp + 1] / l_sc[2 * hp + 1:2 * hp + 2, :]
            o_ref[:, hp * LANES:(hp + 1) * LANES] = jnp.where(dim < DH_FOX, o0, o1).T.astype(o_ref.dtype)


def _fox_prompt(qc, kc, vt, batch, seq):
    tq = min(FOX_TILE, seq)
    tk = min(FOX_KTILE, seq)
    nq = seq // tq
    nk = seq // tk
    pairs = [(i, j) for i in range(nq) for j in range((i * tq) // tk + 1)]
    qi_tab = jnp.asarray([p[0] for p in pairs], jnp.int32)
    ki_tab = jnp.asarray([p[1] for p in pairs], jnp.int32)
    g = FOX_HP_PER_STEP
    grid_spec = pltpu.PrefetchScalarGridSpec(
        num_scalar_prefetch=2,
        grid=(batch, _N_HP // g, len(pairs)),
        in_specs=[
            pl.BlockSpec((tq, g * _FOX_QW), lambda b, h, t, qt, kt: (b * nq + qt[t], h)),
            pl.BlockSpec((tk, g * _FOX_KW), lambda b, h, t, qt, kt: (b * nk + kt[t], h)),
            pl.BlockSpec((1, g * LANES, tk), lambda b, h, t, qt, kt: (b, h, kt[t])),
        ],
        out_specs=pl.BlockSpec((tq, g * LANES), lambda b, h, t, qt, kt: (b * nq + qt[t], h)),
        scratch_shapes=[pltpu.VMEM((2 * g, tq), F32), pltpu.VMEM((2 * g, tq), F32),
                        pltpu.VMEM((2 * g, LANES, tq), F32)],
    )
    return pl.pallas_call(
        _fox_prompt_kernel,
        grid_spec=grid_spec,
        out_shape=jax.ShapeDtypeStruct((batch * seq, FOX_W), BF16),
        compiler_params=_cparams("parallel", "parallel", "arbitrary"),
        name="fox_prompt",
    )(qi_tab, ki_tab, qc, kc, vt)


def _fox_sample_kernel(pt_ref, q_ref, *refs, n_pp):
    kc, vc, lfc = refs[0:n_pp], refs[n_pp:2 * n_pp], refs[2 * n_pp:3 * n_pp]
    (kn_ref, vn_ref, lfn_ref, tri_ref, trin_ref, o_ref,
     qbd_sc, m_sc, l_sc, acc_sc, carry_sc, kall_sc, vall_sc) = refs[3 * n_pp:]
    step = pl.program_id(1)
    ls = q_ref.shape[1]
    nr = H_FOX * ls
    page = tri_ref.shape[0]
    row_head = lax.broadcasted_iota(jnp.int32, (nr, FOX_W), 0) // ls
    lane_head = lax.broadcasted_iota(jnp.int32, (nr, FOX_W), 1) // DH_FOX

    def rows_of_heads(x):
        return jnp.broadcast_to(x[:, None, :], (H_FOX, ls, x.shape[-1])).reshape(nr, x.shape[-1])

    cn_row = rows_of_heads(_sel_dot_r(lfn_ref[0], trin_ref[...]))
    tnew = lax.broadcasted_iota(jnp.int32, (nr, ls), 1)
    tq = lax.broadcasted_iota(jnp.int32, (nr, ls), 0) % ls
    cn_col = jnp.sum(jnp.where(tnew == tq, cn_row, 0.0), axis=-1, keepdims=True)

    @pl.when(step == 0)
    def _():
        qt = jnp.concatenate([q_ref[0]] * H_FOX, axis=0)
        qbd_sc[...] = jnp.where(row_head == lane_head, qt, jnp.zeros_like(qt))
        m_sc[...] = jnp.full_like(m_sc, NEG_INF)
        l_sc[...] = jnp.zeros_like(l_sc)
        acc_sc[...] = jnp.zeros_like(acc_sc)
        carry_sc[...] = jnp.zeros_like(carry_sc)

    def online(scores, weighted_values):
        m_prev = m_sc[...]
        m_new = m_prev
        for s in scores:
            m_new = jnp.maximum(m_new, jnp.max(s, axis=-1, keepdims=True))
        alpha = jnp.exp(m_prev - m_new)
        l_new = alpha * l_sc[...]
        acc = alpha * acc_sc[...]
        for s, pv in zip(scores, weighted_values):
            pr = jnp.exp(s - m_new)
            l_new = l_new + jnp.sum(pr, axis=-1, keepdims=True)
            acc = acc + pv(pr.astype(BF16))
        m_sc[...] = m_new
        l_sc[...] = l_new
        acc_sc[...] = acc

    c_loc = _sel_dot_r(jnp.concatenate([r[0, 0] for r in lfc], axis=0), tri_ref[...])
    carry = carry_sc[...][:, 0:1]
    qbd = qbd_sc[...]
    c_pages = []
    for j in range(n_pp):
        c_page = c_loc[j * H_FOX:(j + 1) * H_FOX] + carry
        carry = c_page[:, page - 1:page]
        c_pages.append(rows_of_heads(c_page))
        kall_sc[:, j * page:(j + 1) * page] = kc[j][0, 0].astype(BF16)
        vall_sc[:, j * page:(j + 1) * page] = vc[j][0, 0].astype(BF16)
    carry_sc[...] = jnp.broadcast_to(carry, carry_sc.shape)
    s_all = _dot(qbd, kall_sc[...]) + cn_col - jnp.concatenate(c_pages, axis=1)
    online([s_all], [lambda p: _dot_nt(p, vall_sc[...])])

    @pl.when(step == pl.num_programs(1) - 1)
    def _():
        s_new = _dot_nt(qbd, kn_ref[0].astype(BF16))
        s_new = s_new + cn_col - cn_row - rows_of_heads(jnp.broadcast_to(carry, (H_FOX, LANES)))[:, 0:1]
        s_new = jnp.where(tnew <= tq, s_new, NEG_INF)
        online([s_new], [lambda p: _dot(p, vn_ref[0].astype(BF16))])
        o = acc_sc[...] / l_sc[...]
        o = jnp.where(row_head == lane_head, o, 0.0)
        out = o[0:ls]
        for h in range(1, H_FOX):
            out = out + o[h * ls:(h + 1) * ls]
        o_ref[0] = out.astype(o_ref.dtype)


def _fox_sample(fq_bf, fk, fv, lfn_t, kcache_t, vcache_t, lfcache_t, page_table, layer):
    bs, n_pages = page_table.shape
    n_pp = min(FOX_PAGES_PER_STEP, n_pages)
    assert n_pages % n_pp == 0
    page = kcache_t.shape[3]
    ls = fq_bf.shape[1]
    nr = H_FOX * ls
    tri = jnp.asarray(np.triu(np.ones((page, page), np.float32)), BF16)
    trin = jnp.asarray(np.triu(np.ones((ls, ls), np.float32)), BF16)
    page_spec = lambda j, rows: pl.BlockSpec(
        (1, 1, rows, page), lambda b, s, pt: (layer, pt[b * n_pages + s * n_pp + j], 0, 0))
    per_seq = lambda *shape: pl.BlockSpec((1,) + shape, lambda b, s, pt: (b,) + (0,) * len(shape))
    const = lambda a: pl.BlockSpec(a.shape, lambda b, s, pt: (0,) * a.ndim)
    grid_spec = pltpu.PrefetchScalarGridSpec(
        num_scalar_prefetch=1,
        grid=(bs, n_pages // n_pp),
        in_specs=([per_seq(ls, FOX_W)] + [page_spec(j, FOX_W) for j in range(n_pp)] * 2
                  + [page_spec(j, H_FOX) for j in range(n_pp)]
                  + [per_seq(ls, FOX_W), per_seq(ls, FOX_W), per_seq(H_FOX, ls), const(tri), const(trin)]),
        out_specs=per_seq(ls, FOX_W),
        scratch_shapes=[pltpu.VMEM((nr, FOX_W), BF16), pltpu.VMEM((nr, 1), F32), pltpu.VMEM((nr, 1), F32),
                        pltpu.VMEM((nr, FOX_W), F32), pltpu.VMEM((H_FOX, LANES), F32),
                        pltpu.VMEM((FOX_W, n_pp * page), BF16), pltpu.VMEM((FOX_W, n_pp * page), BF16)],
    )
    return pl.pallas_call(
        functools.partial(_fox_sample_kernel, n_pp=n_pp),
        grid_spec=grid_spec,
        out_shape=jax.ShapeDtypeStruct((bs, ls, FOX_W), BF16),
        compiler_params=_cparams("parallel", "arbitrary"),
        name="fox_sample",
    )(page_table.reshape(-1), fq_bf, *([kcache_t] * n_pp), *([vcache_t] * n_pp), *([lfcache_t] * n_pp),
      fk, fv, lfn_t, tri, trin)


def _gla_prompt_kernel(q_ref, k_ref, v_ref, la_ref, tril_ref, ones_ref, onesT_ref,
                       o_ref, sfin_ref, s_sc):
    g = pl.program_id(1)
    n_tok = q_ref.shape[0]
    n_chunks = n_tok // GLA_CHUNK

    @pl.when(g == 0)
    def _():
        s_sc[...] = jnp.zeros_like(s_sc)

    la = la_ref[...]
    b = _sel_dot_l(tril_ref[...], la)
    bl = _sel_dot_l(ones_ref[...], la)
    q_dec = q_ref[...] * jnp.exp(b)
    q_dec_bf = q_dec.astype(BF16)
    k = k_ref[...]
    kd_bf = (k * jnp.exp(-b)).astype(BF16)
    kl_t = (k * jnp.exp(bl - b)).T
    v_bf = v_ref[...].astype(BF16)
    bl_col = _sel_dot_r(la.T, onesT_ref[...])

    r = lax.broadcasted_iota(jnp.int32, (n_tok, n_tok), 0)
    c = lax.broadcasted_iota(jnp.int32, (n_tok, n_tok), 1)
    causal = (r // GLA_CHUNK == c // GLA_CHUNK) & (c <= r)
    lane_q = lax.broadcasted_iota(jnp.int32, (n_tok, GLA_KW), 1) // DK_GLA
    zero_q = jnp.zeros_like(q_dec_bf)
    for h in range(H_GLA):
        att = _dot_nt(jnp.where(lane_q == h, q_dec_bf, zero_q), kd_bf)
        att = jnp.where(causal, att, 0.0)
        o_ref[:, h * DV_GLA:(h + 1) * DV_GLA] = _dot(att.astype(BF16), v_bf[:, h * DV_GLA:(h + 1) * DV_GLA])

    row_head = lax.broadcasted_iota(jnp.int32, (GLA_KW, GLA_VW), 0) // DK_GLA
    lane_head = lax.broadcasted_iota(jnp.int32, (GLA_KW, GLA_VW), 1) // DV_GLA
    same_head = row_head == lane_head
    tok_chunk = lax.broadcasted_iota(jnp.int32, (GLA_KW, n_tok), 1) // GLA_CHUNK
    for n in range(n_chunks):
        rows = slice(n * GLA_CHUNK, (n + 1) * GLA_CHUNK)
        s_prev = s_sc[...]
        o_ref[rows, :] += _dot(q_dec_bf[rows], s_prev.astype(BF16))
        kv = _dot(jnp.where(tok_chunk == n, kl_t, 0.0).astype(BF16), v_bf)
        dec = jnp.exp(bl_col[:, n * LANES:(n + 1) * LANES])
        dec = jnp.concatenate([dec] * (GLA_VW // LANES), axis=1)
        s_sc[...] = jnp.where(same_head, dec * s_prev + kv, 0.0)

    @pl.when(g == pl.num_programs(1) - 1)
    def _():
        s = s_sc[...]
        for h in range(H_GLA):
            sfin_ref[0, h] = s[h * DK_GLA:(h + 1) * DK_GLA, h * DV_GLA:(h + 1) * DV_GLA]


def _gla_prompt(gq, gk, gv, la, batch, seq):
    grp = min(GLA_GROUP, seq)
    ng = seq // grp
    nch = grp // GLA_CHUNK
    idx = np.arange(grp)
    same = (idx[:, None] // GLA_CHUNK) == (idx[None, :] // GLA_CHUNK)
    tril = jnp.asarray((same & (idx[None, :] <= idx[:, None])).astype(np.float32), BF16)
    ones = jnp.asarray(same.astype(np.float32), BF16)
    ones_t = jnp.asarray((idx[:, None] // GLA_CHUNK == np.arange(nch * LANES)[None, :] // LANES).astype(np.float32),
                         BF16)
    row = lambda w: pl.BlockSpec((grp, w), lambda b, g: (b * ng + g, 0))
    return pl.pallas_call(
        _gla_prompt_kernel,
        grid=(batch, ng),
        in_specs=[row(GLA_KW), row(GLA_KW), row(GLA_VW), row(GLA_KW),
                  _full_spec(tril.shape), _full_spec(ones.shape), _full_spec(ones_t.shape)],
        out_specs=[row(GLA_VW), pl.BlockSpec((1, H_GLA, DK_GLA, DV_GLA), lambda b, g: (b, 0, 0, 0))],
        out_shape=[jax.ShapeDtypeStruct((batch * seq, GLA_VW), F32),
                   jax.ShapeDtypeStruct((batch, H_GLA, DK_GLA, DV_GLA), F32)],
        scratch_shapes=[pltpu.VMEM((GLA_KW, GLA_VW), F32)],
        compiler_params=_cparams("parallel", "arbitrary"),
        name="gla_prompt",
    )(gq, gk, gv, la, tril, ones, ones_t)


def _gla_sample_kernel(q_ref, k_ref, v_ref, la_ref, s0_ref, tril_ref, o_ref, s1_ref):
    n, ls, _ = q_ref.shape
    la = la_ref[...]
    tril = jnp.broadcast_to(tril_ref[...][None], (n, ls, ls))
    hi, mid, lo = _split3(la)
    bdot = lambda x: jnp.einsum('nts,nsd->ntd', tril, x, preferred_element_type=F32)
    b = (bdot(hi) + bdot(mid)) + bdot(lo)
    b_last = b[:, ls - 1:ls, :]
    q_dec = (q_ref[...] * jnp.exp(b)).astype(BF16)
    k = k_ref[...]
    kd = (k * jnp.exp(-b)).astype(BF16)
    kl = (k * jnp.exp(b_last - b)).astype(BF16)
    v = v_ref[...].astype(BF16)
    s0 = s0_ref[...]
    att = jnp.einsum('ncd,nsd->ncs', q_dec, kd, preferred_element_type=F32)
    r = lax.broadcasted_iota(jnp.int32, (n, ls, ls), 1)
    c = lax.broadcasted_iota(jnp.int32, (n, ls, ls), 2)
    att = jnp.where(c <= r, att, 0.0)
    o = jnp.einsum('ncs,nse->nce', att.astype(BF16), v, preferred_element_type=F32)
    o = o + jnp.einsum('ncd,nde->nce', q_dec, s0.astype(BF16), preferred_element_type=F32)
    o_ref[...] = o
    kv = jnp.einsum('nds,nse->nde', jnp.swapaxes(kl, 1, 2), v, preferred_element_type=F32)
    dec = jnp.swapaxes(jnp.exp(b_last), 1, 2)
    s1_ref[...] = dec * s0 + kv


def _gla_sample(gq, gk, gv, la, state):
    n, ls, _ = gq.shape
    nb = min(32, n)
    tril = jnp.asarray(np.tril(np.ones((ls, ls), np.float32)), BF16)
    blk = lambda a, c: pl.BlockSpec((nb, a, c), lambda i: (i, 0, 0))
    return pl.pallas_call(
        _gla_sample_kernel,
        grid=(n // nb,),
        in_specs=[blk(ls, DK_GLA), blk(ls, DK_GLA), blk(ls, DV_GLA), blk(ls, DK_GLA), blk(DK_GLA, DV_GLA),
                  _full_spec(tril.shape)],
        out_specs=[blk(ls, DV_GLA), blk(DK_GLA, DV_GLA)],
        out_shape=[jax.ShapeDtypeStruct((n, ls, DV_GLA), F32), jax.ShapeDtypeStruct((n, DK_GLA, DV_GLA), F32)],
        compiler_params=_cparams("parallel"),
        name="gla_sample",
    )(gq, gk, gv, la, state, tril)


def _mixproj_kernel(x_ref, fox_ref, gla_ref, gr_ref, gg_ref, wo_ref, gc_ref, wq_ref, *rest, fused_memory):
    if fused_memory:
        mk_ref, mv_ref, x1_ref, q_ref = rest
    else:
        x1_ref, q_ref = rest
    gla = gla_ref[...]
    gr = gr_ref[...]
    acc = _dot(fox_ref[...], wo_ref[0:FOX_W, :])
    for h in range(H_GLA):
        cols = slice(h * DV_GLA, (h + 1) * DV_GLA)
        gh = _rms(gla[:, cols], gg_ref[:, cols])
        grh = gr[:, cols]
        gh = gh * (grh * jax.nn.sigmoid(grh))
        acc = acc + _dot(gh.astype(BF16), wo_ref[FOX_W + h * DV_GLA:FOX_W + (h + 1) * DV_GLA, :])
    x1 = x_ref[...] + acc
    x1_ref[...] = x1
    h2 = _rms(x1, gc_ref[...]).astype(BF16)
    q = (_dot(h2, wq_ref[...]) * (DH_MEM ** -0.5)).astype(BF16)
    if not fused_memory:
        q_ref[...] = q
        return
    for h in range(H_MEM):
        cols = slice(h * DH_MEM, (h + 1) * DH_MEM)
        s = _dot_nt(q[:, cols], mk_ref[0, :, cols].astype(BF16))
        e = jnp.exp(s - jnp.max(s, axis=-1, keepdims=True))
        p = e / jnp.sum(e, axis=-1, keepdims=True)
        q_ref[:, cols] = _dot(p.astype(BF16), mv_ref[0, :, cols].astype(BF16)).astype(q_ref.dtype)


def _mixproj(x2d, fox_o, gla_o, gr, g_gla, w_out_bf, g_cross, w_q_bf, memory=None):
    t = x2d.shape[0]
    tt = min(TOK_TILE, t)
    row = lambda w: pl.BlockSpec((tt, w), lambda i: (i, 0))
    operands = [x2d, fox_o, gla_o, gr, g_gla, w_out_bf, g_cross, w_q_bf]
    in_specs = [row(D_MODEL), row(FOX_W), row(GLA_VW), row(GLA_VW), _full_spec(g_gla.shape),
                _full_spec(w_out_bf.shape), _full_spec(g_cross.shape), _full_spec(w_q_bf.shape)]
    if memory is not None:
        tiles_per_seq = t // memory[0].shape[0] // tt
        assert tiles_per_seq >= 1
        mem_spec = pl.BlockSpec((1,) + memory[0].shape[1:], lambda i: (i // tiles_per_seq, 0, 0))
        operands += list(memory)
        in_specs += [mem_spec, mem_spec]
    return pl.pallas_call(
        functools.partial(_mixproj_kernel, fused_memory=memory is not None),
        grid=(t // tt,),
        in_specs=in_specs,
        out_specs=[row(D_MODEL), row(D_MODEL)],
        out_shape=[jax.ShapeDtypeStruct((t, D_MODEL), F32), jax.ShapeDtypeStruct((t, D_MODEL), BF16)],
        compiler_params=_cparams("parallel"),
        name="mixproj",
    )(*operands)


def _memkv_kernel(m_ref, g_ref, wk_ref, wv_ref, k_ref, v_ref):
    m = _rms(m_ref[...], g_ref[...]).astype(BF16)
    k_ref[...] = _dot(m, wk_ref[...])
    v_ref[...] = _dot(m, wv_ref[...])


def _memkv(mem2d, g_mem, wk_bf, wv_bf):
    t = mem2d.shape[0]
    tt = min(TOK_TILE, t)
    row = pl.BlockSpec((tt, D_MODEL), lambda i: (i, 0))
    return pl.pallas_call(
        _memkv_kernel,
        grid=(t // tt,),
        in_specs=[row, _full_spec(g_mem.shape), _full_spec(wk_bf.shape), _full_spec(wv_bf.shape)],
        out_specs=[row, row],
        out_shape=[jax.ShapeDtypeStruct((t, D_MODEL), F32)] * 2,
        compiler_params=_cparams("parallel"),
        name="memkv",
    )(mem2d, g_mem, wk_bf, wv_bf)


def _cross_kernel(q_ref, mk_ref, mv_ref, o_ref, *, per_head_memory):
    q = q_ref[0]
    if per_head_memory:
        lq = q.shape[0]
        n_mem = mk_ref.shape[2]
        q_hm = jnp.concatenate([q[:, h * DH_MEM:(h + 1) * DH_MEM] for h in range(H_MEM)], axis=0)
        mk = mk_ref[0, 0].reshape(n_mem * H_MEM, DH_MEM).astype(BF16)
        mv = mv_ref[0, 0].reshape(n_mem * H_MEM, DH_MEM).astype(BF16)
        s = _dot_nt(q_hm, mk)
        row_head = lax.broadcasted_iota(jnp.int32, s.shape, 0) // lq
        key_head = lax.broadcasted_iota(jnp.int32, s.shape, 1) % H_MEM
        s = jnp.where(row_head == key_head, s, NEG_INF)
        e = jnp.exp(s - jnp.max(s, axis=-1, keepdims=True))
        p = e / jnp.sum(e, axis=-1, keepdims=True)
        o = _dot(p.astype(BF16), mv).astype(o_ref.dtype)
        for h in range(H_MEM):
            o_ref[0, :, h * DH_MEM:(h + 1) * DH_MEM] = o[h * lq:(h + 1) * lq]
        return
    for h in range(H_MEM):
        cols = slice(h * DH_MEM, (h + 1) * DH_MEM)
        mk = mk_ref[0, :, cols].astype(BF16)
        mv = mv_ref[0, :, cols].astype(BF16)
        s = _dot_nt(q[:, cols], mk)
        e = jnp.exp(s - jnp.max(s, axis=-1, keepdims=True))
        p = e / jnp.sum(e, axis=-1, keepdims=True)
        o_ref[0, :, cols] = _dot(p.astype(BF16), mv).astype(o_ref.dtype)


def _cross(q3, mk, mv, layer=None):
    b, lq, _ = q3.shape
    tl = min(TOK_TILE, lq)
    if layer is None:
        mem_spec = pl.BlockSpec((1,) + mk.shape[1:], lambda i, j: (i, 0, 0))
    else:
        mem_spec = pl.BlockSpec((1, 1) + mk.shape[2:], lambda i, j: (layer, i, 0, 0, 0))
    return pl.pallas_call(
        functools.partial(_cross_kernel, per_head_memory=layer is not None),
        grid=(b, lq // tl),
        in_specs=[pl.BlockSpec((1, tl, D_MODEL), lambda i, j: (i, j, 0)), mem_spec, mem_spec],
        out_specs=pl.BlockSpec((1, tl, D_MODEL), lambda i, j: (i, j, 0)),
        out_shape=jax.ShapeDtypeStruct((b, lq, D_MODEL), BF16),
        compiler_params=_cparams("parallel", "parallel"),
        name="cross_attn",
    )(q3, mk, mv)


def _peer_prep_kernel(u_ref, v_ref, ub_ref, vt_ref):
    ub_ref[...] = u_ref[...].astype(BF16)
    vt_ref[...] = v_ref[...].T.astype(BF16)


def _peer_prep(u, v):
    ne = u.shape[0]
    eb = min(PREP_EBLK, ne)
    return pl.pallas_call(
        _peer_prep_kernel,
        grid=(ne // eb,),
        in_specs=[pl.BlockSpec((eb, D_MODEL), lambda i: (i, 0))] * 2,
        out_specs=[pl.BlockSpec((eb, D_MODEL), lambda i: (i, 0)), pl.BlockSpec((D_MODEL, eb), lambda i: (0, i))],
        out_shape=[jax.ShapeDtypeStruct((ne, D_MODEL), BF16), jax.ShapeDtypeStruct((D_MODEL, ne), BF16)],
        compiler_params=_cparams("parallel"),
        name="peer_prep",
    )(u, v)


def _rows_per_word():
    return 4 // jnp.dtype(BF16).itemsize


def _pack_rows(x):
    return pltpu.bitcast(x, jnp.uint32)


def _unpack_rows(x):
    return pltpu.bitcast(x, BF16)


def _top16_rows(s, break_ties):
    nrow = s.shape[0]
    rowid = lax.broadcasted_iota(jnp.int32, s.shape, 0)
    rank = jnp.full(s.shape, float(PEER_TOPK), F32)
    work = s
    vals = []
    for r in range(PEER_TOPK):
        mx = jnp.max(work, axis=0, keepdims=True)
        if break_ties:
            sel = rowid == jnp.min(jnp.where(work == mx, rowid, nrow), axis=0, keepdims=True)
        else:
            sel = work == mx
        rank = jnp.where(sel, float(r), rank)
        work = jnp.where(sel, -jnp.inf, work)
        vals.append(mx)
    return rank, jnp.concatenate(vals, axis=0)


def _ranked_16(rank):
    n = jnp.sum(jnp.where(rank < float(PEER_TOPK), 1.0, 0.0), axis=0, keepdims=True)
    return jnp.where(n == float(PEER_TOPK), 1.0, 0.0)


def _peer_route_kernel(x_ref, x1_ref, wo_ref, g_ref, wq_ref, sk_ref, ca_ref, cb_ref, cg_ref,
                       x2_ref, hn_ref, c1_ref, nrow_ref, p2_ref, rk2_ref, q_sc, s_sc):
    tt = x_ref.shape[0]
    nch = tt // LANES
    x2 = x1_ref[...] + _dot(x_ref[...], wo_ref[...])
    x2_ref[...] = x2
    hn = _rms(x2, g_ref[...])
    hn_ref[...] = hn.T.astype(BF16)
    q_sc[...] = _dot(hn.astype(BF16), wq_ref[...]).astype(BF16)
    for hj in range(2 * H_PEER):
        st = _dot_nt(sk_ref[hj % 2], q_sc[:, hj * N_KEYS:(hj + 1) * N_KEYS])
        for ch in range(nch):
            s_sc[hj, ch] = st[:, ch * LANES:(ch + 1) * LANES]

    cand_row = lax.broadcasted_iota(jnp.int32, (N_CAND_PAD, LANES), 0)

    per_unit = 2 if nch % 2 == 0 else 1

    def unit(i, carry):
        h = i // (nch // per_unit)
        first = (i % (nch // per_unit)) * per_unit
        def route(break_ties):
            halves = [rank_halves(h, first + d, break_ties) for d in range(per_unit)]
            tie_free = select_pairs(h, first, break_ties, *halves[0])
            for d in range(1, per_unit):
                tie_free = tie_free * select_pairs(h, first + d, break_ties, *halves[d])
            return tie_free

        tie_free = route(False)

        @pl.when(jnp.min(tie_free) < 0.5)
        def _():
            route(True)

        return carry

    def rank_halves(h, ch, break_ties):
        s1 = s_sc[2 * h, ch]
        s2 = s_sc[2 * h + 1, ch]
        return (s1, s2) + _top16_rows(s1, break_ties) + _top16_rows(s2, break_ties)

    def select_pairs(h, ch, break_ties, s1, s2, rank1, v1, rank2, v2):
        cand = _sel_dot_l(ca_ref[...], v1) + _sel_dot_l(cb_ref[...], v2)
        cand = jnp.where(cand_row < N_CAND, cand, -jnp.inf)
        crank, _ = _top16_rows(cand, break_ties)
        chosen = crank < float(PEER_TOPK)
        n_of_a = _dot(cg_ref[...], jnp.where(chosen, 1.0, 0.0).astype(BF16))
        z = jnp.sum(jnp.where(chosen, jnp.exp(cand - cand[0:1]), 0.0), axis=0, keepdims=True)
        nrow = jnp.zeros_like(rank1)
        for a in range(PEER_TOPK):
            nrow = jnp.where(rank1 == float(a), n_of_a[a:a + 1], nrow)
        c1_ref[h, ch] = jnp.exp(s1 - v1[0:1]) / z
        nrow_ref[h, ch] = nrow
        p2_ref[h, ch] = _pack_rows(jnp.exp(s2 - v2[0:1]).astype(BF16))
        rk2_ref[h, ch] = _pack_rows(rank2.astype(BF16))
        return _ranked_16(rank1) * _ranked_16(rank2) * _ranked_16(crank)

    lax.fori_loop(0, H_PEER * nch // per_unit, unit, 0)


def _peer_route(oc2d, x1, w_o_bf, g_ffn, w_q_bf, sk_bf):
    t = x1.shape[0]
    tt = min(PEER_TOK, t)
    nch = tt // LANES
    pairs = [(a, b) for a in range(PEER_TOPK) for b in range(PEER_TOPK) if (a + 1) * (b + 1) <= PEER_TOPK]
    assert len(pairs) == N_CAND
    ca = np.zeros((N_CAND_PAD, PEER_TOPK), np.float32)
    cb = np.zeros((N_CAND_PAD, PEER_TOPK), np.float32)
    cg = np.zeros((PEER_TOPK, N_CAND_PAD), np.float32)
    for i, (a, b) in enumerate(pairs):
        ca[i, a] = 1.0
        cb[i, b] = 1.0
        cg[a, i] = 1.0
    ca, cb, cg = (jnp.asarray(m, BF16) for m in (ca, cb, cg))
    row = lambda w: pl.BlockSpec((tt, w), lambda i: (i, 0))
    krows = N_KEYS // _rows_per_word()
    sel = pl.BlockSpec((H_PEER, nch, N_KEYS, LANES), lambda i: (0, i, 0, 0))
    sel_shape = jax.ShapeDtypeStruct((H_PEER, t // LANES, N_KEYS, LANES), F32)
    packed = pl.BlockSpec((H_PEER, nch, krows, LANES), lambda i: (0, i, 0, 0))
    packed_shape = jax.ShapeDtypeStruct((H_PEER, t // LANES, krows, LANES), jnp.uint32)
    return pl.pallas_call(
        _peer_route_kernel,
        grid=(t // tt,),
        in_specs=[row(D_MODEL), row(D_MODEL), _full_spec(w_o_bf.shape), _full_spec(g_ffn.shape),
                  _full_spec(w_q_bf.shape), _full_spec(sk_bf.shape),
                  _full_spec(ca.shape), _full_spec(cb.shape), _full_spec(cg.shape)],
        out_specs=[row(D_MODEL), pl.BlockSpec((D_MODEL, tt), lambda i: (0, i)), sel, sel, packed, packed],
        out_shape=[jax.ShapeDtypeStruct((t, D_MODEL), F32), jax.ShapeDtypeStruct((D_MODEL, t), BF16),
                   sel_shape, sel_shape, packed_shape, packed_shape],
        scratch_shapes=[pltpu.VMEM((tt, 2 * H_PEER * N_KEYS), BF16),
                        pltpu.VMEM((2 * H_PEER, nch, N_KEYS, LANES), F32)],
        compiler_params=_cparams("parallel"),
        name="peer_route",
    )(oc2d, x1, w_o_bf, g_ffn, w_q_bf, sk_bf, ca, cb, cg)


def _peer_mix_kernel(x2_ref, hn_ref, u_ref, vt_ref, c1_ref, nrow_ref, p2_ref, rk2_ref, gf_ref,
                     y_ref, a_sc, w_sc, acc_sc, *, final_norm):
    e = pl.program_id(1)
    tt = hn_ref.shape[1]
    nch = tt // LANES
    n_i1 = u_ref.shape[0] // N_KEYS

    @pl.when(e == 0)
    def _():
        acc_sc[...] = jnp.zeros_like(acc_sc)

    krows = N_KEYS // _rows_per_word()
    acc = acc_sc[...]
    def activations(c):
        lo = c * PEER_CHUNK_I1
        ex = slice(lo * N_KEYS, (lo + PEER_CHUNK_I1) * N_KEYS)
        a_sc[lo * krows:(lo + PEER_CHUNK_I1) * krows, :] = _pack_rows(
            jax.nn.gelu(_dot(u_ref[ex, :], hn_ref[...])).astype(BF16))

    def weights(c):
        lo = c * PEER_CHUNK_I1
        for ch in range(nch):
            cols = slice(ch * LANES, (ch + 1) * LANES)
            for r in range(lo, lo + PEER_CHUNK_I1):
                w = jnp.zeros((N_KEYS, LANES), BF16)
                for h in range(H_PEER):
                    wide = lambda ref: jnp.broadcast_to(ref[h, ch, r:r + 1, :], (N_KEYS, LANES)).astype(BF16)
                    p2 = _unpack_rows(p2_ref[h, ch])
                    hit = _unpack_rows(rk2_ref[h, ch]) < wide(nrow_ref)
                    w = w + jnp.where(hit, p2, jnp.zeros_like(p2)) * wide(c1_ref)
                w_sc[r * N_KEYS:(r + 1) * N_KEYS, cols] = w * _unpack_rows(a_sc[r * krows:(r + 1) * krows, cols])

    n_chunks = n_i1 // PEER_CHUNK_I1
    per_sub = PEER_SUB_I1 // PEER_CHUNK_I1
    for c in range(min(2, n_chunks)):
        activations(c)
    for c in range(n_chunks):
        weights(c)
        if c + 2 < n_chunks:
            activations(c + 2)
        if (c + 1) % per_sub == 0:
            ex = slice((c + 1 - per_sub) * PEER_CHUNK_I1 * N_KEYS, (c + 1) * PEER_CHUNK_I1 * N_KEYS)
            acc = acc + _dot(vt_ref[:, ex], w_sc[ex, :])
    acc_sc[...] = acc

    @pl.when(e == pl.num_programs(1) - 1)
    def _():
        y = x2_ref[...] + acc_sc[...].T
        y_ref[...] = _rms(y, gf_ref[...]) if final_norm else y


def _peer_mix(x2, hn, u_bf, vt_bf, c1, nrow, p2, rk2, g_final, final_norm):
    t = x2.shape[0]
    tt = min(PEER_TOK, t)
    nch = tt // LANES
    ne = u_bf.shape[0]
    eb = min(PEER_EBLK, ne)
    n_i1 = eb // N_KEYS
    row = lambda w: pl.BlockSpec((tt, w), lambda i, e: (i, 0))
    part = pl.BlockSpec((H_PEER, nch, n_i1, LANES), lambda i, e: (0, i, e, 0))
    full = pl.BlockSpec((H_PEER, nch, N_KEYS // _rows_per_word(), LANES), lambda i, e: (0, i, 0, 0))
    return pl.pallas_call(
        functools.partial(_peer_mix_kernel, final_norm=final_norm),
        grid=(t // tt, ne // eb),
        in_specs=[row(D_MODEL), pl.BlockSpec((D_MODEL, tt), lambda i, e: (0, i)),
                  pl.BlockSpec((eb, D_MODEL), lambda i, e: (e, 0)),
                  pl.BlockSpec((D_MODEL, eb), lambda i, e: (0, e)),
                  part, part, full, full, pl.BlockSpec(g_final.shape, lambda i, e: (0, 0))],
        out_specs=row(D_MODEL),
        out_shape=jax.ShapeDtypeStruct((t, D_MODEL), F32),
        scratch_shapes=[pltpu.VMEM((eb // _rows_per_word(), tt), jnp.uint32), pltpu.VMEM((eb, tt), BF16),
                        pltpu.VMEM((D_MODEL, tt), F32)],
        compiler_params=_cparams("parallel", "arbitrary"),
        name="peer_mix",
    )(x2, hn, u_bf, vt_bf, c1, nrow, p2, rk2, g_final)


def _prep_in_weights(w_in, b_fox_f, w_gate2, b_gate):
    o = np.cumsum((0,) + IN_SIZES)
    seg = lambda i: w_in[:, int(o[i]):int(o[i + 1])]
    pad = jnp.zeros((D_MODEL, LANES - H_FOX - GLA_GATE_RANK), w_in.dtype)
    w_perm = jnp.concatenate([seg(0), seg(1), seg(2), seg(4), seg(5), seg(6), seg(8), seg(3), seg(7), pad],
                             axis=1).astype(BF16)
    b_small = jnp.zeros((1, LANES), F32).at[0, 0:H_FOX].set(b_fox_f)
    wg_pad = jnp.zeros((LANES, GLA_KW), F32).at[H_FOX:H_FOX + GLA_GATE_RANK].set(w_gate2).astype(BF16)
    return w_perm, b_small, wg_pad, b_gate.reshape(1, GLA_KW)


def kernel(x_prompt, x_sample, cache_fox_k, cache_fox_v, cache_fox_logf, state_gla, cache_mem_k, cache_mem_v, page_table, mem_prompt, g_mix, w_in, b_fox_f, w_gla_gate2, b_gla_gate, g_gla_out, w_out, g_cross, g_mem, w_mem_k, w_mem_v, w_cross_q, w_cross_o, g_ffn, peer_w_q, peer_subkeys, peer_u, peer_v, g_final):
    depth = w_in.shape[0]
    bp, lp, _ = x_prompt.shape
    bs, ls, _ = x_sample.shape
    n_mem = mem_prompt.shape[1]
    n_pool, page = cache_fox_k.shape[1:3]
    xp = x_prompt.reshape(bp * lp, D_MODEL)
    xs = x_sample.reshape(bs * ls, D_MODEL)
    vec = lambda a: a.reshape(1, -1)
    kcache_t = cache_fox_k.transpose(0, 1, 3, 4, 2).reshape(depth, n_pool, FOX_W, page)
    vcache_t = cache_fox_v.transpose(0, 1, 3, 4, 2).reshape(depth, n_pool, FOX_W, page)
    lfcache_t = cache_fox_logf.transpose(0, 1, 3, 2)
    outs = [[] for _ in range(10)]
    for l in range(depth):
        w_perm, b_small, wg_pad, b_gate = _prep_in_weights(w_in[l], b_fox_f[l], w_gla_gate2[l], b_gla_gate[l])
        w_out_bf = w_out[l].astype(BF16)
        w_cq_bf = w_cross_q[l].astype(BF16)
        w_co_bf = w_cross_o[l].astype(BF16)
        w_pq_bf = peer_w_q[l].astype(BF16)
        sk_bf = peer_subkeys[l].astype(BF16)
        u_bf, vt_bf = _peer_prep(peer_u[l], peer_v[l])
        last = l == depth - 1

        def tail(x, fox_o, gla_o, gr, mk, mv, nb, mem_layer):
            mix_args = (x, fox_o, gla_o, gr, vec(g_gla_out[l]), w_out_bf, vec(g_cross[l]), w_cq_bf)
            if mem_layer is None and (x.shape[0] // nb) % TOK_TILE == 0:
                x1, oc = _mixproj(*mix_args, memory=(mk, mv))
            else:
                x1, qc = _mixproj(*mix_args)
                oc = _cross(qc.reshape(nb, -1, D_MODEL), mk, mv, mem_layer).reshape(-1, D_MODEL)
            x2, hn, c1, nrow, p2, rk2 = _peer_route(oc, x1, w_co_bf, vec(g_ffn[l]), w_pq_bf, sk_bf)
            return _peer_mix(x2, hn, u_bf, vt_bf, c1, nrow, p2, rk2, vec(g_final), last)

        fq_bf, fk, fv, lf, gq, gk, gv, la, gr = _inproj(xp, vec(g_mix[l]), w_perm, b_small, wg_pad, b_gate,
                                                        DH_FOX ** -0.5 * _LOG2E)
        qc, kc, vt, fk_t, fv_t, lf_t = _fox_prep(fq_bf, fk, fv, lf, bp, lp)
        fox_o = _fox_prompt(qc, kc, vt, bp, lp)
        gla_o, s_fin = _gla_prompt(gq, gk, gv, la, bp, lp)
        mk, mv = _memkv(mem_prompt.reshape(bp * n_mem, D_MODEL), vec(g_mem[l]), w_mem_k[l].astype(BF16),
                        w_mem_v[l].astype(BF16))
        mk3 = mk.reshape(bp, n_mem, D_MODEL)
        mv3 = mv.reshape(bp, n_mem, D_MODEL)
        xp = tail(xp, fox_o, gla_o, gr, mk3, mv3, bp, None)
        outs[0].append(fk_t.reshape(bp, H_FOX, DH_FOX, lp).transpose(0, 3, 1, 2))
        outs[1].append(fv_t.reshape(bp, H_FOX, DH_FOX, lp).transpose(0, 3, 1, 2))
        outs[2].append(lf_t.transpose(0, 2, 1))
        outs[3].append(s_fin)
        outs[4].append(mk.reshape(bp, n_mem, H_MEM, DH_MEM))
        outs[5].append(mv.reshape(bp, n_mem, H_MEM, DH_MEM))

        fq_bf, fk, fv, lf, gq, gk, gv, la, gr = _inproj(xs, vec(g_mix[l]), w_perm, b_small, wg_pad, b_gate,
                                                        DH_FOX ** -0.5)
        seqs = lambda a: a.reshape(bs, ls, FOX_W)
        fox_o = _fox_sample(seqs(fq_bf), seqs(fk), seqs(fv), lf.reshape(bs, ls, H_FOX).transpose(0, 2, 1),
                            kcache_t, vcache_t, lfcache_t, page_table, l).reshape(bs * ls, FOX_W)
        heads = lambda a, d: a.reshape(bs, ls, H_GLA, d).transpose(0, 2, 1, 3).reshape(bs * H_GLA, ls, d)
        gla_o, s_new = _gla_sample(heads(gq, DK_GLA), heads(gk, DK_GLA), heads(gv, DV_GLA), heads(la, DK_GLA),
                                   state_gla[l].reshape(bs * H_GLA, DK_GLA, DV_GLA))
        gla_o = gla_o.reshape(bs, H_GLA, ls, DV_GLA).transpose(0, 2, 1, 3).reshape(bs * ls, GLA_VW)
        xs = tail(xs, fox_o, gla_o, gr, cache_mem_k, cache_mem_v, bs, l)
        outs[6].append(fk.reshape(bs, ls, H_FOX, DH_FOX))
        outs[7].append(fv.reshape(bs, ls, H_FOX, DH_FOX))
        outs[8].append(lf.reshape(bs, ls, H_FOX))
        outs[9].append(s_new.reshape(bs, H_GLA, DK_GLA, DV_GLA))
    stacked = [jnp.stack(o, axis=0) for o in outs]
    return (xp.reshape(bp, lp, D_MODEL), xs.reshape(bs, ls, D_MODEL), *stacked)
```
